```python
import math
import jax
import jax.numpy as jnp
from jax import lax
import numpy as np

D_MODEL = 2048
BATCH = 4
SEQ = 2048
DEPTH = 1

D_MIX = D_MODEL
D_RET = D_MIX // 2
D_S5 = D_MIX - D_RET
RET_HEADS = 4
RET_HEAD_DIM = D_RET // RET_HEADS
RET_CHUNK = 128
S5_GROUP = 16
S5_GROUPS = D_S5 // S5_GROUP
S5_STATE = 64
S5_DT_MIN = 0.001
S5_DT_MAX = 0.1
D_FF = -(-8 * D_MODEL // (3 * 256)) * 256
N_PROJ = 4 * D_RET + D_S5
ROPE_BASE = 10000.0
EPS = 1e-6

kernel_name = 'hybrid_retention_s5_adaln_block'


def rms_norm(x, w):
    xf = x.astype(jnp.float32)
    y = xf * lax.rsqrt(jnp.mean(xf * xf, axis=-1, keepdims=True) + EPS)
    return (y * w.astype(jnp.float32)).astype(x.dtype)


def rotary(x, pos):
    half = x.shape[-1] // 2
    freqs = ROPE_BASE ** (-jnp.arange(half, dtype=jnp.float32) / half)
    ang = pos[:, None] * freqs[None, :]
    cos = jnp.cos(ang)[None, :, None, :].astype(x.dtype)
    sin = jnp.sin(ang)[None, :, None, :].astype(x.dtype)
    x1, x2 = x[..., :half], x[..., half:]
    return jnp.concatenate([x1 * cos - x2 * sin, x2 * cos + x1 * sin], axis=-1)


def retention(q, k, v):
    b, l, h, dk = q.shape
    dv = v.shape[-1]
    nc = l // RET_CHUNK
    log_gamma = jnp.log1p(-jnp.exp2(-5.0 - jnp.arange(h, dtype=jnp.float32)))
    idx = jnp.arange(RET_CHUNK, dtype=jnp.float32)
    diff = idx[:, None] - idx[None, :]
    intra = jnp.where(diff[None] >= 0.0,
                      jnp.exp(log_gamma[:, None, None] * jnp.maximum(diff, 0.0)[None]), 0.0)
    k_decay = jnp.exp(log_gamma[:, None] * (RET_CHUNK - 1.0 - idx)[None])
    q_decay = jnp.exp(log_gamma[:, None] * (idx + 1.0)[None])
    chunk_decay = jnp.exp(log_gamma * RET_CHUNK)
    qc = q.astype(jnp.float32).reshape(b, nc, RET_CHUNK, h, dk)
    kc = (k.astype(jnp.float32) * dk ** -0.5).reshape(b, nc, RET_CHUNK, h, dk)
    vc = v.astype(jnp.float32).reshape(b, nc, RET_CHUNK, h, dv)
    scores = jnp.einsum('bnihd,bnjhd->bnhij', qc, kc) * intra
    inner = jnp.einsum('bnhij,bnjhv->bnihv', scores, vc)
    kv = jnp.einsum('bnjhd,hj,bnjhv->bnhdv', kc, k_decay, vc)

    def step(state, kv_n):
        return state * chunk_decay[None, :, None, None] + kv_n, state

    _, prev = lax.scan(step, jnp.zeros((b, h, dk, dv), jnp.float32), jnp.moveaxis(kv, 1, 0))
    prev = jnp.moveaxis(prev, 0, 1)
    cross = jnp.einsum('bnihd,bnhdv->bnihv', qc, prev) * q_decay.T[None, None, :, :, None]
    return (inner + cross).reshape(b, l, h, dv)


def head_group_norm(y, w):
    mu = jnp.mean(y, axis=-1, keepdims=True)
    var = jnp.mean(jnp.square(y - mu), axis=-1, keepdims=True)
    yn = (y - mu) * lax.rsqrt(var + EPS)
    return yn.reshape(y.shape[0], y.shape[1], -1) * w.astype(jnp.float32)


def _ssm_combine(e1, e2):
    a1, b1 = e1
    a2, b2 = e2
    return (a2 * a1, a2 * b1 + b2)


def s5_scan(u, a_re, a_im, log_step, b_re, b_im, c_re, c_im, d_skip):
    b, l, _ = u.shape
    uf = u.astype(jnp.float32).reshape(b, l, S5_GROUPS, S5_GROUP)
    lam = lax.complex(a_re.astype(jnp.float32), a_im.astype(jnp.float32))
    dt = jnp.exp(log_step.astype(jnp.float32))[:, None]
    lam_bar = jnp.exp(lam * dt)
    b_mat = lax.complex(b_re.astype(jnp.float32), b_im.astype(jnp.float32))
    b_bar = ((lam_bar - 1.0) / lam)[..., None] * b_mat
    bu = jnp.einsum('gnp,blgp->blgn', b_bar, uf.astype(jnp.complex64))
    a = jnp.broadcast_to(lam_bar[None, None], bu.shape)
    _, states = lax.associative_scan(_ssm_combine, (a, bu), axis=1)
    c_mat = lax.complex(c_re.astype(jnp.float32), c_im.astype(jnp.float32))
    y = jnp.einsum('gpn,blgn->blgp', c_mat, states).real + d_skip.astype(jnp.float32) * uf
    return y.reshape(b, l, D_S5)


def setup_inputs(seed: int = 0) -> dict:
    key = jax.random.key(seed)
    ks = jax.random.split(key, 24)
    f32 = jnp.float32

    def nrm(k, shape, scale):
        return jax.random.normal(k, shape, f32) * scale

    n_idx = jnp.arange(S5_STATE, dtype=f32)
    x = nrm(ks[0], (BATCH, SEQ, D_MODEL), 1.0)
    c = nrm(ks[1], (BATCH, D_MODEL), 1.0)
    w_ada = nrm(ks[2], (DEPTH, D_MODEL, 6 * D_MODEL), D_MODEL ** -0.5)
    b_ada = nrm(ks[3], (DEPTH, 6 * D_MODEL), 0.01)
    norm1_w = 1.0 + nrm(ks[4], (DEPTH, D_MODEL), 0.02)
    w_in = nrm(ks[5], (DEPTH, D_MODEL, N_PROJ), D_MODEL ** -0.5)
    ret_norm_w = 1.0 + nrm(ks[6], (DEPTH, D_RET), 0.02)
    s5_a_re = -0.5 + nrm(ks[7], (DEPTH, S5_GROUPS, S5_STATE), 0.01)
    s5_a_im = math.pi * n_idx + nrm(ks[8], (DEPTH, S5_GROUPS, S5_STATE), 0.01)
    s5_log_step = jax.random.uniform(ks[9], (DEPTH, S5_GROUPS), f32,
                                     math.log(S5_DT_MIN), math.log(S5_DT_MAX))
    s5_b_re = nrm(ks[10], (DEPTH, S5_GROUPS, S5_STATE, S5_GROUP), (2 * S5_GROUP) ** -0.5)
    s5_b_im = nrm(ks[11], (DEPTH, S5_GROUPS, S5_STATE, S5_GROUP), (2 * S5_GROUP) ** -0.5)
    s5_c_re = nrm(ks[12], (DEPTH, S5_GROUPS, S5_GROUP, S5_STATE), 0.5)
    s5_c_im = nrm(ks[13], (DEPTH, S5_GROUPS, S5_GROUP, S5_STATE), 0.5)
    s5_d = nrm(ks[14], (DEPTH, S5_GROUPS, S5_GROUP), 1.0)
    w_glu = nrm(ks[15], (DEPTH, D_S5, D_S5), D_S5 ** -0.5)
    b_glu = nrm(ks[16], (DEPTH, D_S5), 0.01)
    w_out = nrm(ks[17], (DEPTH, D_MIX, D_MODEL), D_MIX ** -0.5)
    norm2_w = 1.0 + nrm(ks[18], (DEPTH, D_MODEL), 0.02)
    w_gate_up = nrm(ks[19], (DEPTH, D_MODEL, 2 * D_FF), D_MODEL ** -0.5)
    w_down = nrm(ks[20], (DEPTH, D_FF, D_MODEL), D_FF ** -0.5)
    final_norm_w = 1.0 + nrm(ks[21], (D_MODEL,), 0.02)
    return {'x': x, 'c': c, 'w_ada': w_ada, 'b_ada': b_ada, 'norm1_w': norm1_w,
            'w_in': w_in, 'ret_norm_w': ret_norm_w,
            's5_a_re': s5_a_re, 's5_a_im': s5_a_im, 's5_log_step': s5_log_step,
            's5_b_re': s5_b_re, 's5_b_im': s5_b_im, 's5_c_re': s5_c_re, 's5_c_im': s5_c_im,
            's5_d': s5_d, 'w_glu': w_glu, 'b_glu': b_glu, 'w_out': w_out,
            'norm2_w': norm2_w, 'w_gate_up': w_gate_up, 'w_down': w_down,
            'final_norm_w': final_norm_w}


def reference(x, c, w_ada, b_ada, norm1_w, w_in, ret_norm_w,
              s5_a_re, s5_a_im, s5_log_step, s5_b_re, s5_b_im, s5_c_re, s5_c_im,
              s5_d, w_glu, b_glu, w_out, norm2_w, w_gate_up, w_down, final_norm_w):
    bsz, seq, _ = x.shape
    pos = jnp.arange(seq, dtype=jnp.float32)
    cond = jax.nn.silu(c)
    for layer in range(DEPTH):
        mod = (cond @ w_ada[layer] + b_ada[layer])[:, None, :]
        sh1, sc1, g1, sh2, sc2, g2 = jnp.split(mod, 6, axis=-1)

        h = rms_norm(x, norm1_w[layer]) * (1.0 + sc1) + sh1
        proj = h @ w_in[layer]
        q, k, v, g, u = jnp.split(proj, [D_RET, 2 * D_RET, 3 * D_RET, 4 * D_RET], axis=-1)
        q = rotary(q.reshape(bsz, seq, RET_HEADS, RET_HEAD_DIM), pos)
        k = rotary(k.reshape(bsz, seq, RET_HEADS, RET_HEAD_DIM), pos)
        v = v.reshape(bsz, seq, RET_HEADS, RET_HEAD_DIM)
        ret = head_group_norm(retention(q, k, v), ret_norm_w[layer])
        ret_out = jax.nn.silu(g) * ret.astype(x.dtype)

        ssm = s5_scan(u, s5_a_re[layer], s5_a_im[layer], s5_log_step[layer],
                      s5_b_re[layer], s5_b_im[layer], s5_c_re[layer], s5_c_im[layer],
                      s5_d[layer]).astype(x.dtype)
        ssm_g = jax.nn.gelu(ssm)
        ssm_out = ssm_g * jax.nn.sigmoid(ssm_g @ w_glu[layer] + b_glu[layer])

        mix = jnp.concatenate([ret_out, ssm_out], axis=-1) @ w_out[layer]
        x = x + g1 * mix

        h = rms_norm(x, norm2_w[layer]) * (1.0 + sc2) + sh2
        gate, up = jnp.split(h @ w_gate_up[layer], 2, axis=-1)
        x = x + g2 * ((jax.nn.silu(gate) * up) @ w_down[layer])
    return rms_norm(x, final_norm_w)
```

```python
import functools
import math

import numpy as np
import jax
import jax.numpy as jnp
from jax import lax
from jax.experimental import pallas as pl
from jax.experimental.pallas import tpu as pltpu

D_MODEL = 2048
BATCH = 4
SEQ = 2048
TOK = BATCH * SEQ
D_RET = 1024
D_S5 = 1024
RET_HEADS = 4
RET_HEAD_DIM = 256
RET_CHUNK = 128
S5_GROUP = 16
S5_GROUPS = 64
S5_STATE = 64
S5_SUB = 16
S5_ROW = S5_SUB * S5_GROUP
S5_NSUB = SEQ // S5_SUB
D_FF = 5632
N_PROJ = 4 * D_RET + D_S5
ROPE_BASE = 10000.0
EPS = 1e-6

F32 = jnp.float32
BF16 = jnp.bfloat16
HI = lax.Precision.HIGHEST
VMEM_LIMIT = 52 * 1024 * 1024


def _silu(v):
    return v * jax.nn.sigmoid(v)


def _adaln_kernel(c_ref, w_ref, b_ref, o_ref):
    cond = _silu(c_ref[...])
    o_ref[...] = jnp.dot(cond.astype(BF16), w_ref[...].astype(BF16),
                         preferred_element_type=F32) + b_ref[...]


def _adaln(c, w_ada, b_ada):
    tn = 1024
    n = w_ada.shape[1]
    return pl.pallas_call(
        _adaln_kernel,
        grid=(n // tn,),
        in_specs=[pl.BlockSpec((BATCH, D_MODEL), lambda j: (0, 0)),
                  pl.BlockSpec((D_MODEL, tn), lambda j: (0, j)),
                  pl.BlockSpec((1, tn), lambda j: (0, j))],
        out_specs=pl.BlockSpec((BATCH, tn), lambda j: (0, j)),
        out_shape=jax.ShapeDtypeStruct((BATCH, n), F32),
        compiler_params=pltpu.CompilerParams(
            dimension_semantics=("parallel",), vmem_limit_bytes=VMEM_LIMIT),
        name="adaln",
    )(c, w_ada, b_ada.reshape(1, n))


_INPROJ_TM = 1024
_INPROJ_TN = 256


def _inproj_kernel(x_ref, sc_ref, sh_ref, nw_ref, w_ref, o_ref, h_scr, cos_scr, sin_scr):
    tm = _INPROJ_TM
    half = RET_HEAD_DIM // 2
    i = pl.program_id(0)
    j = pl.program_id(1)

    @pl.when(j == 0)
    def _():
        x = x_ref[...]
        ms = jnp.mean(x * x, axis=-1, keepdims=True)
        y = x * lax.rsqrt(ms + EPS) * nw_ref[...]
        h_scr[...] = (y * (1.0 + sc_ref[...]) + sh_ref[...]).astype(BF16)
        base = (i % (SEQ // tm)) * tm
        pos = (lax.broadcasted_iota(jnp.int32, (tm, half), 0) + base).astype(F32)
        lane = lax.broadcasted_iota(jnp.int32, (tm, half), 1).astype(F32)
        ang = pos * jnp.exp(lane * (-math.log(ROPE_BASE) / half))
        cos_scr[...] = jnp.cos(ang)
        sin_scr[...] = jnp.sin(ang)

    acc = jnp.dot(h_scr[...], w_ref[...], preferred_element_type=F32)

    @pl.when(j < 2 * RET_HEADS)
    def _():
        x1 = acc[:, :half]
        x2 = acc[:, half:]
        cs = cos_scr[...]
        sn = sin_scr[...]
        o_ref[:, :half] = (x1 * cs - x2 * sn).astype(BF16)
        o_ref[:, half:] = (x2 * cs + x1 * sn).astype(BF16)

    @pl.when(j >= 2 * RET_HEADS)
    def _():
        o_ref[...] = acc.astype(BF16)


def _inproj(x2d, sc1, sh1, norm_w, w_in_bf):
    tm, tn = _INPROJ_TM, _INPROJ_TN
    per_b = SEQ // tm
    return pl.pallas_call(
        _inproj_kernel,
        grid=(TOK // tm, N_PROJ // tn),
        in_specs=[pl.BlockSpec((tm, D_MODEL), lambda i, j: (i, 0)),
                  pl.BlockSpec((None, 1, D_MODEL), lambda i, j: (i // per_b, 0, 0)),
                  pl.BlockSpec((None, 1, D_MODEL), lambda i, j: (i // per_b, 0, 0)),
                  pl.BlockSpec((1, D_MODEL), lambda i, j: (0, 0)),
                  pl.BlockSpec((D_MODEL, tn), lambda i, j: (0, j))],
        out_specs=pl.BlockSpec((tm, tn), lambda i, j: (i, j)),
        out_shape=jax.ShapeDtypeStruct((TOK, N_PROJ), BF16),
        scratch_shapes=[pltpu.VMEM((tm, D_MODEL), BF16),
                        pltpu.VMEM((tm, RET_HEAD_DIM // 2), F32),
                        pltpu.VMEM((tm, RET_HEAD_DIM // 2), F32)],
        compiler_params=pltpu.CompilerParams(
            dimension_semantics=("parallel", "arbitrary"), vmem_limit_bytes=VMEM_LIMIT),
        name="inproj",
    )(x2d, sc1, sh1, norm_w, w_in_bf)


def _ret_kernel(lg_ref, q_ref, k_ref, v_ref, g_ref, w_ref, o_ref, st_ref):
    C = RET_CHUNK
    dh = RET_HEAD_DIM
    scale = dh ** -0.5
    lg_c = lg_ref[0:1, 0:C]
    lg_d = lg_ref[0:1, :]
    ii = lax.broadcasted_iota(jnp.int32, (C, C), 0)
    jj = lax.broadcasted_iota(jnp.int32, (C, C), 1)
    diff = (ii - jj).astype(F32)
    intra = jnp.where(diff >= 0.0, jnp.exp(lg_c * jnp.maximum(diff, 0.0)), 0.0) * scale
    row = lax.broadcasted_iota(jnp.int32, (C, dh), 0).astype(F32)
    k_decay = jnp.exp(lg_d * (C - 1.0 - row)) * scale
    q_decay = jnp.exp(lg_d * (row + 1.0))
    chunk_decay = jnp.exp(lg_d * float(C))
    gn_w = w_ref[...]
    st_ref[...] = jnp.zeros_like(st_ref)

    def body(n, carry):
        r0 = pl.multiple_of(n * C, C)
        q = q_ref[pl.ds(r0, C), :]
        k = k_ref[pl.ds(r0, C), :]
        v = v_ref[pl.ds(r0, C), :]
        s = lax.dot_general(q, k, (((1,), (1,)), ((), ())), preferred_element_type=F32) * intra
        inner = jnp.dot(s.astype(BF16), v, preferred_element_type=F32)
        st = st_ref[...]
        cross = jnp.dot(q, st.astype(BF16), preferred_element_type=F32) * q_decay
        kd = (k.astype(F32) * k_decay).astype(BF16)
        kv = lax.dot_general(kd, v, (((0,), (0,)), ((), ())), preferred_element_type=F32)
        st_ref[...] = st * chunk_decay + kv
        y = inner + cross
        mu = jnp.mean(y, axis=-1, keepdims=True)
        yc = y - mu
        var = jnp.mean(yc * yc, axis=-1, keepdims=True)
        yn = yc * lax.rsqrt(var + EPS) * gn_w
        g = g_ref[pl.ds(r0, C), :].astype(F32)
        o_ref[pl.ds(r0, C), :] = (_silu(g) * yn).astype(BF16)
        return carry

    lax.fori_loop(0, SEQ // C, body, 0)


def _retention(proj, ret_norm_w):
    dh = RET_HEAD_DIM
    lg = np.log1p(-np.exp2(-5.0 - np.arange(RET_HEADS, dtype=np.float64)))
    lg_tab = jnp.asarray(np.broadcast_to(lg[:, None, None], (RET_HEADS, 8, dh)), F32)
    spec = lambda off: pl.BlockSpec((SEQ, dh), lambda b, h: (b, off + h))
    return pl.pallas_call(
        _ret_kernel,
        grid=(BATCH, RET_HEADS),
        in_specs=[pl.BlockSpec((None, 8, dh), lambda b, h: (h, 0, 0)),
                  spec(0), spec(RET_HEADS), spec(2 * RET_HEADS), spec(3 * RET_HEADS),
                  pl.BlockSpec((1, dh), lambda b, h: (0, h))],
        out_specs=pl.BlockSpec((SEQ, dh), lambda b, h: (b, h)),
        out_shape=jax.ShapeDtypeStruct((TOK, D_RET), BF16),
        scratch_shapes=[pltpu.VMEM((dh, dh), F32)],
        compiler_params=pltpu.CompilerParams(
            dimension_semantics=("parallel", "parallel"), vmem_limit_bytes=VMEM_LIMIT),
        name="retention",
    )(lg_tab, proj, proj, proj, proj, ret_norm_w)


def _s5gen_kernel(ls_ref, arc_ref, aic_ref, arr_ref, air_ref, br_ref, bi_ref, btr_ref, bti_ref,
                  ctr_ref, cti_ref, d_ref, m_ref, wt_ref, v_ref):
    g = pl.program_id(0)
    N, P, R = S5_STATE, S5_GROUP, S5_ROW
    dt = jnp.exp(ls_ref[...])

    def zoh_coef(ar, ai):
        mag = jnp.exp(ar * dt)
        lbr = mag * jnp.cos(ai * dt)
        lbi = mag * jnp.sin(ai * dt)
        den = ar * ar + ai * ai
        cr = ((lbr - 1.0) * ar + lbi * ai) / den
        ci = (lbi * ar - (lbr - 1.0) * ai) / den
        return cr, ci

    ar = arc_ref[...]
    ai = aic_ref[...]
    lane = lax.broadcasted_iota(jnp.int32, (N, R), 1)
    d = lax.shift_right_logical(lane, 4).astype(F32)

    def lam_pow(e):
        mag = jnp.exp(ar * dt * e)
        ph = ai * dt * e
        return mag * jnp.cos(ph), mag * jnp.sin(ph)

    rep = ((lax.broadcasted_iota(jnp.int32, (P, R), 1) & (P - 1)) ==
           lax.broadcasted_iota(jnp.int32, (P, R), 0)).astype(F32)

    def tile_lanes(a):
        return jnp.dot(a, rep, precision=HI, preferred_element_type=F32)

    cr, ci = zoh_coef(ar, ai)
    b_re = br_ref[...]
    b_im = bi_ref[...]
    bb_r = tile_lanes(cr * b_re - ci * b_im)
    bb_i = tile_lanes(cr * b_im + ci * b_re)
    c_r = tile_lanes(ctr_ref[...])
    c_i = tile_lanes(cti_ref[...])

    odd = (g % 2).astype(F32)

    def pair_pad(a):
        return jnp.concatenate([a * (1.0 - odd), a * odd], axis=0)

    lm_r, lm_i = lam_pow((S5_SUB - 1.0) - d)
    wt_ref[0:2 * N, :] = pair_pad(lm_r * bb_r - lm_i * bb_i).astype(BF16)
    wt_ref[2 * N:4 * N, :] = pair_pad(lm_r * bb_i + lm_i * bb_r).astype(BF16)

    l1_r, l1_i = lam_pow(d + 1.0)
    v_ref[0:2 * N, :] = pair_pad(c_r * l1_r - c_i * l1_i).astype(BF16)
    v_ref[2 * N:4 * N, :] = pair_pad(-(c_r * l1_i + c_i * l1_r)).astype(BF16)

    l_r, l_i = lam_pow(d)
    cd_r = c_r * l_r - c_i * l_i
    cd_i = c_r * l_i + c_i * l_r
    crr, cir = zoh_coef(arr_ref[...], air_ref[...])
    bt_re = btr_ref[...]
    bt_im = bti_ref[...]
    bbt_r = crr * bt_re - cir * bt_im
    bbt_i = crr * bt_im + cir * bt_re
    krow = (jnp.dot(bbt_r, cd_r, precision=HI, preferred_element_type=F32) -
            jnp.dot(bbt_i, cd_i, precision=HI, preferred_element_type=F32))
    lane_p = lax.broadcasted_iota(jnp.int32, (P, R), 1)
    sub_p = lax.broadcasted_iota(jnp.int32, (P, R), 0)
    krow = krow + jnp.where(lane_p == sub_p, d_ref[...], 0.0)
    for tau in range(S5_SUB):
        if tau == 0:
            piece = krow
        else:
            piece = jnp.where(lane_p >= tau * P, pltpu.roll(krow, tau * P, axis=1), 0.0)
        m_ref[tau * P:(tau + 1) * P, :] = piece.astype(BF16)


def _s5gen(a_re, a_im, log_step, b_re, b_im, c_re, c_im, d_skip):
    G, N, P, R = S5_GROUPS, S5_STATE, S5_GROUP, S5_ROW
    col = lambda a: a.reshape(G, N, 1)
    rowv = lambda a: a.reshape(G, 1, N)
    sw = lambda a: jnp.swapaxes(a, 1, 2)
    blk = lambda s: pl.BlockSpec((None,) + s, lambda g: (g, 0, 0))
    return pl.pallas_call(
        _s5gen_kernel,
        grid=(G,),
        in_specs=[blk((1, 1)), blk((N, 1)), blk((N, 1)), blk((1, N)), blk((1, N)),
                  blk((N, P)), blk((N, P)), blk((P, N)), blk((P, N)),
                  blk((N, P)), blk((N, P)), blk((P, 1))],
        out_specs=[blk((R, R)), blk((4 * N, R)), blk((4 * N, R))],
        out_shape=[jax.ShapeDtypeStruct((G, R, R), BF16),
                   jax.ShapeDtypeStruct((G, 4 * N, R), BF16),
                   jax.ShapeDtypeStruct((G, 4 * N, R), BF16)],
        compiler_params=pltpu.CompilerParams(
            dimension_semantics=("parallel",), vmem_limit_bytes=VMEM_LIMIT),
        name="s5gen",
    )(log_step.reshape(G, 1, 1), col(a_re), col(a_im), rowv(a_re), rowv(a_im),
      b_re, b_im, sw(b_re), sw(b_im), sw(c_re), sw(c_im), d_skip.reshape(G, P, 1))


_S5_GB = 8
_S5_NP = _S5_GB // 2


def _s5_kernel(ar_ref, ai_ref, ls_ref, u_ref, m_ref, wt_ref, v_ref, o_ref, e_scr, p_scr):
    N2 = 2 * S5_STATE
    rows = S5_NSUB * BATCH
    nt = (S5_NSUB * BATCH) // 8

    for pr in range(_S5_NP):
        e = None
        for k in range(2):
            gi = 2 * pr + k
            t = lax.dot_general(u_ref[gi], wt_ref[gi], (((1,), (1,)), ((), ())),
                                preferred_element_type=F32)
            e = t if e is None else e + t
        e_scr[pr] = e

    step = float(S5_SUB)
    sub = lax.broadcasted_iota(jnp.int32, (8, N2), 0)
    lo = sub < BATCH
    a_r, a_i = [], []
    for pr in range(_S5_NP):
        sl = slice(pr * N2, (pr + 1) * N2)
        dt = jnp.exp(ls_ref[:, sl])
        mag = jnp.exp(ar_ref[:, sl] * dt * step)
        ph = ai_ref[:, sl] * dt * step
        a_r.append(jnp.broadcast_to(mag * jnp.cos(ph), (8, N2)))
        a_i.append(jnp.broadcast_to(mag * jnp.sin(ph), (8, N2)))

    def body(t, carry):
        r0 = pl.multiple_of(t * 8, 8)
        new = []
        for pr in range(_S5_NP):
            s_r, s_i = carry[2 * pr], carry[2 * pr + 1]
            e_r = e_scr[pr, pl.ds(r0, 8), 0:N2]
            e_i = e_scr[pr, pl.ds(r0, 8), N2:2 * N2]
            x_r = pltpu.roll(e_r, BATCH, axis=0)
            x_i = pltpu.roll(e_i, BATCH, axis=0)
            elo_r = jnp.where(lo, e_r, x_r)
            elo_i = jnp.where(lo, e_i, x_i)
            ehi_r = jnp.where(lo, x_r, e_r)
            ehi_i = jnp.where(lo, x_i, e_i)
            t_r = a_r[pr] * s_r - a_i[pr] * s_i + elo_r
            t_i = a_r[pr] * s_i + a_i[pr] * s_r + elo_i
            p_scr[pr, pl.ds(r0, 8), 0:N2] = jnp.where(lo, s_r, t_r)
            p_scr[pr, pl.ds(r0, 8), N2:2 * N2] = jnp.where(lo, s_i, t_i)
            new.append(a_r[pr] * t_r - a_i[pr] * t_i + ehi_r)
            new.append(a_r[pr] * t_i + a_i[pr] * t_r + ehi_i)
        return tuple(new)

    zero = jnp.zeros((8, N2), F32)
    lax.fori_loop(0, nt, body, tuple(zero for _ in range(2 * _S5_NP)))

    for pr in range(_S5_NP):
        p_bf = p_scr[pr].astype(BF16)
        for k in range(2):
            gi = 2 * pr + k
            y = jnp.dot(u_ref[gi], m_ref[gi], preferred_element_type=F32)
            y = y + jnp.dot(p_bf, v_ref[gi], preferred_element_type=F32)
            o_ref[gi] = y


def _s5(u_blk, m, wt, v, a_re, a_im, log_step):
    G, N, R = S5_GROUPS, S5_STATE, S5_ROW
    rows = S5_NSUB * BATCH
    gb = _S5_GB
    flat = lambda a: a.reshape(1, G * N)
    ls_rep = jnp.broadcast_to(log_step[:, None], (G, N))
    vec = pl.BlockSpec((1, gb * N), lambda i: (0, i))
    big = lambda r: pl.BlockSpec((gb, r, R), lambda i: (i, 0, 0))
    return pl.pallas_call(
        _s5_kernel,
        grid=(G // gb,),
        in_specs=[vec, vec, vec, big(rows), big(R), big(4 * N), big(4 * N)],
        out_specs=big(rows),
        out_shape=jax.ShapeDtypeStruct((G, rows, R), F32),
        scratch_shapes=[pltpu.VMEM((_S5_NP, rows, 4 * N), F32),
                        pltpu.VMEM((_S5_NP, rows, 4 * N), F32)],
        compiler_params=pltpu.CompilerParams(
            dimension_semantics=("parallel",), vmem_limit_bytes=VMEM_LIMIT),
        name="s5",
    )(flat(a_re), flat(a_im), flat(ls_rep), u_blk, m, wt, v)


_MIX_TM = 512


def _mix_kernel(ssm_ref, ret_ref, x_ref, g1_ref, wglu_ref, bglu_ref, wout_ref, o_ref):
    s = ssm_ref[...]
    cdf = 0.5 * (1.0 + jnp.tanh(math.sqrt(2.0 / math.pi) * (s + 0.044715 * (s * s * s))))
    sg = s * cdf
    z = jnp.dot(sg.astype(BF16), wglu_ref[...], preferred_element_type=F32) + bglu_ref[...]
    so = sg * jax.nn.sigmoid(z)
    mix = jnp.dot(ret_ref[...], wout_ref[0:D_RET, :], preferred_element_type=F32)
    mix = mix + jnp.dot(so.astype(BF16), wout_ref[D_RET:, :], preferred_element_type=F32)
    o_ref[...] = x_ref[...] + g1_ref[...] * mix


def _mix(ssm, ret_out, x2d, g1, w_glu_bf, b_glu, w_out_bf):
    tm = _MIX_TM
    per_b = SEQ // tm
    return pl.pallas_call(
        _mix_kernel,
        grid=(TOK // tm,),
        in_specs=[pl.BlockSpec((tm, D_S5), lambda i: (i, 0)),
                  pl.BlockSpec((tm, D_RET), lambda i: (i, 0)),
                  pl.BlockSpec((tm, D_MODEL), lambda i: (i, 0)),
                  pl.BlockSpec((None, 1, D_MODEL), lambda i: (i // per_b, 0, 0)),
                  pl.BlockSpec((D_S5, D_S5), lambda i: (0, 0)),
                  pl.BlockSpec((1, D_S5), lambda i: (0, 0)),
                  pl.BlockSpec((D_RET + D_S5, D_MODEL), lambda i: (0, 0))],
        out_specs=pl.BlockSpec((tm, D_MODEL), lambda i: (i, 0)),
        out_shape=jax.ShapeDtypeStruct((TOK, D_MODEL), F32),
        compiler_params=pltpu.CompilerParams(
            dimension_semantics=("parallel",), vmem_limit_bytes=VMEM_LIMIT),
        name="mix",
    )(ssm, ret_out, x2d, g1, w_glu_bf, b_glu, w_out_bf)


_FFN_TM = 512
_FFN_TF = 512


def _ffn_kernel(x_ref, sc_ref, sh_ref, g2_ref, nw_ref, fw_ref, wg_ref, wu_ref, wd_ref,
                o_ref, h_scr, acc_scr):
    f = pl.program_id(1)

    @pl.when(f == 0)
    def _():
        x = x_ref[...]
        ms = jnp.mean(x * x, axis=-1, keepdims=True)
        y = x * lax.rsqrt(ms + EPS) * nw_ref[...]
        h_scr[...] = (y * (1.0 + sc_ref[...]) + sh_ref[...]).astype(BF16)
        acc_scr[...] = jnp.zeros_like(acc_scr)

    h = h_scr[...]
    gate = jnp.dot(h, wg_ref[...], preferred_element_type=F32)
    up = jnp.dot(h, wu_ref[...], preferred_element_type=F32)
    act = (_silu(gate) * up).astype(BF16)
    acc_scr[...] += jnp.dot(act, wd_ref[...], preferred_element_type=F32)

    @pl.when(f == pl.num_programs(1) - 1)
    def _():
        x2 = x_ref[...] + g2_ref[...] * acc_scr[...]
        ms = jnp.mean(x2 * x2, axis=-1, keepdims=True)
        o_ref[...] = x2 * lax.rsqrt(ms + EPS) * fw_ref[...]


def _ffn(x1, sc2, sh2, g2, norm_w, final_w, w_gu_bf, w_down_bf):
    tm, tf = _FFN_TM, _FFN_TF
    per_b = SEQ // tm
    nf = D_FF // tf
    mod = pl.BlockSpec((None, 1, D_MODEL), lambda i, f: (i // per_b, 0, 0))
    vecw = pl.BlockSpec((1, D_MODEL), lambda i, f: (0, 0))
    return pl.pallas_call(
        _ffn_kernel,
        grid=(TOK // tm, nf),
        in_specs=[pl.BlockSpec((tm, D_MODEL), lambda i, f: (i, 0)),
                  mod, mod, mod, vecw, vecw,
                  pl.BlockSpec((D_MODEL, tf), lambda i, f: (0, f)),
                  pl.BlockSpec((D_MODEL, tf), lambda i, f: (0, nf + f)),
                  pl.BlockSpec((tf, D_MODEL), lambda i, f: (f, 0))],
        out_specs=pl.BlockSpec((tm, D_MODEL), lambda i, f: (i, 0)),
        out_shape=jax.ShapeDtypeStruct((TOK, D_MODEL), F32),
        scratch_shapes=[pltpu.VMEM((tm, D_MODEL), BF16),
                        pltpu.VMEM((tm, D_MODEL), F32)],
        compiler_params=pltpu.CompilerParams(
            dimension_semantics=("parallel", "arbitrary"), vmem_limit_bytes=VMEM_LIMIT),
        name="ffn",
    )(x1, sc2, sh2, g2, norm_w, final_w, w_gu_bf, w_gu_bf, w_down_bf)


def kernel(x, c, w_ada, b_ada, norm1_w, w_in, ret_norm_w, s5_a_re, s5_a_im, s5_log_step,
           s5_b_re, s5_b_im, s5_c_re, s5_c_im, s5_d, w_glu, b_glu, w_out, norm2_w,
           w_gate_up, w_down, final_norm_w):
    G, P = S5_GROUPS, S5_GROUP
    x2d = x.reshape(TOK, D_MODEL)
    layer = 0
    mod = _adaln(c, w_ada[layer], b_ada[layer])
    sh1, sc1, g1, sh2, sc2, g2 = [m.reshape(BATCH, 1, D_MODEL) for m in jnp.split(mod, 6, axis=-1)]

    proj = _inproj(x2d, sc1, sh1, norm1_w[layer].reshape(1, D_MODEL), w_in[layer].astype(BF16))
    ret_out = _retention(proj, ret_norm_w[layer].reshape(1, D_RET))

    m_mat, wt_mat, v_mat = _s5gen(s5_a_re[layer], s5_a_im[layer], s5_log_step[layer],
                                  s5_b_re[layer], s5_b_im[layer], s5_c_re[layer], s5_c_im[layer],
                                  s5_d[layer])
    u_blk = proj[:, 4 * D_RET:].reshape(BATCH, S5_NSUB, S5_SUB, G, P)
    u_blk = u_blk.transpose(3, 1, 0, 2, 4).reshape(G, S5_NSUB * BATCH, S5_ROW)
    y_blk = _s5(u_blk, m_mat, wt_mat, v_mat, s5_a_re[layer], s5_a_im[layer], s5_log_step[layer])
    ssm = y_blk.reshape(G, S5_NSUB, BATCH, S5_SUB, P).transpose(2, 1, 3, 0, 4).reshape(TOK, D_S5)

    x1 = _mix(ssm, ret_out, x2d, g1, w_glu[layer].astype(BF16), b_glu[layer].reshape(1, D_S5),
              w_out[layer].astype(BF16))
    out = _ffn(x1, sc2, sh2, g2, norm2_w[layer].reshape(1, D_MODEL),
               final_norm_w.reshape(1, D_MODEL), w_gate_up[layer].astype(BF16),
               w_down[layer].astype(BF16))
    return out.reshape(BATCH, SEQ, D_MODEL)
```

```python
import math

import numpy as np
import jax
import jax.numpy as jnp
from jax import lax
from jax.experimental import pallas as pl
from jax.experimental.pallas import tpu as pltpu

D_MODEL = 2048
BATCH = 4
SEQ = 2048
TOK = BATCH * SEQ
D_RET = 1024
D_S5 = 1024
RET_HEADS = 4
RET_HEAD_DIM = 256
RET_CHUNK = 128
S5_GROUP = 16
S5_GROUPS = 64
S5_STATE = 64
S5_SUB = 16
S5_ROW = S5_SUB * S5_GROUP
S5_NSUB = SEQ // S5_SUB
S5_COLS = S5_NSUB * BATCH
D_FF = 5632
ROPE_BASE = 10000.0
EPS = 1e-6
LANES = 128

F32 = jnp.float32
BF16 = jnp.bfloat16
HI = lax.Precision.HIGHEST
VMEM_LIMIT = 58 * 1024 * 1024


def _silu(v):
    return v * jax.nn.sigmoid(v)


def _adaln_kernel(c_ref, w_ref, b_ref, o_ref):
    cond = _silu(c_ref[...])
    o_ref[...] = jnp.dot(cond.astype(BF16), w_ref[...].astype(BF16),
                         preferred_element_type=F32) + b_ref[...]


def _adaln(c, w_ada, b_ada):
    tn = 1024
    n = w_ada.shape[1]
    return pl.pallas_call(
        _adaln_kernel,
        grid=(n // tn,),
        in_specs=[pl.BlockSpec((BATCH, D_MODEL), lambda j: (0, 0)),
                  pl.BlockSpec((D_MODEL, tn), lambda j: (0, j)),
                  pl.BlockSpec((1, tn), lambda j: (0, j))],
        out_specs=pl.BlockSpec((BATCH, tn), lambda j: (0, j)),
        out_shape=jax.ShapeDtypeStruct((BATCH, n), F32),
        compiler_params=pltpu.CompilerParams(
            dimension_semantics=("parallel",), vmem_limit_bytes=VMEM_LIMIT),
        name="adaln",
    )(c, w_ada, b_ada.reshape(1, n))


_INPROJ_TM = 1024
_INPROJ_TN = 1024
_INPROJ_NS = _INPROJ_TM // S5_SUB


def _inproj_kernel(x_ref, sc_ref, sh_ref, nw_ref, w_ref, o_ref, u_ref, h_scr, cos_scr, sin_scr):
    tm = _INPROJ_TM
    half = RET_HEAD_DIM // 2
    per_b = SEQ // tm
    i = pl.program_id(0)
    j = pl.program_id(1)
    slot = i % per_b

    @pl.when(j == 0)
    def _():
        x = x_ref[...]
        ms = jnp.mean(x * x, axis=-1, keepdims=True)
        y = x * lax.rsqrt(ms + EPS) * nw_ref[...]
        h_scr[...] = (y * (1.0 + sc_ref[...]) + sh_ref[...]).astype(BF16)

    @pl.when((j == 0) & (i < per_b))
    def _():
        pos = (lax.broadcasted_iota(jnp.int32, (tm, half), 0) + slot * tm).astype(F32)
        lane = lax.broadcasted_iota(jnp.int32, (tm, half), 1).astype(F32)
        ang = pos * jnp.exp(lane * (-math.log(ROPE_BASE) / half))
        cos_scr[slot] = jnp.cos(ang)
        sin_scr[slot] = jnp.sin(ang)

    acc = jnp.dot(h_scr[...], w_ref[...], preferred_element_type=F32)

    @pl.when(j < 2)
    def _():
        cs = cos_scr[slot]
        sn = sin_scr[slot]
        for hh in range(RET_HEADS):
            c0 = hh * RET_HEAD_DIM
            x1 = acc[:, c0:c0 + half]
            x2 = acc[:, c0 + half:c0 + 2 * half]
            o_ref[:, c0:c0 + half] = (x1 * cs - x2 * sn).astype(BF16)
            o_ref[:, c0 + half:c0 + 2 * half] = (x2 * cs + x1 * sn).astype(BF16)

    @pl.when((j >= 2) & (j < 4))
    def _():
        o_ref[...] = acc.astype(BF16)

    @pl.when(j == 4)
    def _():
        u_ref[...] = acc.reshape(_INPROJ_NS, S5_SUB, _INPROJ_TN)


def _inproj(x2d, sc1, sh1, norm_w, w_in_bf):
    tm, tn = _INPROJ_TM, _INPROJ_TN
    per_b = SEQ // tm
    half = RET_HEAD_DIM // 2
    return pl.pallas_call(
        _inproj_kernel,
        grid=(TOK // tm, 5),
        in_specs=[pl.BlockSpec((tm, D_MODEL), lambda i, j: (i, 0)),
                  pl.BlockSpec((None, 1, D_MODEL), lambda i, j: (i // per_b, 0, 0)),
                  pl.BlockSpec((None, 1, D_MODEL), lambda i, j: (i // per_b, 0, 0)),
                  pl.BlockSpec((1, D_MODEL), lambda i, j: (0, 0)),
                  pl.BlockSpec((D_MODEL, tn), lambda i, j: (0, j))],
        out_specs=[pl.BlockSpec((tm, tn), lambda i, j: (i, jnp.minimum(j, 3))),
                   pl.BlockSpec((_INPROJ_NS, None, S5_SUB, D_S5),
                                lambda i, j: (i % per_b, i // per_b, 0, 0))],
        out_shape=[jax.ShapeDtypeStruct((TOK, 4 * D_RET), BF16),
                   jax.ShapeDtypeStruct((S5_NSUB, BATCH, S5_SUB, D_S5), F32)],
        scratch_shapes=[pltpu.VMEM((tm, D_MODEL), BF16),
                        pltpu.VMEM((per_b, tm, half), F32),
                        pltpu.VMEM((per_b, tm, half), F32)],
        compiler_params=pltpu.CompilerParams(
            dimension_semantics=("arbitrary", "arbitrary"), vmem_limit_bytes=VMEM_LIMIT),
        name="inproj",
    )(x2d, sc1, sh1, norm_w, w_in_bf)


def _ret_kernel(lg_ref, q_ref, k_ref, v_ref, g_ref, w_ref, o_ref, kv_scr, prev_scr):
    C = RET_CHUNK
    dh = RET_HEAD_DIM
    nc = SEQ // C
    scale = dh ** -0.5
    lg_c = lg_ref[0:1, 0:C]
    lg_d = lg_ref[0:1, :]
    ii = lax.broadcasted_iota(jnp.int32, (C, C), 0)
    jj = lax.broadcasted_iota(jnp.int32, (C, C), 1)
    diff = (ii - jj).astype(F32)
    intra = jnp.where(diff >= 0.0, jnp.exp(lg_c * jnp.maximum(diff, 0.0)), 0.0) * scale
    row = lax.broadcasted_iota(jnp.int32, (C, dh), 0).astype(F32)
    k_decay = jnp.exp(lg_d * (C - 1.0 - row)) * scale
    q_decay = jnp.exp(lg_d * (row + 1.0))
    chunk_decay = jnp.exp(lg_d * float(C))
    gn_w = w_ref[...]

    def kv_body(n, carry):
        r0 = pl.multiple_of(n * C, C)
        kd = (k_ref[pl.ds(r0, C), :].astype(F32) * k_decay).astype(BF16)
        kv_scr[n] = lax.dot_general(kd, v_ref[pl.ds(r0, C), :], (((0,), (0,)), ((), ())),
                                    preferred_element_type=F32)
        return carry

    lax.fori_loop(0, nc, kv_body, 0, unroll=2)

    band = 64
    for rb in range(dh // band):
        rs = slice(rb * band, (rb + 1) * band)

        def st_body(n, st):
            prev_scr[n, rs, :] = st.astype(BF16)
            return st * chunk_decay + kv_scr[n, rs, :]

        lax.fori_loop(0, nc, st_body, jnp.zeros((band, dh), F32))

    def out_body(n, carry):
        r0 = pl.multiple_of(n * C, C)
        q = q_ref[pl.ds(r0, C), :]
        k = k_ref[pl.ds(r0, C), :]
        v = v_ref[pl.ds(r0, C), :]
        s = lax.dot_general(q, k, (((1,), (1,)), ((), ())), preferred_element_type=F32) * intra
        y = jnp.dot(s.astype(BF16), v, preferred_element_type=F32)
        y = y + jnp.dot(q, prev_scr[n], preferred_element_type=F32) * q_decay
        mu = jnp.mean(y, axis=-1, keepdims=True)
        yc = y - mu
        var = jnp.mean(yc * yc, axis=-1, keepdims=True)
        yn = yc * lax.rsqrt(var + EPS) * gn_w
        g = g_ref[pl.ds(r0, C), :].astype(F32)
        o_ref[pl.ds(r0, C), :] = (_silu(g) * yn).astype(BF16)
        return carry

    lax.fori_loop(0, nc, out_body, 0, unroll=2)


def _retention(qkvg, ret_norm_w):
    dh = RET_HEAD_DIM
    nc = SEQ // RET_CHUNK
    lg = np.log1p(-np.exp2(-5.0 - np.arange(RET_HEADS, dtype=np.float64)))
    lg_tab = jnp.asarray(np.broadcast_to(lg[:, None, None], (RET_HEADS, 8, dh)), F32)
    spec = lambda off: pl.BlockSpec((SEQ, dh), lambda b, h: (b, off + h))
    return pl.pallas_call(
        _ret_kernel,
        grid=(BATCH, RET_HEADS),
        in_specs=[pl.BlockSpec((None, 8, dh), lambda b, h: (h, 0, 0)),
                  spec(0), spec(RET_HEADS), spec(2 * RET_HEADS), spec(3 * RET_HEADS),
                  pl.BlockSpec((1, dh), lambda b, h: (0, h))],
        out_specs=pl.BlockSpec((SEQ, dh), lambda b, h: (b, h)),
        out_shape=jax.ShapeDtypeStruct((TOK, D_RET), BF16),
        scratch_shapes=[pltpu.VMEM((nc, dh, dh), F32),
                        pltpu.VMEM((nc, dh, dh), BF16)],
        compiler_params=pltpu.CompilerParams(
            dimension_semantics=("parallel", "parallel"), vmem_limit_bytes=VMEM_LIMIT),
        name="retention",
    )(lg_tab, qkvg, qkvg, qkvg, qkvg, ret_norm_w)


_S5GEN_GG = 4


def _s5gen_kernel(ls_ref, arc_ref, aic_ref, arr_ref, air_ref, br_ref, bi_ref, cr_ref, ci_ref,
                  d_ref, mt_ref, wt_ref, vt_ref):
    N, P, R = S5_STATE, S5_GROUP, S5_ROW
    lane_n = lax.broadcasted_iota(jnp.int32, (N, R), 1)
    pow_w = (S5_SUB - 1.0) - lax.shift_right_logical(lane_n, 4).astype(F32)
    lane_p = lax.broadcasted_iota(jnp.int32, (P, R), 1)
    sub_p = lax.broadcasted_iota(jnp.int32, (P, R), 0)
    rep_p = ((lane_p & (P - 1)) == sub_p).astype(F32)
    rep_n = ((lane_n & (N - 1)) ==
             lax.broadcasted_iota(jnp.int32, (N, R), 0)).astype(F32)
    pow_v = sub_p.astype(F32) + 1.0
    lag0_diag = lane_p == (R - P) + sub_p
    is_re = lane_p < 2 * N
    lane_slot = lax.shift_right_logical(lane_p, 6) & 1

    def hdot(a, b):
        return jnp.dot(a, b, precision=HI, preferred_element_type=F32)

    for k in range(_S5GEN_GG):
        odd = k % 2
        dt = jnp.exp(ls_ref[k])
        ar = arc_ref[k]
        ai = aic_ref[k]
        mag = jnp.exp(ar * dt)
        lbr = mag * jnp.cos(ai * dt)
        lbi = mag * jnp.sin(ai * dt)
        den = ar * ar + ai * ai
        zr = ((lbr - 1.0) * ar + lbi * ai) / den
        zi = (lbi * ar - (lbr - 1.0) * ai) / den
        b_re = br_ref[k]
        b_im = bi_ref[k]
        bb_r = hdot(zr * b_re - zi * b_im, rep_p)
        bb_i = hdot(zr * b_im + zi * b_re, rep_p)
        mag_w = jnp.exp(ar * dt * pow_w)
        ph_w = ai * dt * pow_w
        lm_r = mag_w * jnp.cos(ph_w)
        lm_i = mag_w * jnp.sin(ph_w)
        w_r = lm_r * bb_r - lm_i * bb_i
        w_i = lm_r * bb_i + lm_i * bb_r
        zero = jnp.zeros_like(w_r)
        wt_ref[k, 0:2 * N, :] = jnp.concatenate(
            [zero, w_r] if odd else [w_r, zero], axis=0).astype(BF16)
        wt_ref[k, 2 * N:4 * N, :] = jnp.concatenate(
            [zero, w_i] if odd else [w_i, zero], axis=0).astype(BF16)

        c_re = cr_ref[k]
        c_im = ci_ref[k]
        krev = hdot(c_re, w_r) - hdot(c_im, w_i)
        krev = krev + jnp.where(lag0_diag, d_ref[k], 0.0)
        for tp in range(S5_SUB):
            width = (tp + 1) * P
            piece = krev if width == R else jnp.where(
                lane_p < width, pltpu.roll(krev, width, axis=1), 0.0)
            mt_ref[k, tp * P:(tp + 1) * P, :] = piece.astype(BF16)

        c4_r = hdot(c_re, rep_n)
        c4_i = hdot(c_im, rep_n)
        arow = arr_ref[k]
        airow = air_ref[k]
        mag_v = jnp.exp(arow * dt * pow_v)
        ph_v = airow * dt * pow_v
        l_r = mag_v * jnp.cos(ph_v)
        l_i = mag_v * jnp.sin(ph_v)
        keep = lane_slot == odd
        for tp in range(S5_SUB):
            lr = l_r[tp:tp + 1, :]
            li = l_i[tp:tp + 1, :]
            val = jnp.where(is_re, c4_r * lr - c4_i * li, -(c4_r * li + c4_i * lr))
            vt_ref[k, tp * P:(tp + 1) * P, :] = jnp.where(keep, val, 0.0).astype(BF16)


def _s5gen(a_re, a_im, log_step, b_re, b_im, c_re, c_im, d_skip):
    G, N, P, R = S5_GROUPS, S5_STATE, S5_GROUP, S5_ROW
    gg = _S5GEN_GG
    col = lambda a: a.reshape(G, N, 1)
    row4 = lambda a: jnp.tile(a, (1, R // N)).reshape(G, 1, R)
    blk = lambda s: pl.BlockSpec((gg,) + s, lambda g: (g, 0, 0))
    return pl.pallas_call(
        _s5gen_kernel,
        grid=(G // gg,),
        in_specs=[blk((1, 1)), blk((N, 1)), blk((N, 1)), blk((1, R)), blk((1, R)),
                  blk((N, P)), blk((N, P)), blk((P, N)), blk((P, N)), blk((P, 1))],
        out_specs=[blk((R, R)), blk((R, R)), blk((R, R))],
        out_shape=[jax.ShapeDtypeStruct((G, R, R), BF16)] * 3,
        compiler_params=pltpu.CompilerParams(
            dimension_semantics=("parallel",), vmem_limit_bytes=VMEM_LIMIT),
        name="s5gen",
    )(log_step.reshape(G, 1, 1), col(a_re), col(a_im), row4(a_re), row4(a_im),
      b_re, b_im, c_re, c_im, d_skip.reshape(G, P, 1))


_S5_GB = LANES // S5_GROUP
_S5_NP = _S5_GB // 2


def _s5_kernel(ar_ref, ai_ref, ls_ref, u_ref, mt_ref, wt_ref, vt_ref, o_ref,
               ut_scr, yt_scr, e_scr, p_scr):
    P = S5_GROUP
    N2 = 2 * S5_STATE
    cols = S5_COLS
    nblk = cols // LANES
    nt = cols // 8

    def load_body(tau, carry):
        for c in range(nblk):
            x = u_ref[pl.ds(tau + c * LANES * S5_SUB, LANES, stride=S5_SUB), :]
            xt = x.T
            r0 = pl.multiple_of(tau * P, P)
            for k in range(_S5_GB):
                ut_scr[k, pl.ds(r0, P), c * LANES:(c + 1) * LANES] = \
                    xt[k * P:(k + 1) * P, :].astype(BF16)
        return carry

    lax.fori_loop(0, S5_SUB, load_body, 0)

    for pr in range(_S5_NP):
        et = (jnp.dot(wt_ref[2 * pr], ut_scr[2 * pr], preferred_element_type=F32) +
              jnp.dot(wt_ref[2 * pr + 1], ut_scr[2 * pr + 1], preferred_element_type=F32))
        e_scr[pr] = et.T

    step = float(S5_SUB)
    sub = lax.broadcasted_iota(jnp.int32, (8, N2), 0)
    lo = sub < BATCH
    a_r, a_i = [], []
    for pr in range(_S5_NP):
        sl = slice(pr * N2, (pr + 1) * N2)
        dt = jnp.exp(ls_ref[:, sl])
        mag = jnp.exp(ar_ref[:, sl] * dt * step)
        ph = ai_ref[:, sl] * dt * step
        a_r.append(jnp.broadcast_to(mag * jnp.cos(ph), (8, N2)))
        a_i.append(jnp.broadcast_to(mag * jnp.sin(ph), (8, N2)))

    def scan_body(t, carry):
        r0 = pl.multiple_of(t * 8, 8)
        new = []
        for pr in range(_S5_NP):
            s_r, s_i = carry[2 * pr], carry[2 * pr + 1]
            e_r = e_scr[pr, pl.ds(r0, 8), 0:N2]
            e_i = e_scr[pr, pl.ds(r0, 8), N2:2 * N2]
            x_r = pltpu.roll(e_r, BATCH, axis=0)
            x_i = pltpu.roll(e_i, BATCH, axis=0)
            elo_r = jnp.where(lo, e_r, x_r)
            elo_i = jnp.where(lo, e_i, x_i)
            ehi_r = jnp.where(lo, x_r, e_r)
            ehi_i = jnp.where(lo, x_i, e_i)
            t_r = a_r[pr] * s_r - a_i[pr] * s_i + elo_r
            t_i = a_r[pr] * s_i + a_i[pr] * s_r + elo_i
            p_scr[pr, pl.ds(r0, 8), 0:N2] = jnp.where(lo, s_r, t_r)
            p_scr[pr, pl.ds(r0, 8), N2:2 * N2] = jnp.where(lo, s_i, t_i)
            new.append(a_r[pr] * t_r - a_i[pr] * t_i + ehi_r)
            new.append(a_r[pr] * t_i + a_i[pr] * t_r + ehi_i)
        return tuple(new)

    zero = jnp.zeros((8, N2), F32)
    lax.fori_loop(0, nt, scan_body, tuple(zero for _ in range(2 * _S5_NP)))

    for pr in range(_S5_NP):
        pt = p_scr[pr].T.astype(BF16)
        for k in range(2):
            gi = 2 * pr + k
            yt = (jnp.dot(mt_ref[gi], ut_scr[gi], preferred_element_type=F32) +
                  jnp.dot(vt_ref[gi], pt, preferred_element_type=F32))
            for tau in range(S5_SUB):
                yt_scr[tau, gi * P:(gi + 1) * P, :] = yt[tau * P:(tau + 1) * P, :]

    def store_body(tau, carry):
        o_ref[pl.ds(tau, cols, stride=S5_SUB), :] = yt_scr[tau].T
        return carry

    lax.fori_loop(0, S5_SUB, store_body, 0)


def _s5(u_rows, mt, wt, vt, a_re, a_im, log_step):
    G, N, R = S5_GROUPS, S5_STATE, S5_ROW
    gb = _S5_GB
    flat = lambda a: a.reshape(1, G * N)
    ls_rep = jnp.broadcast_to(log_step[:, None], (G, N))
    vec = pl.BlockSpec((1, gb * N), lambda i: (0, i))
    mat = pl.BlockSpec((gb, R, R), lambda i: (i, 0, 0))
    slab = pl.BlockSpec((TOK, LANES), lambda i: (0, i))
    return pl.pallas_call(
        _s5_kernel,
        grid=(G // gb,),
        in_specs=[vec, vec, vec, slab, mat, mat, mat],
        out_specs=slab,
        out_shape=jax.ShapeDtypeStruct((TOK, D_S5), F32),
        scratch_shapes=[pltpu.VMEM((gb, R, S5_COLS), BF16),
                        pltpu.VMEM((S5_SUB, LANES, S5_COLS), F32),
                        pltpu.VMEM((_S5_NP, S5_COLS, 4 * N), F32),
                        pltpu.VMEM((_S5_NP, S5_COLS, 4 * N), F32)],
        compiler_params=pltpu.CompilerParams(
            dimension_semantics=("parallel",), vmem_limit_bytes=VMEM_LIMIT),
        name="s5",
    )(flat(a_re), flat(a_im), flat(ls_rep), u_rows, mt, wt, vt)


_MIX_TM = 512
_MIX_NS = _MIX_TM // S5_SUB


def _mix_kernel(ssm_ref, ret_ref, x_ref, g1_ref, wglu_ref, bglu_ref, wout_ref, o_ref):
    s = ssm_ref[...].reshape(_MIX_TM, D_S5)
    cdf = 0.5 * (1.0 + jnp.tanh(math.sqrt(2.0 / math.pi) * (s + 0.044715 * (s * s * s))))
    sg = s * cdf
    z = jnp.dot(sg.astype(BF16), wglu_ref[...], preferred_element_type=F32) + bglu_ref[...]
    so = sg * jax.nn.sigmoid(z)
    mix = jnp.dot(ret_ref[...], wout_ref[0:D_RET, :], preferred_element_type=F32)
    mix = mix + jnp.dot(so.astype(BF16), wout_ref[D_RET:, :], preferred_element_type=F32)
    o_ref[...] = x_ref[...] + g1_ref[...] * mix


def _mix(ssm_sb, ret_out, x2d, g1, w_glu_bf, b_glu, w_out_bf):
    tm = _MIX_TM
    per_b = SEQ // tm
    return pl.pallas_call(
        _mix_kernel,
        grid=(TOK // tm,),
        in_specs=[pl.BlockSpec((_MIX_NS, None, S5_SUB, D_S5),
                               lambda i: (i % per_b, i // per_b, 0, 0)),
                  pl.BlockSpec((tm, D_RET), lambda i: (i, 0)),
                  pl.BlockSpec((tm, D_MODEL), lambda i: (i, 0)),
                  pl.BlockSpec((None, 1, D_MODEL), lambda i: (i // per_b, 0, 0)),
                  pl.BlockSpec((D_S5, D_S5), lambda i: (0, 0)),
                  pl.BlockSpec((1, D_S5), lambda i: (0, 0)),
                  pl.BlockSpec((D_RET + D_S5, D_MODEL), lambda i: (0, 0))],
        out_specs=pl.BlockSpec((tm, D_MODEL), lambda i: (i, 0)),
        out_shape=jax.ShapeDtypeStruct((TOK, D_MODEL), F32),
        compiler_params=pltpu.CompilerParams(
            dimension_semantics=("parallel",), vmem_limit_bytes=VMEM_LIMIT),
        name="mix",
    )(ssm_sb, ret_out, x2d, g1, w_glu_bf, b_glu, w_out_bf)


_FFN_TM = 512
_FFN_TF = 512


def _ffn_kernel(x_ref, sc_ref, sh_ref, g2_ref, nw_ref, fw_ref, wg_ref, wu_ref, wd_ref,
                o_ref, h_scr):
    f = pl.program_id(1)

    @pl.when(f == 0)
    def _():
        x = x_ref[...]
        ms = jnp.mean(x * x, axis=-1, keepdims=True)
        y = x * lax.rsqrt(ms + EPS) * nw_ref[...]
        h_scr[...] = (y * (1.0 + sc_ref[...]) + sh_ref[...]).astype(BF16)

    h = h_scr[...]
    gate = jnp.dot(h, wg_ref[...], preferred_element_type=F32)
    up = jnp.dot(h, wu_ref[...], preferred_element_type=F32)
    act = (_silu(gate) * up).astype(BF16)
    down = jnp.dot(act, wd_ref[...], preferred_element_type=F32)

    @pl.when(f == 0)
    def _():
        o_ref[...] = down

    @pl.when(f > 0)
    def _():
        o_ref[...] += down

    @pl.when(f == pl.num_programs(1) - 1)
    def _():
        x2 = x_ref[...] + g2_ref[...] * o_ref[...]
        ms = jnp.mean(x2 * x2, axis=-1, keepdims=True)
        o_ref[...] = x2 * lax.rsqrt(ms + EPS) * fw_ref[...]


def _ffn(x1, sc2, sh2, g2, norm_w, final_w, w_gu_bf, w_down_bf):
    tm, tf = _FFN_TM, _FFN_TF
    per_b = SEQ // tm
    nf = D_FF // tf
    mod = pl.BlockSpec((None, 1, D_MODEL), lambda i, f: (i // per_b, 0, 0))
    vecw = pl.BlockSpec((1, D_MODEL), lambda i, f: (0, 0))
    return pl.pallas_call(
        _ffn_kernel,
        grid=(TOK // tm, nf),
        in_specs=[pl.BlockSpec((tm, D_MODEL), lambda i, f: (i, 0)),
                  mod, mod, mod, vecw, vecw,
                  pl.BlockSpec((D_MODEL, tf), lambda i, f: (0, f)),
                  pl.BlockSpec((D_MODEL, tf), lambda i, f: (0, nf + f)),
                  pl.BlockSpec((tf, D_MODEL), lambda i, f: (f, 0))],
        out_specs=pl.BlockSpec((tm, D_MODEL), lambda i, f: (i, 0)),
        out_shape=jax.ShapeDtypeStruct((TOK, D_MODEL), F32),
        scratch_shapes=[pltpu.VMEM((tm, D_MODEL), BF16)],
        compiler_params=pltpu.CompilerParams(
            dimension_semantics=("parallel", "arbitrary"), vmem_limit_bytes=VMEM_LIMIT),
        name="ffn",
    )(x1, sc2, sh2, g2, norm_w, final_w, w_gu_bf, w_gu_bf, w_down_bf)


def kernel(x, c, w_ada, b_ada, norm1_w, w_in, ret_norm_w, s5_a_re, s5_a_im, s5_log_step,
           s5_b_re, s5_b_im, s5_c_re, s5_c_im, s5_d, w_glu, b_glu, w_out, norm2_w,
           w_gate_up, w_down, final_norm_w):
    x2d = x.reshape(TOK, D_MODEL)
    layer = 0
    mod = _adaln(c, w_ada[layer], b_ada[layer])
    sh1, sc1, g1, sh2, sc2, g2 = [m.reshape(BATCH, 1, D_MODEL) for m in jnp.split(mod, 6, axis=-1)]

    qkvg, u_sb = _inproj(x2d, sc1, sh1, norm1_w[layer].reshape(1, D_MODEL),
                         w_in[layer].astype(BF16))
    ret_out = _retention(qkvg, ret_norm_w[layer].reshape(1, D_RET))

    mt, wt, vt = _s5gen(s5_a_re[layer], s5_a_im[layer], s5_log_step[layer],
                        s5_b_re[layer], s5_b_im[layer], s5_c_re[layer], s5_c_im[layer],
                        s5_d[layer])
    y_rows = _s5(u_sb.reshape(TOK, D_S5), mt, wt, vt,
                 s5_a_re[layer], s5_a_im[layer], s5_log_step[layer])
    ssm_sb = y_rows.reshape(S5_NSUB, BATCH, S5_SUB, D_S5)

    x1 = _mix(ssm_sb, ret_out, x2d, g1, w_glu[layer].astype(BF16), b_glu[layer].reshape(1, D_S5),
              w_out[layer].astype(BF16))
    out = _ffn(x1, sc2, sh2, g2, norm2_w[layer].reshape(1, D_MODEL),
               final_norm_w.reshape(1, D_MODEL), w_gate_up[layer].astype(BF16),
               w_down[layer].astype(BF16))
    return out.reshape(BATCH, SEQ, D_MODEL)
```

```python
import math

import numpy as np
import jax
import jax.numpy as jnp
from jax import lax
from jax.experimental import pallas as pl
from jax.experimental.pallas import tpu as pltpu

D_MODEL = 2048
BATCH = 4
SEQ = 2048
TOK = BATCH * SEQ
D_RET = 1024
D_S5 = 1024
RET_HEADS = 4
RET_HEAD_DIM = 256
RET_CHUNK = 128
S5_GROUP = 16
S5_GROUPS = 64
S5_STATE = 64
S5_SUB = 16
S5_ROW = S5_SUB * S5_GROUP
S5_NSUB = SEQ // S5_SUB
S5_COLS = S5_NSUB * BATCH
D_FF = 5632
ROPE_BASE = 10000.0
EPS = 1e-6
LANES = 128

F32 = jnp.float32
BF16 = jnp.bfloat16
HI = lax.Precision.HIGHEST
VMEM_LIMIT = 58 * 1024 * 1024


def _silu(v):
    return v * jax.nn.sigmoid(v)


def _adaln_kernel(c_ref, w_ref, b_ref, o_ref):
    cond = _silu(c_ref[...])
    o_ref[...] = jnp.dot(cond.astype(BF16), w_ref[...].astype(BF16),
                         preferred_element_type=F32) + b_ref[...]


def _adaln(c, w_ada, b_ada):
    tn = 1024
    n = w_ada.shape[1]
    return pl.pallas_call(
        _adaln_kernel,
        grid=(n // tn,),
        in_specs=[pl.BlockSpec((BATCH, D_MODEL), lambda j: (0, 0)),
                  pl.BlockSpec((D_MODEL, tn), lambda j: (0, j)),
                  pl.BlockSpec((1, tn), lambda j: (0, j))],
        out_specs=pl.BlockSpec((BATCH, tn), lambda j: (0, j)),
        out_shape=jax.ShapeDtypeStruct((BATCH, n), F32),
        compiler_params=pltpu.CompilerParams(
            dimension_semantics=("parallel",), vmem_limit_bytes=VMEM_LIMIT),
        name="adaln",
    )(c, w_ada, b_ada.reshape(1, n))


_INPROJ_TM = 1024
_INPROJ_TN = 1024
_INPROJ_NS = _INPROJ_TM // S5_SUB


def _inproj_kernel(x_ref, sc_ref, sh_ref, nw_ref, w_ref, o_ref, u_ref, h_scr, cos_scr, sin_scr):
    tm = _INPROJ_TM
    half = RET_HEAD_DIM // 2
    per_b = SEQ // tm
    i = pl.program_id(0)
    j = pl.program_id(1)
    slot = i % per_b

    @pl.when(j == 0)
    def _():
        gain = nw_ref[...] * (1.0 + sc_ref[...])
        shift = sh_ref[...]
        rb = 16

        def norm_body(r, carry):
            r0 = pl.multiple_of(r * rb, rb)
            x = x_ref[pl.ds(r0, rb), :]
            ms = jnp.mean(x * x, axis=-1, keepdims=True)
            h_scr[pl.ds(r0, rb), :] = (x * lax.rsqrt(ms + EPS) * gain + shift).astype(BF16)
            return carry

        lax.fori_loop(0, tm // rb, norm_body, 0, unroll=4)

    @pl.when((j == 0) & (i < per_b))
    def _():
        pos = (lax.broadcasted_iota(jnp.int32, (tm, half), 0) + slot * tm).astype(F32)
        lane = lax.broadcasted_iota(jnp.int32, (tm, half), 1).astype(F32)
        ang = pos * jnp.exp(lane * (-math.log(ROPE_BASE) / half))
        cos_scr[slot] = jnp.cos(ang)
        sin_scr[slot] = jnp.sin(ang)

    acc = jnp.dot(h_scr[...], w_ref[...], preferred_element_type=F32)

    rot = (j < 2).astype(F32)
    cs = cos_scr[slot] * rot + (1.0 - rot)
    sn = sin_scr[slot] * rot
    for hh in range(RET_HEADS):
        c0 = hh * RET_HEAD_DIM
        x1 = acc[:, c0:c0 + half]
        x2 = acc[:, c0 + half:c0 + 2 * half]
        o_ref[:, c0:c0 + half] = (x1 * cs - x2 * sn).astype(BF16)
        o_ref[:, c0 + half:c0 + 2 * half] = (x2 * cs + x1 * sn).astype(BF16)
    u_ref[...] = acc.reshape(_INPROJ_NS, S5_SUB, _INPROJ_TN)


def _inproj(x2d, sc1, sh1, norm_w, w_in_bf):
    tm, tn = _INPROJ_TM, _INPROJ_TN
    per_b = SEQ // tm
    half = RET_HEAD_DIM // 2
    return pl.pallas_call(
        _inproj_kernel,
        grid=(TOK // tm, 5),
        in_specs=[pl.BlockSpec((tm, D_MODEL), lambda i, j: (i, 0)),
                  pl.BlockSpec((None, 1, D_MODEL), lambda i, j: (i // per_b, 0, 0)),
                  pl.BlockSpec((None, 1, D_MODEL), lambda i, j: (i // per_b, 0, 0)),
                  pl.BlockSpec((1, D_MODEL), lambda i, j: (0, 0)),
                  pl.BlockSpec((D_MODEL, tn), lambda i, j: (0, j))],
        out_specs=[pl.BlockSpec((tm, tn), lambda i, j: (i, j)),
                   pl.BlockSpec((_INPROJ_NS, None, S5_SUB, D_S5),
                                lambda i, j: (i % per_b, i // per_b, 0, 0))],
        out_shape=[jax.ShapeDtypeStruct((TOK, 4 * D_RET + D_S5), BF16),
                   jax.ShapeDtypeStruct((S5_NSUB, BATCH, S5_SUB, D_S5), F32)],
        scratch_shapes=[pltpu.VMEM((tm, D_MODEL), BF16),
                        pltpu.VMEM((per_b, tm, half), F32),
                        pltpu.VMEM((per_b, tm, half), F32)],
        compiler_params=pltpu.CompilerParams(
            dimension_semantics=("arbitrary", "arbitrary"), vmem_limit_bytes=VMEM_LIMIT),
        name="inproj",
    )(x2d, sc1, sh1, norm_w, w_in_bf)


_RET_BLK = 256


def _ret_kernel(lg_ref, q_ref, k_ref, v_ref, g_ref, w_ref, o_ref,
                kv_scr, prev_scr, intra_scr, kdec_scr, qdec_scr):
    C = _RET_BLK
    dh = RET_HEAD_DIM
    nc = SEQ // C
    scale = dh ** -0.5
    lg = lg_ref[0:1, :]
    ii = lax.broadcasted_iota(jnp.int32, (C, C), 0)
    jj = lax.broadcasted_iota(jnp.int32, (C, C), 1)
    diff = (ii - jj).astype(F32)
    intra_scr[...] = jnp.where(diff >= 0.0, jnp.exp(lg * jnp.maximum(diff, 0.0)), 0.0) * scale
    row = lax.broadcasted_iota(jnp.int32, (C, dh), 0).astype(F32)
    kdec_scr[...] = jnp.exp(lg * (C - 1.0 - row)) * scale
    qdec_scr[...] = jnp.exp(lg * (row + 1.0))
    block_decay = jnp.exp(lg * float(C))
    gn_w = w_ref[...]

    for n in range(nc):
        rs = slice(n * C, (n + 1) * C)
        kd = (k_ref[rs, :].astype(F32) * kdec_scr[...]).astype(BF16)
        kv_scr[n] = lax.dot_general(kd, v_ref[rs, :], (((0,), (0,)), ((), ())),
                                    preferred_element_type=F32)

    band = 64
    for rb in range(dh // band):
        bs = slice(rb * band, (rb + 1) * band)
        st = jnp.zeros((band, dh), F32)
        for n in range(nc):
            prev_scr[n, bs, :] = st.astype(BF16)
            if n + 1 < nc:
                st = st * block_decay + kv_scr[n, bs, :]

    for n in range(nc):
        rs = slice(n * C, (n + 1) * C)
        q = q_ref[rs, :]
        s = lax.dot_general(q, k_ref[rs, :], (((1,), (1,)), ((), ())),
                            preferred_element_type=F32) * intra_scr[...]
        y = jnp.dot(s.astype(BF16), v_ref[rs, :], preferred_element_type=F32)
        y = y + jnp.dot(q, prev_scr[n], preferred_element_type=F32) * qdec_scr[...]
        mu = jnp.mean(y, axis=-1, keepdims=True)
        yc = y - mu
        var = jnp.mean(yc * yc, axis=-1, keepdims=True)
        yn = yc * lax.rsqrt(var + EPS) * gn_w
        g = g_ref[rs, :].astype(F32)
        o_ref[rs, :] = (_silu(g) * yn).astype(BF16)


def _retention(qkvg, ret_norm_w):
    dh = RET_HEAD_DIM
    nc = SEQ // _RET_BLK
    lg = np.log1p(-np.exp2(-5.0 - np.arange(RET_HEADS, dtype=np.float64)))
    lg_tab = jnp.asarray(np.broadcast_to(lg[:, None, None], (RET_HEADS, 8, dh)), F32)
    spec = lambda off: pl.BlockSpec((SEQ, dh), lambda b, h: (b, off + h))
    return pl.pallas_call(
        _ret_kernel,
        grid=(BATCH, RET_HEADS),
        in_specs=[pl.BlockSpec((None, 8, dh), lambda b, h: (h, 0, 0)),
                  spec(0), spec(RET_HEADS), spec(2 * RET_HEADS), spec(3 * RET_HEADS),
                  pl.BlockSpec((1, dh), lambda b, h: (0, h))],
        out_specs=pl.BlockSpec((SEQ, dh), lambda b, h: (b, h)),
        out_shape=jax.ShapeDtypeStruct((TOK, D_RET), BF16),
        scratch_shapes=[pltpu.VMEM((nc, dh, dh), F32),
                        pltpu.VMEM((nc, dh, dh), BF16),
                        pltpu.VMEM((_RET_BLK, _RET_BLK), F32),
                        pltpu.VMEM((_RET_BLK, dh), F32),
                        pltpu.VMEM((_RET_BLK, dh), F32)],
        compiler_params=pltpu.CompilerParams(
            dimension_semantics=("parallel", "parallel"), vmem_limit_bytes=VMEM_LIMIT),
        name="retention",
    )(lg_tab, qkvg, qkvg, qkvg, qkvg, ret_norm_w)


_S5GEN_GG = 4


def _s5gen_kernel(ls_ref, arc_ref, aic_ref, arr_ref, air_ref, br_ref, bi_ref, cr_ref, ci_ref,
                  d_ref, mt_ref, wt_ref, vt_ref):
    N, P, R = S5_STATE, S5_GROUP, S5_ROW
    lane_n = lax.broadcasted_iota(jnp.int32, (N, R), 1)
    pow_w = (S5_SUB - 1.0) - lax.shift_right_logical(lane_n, 4).astype(F32)
    lane_p = lax.broadcasted_iota(jnp.int32, (P, R), 1)
    sub_p = lax.broadcasted_iota(jnp.int32, (P, R), 0)
    rep_p = ((lane_p & (P - 1)) == sub_p).astype(F32)
    rep_n = ((lane_n & (N - 1)) ==
             lax.broadcasted_iota(jnp.int32, (N, R), 0)).astype(F32)
    pow_v = sub_p.astype(F32) + 1.0
    lag0_diag = lane_p == (R - P) + sub_p
    is_re = lane_p < 2 * N
    lane_slot = lax.shift_right_logical(lane_p, 6) & 1

    def hdot(a, b):
        return jnp.dot(a, b, precision=HI, preferred_element_type=F32)

    for k in range(_S5GEN_GG):
        odd = k % 2
        dt = jnp.exp(ls_ref[k])
        ar = arc_ref[k]
        ai = aic_ref[k]
        mag = jnp.exp(ar * dt)
        lbr = mag * jnp.cos(ai * dt)
        lbi = mag * jnp.sin(ai * dt)
        den = ar * ar + ai * ai
        zr = ((lbr - 1.0) * ar + lbi * ai) / den
        zi = (lbi * ar - (lbr - 1.0) * ai) / den
        b_re = br_ref[k]
        b_im = bi_ref[k]
        bb_r = hdot(zr * b_re - zi * b_im, rep_p)
        bb_i = hdot(zr * b_im + zi * b_re, rep_p)
        mag_w = jnp.exp(ar * dt * pow_w)
        ph_w = ai * dt * pow_w
        lm_r = mag_w * jnp.cos(ph_w)
        lm_i = mag_w * jnp.sin(ph_w)
        w_r = lm_r * bb_r - lm_i * bb_i
        w_i = lm_r * bb_i + lm_i * bb_r
        zero = jnp.zeros_like(w_r)
        wt_ref[k, 0:2 * N, :] = jnp.concatenate(
            [zero, w_r] if odd else [w_r, zero], axis=0).astype(BF16)
        wt_ref[k, 2 * N:4 * N, :] = jnp.concatenate(
            [zero, w_i] if odd else [w_i, zero], axis=0).astype(BF16)

        c_re = cr_ref[k]
        c_im = ci_ref[k]
        krev = hdot(c_re, w_r) - hdot(c_im, w_i)
        krev = krev + jnp.where(lag0_diag, d_ref[k], 0.0)
        for tp in range(S5_SUB):
            width = (tp + 1) * P
            piece = krev if width == R else jnp.where(
                lane_p < width, pltpu.roll(krev, width, axis=1), 0.0)
            mt_ref[k, tp * P:(tp + 1) * P, :] = piece.astype(BF16)

        c4_r = hdot(c_re, rep_n)
        c4_i = hdot(c_im, rep_n)
        arow = arr_ref[k]
        airow = air_ref[k]
        mag_v = jnp.exp(arow * dt * pow_v)
        ph_v = airow * dt * pow_v
        l_r = mag_v * jnp.cos(ph_v)
        l_i = mag_v * jnp.sin(ph_v)
        keep = lane_slot == odd
        for tp in range(S5_SUB):
            lr = l_r[tp:tp + 1, :]
            li = l_i[tp:tp + 1, :]
            val = jnp.where(is_re, c4_r * lr - c4_i * li, -(c4_r * li + c4_i * lr))
            vt_ref[k, tp * P:(tp + 1) * P, :] = jnp.where(keep, val, 0.0).astype(BF16)


def _s5gen(a_re, a_im, log_step, b_re, b_im, c_re, c_im, d_skip):
    G, N, P, R = S5_GROUPS, S5_STATE, S5_GROUP, S5_ROW
    gg = _S5GEN_GG
    col = lambda a: a.reshape(G, N, 1)
    row4 = lambda a: jnp.tile(a, (1, R // N)).reshape(G, 1, R)
    blk = lambda s: pl.BlockSpec((gg,) + s, lambda g: (g, 0, 0))
    return pl.pallas_call(
        _s5gen_kernel,
        grid=(G // gg,),
        in_specs=[blk((1, 1)), blk((N, 1)), blk((N, 1)), blk((1, R)), blk((1, R)),
                  blk((N, P)), blk((N, P)), blk((P, N)), blk((P, N)), blk((P, 1))],
        out_specs=[blk((R, R)), blk((R, R)), blk((R, R))],
        out_shape=[jax.ShapeDtypeStruct((G, R, R), BF16)] * 3,
        compiler_params=pltpu.CompilerParams(
            dimension_semantics=("parallel",), vmem_limit_bytes=VMEM_LIMIT),
        name="s5gen",
    )(log_step.reshape(G, 1, 1), col(a_re), col(a_im), row4(a_re), row4(a_im),
      b_re, b_im, c_re, c_im, d_skip.reshape(G, P, 1))


_S5_GB = LANES // S5_GROUP
_S5_NP = _S5_GB // 2


def _s5_kernel(ar_ref, ai_ref, ls_ref, u_ref, mt_ref, wt_ref, vt_ref, o_ref,
               ut_scr, yt_scr, e_scr, p_scr):
    P = S5_GROUP
    N2 = 2 * S5_STATE
    cols = S5_COLS
    nblk = cols // LANES
    nt = cols // 8

    def load_body(tau, carry):
        for c in range(nblk):
            x = u_ref[pl.ds(tau + c * LANES * S5_SUB, LANES, stride=S5_SUB), :]
            xt = x.T
            r0 = pl.multiple_of(tau * P, P)
            for k in range(_S5_GB):
                ut_scr[k, pl.ds(r0, P), c * LANES:(c + 1) * LANES] = \
                    xt[k * P:(k + 1) * P, :].astype(BF16)
        return carry

    lax.fori_loop(0, S5_SUB, load_body, 0)

    for pr in range(_S5_NP):
        et = (jnp.dot(wt_ref[2 * pr], ut_scr[2 * pr], preferred_element_type=F32) +
              jnp.dot(wt_ref[2 * pr + 1], ut_scr[2 * pr + 1], preferred_element_type=F32))
        e_scr[pr] = et.T

    step = float(S5_SUB)
    sub = lax.broadcasted_iota(jnp.int32, (8, N2), 0)
    lo = sub < BATCH
    a_r, a_i = [], []
    for pr in range(_S5_NP):
        sl = slice(pr * N2, (pr + 1) * N2)
        dt = jnp.exp(ls_ref[:, sl])
        mag = jnp.exp(ar_ref[:, sl] * dt * step)
        ph = ai_ref[:, sl] * dt * step
        a_r.append(jnp.broadcast_to(mag * jnp.cos(ph), (8, N2)))
        a_i.append(jnp.broadcast_to(mag * jnp.sin(ph), (8, N2)))

    def scan_body(t, carry):
        r0 = pl.multiple_of(t * 8, 8)
        new = []
        for pr in range(_S5_NP):
            s_r, s_i = carry[2 * pr], carry[2 * pr + 1]
            e_r = e_scr[pr, pl.ds(r0, 8), 0:N2]
            e_i = e_scr[pr, pl.ds(r0, 8), N2:2 * N2]
            x_r = pltpu.roll(e_r, BATCH, axis=0)
            x_i = pltpu.roll(e_i, BATCH, axis=0)
            elo_r = jnp.where(lo, e_r, x_r)
            elo_i = jnp.where(lo, e_i, x_i)
            ehi_r = jnp.where(lo, x_r, e_r)
            ehi_i = jnp.where(lo, x_i, e_i)
            t_r = a_r[pr] * s_r - a_i[pr] * s_i + elo_r
            t_i = a_r[pr] * s_i + a_i[pr] * s_r + elo_i
            p_scr[pr, pl.ds(r0, 8), 0:N2] = jnp.where(lo, s_r, t_r)
            p_scr[pr, pl.ds(r0, 8), N2:2 * N2] = jnp.where(lo, s_i, t_i)
            new.append(a_r[pr] * t_r - a_i[pr] * t_i + ehi_r)
            new.append(a_r[pr] * t_i + a_i[pr] * t_r + ehi_i)
        return tuple(new)

    zero = jnp.zeros((8, N2), F32)
    lax.fori_loop(0, nt, scan_body, tuple(zero for _ in range(2 * _S5_NP)))

    for pr in range(_S5_NP):
        pt = p_scr[pr].T.astype(BF16)
        for k in range(2):
            gi = 2 * pr + k
            yt = (jnp.dot(mt_ref[gi], ut_scr[gi], preferred_element_type=F32) +
                  jnp.dot(vt_ref[gi], pt, preferred_element_type=F32))
            for tau in range(S5_SUB):
                yt_scr[tau, gi * P:(gi + 1) * P, :] = yt[tau * P:(tau + 1) * P, :]

    def store_body(tau, carry):
        o_ref[pl.ds(tau, cols, stride=S5_SUB), :] = yt_scr[tau].T
        return carry

    lax.fori_loop(0, S5_SUB, store_body, 0)


def _s5(u_rows, mt, wt, vt, a_re, a_im, log_step):
    G, N, R = S5_GROUPS, S5_STATE, S5_ROW
    gb = _S5_GB
    flat = lambda a: a.reshape(1, G * N)
    ls_rep = jnp.broadcast_to(log_step[:, None], (G, N))
    vec = pl.BlockSpec((1, gb * N), lambda i: (0, i))
    mat = pl.BlockSpec((gb, R, R), lambda i: (i, 0, 0))
    slab = pl.BlockSpec((TOK, LANES), lambda i: (0, i))
    return pl.pallas_call(
        _s5_kernel,
        grid=(G // gb,),
        in_specs=[vec, vec, vec, slab, mat, mat, mat],
        out_specs=slab,
        out_shape=jax.ShapeDtypeStruct((TOK, D_S5), F32),
        scratch_shapes=[pltpu.VMEM((gb, R, S5_COLS), BF16),
                        pltpu.VMEM((S5_SUB, LANES, S5_COLS), F32),
                        pltpu.VMEM((_S5_NP, S5_COLS, 4 * N), F32),
                        pltpu.VMEM((_S5_NP, S5_COLS, 4 * N), F32)],
        compiler_params=pltpu.CompilerParams(
            dimension_semantics=("parallel",), vmem_limit_bytes=VMEM_LIMIT),
        name="s5",
    )(flat(a_re), flat(a_im), flat(ls_rep), u_rows, mt, wt, vt)


_MIX_TM = 512
_MIX_NS = _MIX_TM // S5_SUB


def _mix_kernel(ssm_ref, ret_ref, x_ref, g1_ref, sc_ref, sh_ref, nw_ref, wglu_ref, bglu_ref,
                wout_ref, o_ref, h_ref):
    s = ssm_ref[...].reshape(_MIX_TM, D_S5)
    cdf = 0.5 * (1.0 + jnp.tanh(math.sqrt(2.0 / math.pi) * (s + 0.044715 * (s * s * s))))
    sg = s * cdf
    z = jnp.dot(sg.astype(BF16), wglu_ref[...], preferred_element_type=F32) + bglu_ref[...]
    so = sg * jax.nn.sigmoid(z)
    mix = jnp.dot(ret_ref[...], wout_ref[0:D_RET, :], preferred_element_type=F32)
    mix = mix + jnp.dot(so.astype(BF16), wout_ref[D_RET:, :], preferred_element_type=F32)
    x1 = x_ref[...] + g1_ref[...] * mix
    o_ref[...] = x1
    ms = jnp.mean(x1 * x1, axis=-1, keepdims=True)
    y = x1 * lax.rsqrt(ms + EPS) * nw_ref[...]
    h_ref[...] = (y * (1.0 + sc_ref[...]) + sh_ref[...]).astype(BF16)


def _mix(ssm_sb, ret_out, x2d, g1, sc2, sh2, norm_w, w_glu_bf, b_glu, w_out_bf):
    tm = _MIX_TM
    per_b = SEQ // tm
    mod = pl.BlockSpec((None, 1, D_MODEL), lambda i: (i // per_b, 0, 0))
    rows = pl.BlockSpec((tm, D_MODEL), lambda i: (i, 0))
    return pl.pallas_call(
        _mix_kernel,
        grid=(TOK // tm,),
        in_specs=[pl.BlockSpec((_MIX_NS, None, S5_SUB, D_S5),
                               lambda i: (i % per_b, i // per_b, 0, 0)),
                  pl.BlockSpec((tm, D_RET), lambda i: (i, 0)),
                  rows, mod, mod, mod,
                  pl.BlockSpec((1, D_MODEL), lambda i: (0, 0)),
                  pl.BlockSpec((D_S5, D_S5), lambda i: (0, 0)),
                  pl.BlockSpec((1, D_S5), lambda i: (0, 0)),
                  pl.BlockSpec((D_RET + D_S5, D_MODEL), lambda i: (0, 0))],
        out_specs=[rows, rows],
        out_shape=[jax.ShapeDtypeStruct((TOK, D_MODEL), F32),
                   jax.ShapeDtypeStruct((TOK, D_MODEL), BF16)],
        compiler_params=pltpu.CompilerParams(
            dimension_semantics=("parallel",), vmem_limit_bytes=VMEM_LIMIT),
        name="mix",
    )(ssm_sb, ret_out, x2d, g1, sc2, sh2, norm_w, w_glu_bf, b_glu, w_out_bf)


_FFN_UP_TM = 1024
_FFN_UP_TF = 512
_FFN_DOWN_TM = 256


def _ffn_up_kernel(h_ref, wg_ref, wu_ref, o_ref):
    h = h_ref[...]
    gate = jnp.dot(h, wg_ref[...], preferred_element_type=F32)
    up = jnp.dot(h, wu_ref[...], preferred_element_type=F32)
    o_ref[...] = (_silu(gate) * up).astype(BF16)


def _ffn_up(h2, w_gu_bf):
    tm, tf = _FFN_UP_TM, _FFN_UP_TF
    nf = D_FF // tf
    return pl.pallas_call(
        _ffn_up_kernel,
        grid=(TOK // tm, nf),
        in_specs=[pl.BlockSpec((tm, D_MODEL), lambda i, f: (i, 0)),
                  pl.BlockSpec((D_MODEL, tf), lambda i, f: (0, f)),
                  pl.BlockSpec((D_MODEL, tf), lambda i, f: (0, nf + f))],
        out_specs=pl.BlockSpec((tm, tf), lambda i, f: (i, f)),
        out_shape=jax.ShapeDtypeStruct((TOK, D_FF), BF16),
        compiler_params=pltpu.CompilerParams(
            dimension_semantics=("parallel", "parallel"), vmem_limit_bytes=VMEM_LIMIT),
        name="ffn_up",
    )(h2, w_gu_bf, w_gu_bf)


def _ffn_down_kernel(a_ref, x_ref, g2_ref, fw_ref, wd_ref, o_ref):
    down = jnp.dot(a_ref[...], wd_ref[...], preferred_element_type=F32)
    x2 = x_ref[...] + g2_ref[...] * down
    ms = jnp.mean(x2 * x2, axis=-1, keepdims=True)
    o_ref[...] = x2 * lax.rsqrt(ms + EPS) * fw_ref[...]


def _ffn_down(act, x1, g2, final_w, w_down_bf):
    tm = _FFN_DOWN_TM
    per_b = SEQ // tm
    rows = pl.BlockSpec((tm, D_MODEL), lambda i: (i, 0))
    return pl.pallas_call(
        _ffn_down_kernel,
        grid=(TOK // tm,),
        in_specs=[pl.BlockSpec((tm, D_FF), lambda i: (i, 0)),
                  rows,
                  pl.BlockSpec((None, 1, D_MODEL), lambda i: (i // per_b, 0, 0)),
                  pl.BlockSpec((1, D_MODEL), lambda i: (0, 0)),
                  pl.BlockSpec((D_FF, D_MODEL), lambda i: (0, 0))],
        out_specs=rows,
        out_shape=jax.ShapeDtypeStruct((TOK, D_MODEL), F32),
        compiler_params=pltpu.CompilerParams(
            dimension_semantics=("parallel",), vmem_limit_bytes=VMEM_LIMIT),
        name="ffn_down",
    )(act, x1, g2, final_w, w_down_bf)


def kernel(x, c, w_ada, b_ada, norm1_w, w_in, ret_norm_w, s5_a_re, s5_a_im, s5_log_step,
           s5_b_re, s5_b_im, s5_c_re, s5_c_im, s5_d, w_glu, b_glu, w_out, norm2_w,
           w_gate_up, w_down, final_norm_w):
    x2d = x.reshape(TOK, D_MODEL)
    layer = 0
    mod = _adaln(c, w_ada[layer], b_ada[layer])
    sh1, sc1, g1, sh2, sc2, g2 = [m.reshape(BATCH, 1, D_MODEL) for m in jnp.split(mod, 6, axis=-1)]

    qkvg, u_sb = _inproj(x2d, sc1, sh1, norm1_w[layer].reshape(1, D_MODEL),
                         w_in[layer].astype(BF16))
    ret_out = _retention(qkvg, ret_norm_w[layer].reshape(1, D_RET))

    mt, wt, vt = _s5gen(s5_a_re[layer], s5_a_im[layer], s5_log_step[layer],
                        s5_b_re[layer], s5_b_im[layer], s5_c_re[layer], s5_c_im[layer],
                        s5_d[layer])
    y_rows = _s5(u_sb.reshape(TOK, D_S5), mt, wt, vt,
                 s5_a_re[layer], s5_a_im[layer], s5_log_step[layer])
    ssm_sb = y_rows.reshape(S5_NSUB, BATCH, S5_SUB, D_S5)

    x1, h2 = _mix(ssm_sb, ret_out, x2d, g1, sc2, sh2, norm2_w[layer].reshape(1, D_MODEL),
                  w_glu[layer].astype(BF16), b_glu[layer].reshape(1, D_S5),
                  w_out[layer].astype(BF16))
    act = _ffn_up(h2, w_gate_up[layer].astype(BF16))
    out = _ffn_down(act, x1, g2, final_norm_w.reshape(1, D_MODEL), w_down[layer].astype(BF16))
    return out.reshape(BATCH, SEQ, D_MODEL)
```

```python
import math

import numpy as np
import jax
import jax.numpy as jnp
from jax import lax
from jax.experimental import pallas as pl
from jax.experimental.pallas import tpu as pltpu

D_MODEL = 2048
BATCH = 4
SEQ = 2048
TOK = BATCH * SEQ
D_RET = 1024
D_S5 = 1024
RET_HEADS = 4
RET_HEAD_DIM = 256
RET_CHUNK = 128
S5_GROUP = 16
S5_GROUPS = 64
S5_STATE = 64
S5_SUB = 16
S5_ROW = S5_SUB * S5_GROUP
S5_NSUB = SEQ // S5_SUB
S5_COLS = S5_NSUB * BATCH
D_FF = 5632
ROPE_BASE = 10000.0
EPS = 1e-6
LANES = 128

F32 = jnp.float32
BF16 = jnp.bfloat16
HI = lax.Precision.HIGHEST
VMEM_LIMIT = 58 * 1024 * 1024


def _silu(v):
    return v * jax.nn.sigmoid(v)


def _adaln_kernel(c_ref, w_ref, b_ref, o_ref):
    cond = _silu(c_ref[...])
    o_ref[...] = jnp.dot(cond.astype(BF16), w_ref[...].astype(BF16),
                         preferred_element_type=F32) + b_ref[...]


def _adaln(c, w_ada, b_ada):
    tn = 1024
    n = w_ada.shape[1]
    return pl.pallas_call(
        _adaln_kernel,
        grid=(n // tn,),
        in_specs=[pl.BlockSpec((BATCH, D_MODEL), lambda j: (0, 0)),
                  pl.BlockSpec((D_MODEL, tn), lambda j: (0, j)),
                  pl.BlockSpec((1, tn), lambda j: (0, j))],
        out_specs=pl.BlockSpec((BATCH, tn), lambda j: (0, j)),
        out_shape=jax.ShapeDtypeStruct((BATCH, n), F32),
        compiler_params=pltpu.CompilerParams(
            dimension_semantics=("parallel",), vmem_limit_bytes=VMEM_LIMIT),
        name="adaln",
    )(c, w_ada, b_ada.reshape(1, n))


_INPROJ_TM = 1024
_INPROJ_TN = 1024
_INPROJ_NS = _INPROJ_TM // S5_SUB


def _inproj_kernel(x_ref, sc_ref, sh_ref, nw_ref, w_ref, o_ref, u_ref, h_scr, cos_scr, sin_scr):
    tm = _INPROJ_TM
    half = RET_HEAD_DIM // 2
    per_b = SEQ // tm
    i = pl.program_id(0)
    j = pl.program_id(1)
    slot = i % per_b

    @pl.when(j == 0)
    def _():
        gain = nw_ref[...] * (1.0 + sc_ref[...])
        shift = sh_ref[...]
        rb = 16

        def norm_body(r, carry):
            r0 = pl.multiple_of(r * rb, rb)
            x = x_ref[pl.ds(r0, rb), :]
            ms = jnp.mean(x * x, axis=-1, keepdims=True)
            h_scr[pl.ds(r0, rb), :] = (x * lax.rsqrt(ms + EPS) * gain + shift).astype(BF16)
            return carry

        lax.fori_loop(0, tm // rb, norm_body, 0, unroll=4)

    @pl.when((j == 0) & (i < per_b))
    def _():
        pos = (lax.broadcasted_iota(jnp.int32, (tm, half), 0) + slot * tm).astype(F32)
        lane = lax.broadcasted_iota(jnp.int32, (tm, half), 1).astype(F32)
        ang = pos * jnp.exp(lane * (-math.log(ROPE_BASE) / half))
        cos_scr[slot] = jnp.cos(ang)
        sin_scr[slot] = jnp.sin(ang)

    acc = jnp.dot(h_scr[...], w_ref[...], preferred_element_type=F32)

    rot = (j < 2).astype(F32)
    cs = cos_scr[slot] * rot + (1.0 - rot)
    sn = sin_scr[slot] * rot
    for hh in range(RET_HEADS):
        c0 = hh * RET_HEAD_DIM
        x1 = acc[:, c0:c0 + half]
        x2 = acc[:, c0 + half:c0 + 2 * half]
        o_ref[:, c0:c0 + half] = (x1 * cs - x2 * sn).astype(BF16)
        o_ref[:, c0 + half:c0 + 2 * half] = (x2 * cs + x1 * sn).astype(BF16)
    u_ref[...] = acc.reshape(_INPROJ_NS, S5_SUB, _INPROJ_TN)


def _inproj(x2d, sc1, sh1, norm_w, w_in_bf):
    tm, tn = _INPROJ_TM, _INPROJ_TN
    per_b = SEQ // tm
    half = RET_HEAD_DIM // 2
    return pl.pallas_call(
        _inproj_kernel,
        grid=(TOK // tm, 5),
        in_specs=[pl.BlockSpec((tm, D_MODEL), lambda i, j: (i, 0)),
                  pl.BlockSpec((None, 1, D_MODEL), lambda i, j: (i // per_b, 0, 0)),
                  pl.BlockSpec((None, 1, D_MODEL), lambda i, j: (i // per_b, 0, 0)),
                  pl.BlockSpec((1, D_MODEL), lambda i, j: (0, 0)),
                  pl.BlockSpec((D_MODEL, tn), lambda i, j: (0, j))],
        out_specs=[pl.BlockSpec((tm, tn), lambda i, j: (i, j)),
                   pl.BlockSpec((_INPROJ_NS, None, S5_SUB, D_S5),
                                lambda i, j: (i % per_b, i // per_b, 0, 0))],
        out_shape=[jax.ShapeDtypeStruct((TOK, 4 * D_RET + D_S5), BF16),
                   jax.ShapeDtypeStruct((S5_NSUB, BATCH, S5_SUB, D_S5), F32)],
        scratch_shapes=[pltpu.VMEM((tm, D_MODEL), BF16),
                        pltpu.VMEM((per_b, tm, half), F32),
                        pltpu.VMEM((per_b, tm, half), F32)],
        compiler_params=pltpu.CompilerParams(
            dimension_semantics=("arbitrary", "arbitrary"), vmem_limit_bytes=VMEM_LIMIT),
        name="inproj",
    )(x2d, sc1, sh1, norm_w, w_in_bf)


_RET_BLK = 256


def _ret_kernel(lg_ref, q_ref, k_ref, v_ref, g_ref, w_ref, o_ref,
                kv_scr, prev_scr, intra_scr, kdec_scr, qdec_scr):
    C = _RET_BLK
    dh = RET_HEAD_DIM
    nc = SEQ // C
    scale = dh ** -0.5
    lg = lg_ref[0:1, :]
    ii = lax.broadcasted_iota(jnp.int32, (C, C), 0)
    jj = lax.broadcasted_iota(jnp.int32, (C, C), 1)
    diff = (ii - jj).astype(F32)
    intra_scr[...] = jnp.where(diff >= 0.0, jnp.exp(lg * jnp.maximum(diff, 0.0)), 0.0) * scale
    row = lax.broadcasted_iota(jnp.int32, (C, dh), 0).astype(F32)
    kdec_scr[...] = jnp.exp(lg * (C - 1.0 - row)) * scale
    qdec_scr[...] = jnp.exp(lg * (row + 1.0))
    block_decay = jnp.exp(lg * float(C))
    gn_w = w_ref[...]

    for n in range(nc):
        rs = slice(n * C, (n + 1) * C)
        kd = (k_ref[rs, :].astype(F32) * kdec_scr[...]).astype(BF16)
        kv_scr[n] = lax.dot_general(kd, v_ref[rs, :], (((0,), (0,)), ((), ())),
                                    preferred_element_type=F32)

    band = 64
    for rb in range(dh // band):
        bs = slice(rb * band, (rb + 1) * band)
        st = jnp.zeros((band, dh), F32)
        for n in range(nc):
            prev_scr[n, bs, :] = st.astype(BF16)
            if n + 1 < nc:
                st = st * block_decay + kv_scr[n, bs, :]

    for n in range(nc):
        rs = slice(n * C, (n + 1) * C)
        q = q_ref[rs, :]
        s = lax.dot_general(q, k_ref[rs, :], (((1,), (1,)), ((), ())),
                            preferred_element_type=F32) * intra_scr[...]
        y = jnp.dot(s.astype(BF16), v_ref[rs, :], preferred_element_type=F32)
        y = y + jnp.dot(q, prev_scr[n], preferred_element_type=F32) * qdec_scr[...]
        mu = jnp.mean(y, axis=-1, keepdims=True)
        yc = y - mu
        var = jnp.mean(yc * yc, axis=-1, keepdims=True)
        yn = yc * lax.rsqrt(var + EPS) * gn_w
        g = g_ref[rs, :].astype(F32)
        o_ref[rs, :] = (_silu(g) * yn).astype(BF16)


def _retention(qkvg, ret_norm_w):
    dh = RET_HEAD_DIM
    nc = SEQ // _RET_BLK
    lg = np.log1p(-np.exp2(-5.0 - np.arange(RET_HEADS, dtype=np.float64)))
    lg_tab = jnp.asarray(np.broadcast_to(lg[:, None, None], (RET_HEADS, 8, dh)), F32)
    spec = lambda off: pl.BlockSpec((SEQ, dh), lambda b, h: (b, off + h))
    return pl.pallas_call(
        _ret_kernel,
        grid=(BATCH, RET_HEADS),
        in_specs=[pl.BlockSpec((None, 8, dh), lambda b, h: (h, 0, 0)),
                  spec(0), spec(RET_HEADS), spec(2 * RET_HEADS), spec(3 * RET_HEADS),
                  pl.BlockSpec((1, dh), lambda b, h: (0, h))],
        out_specs=pl.BlockSpec((SEQ, dh), lambda b, h: (b, h)),
        out_shape=jax.ShapeDtypeStruct((TOK, D_RET), BF16),
        scratch_shapes=[pltpu.VMEM((nc, dh, dh), F32),
                        pltpu.VMEM((nc, dh, dh), BF16),
                        pltpu.VMEM((_RET_BLK, _RET_BLK), F32),
                        pltpu.VMEM((_RET_BLK, dh), F32),
                        pltpu.VMEM((_RET_BLK, dh), F32)],
        compiler_params=pltpu.CompilerParams(
            dimension_semantics=("parallel", "parallel"), vmem_limit_bytes=VMEM_LIMIT),
        name="retention",
    )(lg_tab, qkvg, qkvg, qkvg, qkvg, ret_norm_w)


_S5GEN_GG = 4


def _s5gen_kernel(ls_ref, ar_ref, ai_ref, bt_ref, cr_ref, ci_ref, c4r_ref, c4i_ref, d_ref,
                  mt_ref, wt_ref, vt_ref, a16r_ref, a16i_ref):
    N, P, R = S5_STATE, S5_GROUP, S5_ROW
    re_half = lax.broadcasted_iota(jnp.int32, (P, 2 * N), 1) < N
    sub_e = lax.broadcasted_iota(jnp.int32, (P, 2 * N), 0)
    lane_p = lax.broadcasted_iota(jnp.int32, (P, R), 1)
    sub_p = lax.broadcasted_iota(jnp.int32, (P, R), 0)
    lag0_diag = lane_p == (R - P) + sub_p
    is_re = lane_p < 2 * N
    lane_slot = lax.shift_right_logical(lane_p, 6) & 1

    def hdot(a, b):
        return jnp.dot(a, b, precision=HI, preferred_element_type=F32)

    def cmul(xr, xi, yr, yi):
        return xr * yr - xi * yi, xr * yi + xi * yr

    for k in range(_S5GEN_GG):
        odd = k % 2
        dt = jnp.exp(ls_ref[k])
        ar = ar_ref[k]
        ai = ai_ref[k]
        mag = jnp.exp(ar * dt)
        lbr = mag * jnp.cos(ai * dt)
        lbi = mag * jnp.sin(ai * dt)
        den = ar * ar + ai * ai
        zr = ((lbr - 1.0) * ar + lbi * ai) / den
        zi = (lbi * ar - (lbr - 1.0) * ai) / den

        pw_r = jnp.ones((P, 2 * N), F32)
        pw_i = jnp.zeros((P, 2 * N), F32)
        sq_r, sq_i = lbr, lbi
        for bit in range(4):
            nr, ni = cmul(pw_r, pw_i, sq_r, sq_i)
            take = (lax.shift_right_logical(sub_e, bit) & 1) == 1
            pw_r = jnp.where(take, nr, pw_r)
            pw_i = jnp.where(take, ni, pw_i)
            if bit < 3:
                sq_r, sq_i = cmul(sq_r, sq_i, sq_r, sq_i)
        p1_r, p1_i = cmul(pw_r, pw_i, lbr, lbi)
        a16r_ref[k] = p1_r[S5_SUB - 1:S5_SUB, :]
        a16i_ref[k] = p1_i[S5_SUB - 1:S5_SUB, :]

        bt = bt_ref[k]
        x = zr * bt + jnp.where(re_half[0:1], -zi, zi) * pltpu.roll(bt, N, axis=1)
        xs = pltpu.roll(x, N, axis=1)
        pw_is = jnp.where(re_half, -pw_i, pw_i)
        w_rows = jnp.concatenate(
            [pw_r[e:e + 1] * x + pw_is[e:e + 1] * xs for e in range(S5_SUB - 1, -1, -1)], axis=0)
        w_t = w_rows.T
        w_r = w_t[0:N]
        w_i = w_t[N:2 * N]
        zero = jnp.zeros_like(w_r)
        wt_ref[k, 0:2 * N, :] = jnp.concatenate(
            [zero, w_r] if odd else [w_r, zero], axis=0).astype(BF16)
        wt_ref[k, 2 * N:4 * N, :] = jnp.concatenate(
            [zero, w_i] if odd else [w_i, zero], axis=0).astype(BF16)

        c_re = cr_ref[k]
        c_im = ci_ref[k]
        krev = hdot(c_re, w_r) - hdot(c_im, w_i)
        krev = krev + jnp.where(lag0_diag, d_ref[k], 0.0)
        for tp in range(S5_SUB):
            width = (tp + 1) * P
            piece = krev if width == R else jnp.where(
                lane_p < width, pltpu.roll(krev, width, axis=1), 0.0)
            mt_ref[k, tp * P:(tp + 1) * P, :] = piece.astype(BF16)

        c4_r = c4r_ref[k]
        c4_i = c4i_ref[k]
        keep = lane_slot == odd
        for tp in range(S5_SUB):
            lr = jnp.concatenate([p1_r[tp:tp + 1], p1_r[tp:tp + 1]], axis=1)
            li = jnp.concatenate([p1_i[tp:tp + 1], p1_i[tp:tp + 1]], axis=1)
            val = jnp.where(is_re, c4_r * lr - c4_i * li, -(c4_r * li + c4_i * lr))
            vt_ref[k, tp * P:(tp + 1) * P, :] = jnp.where(keep, val, 0.0).astype(BF16)


def _s5gen(a_re, a_im, log_step, b_re, b_im, c_re, c_im, d_skip):
    G, N, P, R = S5_GROUPS, S5_STATE, S5_GROUP, S5_ROW
    gg = _S5GEN_GG
    dup = lambda a: jnp.tile(a, (1, 2)).reshape(G, 1, 2 * N)
    bt = jnp.concatenate([jnp.swapaxes(b_re, 1, 2), jnp.swapaxes(b_im, 1, 2)], axis=-1)
    tile4 = lambda a: jnp.tile(a, (1, 1, R // N))
    blk = lambda s: pl.BlockSpec((gg,) + s, lambda g: (g, 0, 0))
    return pl.pallas_call(
        _s5gen_kernel,
        grid=(G // gg,),
        in_specs=[blk((1, 1)), blk((1, 2 * N)), blk((1, 2 * N)), blk((P, 2 * N)),
                  blk((P, N)), blk((P, N)), blk((P, R)), blk((P, R)), blk((P, 1))],
        out_specs=[blk((R, R)), blk((R, R)), blk((R, R)), blk((1, 2 * N)), blk((1, 2 * N))],
        out_shape=[jax.ShapeDtypeStruct((G, R, R), BF16)] * 3 +
                  [jax.ShapeDtypeStruct((G, 1, 2 * N), F32)] * 2,
        compiler_params=pltpu.CompilerParams(
            dimension_semantics=("parallel",), vmem_limit_bytes=VMEM_LIMIT),
        name="s5gen",
    )(log_step.reshape(G, 1, 1), dup(a_re), dup(a_im), bt, c_re, c_im, tile4(c_re), tile4(c_im),
      d_skip.reshape(G, P, 1))


_S5_GB = LANES // S5_GROUP
_S5_NP = _S5_GB // 2


def _s5_kernel(a16r_ref, a16i_ref, u_ref, mt_ref, wt_ref, vt_ref, o_ref,
               ut_scr, yt_scr, e_scr, p_scr):
    P = S5_GROUP
    N2 = 2 * S5_STATE
    cols = S5_COLS
    nblk = cols // LANES
    nt = cols // 8

    def load_body(tau, carry):
        for c in range(nblk):
            x = u_ref[pl.ds(tau + c * LANES * S5_SUB, LANES, stride=S5_SUB), :]
            xt = x.T
            r0 = pl.multiple_of(tau * P, P)
            for k in range(_S5_GB):
                ut_scr[k, pl.ds(r0, P), c * LANES:(c + 1) * LANES] = \
                    xt[k * P:(k + 1) * P, :].astype(BF16)
        return carry

    lax.fori_loop(0, S5_SUB, load_body, 0, unroll=2)

    for pr in range(_S5_NP):
        et = (jnp.dot(wt_ref[2 * pr], ut_scr[2 * pr], preferred_element_type=F32) +
              jnp.dot(wt_ref[2 * pr + 1], ut_scr[2 * pr + 1], preferred_element_type=F32))
        e_scr[pr] = et.T

    sub = lax.broadcasted_iota(jnp.int32, (8, N2), 0)
    lo = sub < BATCH
    first = lax.broadcasted_iota(jnp.int32, (1, N2), 1) < S5_STATE
    a_r, a_i = [], []
    for pr in range(_S5_NP):
        a_r.append(jnp.broadcast_to(
            jnp.where(first, a16r_ref[2 * pr], a16r_ref[2 * pr + 1]), (8, N2)))
        a_i.append(jnp.broadcast_to(
            jnp.where(first, a16i_ref[2 * pr], a16i_ref[2 * pr + 1]), (8, N2)))

    def scan_body(t, carry):
        r0 = pl.multiple_of(t * 8, 8)
        new = []
        for pr in range(_S5_NP):
            s_r, s_i = carry[2 * pr], carry[2 * pr + 1]
            e_r = e_scr[pr, pl.ds(r0, 8), 0:N2]
            e_i = e_scr[pr, pl.ds(r0, 8), N2:2 * N2]
            x_r = pltpu.roll(e_r, BATCH, axis=0)
            x_i = pltpu.roll(e_i, BATCH, axis=0)
            elo_r = jnp.where(lo, e_r, x_r)
            elo_i = jnp.where(lo, e_i, x_i)
            ehi_r = jnp.where(lo, x_r, e_r)
            ehi_i = jnp.where(lo, x_i, e_i)
            t_r = a_r[pr] * s_r - a_i[pr] * s_i + elo_r
            t_i = a_r[pr] * s_i + a_i[pr] * s_r + elo_i
            p_scr[pr, pl.ds(r0, 8), 0:N2] = jnp.where(lo, s_r, t_r)
            p_scr[pr, pl.ds(r0, 8), N2:2 * N2] = jnp.where(lo, s_i, t_i)
            new.append(a_r[pr] * t_r - a_i[pr] * t_i + ehi_r)
            new.append(a_r[pr] * t_i + a_i[pr] * t_r + ehi_i)
        return tuple(new)

    zero = jnp.zeros((8, N2), F32)
    lax.fori_loop(0, nt, scan_body, tuple(zero for _ in range(2 * _S5_NP)))

    for pr in range(_S5_NP):
        pt = p_scr[pr].T.astype(BF16)
        for k in range(2):
            gi = 2 * pr + k
            yt = (jnp.dot(mt_ref[gi], ut_scr[gi], preferred_element_type=F32) +
                  jnp.dot(vt_ref[gi], pt, preferred_element_type=F32))
            for tau in range(S5_SUB):
                yt_scr[tau, gi * P:(gi + 1) * P, :] = yt[tau * P:(tau + 1) * P, :]

    def store_body(tau, carry):
        o_ref[pl.ds(tau, cols, stride=S5_SUB), :] = yt_scr[tau].T
        return carry

    lax.fori_loop(0, S5_SUB, store_body, 0, unroll=2)


def _s5(u_rows, mt, wt, vt, a16r, a16i):
    G, N, R = S5_GROUPS, S5_STATE, S5_ROW
    gb = _S5_GB
    vec = pl.BlockSpec((gb, 1, 2 * N), lambda i: (i, 0, 0))
    mat = pl.BlockSpec((gb, R, R), lambda i: (i, 0, 0))
    slab = pl.BlockSpec((TOK, LANES), lambda i: (0, i))
    return pl.pallas_call(
        _s5_kernel,
        grid=(G // gb,),
        in_specs=[vec, vec, slab, mat, mat, mat],
        out_specs=slab,
        out_shape=jax.ShapeDtypeStruct((TOK, D_S5), F32),
        scratch_shapes=[pltpu.VMEM((gb, R, S5_COLS), BF16),
                        pltpu.VMEM((S5_SUB, LANES, S5_COLS), F32),
                        pltpu.VMEM((_S5_NP, S5_COLS, 4 * N), F32),
                        pltpu.VMEM((_S5_NP, S5_COLS, 4 * N), F32)],
        compiler_params=pltpu.CompilerParams(
            dimension_semantics=("parallel",), vmem_limit_bytes=VMEM_LIMIT),
        name="s5",
    )(a16r, a16i, u_rows, mt, wt, vt)


_MIX_TM = 512
_MIX_NS = _MIX_TM // S5_SUB


def _mix_kernel(ssm_ref, ret_ref, x_ref, g1_ref, sc_ref, sh_ref, nw_ref, wglu_ref, bglu_ref,
                wout_ref, o_ref, h_ref):
    s = ssm_ref[...].reshape(_MIX_TM, D_S5)
    cdf = 0.5 * (1.0 + jnp.tanh(math.sqrt(2.0 / math.pi) * (s + 0.044715 * (s * s * s))))
    sg = s * cdf
    z = jnp.dot(sg.astype(BF16), wglu_ref[...], preferred_element_type=F32) + bglu_ref[...]
    so = sg * jax.nn.sigmoid(z)
    mix = jnp.dot(ret_ref[...], wout_ref[0:D_RET, :], preferred_element_type=F32)
    mix = mix + jnp.dot(so.astype(BF16), wout_ref[D_RET:, :], preferred_element_type=F32)
    x1 = x_ref[...] + g1_ref[...] * mix
    o_ref[...] = x1
    ms = jnp.mean(x1 * x1, axis=-1, keepdims=True)
    y = x1 * lax.rsqrt(ms + EPS) * nw_ref[...]
    h_ref[...] = (y * (1.0 + sc_ref[...]) + sh_ref[...]).astype(BF16)


def _mix(ssm_sb, ret_out, x2d, g1, sc2, sh2, norm_w, w_glu_bf, b_glu, w_out_bf):
    tm = _MIX_TM
    per_b = SEQ // tm
    mod = pl.BlockSpec((None, 1, D_MODEL), lambda i: (i // per_b, 0, 0))
    rows = pl.BlockSpec((tm, D_MODEL), lambda i: (i, 0))
    return pl.pallas_call(
        _mix_kernel,
        grid=(TOK // tm,),
        in_specs=[pl.BlockSpec((_MIX_NS, None, S5_SUB, D_S5),
                               lambda i: (i % per_b, i // per_b, 0, 0)),
                  pl.BlockSpec((tm, D_RET), lambda i: (i, 0)),
                  rows, mod, mod, mod,
                  pl.BlockSpec((1, D_MODEL), lambda i: (0, 0)),
                  pl.BlockSpec((D_S5, D_S5), lambda i: (0, 0)),
                  pl.BlockSpec((1, D_S5), lambda i: (0, 0)),
                  pl.BlockSpec((D_RET + D_S5, D_MODEL), lambda i: (0, 0))],
        out_specs=[rows, rows],
        out_shape=[jax.ShapeDtypeStruct((TOK, D_MODEL), F32),
                   jax.ShapeDtypeStruct((TOK, D_MODEL), BF16)],
        compiler_params=pltpu.CompilerParams(
            dimension_semantics=("parallel",), vmem_limit_bytes=VMEM_LIMIT),
        name="mix",
    )(ssm_sb, ret_out, x2d, g1, sc2, sh2, norm_w, w_glu_bf, b_glu, w_out_bf)


_FFN_UP_TM = 1024
_FFN_UP_TF = 512
_FFN_DOWN_TM = 256


def _ffn_up_kernel(h_ref, wg_ref, wu_ref, wd_ref, o_ref, wd_bf_ref, wg_scr, wu_scr):
    @pl.when(pl.program_id(1) == 0)
    def _():
        wg_scr[...] = wg_ref[...].astype(BF16)
        wu_scr[...] = wu_ref[...].astype(BF16)

    h = h_ref[...]
    gate = jnp.dot(h, wg_scr[...], preferred_element_type=F32)
    up = jnp.dot(h, wu_scr[...], preferred_element_type=F32)
    o_ref[...] = (_silu(gate) * up).astype(BF16)
    wd_bf_ref[...] = wd_ref[...].astype(BF16)


def _ffn_up(h2, w_gate_up, w_down):
    tm, tf = _FFN_UP_TM, _FFN_UP_TF
    nf = D_FF // tf
    nm = TOK // tm
    slab = D_FF // (nf * nm)
    assert slab * nf * nm == D_FF and slab % 16 == 0
    return pl.pallas_call(
        _ffn_up_kernel,
        grid=(nf, nm),
        in_specs=[pl.BlockSpec((tm, D_MODEL), lambda f, i: (i, 0)),
                  pl.BlockSpec((D_MODEL, tf), lambda f, i: (0, f)),
                  pl.BlockSpec((D_MODEL, tf), lambda f, i: (0, nf + f)),
                  pl.BlockSpec((slab, D_MODEL), lambda f, i: (f * nm + i, 0))],
        out_specs=[pl.BlockSpec((tm, tf), lambda f, i: (i, f)),
                   pl.BlockSpec((slab, D_MODEL), lambda f, i: (f * nm + i, 0))],
        out_shape=[jax.ShapeDtypeStruct((TOK, D_FF), BF16),
                   jax.ShapeDtypeStruct((D_FF, D_MODEL), BF16)],
        scratch_shapes=[pltpu.VMEM((D_MODEL, tf), BF16),
                        pltpu.VMEM((D_MODEL, tf), BF16)],
        compiler_params=pltpu.CompilerParams(
            dimension_semantics=("arbitrary", "arbitrary"), vmem_limit_bytes=VMEM_LIMIT),
        name="ffn_up",
    )(h2, w_gate_up, w_gate_up, w_down)


def _ffn_down_kernel(a_ref, x_ref, g2_ref, fw_ref, wd_ref, o_ref):
    down = jnp.dot(a_ref[...], wd_ref[...], preferred_element_type=F32)
    x2 = x_ref[...] + g2_ref[...] * down
    ms = jnp.mean(x2 * x2, axis=-1, keepdims=True)
    o_ref[...] = x2 * lax.rsqrt(ms + EPS) * fw_ref[...]


def _ffn_down(act, x1, g2, final_w, w_down_bf):
    tm = _FFN_DOWN_TM
    per_b = SEQ // tm
    rows = pl.BlockSpec((tm, D_MODEL), lambda i: (i, 0))
    return pl.pallas_call(
        _ffn_down_kernel,
        grid=(TOK // tm,),
        in_specs=[pl.BlockSpec((tm, D_FF), lambda i: (i, 0)),
                  rows,
                  pl.BlockSpec((None, 1, D_MODEL), lambda i: (i // per_b, 0, 0)),
                  pl.BlockSpec((1, D_MODEL), lambda i: (0, 0)),
                  pl.BlockSpec((D_FF, D_MODEL), lambda i: (0, 0))],
        out_specs=rows,
        out_shape=jax.ShapeDtypeStruct((TOK, D_MODEL), F32),
        compiler_params=pltpu.CompilerParams(
            dimension_semantics=("parallel",), vmem_limit_bytes=VMEM_LIMIT),
        name="ffn_down",
    )(act, x1, g2, final_w, w_down_bf)


def kernel(x, c, w_ada, b_ada, norm1_w, w_in, ret_norm_w, s5_a_re, s5_a_im, s5_log_step,
           s5_b_re, s5_b_im, s5_c_re, s5_c_im, s5_d, w_glu, b_glu, w_out, norm2_w,
           w_gate_up, w_down, final_norm_w):
    x2d = x.reshape(TOK, D_MODEL)
    layer = 0
    mod = _adaln(c, w_ada[layer], b_ada[layer])
    sh1, sc1, g1, sh2, sc2, g2 = [m.reshape(BATCH, 1, D_MODEL) for m in jnp.split(mod, 6, axis=-1)]

    qkvg, u_sb = _inproj(x2d, sc1, sh1, norm1_w[layer].reshape(1, D_MODEL),
                         w_in[layer].astype(BF16))
    ret_out = _retention(qkvg, ret_norm_w[layer].reshape(1, D_RET))

    mt, wt, vt, a16r, a16i = _s5gen(s5_a_re[layer], s5_a_im[layer], s5_log_step[layer],
                                    s5_b_re[layer], s5_b_im[layer], s5_c_re[layer],
                                    s5_c_im[layer], s5_d[layer])
    y_rows = _s5(u_sb.reshape(TOK, D_S5), mt, wt, vt, a16r, a16i)
    ssm_sb = y_rows.reshape(S5_NSUB, BATCH, S5_SUB, D_S5)

    x1, h2 = _mix(ssm_sb, ret_out, x2d, g1, sc2, sh2, norm2_w[layer].reshape(1, D_MODEL),
                  w_glu[layer].astype(BF16), b_glu[layer].reshape(1, D_S5),
                  w_out[layer].astype(BF16))
    act, w_down_bf = _ffn_up(h2, w_gate_up[layer], w_down[layer])
    out = _ffn_down(act, x1, g2, final_norm_w.reshape(1, D_MODEL), w_down_bf)
    return out.reshape(BATCH, SEQ, D_MODEL)
```

```python
import math

import numpy as np
import jax
import jax.numpy as jnp
from jax import lax
from jax.experimental import pallas as pl
from jax.experimental.pallas import tpu as pltpu

D_MODEL = 2048
BATCH = 4
SEQ = 2048
TOK = BATCH * SEQ
D_RET = 1024
D_S5 = 1024
RET_HEADS = 4
RET_HEAD_DIM = 256
RET_CHUNK = 128
S5_GROUP = 16
S5_GROUPS = 64
S5_STATE = 64
S5_SUB = 16
S5_ROW = S5_SUB * S5_GROUP
S5_NSUB = SEQ // S5_SUB
S5_COLS = S5_NSUB * BATCH
D_FF = 5632
ROPE_BASE = 10000.0
EPS = 1e-6
LANES = 128

F32 = jnp.float32
BF16 = jnp.bfloat16
HI = lax.Precision.HIGHEST
VMEM_LIMIT = 58 * 1024 * 1024


def _silu(v):
    return v * jax.nn.sigmoid(v)


def _adaln_kernel(c_ref, w_ref, b_ref, o_ref):
    cond = _silu(c_ref[...])
    o_ref[...] = jnp.dot(cond.astype(BF16), w_ref[...].astype(BF16),
                         preferred_element_type=F32) + b_ref[...]


def _adaln(c, w_ada, b_ada):
    tn = 1024
    n = w_ada.shape[1]
    return pl.pallas_call(
        _adaln_kernel,
        grid=(n // tn,),
        in_specs=[pl.BlockSpec((BATCH, D_MODEL), lambda j: (0, 0)),
                  pl.BlockSpec((D_MODEL, tn), lambda j: (0, j)),
                  pl.BlockSpec((1, tn), lambda j: (0, j))],
        out_specs=pl.BlockSpec((BATCH, tn), lambda j: (0, j)),
        out_shape=jax.ShapeDtypeStruct((BATCH, n), F32),
        compiler_params=pltpu.CompilerParams(
            dimension_semantics=("parallel",), vmem_limit_bytes=VMEM_LIMIT),
        name="adaln",
    )(c, w_ada, b_ada.reshape(1, n))


_INPROJ_TM = 1024
_INPROJ_TN = 1024
_INPROJ_NS = _INPROJ_TM // S5_SUB


def _inproj_kernel(x_ref, sc_ref, sh_ref, nw_ref, w_ref, o_ref, u_ref, h_scr, cos_scr, sin_scr):
    tm = _INPROJ_TM
    half = RET_HEAD_DIM // 2
    per_b = SEQ // tm
    i = pl.program_id(0)
    j = pl.program_id(1)
    slot = i % per_b

    @pl.when(j == 0)
    def _():
        gain = nw_ref[...] * (1.0 + sc_ref[...])
        shift = sh_ref[...]
        rb = 16

        def norm_body(r, carry):
            r0 = pl.multiple_of(r * rb, rb)
            x = x_ref[pl.ds(r0, rb), :]
            ms = jnp.mean(x * x, axis=-1, keepdims=True)
            h_scr[pl.ds(r0, rb), :] = (x * lax.rsqrt(ms + EPS) * gain + shift).astype(BF16)
            return carry

        lax.fori_loop(0, tm // rb, norm_body, 0, unroll=4)

    @pl.when((j == 0) & (i < per_b))
    def _():
        pos = (lax.broadcasted_iota(jnp.int32, (tm, half), 0) + slot * tm).astype(F32)
        lane = lax.broadcasted_iota(jnp.int32, (tm, half), 1).astype(F32)
        ang = pos * jnp.exp(lane * (-math.log(ROPE_BASE) / half))
        cos_scr[slot] = jnp.cos(ang)
        sin_scr[slot] = jnp.sin(ang)

    acc = jnp.dot(h_scr[...], w_ref[...], preferred_element_type=F32)

    rot = (j < 2).astype(F32)
    cs = cos_scr[slot] * rot + (1.0 - rot)
    sn = sin_scr[slot] * rot
    for hh in range(RET_HEADS):
        c0 = hh * RET_HEAD_DIM
        x1 = acc[:, c0:c0 + half]
        x2 = acc[:, c0 + half:c0 + 2 * half]
        o_ref[:, c0:c0 + half] = (x1 * cs - x2 * sn).astype(BF16)
        o_ref[:, c0 + half:c0 + 2 * half] = (x2 * cs + x1 * sn).astype(BF16)
    u_ref[...] = acc.reshape(_INPROJ_NS, S5_SUB, _INPROJ_TN)


def _inproj(x2d, sc1, sh1, norm_w, w_in_bf):
    tm, tn = _INPROJ_TM, _INPROJ_TN
    per_b = SEQ // tm
    half = RET_HEAD_DIM // 2
    return pl.pallas_call(
        _inproj_kernel,
        grid=(TOK // tm, 5),
        in_specs=[pl.BlockSpec((tm, D_MODEL), lambda i, j: (i, 0)),
                  pl.BlockSpec((None, 1, D_MODEL), lambda i, j: (i // per_b, 0, 0)),
                  pl.BlockSpec((None, 1, D_MODEL), lambda i, j: (i // per_b, 0, 0)),
                  pl.BlockSpec((1, D_MODEL), lambda i, j: (0, 0)),
                  pl.BlockSpec((D_MODEL, tn), lambda i, j: (0, j))],
        out_specs=[pl.BlockSpec((tm, tn), lambda i, j: (i, j)),
                   pl.BlockSpec((_INPROJ_NS, None, S5_SUB, D_S5),
                                lambda i, j: (i % per_b, i // per_b, 0, 0))],
        out_shape=[jax.ShapeDtypeStruct((TOK, 4 * D_RET + D_S5), BF16),
                   jax.ShapeDtypeStruct((S5_NSUB, BATCH, S5_SUB, D_S5), F32)],
        scratch_shapes=[pltpu.VMEM((tm, D_MODEL), BF16),
                        pltpu.VMEM((per_b, tm, half), F32),
                        pltpu.VMEM((per_b, tm, half), F32)],
        compiler_params=pltpu.CompilerParams(
            dimension_semantics=("arbitrary", "arbitrary"), vmem_limit_bytes=VMEM_LIMIT),
        name="inproj",
    )(x2d, sc1, sh1, norm_w, w_in_bf)


_RET_BLK = 256


def _ret_kernel(lg_ref, q_ref, k_ref, v_ref, g_ref, w_ref, wout_ref, wglu_ref,
                o_ref, wout_bf_ref, wglu_bf_ref,
                kv_scr, prev_scr, intra_scr, kdec_scr, qdec_scr):
    wout_bf_ref[...] = wout_ref[...].astype(BF16)
    wglu_bf_ref[...] = wglu_ref[...].astype(BF16)
    C = _RET_BLK
    dh = RET_HEAD_DIM
    nc = SEQ // C
    scale = dh ** -0.5
    lg = lg_ref[0:1, :]
    ii = lax.broadcasted_iota(jnp.int32, (C, C), 0)
    jj = lax.broadcasted_iota(jnp.int32, (C, C), 1)
    diff = (ii - jj).astype(F32)
    intra_scr[...] = jnp.where(diff >= 0.0, jnp.exp(lg * jnp.maximum(diff, 0.0)), 0.0) * scale
    row = lax.broadcasted_iota(jnp.int32, (C, dh), 0).astype(F32)
    kdec_scr[...] = jnp.exp(lg * (C - 1.0 - row)) * scale
    qdec_scr[...] = jnp.exp(lg * (row + 1.0))
    block_decay = jnp.exp(lg * float(C))
    gn_w = w_ref[...]

    for n in range(nc):
        rs = slice(n * C, (n + 1) * C)
        kd = (k_ref[rs, :].astype(F32) * kdec_scr[...]).astype(BF16)
        kv_scr[n] = lax.dot_general(kd, v_ref[rs, :], (((0,), (0,)), ((), ())),
                                    preferred_element_type=F32)

    band = 64
    for rb in range(dh // band):
        bs = slice(rb * band, (rb + 1) * band)
        st = jnp.zeros((band, dh), F32)
        for n in range(nc):
            prev_scr[n, bs, :] = st.astype(BF16)
            if n + 1 < nc:
                st = st * block_decay + kv_scr[n, bs, :]

    for n in range(nc):
        rs = slice(n * C, (n + 1) * C)
        q = q_ref[rs, :]
        s = lax.dot_general(q, k_ref[rs, :], (((1,), (1,)), ((), ())),
                            preferred_element_type=F32) * intra_scr[...]
        y = jnp.dot(s.astype(BF16), v_ref[rs, :], preferred_element_type=F32)
        y = y + jnp.dot(q, prev_scr[n], preferred_element_type=F32) * qdec_scr[...]
        mu = jnp.mean(y, axis=-1, keepdims=True)
        yc = y - mu
        var = jnp.mean(yc * yc, axis=-1, keepdims=True)
        yn = yc * lax.rsqrt(var + EPS) * gn_w
        g = g_ref[rs, :].astype(F32)
        o_ref[rs, :] = (_silu(g) * yn).astype(BF16)


def _retention(qkvg, ret_norm_w, w_out, w_glu):
    dh = RET_HEAD_DIM
    nc = SEQ // _RET_BLK
    steps = BATCH * RET_HEADS
    so = w_out.shape[0] // steps
    sg = w_glu.shape[0] // steps
    assert so * steps == w_out.shape[0] and sg * steps == w_glu.shape[0] and sg % 16 == 0
    lg = np.log1p(-np.exp2(-5.0 - np.arange(RET_HEADS, dtype=np.float64)))
    lg_tab = jnp.asarray(np.broadcast_to(lg[:, None, None], (RET_HEADS, 8, dh)), F32)
    spec = lambda off: pl.BlockSpec((SEQ, dh), lambda b, h: (b, off + h))
    slab_o = pl.BlockSpec((so, w_out.shape[1]), lambda b, h: (b * RET_HEADS + h, 0))
    slab_g = pl.BlockSpec((sg, w_glu.shape[1]), lambda b, h: (b * RET_HEADS + h, 0))
    return pl.pallas_call(
        _ret_kernel,
        grid=(BATCH, RET_HEADS),
        in_specs=[pl.BlockSpec((None, 8, dh), lambda b, h: (h, 0, 0)),
                  spec(0), spec(RET_HEADS), spec(2 * RET_HEADS), spec(3 * RET_HEADS),
                  pl.BlockSpec((1, dh), lambda b, h: (0, h)), slab_o, slab_g],
        out_specs=[pl.BlockSpec((SEQ, dh), lambda b, h: (b, h)), slab_o, slab_g],
        out_shape=[jax.ShapeDtypeStruct((TOK, D_RET), BF16),
                   jax.ShapeDtypeStruct(w_out.shape, BF16),
                   jax.ShapeDtypeStruct(w_glu.shape, BF16)],
        scratch_shapes=[pltpu.VMEM((nc, dh, dh), F32),
                        pltpu.VMEM((nc, dh, dh), BF16),
                        pltpu.VMEM((_RET_BLK, _RET_BLK), F32),
                        pltpu.VMEM((_RET_BLK, dh), F32),
                        pltpu.VMEM((_RET_BLK, dh), F32)],
        compiler_params=pltpu.CompilerParams(
            dimension_semantics=("parallel", "parallel"), vmem_limit_bytes=VMEM_LIMIT),
        name="retention",
    )(lg_tab, qkvg, qkvg, qkvg, qkvg, ret_norm_w, w_out, w_glu)


_S5GEN_GG = 4


def _s5gen_kernel(ls_ref, ar_ref, ai_ref, bt_ref, cr_ref, ci_ref, c4r_ref, c4i_ref, d_ref, win_ref,
                  mt_ref, wt_ref, vt_ref, a16r_ref, a16i_ref, win_bf_ref):
    win_bf_ref[...] = win_ref[...].astype(BF16)
    N, P, R = S5_STATE, S5_GROUP, S5_ROW
    re_half = lax.broadcasted_iota(jnp.int32, (P, 2 * N), 1) < N
    sub_e = lax.broadcasted_iota(jnp.int32, (P, 2 * N), 0)
    lane_p = lax.broadcasted_iota(jnp.int32, (P, R), 1)
    sub_p = lax.broadcasted_iota(jnp.int32, (P, R), 0)
    lag0_diag = lane_p == (R - P) + sub_p
    is_re = lane_p < 2 * N
    lane_slot = lax.shift_right_logical(lane_p, 6) & 1

    def hdot(a, b):
        return jnp.dot(a, b, precision=HI, preferred_element_type=F32)

    def cmul(xr, xi, yr, yi):
        return xr * yr - xi * yi, xr * yi + xi * yr

    for k in range(_S5GEN_GG):
        odd = k % 2
        dt = jnp.exp(ls_ref[k])
        ar = ar_ref[k]
        ai = ai_ref[k]
        mag = jnp.exp(ar * dt)
        lbr = mag * jnp.cos(ai * dt)
        lbi = mag * jnp.sin(ai * dt)
        den = ar * ar + ai * ai
        zr = ((lbr - 1.0) * ar + lbi * ai) / den
        zi = (lbi * ar - (lbr - 1.0) * ai) / den

        pw_r = jnp.ones((P, 2 * N), F32)
        pw_i = jnp.zeros((P, 2 * N), F32)
        sq_r, sq_i = lbr, lbi
        for bit in range(4):
            nr, ni = cmul(pw_r, pw_i, sq_r, sq_i)
            take = (lax.shift_right_logical(sub_e, bit) & 1) == 1
            pw_r = jnp.where(take, nr, pw_r)
            pw_i = jnp.where(take, ni, pw_i)
            if bit < 3:
                sq_r, sq_i = cmul(sq_r, sq_i, sq_r, sq_i)
        p1_r, p1_i = cmul(pw_r, pw_i, lbr, lbi)
        a16r_ref[k] = p1_r[S5_SUB - 1:S5_SUB, :]
        a16i_ref[k] = p1_i[S5_SUB - 1:S5_SUB, :]

        bt = bt_ref[k]
        x = zr * bt + jnp.where(re_half[0:1], -zi, zi) * pltpu.roll(bt, N, axis=1)
        xs = pltpu.roll(x, N, axis=1)
        pw_is = jnp.where(re_half, -pw_i, pw_i)
        w_rows = jnp.concatenate(
            [pw_r[e:e + 1] * x + pw_is[e:e + 1] * xs for e in range(S5_SUB - 1, -1, -1)], axis=0)
        w_t = w_rows.T
        w_r = w_t[0:N]
        w_i = w_t[N:2 * N]
        zero = jnp.zeros_like(w_r)
        wt_ref[k, 0:2 * N, :] = jnp.concatenate(
            [zero, w_r] if odd else [w_r, zero], axis=0).astype(BF16)
        wt_ref[k, 2 * N:4 * N, :] = jnp.concatenate(
            [zero, w_i] if odd else [w_i, zero], axis=0).astype(BF16)

        c_re = cr_ref[k]
        c_im = ci_ref[k]
        krev = hdot(c_re, w_r) - hdot(c_im, w_i)
        krev = krev + jnp.where(lag0_diag, d_ref[k], 0.0)
        for tp in range(S5_SUB):
            width = (tp + 1) * P
            piece = krev if width == R else jnp.where(
                lane_p < width, pltpu.roll(krev, width, axis=1), 0.0)
            mt_ref[k, tp * P:(tp + 1) * P, :] = piece.astype(BF16)

        c4_r = c4r_ref[k]
        c4_i = c4i_ref[k]
        keep = lane_slot == odd
        for tp in range(S5_SUB):
            lr = jnp.concatenate([p1_r[tp:tp + 1], p1_r[tp:tp + 1]], axis=1)
            li = jnp.concatenate([p1_i[tp:tp + 1], p1_i[tp:tp + 1]], axis=1)
            val = jnp.where(is_re, c4_r * lr - c4_i * li, -(c4_r * li + c4_i * lr))
            vt_ref[k, tp * P:(tp + 1) * P, :] = jnp.where(keep, val, 0.0).astype(BF16)


def _s5gen(a_re, a_im, log_step, b_re, b_im, c_re, c_im, d_skip, w_in):
    G, N, P, R = S5_GROUPS, S5_STATE, S5_GROUP, S5_ROW
    gg = _S5GEN_GG
    sw = w_in.shape[0] // (G // gg)
    assert sw * (G // gg) == w_in.shape[0] and sw % 16 == 0
    slab_w = pl.BlockSpec((sw, w_in.shape[1]), lambda g: (g, 0))
    dup = lambda a: jnp.tile(a, (1, 2)).reshape(G, 1, 2 * N)
    bt = jnp.concatenate([jnp.swapaxes(b_re, 1, 2), jnp.swapaxes(b_im, 1, 2)], axis=-1)
    tile4 = lambda a: jnp.tile(a, (1, 1, R // N))
    blk = lambda s: pl.BlockSpec((gg,) + s, lambda g: (g, 0, 0))
    return pl.pallas_call(
        _s5gen_kernel,
        grid=(G // gg,),
        in_specs=[blk((1, 1)), blk((1, 2 * N)), blk((1, 2 * N)), blk((P, 2 * N)),
                  blk((P, N)), blk((P, N)), blk((P, R)), blk((P, R)), blk((P, 1)), slab_w],
        out_specs=[blk((R, R)), blk((R, R)), blk((R, R)), blk((1, 2 * N)), blk((1, 2 * N)),
                   slab_w],
        out_shape=[jax.ShapeDtypeStruct((G, R, R), BF16)] * 3 +
                  [jax.ShapeDtypeStruct((G, 1, 2 * N), F32)] * 2 +
                  [jax.ShapeDtypeStruct(w_in.shape, BF16)],
        compiler_params=pltpu.CompilerParams(
            dimension_semantics=("parallel",), vmem_limit_bytes=VMEM_LIMIT),
        name="s5gen",
    )(log_step.reshape(G, 1, 1), dup(a_re), dup(a_im), bt, c_re, c_im, tile4(c_re), tile4(c_im),
      d_skip.reshape(G, P, 1), w_in)


_S5_GB = LANES // S5_GROUP
_S5_NP = _S5_GB // 2


def _s5_kernel(a16r_ref, a16i_ref, u_ref, mt_ref, wt_ref, vt_ref, o_ref,
               ut_scr, yt_scr, e_scr, p_scr):
    P = S5_GROUP
    N2 = 2 * S5_STATE
    cols = S5_COLS
    nblk = cols // LANES
    nt = cols // 8

    def load_body(tau, carry):
        for c in range(nblk):
            x = u_ref[pl.ds(tau + c * LANES * S5_SUB, LANES, stride=S5_SUB), :]
            xt = x.T
            r0 = pl.multiple_of(tau * P, P)
            for k in range(_S5_GB):
                ut_scr[k, pl.ds(r0, P), c * LANES:(c + 1) * LANES] = \
                    xt[k * P:(k + 1) * P, :].astype(BF16)
        return carry

    lax.fori_loop(0, S5_SUB, load_body, 0, unroll=4)

    for pr in range(_S5_NP):
        et = (jnp.dot(wt_ref[2 * pr], ut_scr[2 * pr], preferred_element_type=F32) +
              jnp.dot(wt_ref[2 * pr + 1], ut_scr[2 * pr + 1], preferred_element_type=F32))
        e_scr[pr] = et.T

    sub = lax.broadcasted_iota(jnp.int32, (8, N2), 0)
    lo = sub < BATCH
    first = lax.broadcasted_iota(jnp.int32, (1, N2), 1) < S5_STATE
    a_r, a_i = [], []
    for pr in range(_S5_NP):
        a_r.append(jnp.broadcast_to(
            jnp.where(first, a16r_ref[2 * pr], a16r_ref[2 * pr + 1]), (8, N2)))
        a_i.append(jnp.broadcast_to(
            jnp.where(first, a16i_ref[2 * pr], a16i_ref[2 * pr + 1]), (8, N2)))

    def scan_body(t, carry):
        r0 = pl.multiple_of(t * 8, 8)
        new = []
        for pr in range(_S5_NP):
            s_r, s_i = carry[2 * pr], carry[2 * pr + 1]
            e_r = e_scr[pr, pl.ds(r0, 8), 0:N2]
            e_i = e_scr[pr, pl.ds(r0, 8), N2:2 * N2]
            x_r = pltpu.roll(e_r, BATCH, axis=0)
            x_i = pltpu.roll(e_i, BATCH, axis=0)
            elo_r = jnp.where(lo, e_r, x_r)
            elo_i = jnp.where(lo, e_i, x_i)
            ehi_r = jnp.where(lo, x_r, e_r)
            ehi_i = jnp.where(lo, x_i, e_i)
            t_r = a_r[pr] * s_r - a_i[pr] * s_i + elo_r
            t_i = a_r[pr] * s_i + a_i[pr] * s_r + elo_i
            p_scr[pr, pl.ds(r0, 8), 0:N2] = jnp.where(lo, s_r, t_r)
            p_scr[pr, pl.ds(r0, 8), N2:2 * N2] = jnp.where(lo, s_i, t_i)
            new.append(a_r[pr] * t_r - a_i[pr] * t_i + ehi_r)
            new.append(a_r[pr] * t_i + a_i[pr] * t_r + ehi_i)
        return tuple(new)

    zero = jnp.zeros((8, N2), F32)
    lax.fori_loop(0, nt, scan_body, tuple(zero for _ in range(2 * _S5_NP)))

    for pr in range(_S5_NP):
        pt = p_scr[pr].T.astype(BF16)
        for k in range(2):
            gi = 2 * pr + k
            yt = (jnp.dot(mt_ref[gi], ut_scr[gi], preferred_element_type=F32) +
                  jnp.dot(vt_ref[gi], pt, preferred_element_type=F32))
            for tau in range(S5_SUB):
                yt_scr[tau, gi * P:(gi + 1) * P, :] = yt[tau * P:(tau + 1) * P, :]

    def store_body(tau, carry):
        o_ref[pl.ds(tau, cols, stride=S5_SUB), :] = yt_scr[tau].T
        return carry

    lax.fori_loop(0, S5_SUB, store_body, 0, unroll=4)


def _s5(u_rows, mt, wt, vt, a16r, a16i):
    G, N, R = S5_GROUPS, S5_STATE, S5_ROW
    gb = _S5_GB
    vec = pl.BlockSpec((gb, 1, 2 * N), lambda i: (i, 0, 0))
    mat = pl.BlockSpec((gb, R, R), lambda i: (i, 0, 0))
    slab = pl.BlockSpec((TOK, LANES), lambda i: (0, i))
    return pl.pallas_call(
        _s5_kernel,
        grid=(G // gb,),
        in_specs=[vec, vec, slab, mat, mat, mat],
        out_specs=slab,
        out_shape=jax.ShapeDtypeStruct((TOK, D_S5), F32),
        scratch_shapes=[pltpu.VMEM((gb, R, S5_COLS), BF16),
                        pltpu.VMEM((S5_SUB, LANES, S5_COLS), F32),
                        pltpu.VMEM((_S5_NP, S5_COLS, 4 * N), F32),
                        pltpu.VMEM((_S5_NP, S5_COLS, 4 * N), F32)],
        compiler_params=pltpu.CompilerParams(
            dimension_semantics=("parallel",), vmem_limit_bytes=VMEM_LIMIT),
        name="s5",
    )(a16r, a16i, u_rows, mt, wt, vt)


_MIX_TM = 512
_MIX_NS = _MIX_TM // S5_SUB


def _mix_kernel(ssm_ref, ret_ref, x_ref, g1_ref, sc_ref, sh_ref, nw_ref, wglu_ref, bglu_ref,
                wout_ref, o_ref, h_ref):
    s = ssm_ref[...].reshape(_MIX_TM, D_S5)
    cdf = 0.5 * (1.0 + jnp.tanh(math.sqrt(2.0 / math.pi) * (s + 0.044715 * (s * s * s))))
    sg = s * cdf
    z = jnp.dot(sg.astype(BF16), wglu_ref[...], preferred_element_type=F32) + bglu_ref[...]
    so = sg * jax.nn.sigmoid(z)
    mix = jnp.dot(ret_ref[...], wout_ref[0:D_RET, :], preferred_element_type=F32)
    mix = mix + jnp.dot(so.astype(BF16), wout_ref[D_RET:, :], preferred_element_type=F32)
    x1 = x_ref[...] + g1_ref[...] * mix
    o_ref[...] = x1
    ms = jnp.mean(x1 * x1, axis=-1, keepdims=True)
    y = x1 * lax.rsqrt(ms + EPS) * nw_ref[...]
    h_ref[...] = (y * (1.0 + sc_ref[...]) + sh_ref[...]).astype(BF16)


def _mix(ssm_sb, ret_out, x2d, g1, sc2, sh2, norm_w, w_glu_bf, b_glu, w_out_bf):
    tm = _MIX_TM
    per_b = SEQ // tm
    mod = pl.BlockSpec((None, 1, D_MODEL), lambda i: (i // per_b, 0, 0))
    rows = pl.BlockSpec((tm, D_MODEL), lambda i: (i, 0))
    return pl.pallas_call(
        _mix_kernel,
        grid=(TOK // tm,),
        in_specs=[pl.BlockSpec((_MIX_NS, None, S5_SUB, D_S5),
                               lambda i: (i % per_b, i // per_b, 0, 0)),
                  pl.BlockSpec((tm, D_RET), lambda i: (i, 0)),
                  rows, mod, mod, mod,
                  pl.BlockSpec((1, D_MODEL), lambda i: (0, 0)),
                  pl.BlockSpec((D_S5, D_S5), lambda i: (0, 0)),
                  pl.BlockSpec((1, D_S5), lambda i: (0, 0)),
                  pl.BlockSpec((D_RET + D_S5, D_MODEL), lambda i: (0, 0))],
        out_specs=[rows, rows],
        out_shape=[jax.ShapeDtypeStruct((TOK, D_MODEL), F32),
                   jax.ShapeDtypeStruct((TOK, D_MODEL), BF16)],
        compiler_params=pltpu.CompilerParams(
            dimension_semantics=("parallel",), vmem_limit_bytes=VMEM_LIMIT),
        name="mix",
    )(ssm_sb, ret_out, x2d, g1, sc2, sh2, norm_w, w_glu_bf, b_glu, w_out_bf)


_FFN_UP_TM = 1024
_FFN_UP_TF = 512
_FFN_DOWN_TM = 256


def _ffn_up_kernel(h_ref, wg_ref, wu_ref, wd_ref, o_ref, wd_bf_ref, wg_scr, wu_scr):
    @pl.when(pl.program_id(1) == 0)
    def _():
        wg_scr[...] = wg_ref[...].astype(BF16)
        wu_scr[...] = wu_ref[...].astype(BF16)

    h = h_ref[...]
    gate = jnp.dot(h, wg_scr[...], preferred_element_type=F32)
    up = jnp.dot(h, wu_scr[...], preferred_element_type=F32)
    o_ref[...] = (_silu(gate) * up).astype(BF16)
    wd_bf_ref[...] = wd_ref[...].astype(BF16)


def _ffn_up(h2, w_gate_up, w_down):
    tm, tf = _FFN_UP_TM, _FFN_UP_TF
    nf = D_FF // tf
    nm = TOK // tm
    slab = D_FF // (nf * nm)
    assert slab * nf * nm == D_FF and slab % 16 == 0
    return pl.pallas_call(
        _ffn_up_kernel,
        grid=(nf, nm),
        in_specs=[pl.BlockSpec((tm, D_MODEL), lambda f, i: (i, 0)),
                  pl.BlockSpec((D_MODEL, tf), lambda f, i: (0, f)),
                  pl.BlockSpec((D_MODEL, tf), lambda f, i: (0, nf + f)),
                  pl.BlockSpec((slab, D_MODEL), lambda f, i: (f * nm + i, 0))],
        out_specs=[pl.BlockSpec((tm, tf), lambda f, i: (i, f)),
                   pl.BlockSpec((slab, D_MODEL), lambda f, i: (f * nm + i, 0))],
        out_shape=[jax.ShapeDtypeStruct((TOK, D_FF), BF16),
                   jax.ShapeDtypeStruct((D_FF, D_MODEL), BF16)],
        scratch_shapes=[pltpu.VMEM((D_MODEL, tf), BF16),
                        pltpu.VMEM((D_MODEL, tf), BF16)],
        compiler_params=pltpu.CompilerParams(
            dimension_semantics=("arbitrary", "arbitrary"), vmem_limit_bytes=VMEM_LIMIT),
        name="ffn_up",
    )(h2, w_gate_up, w_gate_up, w_down)


def _ffn_down_kernel(a_ref, x_ref, g2_ref, fw_ref, wd_ref, o_ref):
    down = jnp.dot(a_ref[...], wd_ref[...], preferred_element_type=F32)
    x2 = x_ref[...] + g2_ref[...] * down
    ms = jnp.mean(x2 * x2, axis=-1, keepdims=True)
    o_ref[...] = x2 * lax.rsqrt(ms + EPS) * fw_ref[...]


def _ffn_down(act, x1, g2, final_w, w_down_bf):
    tm = _FFN_DOWN_TM
    per_b = SEQ // tm
    rows = pl.BlockSpec((tm, D_MODEL), lambda i: (i, 0))
    return pl.pallas_call(
        _ffn_down_kernel,
        grid=(TOK // tm,),
        in_specs=[pl.BlockSpec((tm, D_FF), lambda i: (i, 0)),
                  rows,
                  pl.BlockSpec((None, 1, D_MODEL), lambda i: (i // per_b, 0, 0)),
                  pl.BlockSpec((1, D_MODEL), lambda i: (0, 0)),
                  pl.BlockSpec((D_FF, D_MODEL), lambda i: (0, 0))],
        out_specs=rows,
        out_shape=jax.ShapeDtypeStruct((TOK, D_MODEL), F32),
        compiler_params=pltpu.CompilerParams(
            dimension_semantics=("parallel",), vmem_limit_bytes=VMEM_LIMIT),
        name="ffn_down",
    )(act, x1, g2, final_w, w_down_bf)


def kernel(x, c, w_ada, b_ada, norm1_w, w_in, ret_norm_w, s5_a_re, s5_a_im, s5_log_step,
           s5_b_re, s5_b_im, s5_c_re, s5_c_im, s5_d, w_glu, b_glu, w_out, norm2_w,
           w_gate_up, w_down, final_norm_w):
    x2d = x.reshape(TOK, D_MODEL)
    layer = 0
    mod = _adaln(c, w_ada[layer], b_ada[layer])
    sh1, sc1, g1, sh2, sc2, g2 = [m.reshape(BATCH, 1, D_MODEL) for m in jnp.split(mod, 6, axis=-1)]

    mt, wt, vt, a16r, a16i, w_in_bf = _s5gen(
        s5_a_re[layer], s5_a_im[layer], s5_log_step[layer], s5_b_re[layer], s5_b_im[layer],
        s5_c_re[layer], s5_c_im[layer], s5_d[layer], w_in[layer])

    qkvg, u_sb = _inproj(x2d, sc1, sh1, norm1_w[layer].reshape(1, D_MODEL), w_in_bf)
    ret_out, w_out_bf, w_glu_bf = _retention(qkvg, ret_norm_w[layer].reshape(1, D_RET),
                                             w_out[layer], w_glu[layer])
    y_rows = _s5(u_sb.reshape(TOK, D_S5), mt, wt, vt, a16r, a16i)
    ssm_sb = y_rows.reshape(S5_NSUB, BATCH, S5_SUB, D_S5)

    x1, h2 = _mix(ssm_sb, ret_out, x2d, g1, sc2, sh2, norm2_w[layer].reshape(1, D_MODEL),
                  w_glu_bf, b_glu[layer].reshape(1, D_S5), w_out_bf)
    act, w_down_bf = _ffn_up(h2, w_gate_up[layer], w_down[layer])
    out = _ffn_down(act, x1, g2, final_norm_w.reshape(1, D_MODEL), w_down_bf)
    return out.reshape(BATCH, SEQ, D_MODEL)
```

```python
import math

import numpy as np
import jax
import jax.numpy as jnp
from jax import lax
from jax.experimental import pallas as pl
from jax.experimental.pallas import tpu as pltpu

D_MODEL = 2048
BATCH = 4
SEQ = 2048
TOK = BATCH * SEQ
D_RET = 1024
D_S5 = 1024
RET_HEADS = 4
RET_HEAD_DIM = 256
RET_CHUNK = 128
S5_GROUP = 16
S5_GROUPS = 64
S5_STATE = 64
S5_SUB = 16
S5_ROW = S5_SUB * S5_GROUP
S5_NSUB = SEQ // S5_SUB
S5_COLS = S5_NSUB * BATCH
D_FF = 5632
ROPE_BASE = 10000.0
EPS = 1e-6
LANES = 128

F32 = jnp.float32
BF16 = jnp.bfloat16
HI = lax.Precision.HIGHEST
VMEM_LIMIT = 58 * 1024 * 1024


def _silu(v):
    return v * jax.nn.sigmoid(v)


def _adaln_kernel(c_ref, w_ref, b_ref, o_ref):
    cond = _silu(c_ref[...])
    o_ref[...] = jnp.dot(cond.astype(BF16), w_ref[...].astype(BF16),
                         preferred_element_type=F32) + b_ref[...]


def _adaln(c, w_ada, b_ada):
    tn = 1024
    n = w_ada.shape[1]
    return pl.pallas_call(
        _adaln_kernel,
        grid=(n // tn,),
        in_specs=[pl.BlockSpec((BATCH, D_MODEL), lambda j: (0, 0)),
                  pl.BlockSpec((D_MODEL, tn), lambda j: (0, j)),
                  pl.BlockSpec((1, tn), lambda j: (0, j))],
        out_specs=pl.BlockSpec((BATCH, tn), lambda j: (0, j)),
        out_shape=jax.ShapeDtypeStruct((BATCH, n), F32),
        compiler_params=pltpu.CompilerParams(
            dimension_semantics=("parallel",), vmem_limit_bytes=VMEM_LIMIT),
        name="adaln",
    )(c, w_ada, b_ada.reshape(1, n))


_INPROJ_TM = 1024
_INPROJ_TN = 1024
_INPROJ_NS = _INPROJ_TM // S5_SUB


def _inproj_kernel(x_ref, sc_ref, sh_ref, nw_ref, w_ref, o_ref, u_ref, h_scr, cos_scr, sin_scr):
    tm = _INPROJ_TM
    half = RET_HEAD_DIM // 2
    per_b = SEQ // tm
    i = pl.program_id(0)
    j = pl.program_id(1)
    slot = i % per_b

    @pl.when(j == 0)
    def _():
        gain = nw_ref[...] * (1.0 + sc_ref[...])
        shift = sh_ref[...]
        rb = 16

        def norm_body(r, carry):
            r0 = pl.multiple_of(r * rb, rb)
            x = x_ref[pl.ds(r0, rb), :]
            ms = jnp.mean(x * x, axis=-1, keepdims=True)
            h_scr[pl.ds(r0, rb), :] = (x * lax.rsqrt(ms + EPS) * gain + shift).astype(BF16)
            return carry

        lax.fori_loop(0, tm // rb, norm_body, 0, unroll=4)

    @pl.when((j == 0) & (i < per_b))
    def _():
        pos = (lax.broadcasted_iota(jnp.int32, (tm, half), 0) + slot * tm).astype(F32)
        lane = lax.broadcasted_iota(jnp.int32, (tm, half), 1).astype(F32)
        ang = pos * jnp.exp(lane * (-math.log(ROPE_BASE) / half))
        cos_scr[slot] = jnp.cos(ang)
        sin_scr[slot] = jnp.sin(ang)

    acc = jnp.dot(h_scr[...], w_ref[...], preferred_element_type=F32)

    rot = (j < 2).astype(F32)
    cs = cos_scr[slot] * rot + (1.0 - rot)
    sn = sin_scr[slot] * rot
    for hh in range(RET_HEADS):
        c0 = hh * RET_HEAD_DIM
        x1 = acc[:, c0:c0 + half]
        x2 = acc[:, c0 + half:c0 + 2 * half]
        o_ref[:, c0:c0 + half] = (x1 * cs - x2 * sn).astype(BF16)
        o_ref[:, c0 + half:c0 + 2 * half] = (x2 * cs + x1 * sn).astype(BF16)
    u_ref[...] = acc.reshape(_INPROJ_NS, S5_SUB, _INPROJ_TN)


def _inproj(x2d, sc1, sh1, norm_w, w_in_bf):
    tm, tn = _INPROJ_TM, _INPROJ_TN
    per_b = SEQ // tm
    half = RET_HEAD_DIM // 2
    return pl.pallas_call(
        _inproj_kernel,
        grid=(TOK // tm, 5),
        in_specs=[pl.BlockSpec((tm, D_MODEL), lambda i, j: (i, 0)),
                  pl.BlockSpec((None, 1, D_MODEL), lambda i, j: (i // per_b, 0, 0)),
                  pl.BlockSpec((None, 1, D_MODEL), lambda i, j: (i // per_b, 0, 0)),
                  pl.BlockSpec((1, D_MODEL), lambda i, j: (0, 0)),
                  pl.BlockSpec((D_MODEL, tn), lambda i, j: (0, j))],
        out_specs=[pl.BlockSpec((tm, tn), lambda i, j: (i, j)),
                   pl.BlockSpec((_INPROJ_NS, None, S5_SUB, D_S5),
                                lambda i, j: (i % per_b, i // per_b, 0, 0))],
        out_shape=[jax.ShapeDtypeStruct((TOK, 4 * D_RET + D_S5), BF16),
                   jax.ShapeDtypeStruct((S5_NSUB, BATCH, S5_SUB, D_S5), F32)],
        scratch_shapes=[pltpu.VMEM((tm, D_MODEL), BF16),
                        pltpu.VMEM((per_b, tm, half), F32),
                        pltpu.VMEM((per_b, tm, half), F32)],
        compiler_params=pltpu.CompilerParams(
            dimension_semantics=("arbitrary", "arbitrary"), vmem_limit_bytes=VMEM_LIMIT),
        name="inproj",
    )(x2d, sc1, sh1, norm_w, w_in_bf)


_RET_BLK = 256


def _ret_kernel(lg_ref, q_ref, k_ref, v_ref, g_ref, w_ref, wout_ref, wglu_ref,
                o_ref, wout_bf_ref, wglu_bf_ref,
                kv_scr, prev_scr, intra_scr, kdec_scr, qdec_scr):
    wout_bf_ref[...] = wout_ref[...].astype(BF16)
    wglu_bf_ref[...] = wglu_ref[...].astype(BF16)
    C = _RET_BLK
    dh = RET_HEAD_DIM
    nc = SEQ // C
    scale = dh ** -0.5
    lg = lg_ref[0:1, :]
    ii = lax.broadcasted_iota(jnp.int32, (C, C), 0)
    jj = lax.broadcasted_iota(jnp.int32, (C, C), 1)
    diff = (ii - jj).astype(F32)
    intra_scr[...] = jnp.where(diff >= 0.0, jnp.exp(lg * jnp.maximum(diff, 0.0)), 0.0) * scale
    row = lax.broadcasted_iota(jnp.int32, (C, dh), 0).astype(F32)
    kdec_scr[...] = jnp.exp(lg * (C - 1.0 - row)) * scale
    qdec_scr[...] = jnp.exp(lg * (row + 1.0))
    block_decay = jnp.exp(lg * float(C))
    gn_w = w_ref[...]

    for n in range(nc):
        rs = slice(n * C, (n + 1) * C)
        kd = (k_ref[rs, :].astype(F32) * kdec_scr[...]).astype(BF16)
        kv_scr[n] = lax.dot_general(kd, v_ref[rs, :], (((0,), (0,)), ((), ())),
                                    preferred_element_type=F32)

    band = 64
    for rb in range(dh // band):
        bs = slice(rb * band, (rb + 1) * band)
        st = jnp.zeros((band, dh), F32)
        for n in range(nc):
            prev_scr[n, bs, :] = st.astype(BF16)
            if n + 1 < nc:
                st = st * block_decay + kv_scr[n, bs, :]

    for n in range(nc):
        rs = slice(n * C, (n + 1) * C)
        q = q_ref[rs, :]
        s = lax.dot_general(q, k_ref[rs, :], (((1,), (1,)), ((), ())),
                            preferred_element_type=F32) * intra_scr[...]
        y = jnp.dot(s.astype(BF16), v_ref[rs, :], preferred_element_type=F32)
        y = y + jnp.dot(q, prev_scr[n], preferred_element_type=F32) * qdec_scr[...]
        mu = jnp.mean(y, axis=-1, keepdims=True)
        yc = y - mu
        var = jnp.mean(yc * yc, axis=-1, keepdims=True)
        yn = yc * lax.rsqrt(var + EPS) * gn_w
        g = g_ref[rs, :].astype(F32)
        o_ref[rs, :] = (_silu(g) * yn).astype(BF16)


def _retention(qkvg, ret_norm_w, w_out, w_glu):
    dh = RET_HEAD_DIM
    nc = SEQ // _RET_BLK
    steps = BATCH * RET_HEADS
    so = w_out.shape[0] // steps
    sg = w_glu.shape[0] // steps
    assert so * steps == w_out.shape[0] and sg * steps == w_glu.shape[0] and sg % 16 == 0
    lg = np.log1p(-np.exp2(-5.0 - np.arange(RET_HEADS, dtype=np.float64)))
    lg_tab = jnp.asarray(np.broadcast_to(lg[:, None, None], (RET_HEADS, 8, dh)), F32)
    spec = lambda off: pl.BlockSpec((SEQ, dh), lambda b, h: (b, off + h))
    slab_o = pl.BlockSpec((so, w_out.shape[1]), lambda b, h: (b * RET_HEADS + h, 0))
    slab_g = pl.BlockSpec((sg, w_glu.shape[1]), lambda b, h: (b * RET_HEADS + h, 0))
    return pl.pallas_call(
        _ret_kernel,
        grid=(BATCH, RET_HEADS),
        in_specs=[pl.BlockSpec((None, 8, dh), lambda b, h: (h, 0, 0)),
                  spec(0), spec(RET_HEADS), spec(2 * RET_HEADS), spec(3 * RET_HEADS),
                  pl.BlockSpec((1, dh), lambda b, h: (0, h)), slab_o, slab_g],
        out_specs=[pl.BlockSpec((SEQ, dh), lambda b, h: (b, h)), slab_o, slab_g],
        out_shape=[jax.ShapeDtypeStruct((TOK, D_RET), BF16),
                   jax.ShapeDtypeStruct(w_out.shape, BF16),
                   jax.ShapeDtypeStruct(w_glu.shape, BF16)],
        scratch_shapes=[pltpu.VMEM((nc, dh, dh), F32),
                        pltpu.VMEM((nc, dh, dh), BF16),
                        pltpu.VMEM((_RET_BLK, _RET_BLK), F32),
                        pltpu.VMEM((_RET_BLK, dh), F32),
                        pltpu.VMEM((_RET_BLK, dh), F32)],
        compiler_params=pltpu.CompilerParams(
            dimension_semantics=("parallel", "parallel"), vmem_limit_bytes=VMEM_LIMIT),
        name="retention",
    )(lg_tab, qkvg, qkvg, qkvg, qkvg, ret_norm_w, w_out, w_glu)


_S5GEN_GG = 4


def _s5gen_kernel(ls_ref, ar_ref, ai_ref, bt_ref, cr_ref, ci_ref, c4r_ref, c4i_ref, d_ref, win_ref,
                  mt_ref, wt_ref, vt_ref, a16r_ref, a16i_ref, win_bf_ref):
    win_bf_ref[...] = win_ref[...].astype(BF16)
    N, P, R = S5_STATE, S5_GROUP, S5_ROW
    re_half = lax.broadcasted_iota(jnp.int32, (P, 2 * N), 1) < N
    sub_e = lax.broadcasted_iota(jnp.int32, (P, 2 * N), 0)
    lane_p = lax.broadcasted_iota(jnp.int32, (P, R), 1)
    sub_p = lax.broadcasted_iota(jnp.int32, (P, R), 0)
    lag0_diag = lane_p == (R - P) + sub_p
    is_re = lane_p < 2 * N
    lane_slot = lax.shift_right_logical(lane_p, 6) & 1

    def hdot(a, b):
        return jnp.dot(a, b, precision=HI, preferred_element_type=F32)

    def cmul(xr, xi, yr, yi):
        return xr * yr - xi * yi, xr * yi + xi * yr

    for k in range(_S5GEN_GG):
        odd = k % 2
        dt = jnp.exp(ls_ref[k])
        ar = ar_ref[k]
        ai = ai_ref[k]
        mag = jnp.exp(ar * dt)
        lbr = mag * jnp.cos(ai * dt)
        lbi = mag * jnp.sin(ai * dt)
        den = ar * ar + ai * ai
        zr = ((lbr - 1.0) * ar + lbi * ai) / den
        zi = (lbi * ar - (lbr - 1.0) * ai) / den

        pw_r = jnp.ones((P, 2 * N), F32)
        pw_i = jnp.zeros((P, 2 * N), F32)
        sq_r, sq_i = lbr, lbi
        for bit in range(4):
            nr, ni = cmul(pw_r, pw_i, sq_r, sq_i)
            take = (lax.shift_right_logical(sub_e, bit) & 1) == 1
            pw_r = jnp.where(take, nr, pw_r)
            pw_i = jnp.where(take, ni, pw_i)
            if bit < 3:
                sq_r, sq_i = cmul(sq_r, sq_i, sq_r, sq_i)
        p1_r, p1_i = cmul(pw_r, pw_i, lbr, lbi)
        a16r_ref[k] = p1_r[S5_SUB - 1:S5_SUB, :]
        a16i_ref[k] = p1_i[S5_SUB - 1:S5_SUB, :]

        bt = bt_ref[k]
        x = zr * bt + jnp.where(re_half[0:1], -zi, zi) * pltpu.roll(bt, N, axis=1)
        xs = pltpu.roll(x, N, axis=1)
        pw_is = jnp.where(re_half, -pw_i, pw_i)
        w_rows = jnp.concatenate(
            [pw_r[e:e + 1] * x + pw_is[e:e + 1] * xs for e in range(S5_SUB - 1, -1, -1)], axis=0)
        w_t = w_rows.T
        w_r = w_t[0:N]
        w_i = w_t[N:2 * N]
        zero = jnp.zeros_like(w_r)
        wt_ref[k, 0:2 * N, :] = jnp.concatenate(
            [zero, w_r] if odd else [w_r, zero], axis=0).astype(BF16)
        wt_ref[k, 2 * N:4 * N, :] = jnp.concatenate(
            [zero, w_i] if odd else [w_i, zero], axis=0).astype(BF16)

        c_re = cr_ref[k]
        c_im = ci_ref[k]
        krev = hdot(c_re, w_r) - hdot(c_im, w_i)
        krev = krev + jnp.where(lag0_diag, d_ref[k], 0.0)
        for tp in range(S5_SUB):
            width = (tp + 1) * P
            piece = krev if width == R else jnp.where(
                lane_p < width, pltpu.roll(krev, width, axis=1), 0.0)
            mt_ref[k, tp * P:(tp + 1) * P, :] = piece.astype(BF16)

        c4_r = c4r_ref[k]
        c4_i = c4i_ref[k]
        keep = lane_slot == odd
        for tp in range(S5_SUB):
            lr = jnp.concatenate([p1_r[tp:tp + 1], p1_r[tp:tp + 1]], axis=1)
            li = jnp.concatenate([p1_i[tp:tp + 1], p1_i[tp:tp + 1]], axis=1)
            val = jnp.where(is_re, c4_r * lr - c4_i * li, -(c4_r * li + c4_i * lr))
            vt_ref[k, tp * P:(tp + 1) * P, :] = jnp.where(keep, val, 0.0).astype(BF16)


def _s5gen(a_re, a_im, log_step, b_re, b_im, c_re, c_im, d_skip, w_in):
    G, N, P, R = S5_GROUPS, S5_STATE, S5_GROUP, S5_ROW
    gg = _S5GEN_GG
    sw = w_in.shape[0] // (G // gg)
    assert sw * (G // gg) == w_in.shape[0] and sw % 16 == 0
    slab_w = pl.BlockSpec((sw, w_in.shape[1]), lambda g: (g, 0))
    dup = lambda a: jnp.tile(a, (1, 2)).reshape(G, 1, 2 * N)
    bt = jnp.concatenate([jnp.swapaxes(b_re, 1, 2), jnp.swapaxes(b_im, 1, 2)], axis=-1)
    tile4 = lambda a: jnp.tile(a, (1, 1, R // N))
    blk = lambda s: pl.BlockSpec((gg,) + s, lambda g: (g, 0, 0))
    return pl.pallas_call(
        _s5gen_kernel,
        grid=(G // gg,),
        in_specs=[blk((1, 1)), blk((1, 2 * N)), blk((1, 2 * N)), blk((P, 2 * N)),
                  blk((P, N)), blk((P, N)), blk((P, R)), blk((P, R)), blk((P, 1)), slab_w],
        out_specs=[blk((R, R)), blk((R, R)), blk((R, R)), blk((1, 2 * N)), blk((1, 2 * N)),
                   slab_w],
        out_shape=[jax.ShapeDtypeStruct((G, R, R), BF16)] * 3 +
                  [jax.ShapeDtypeStruct((G, 1, 2 * N), F32)] * 2 +
                  [jax.ShapeDtypeStruct(w_in.shape, BF16)],
        compiler_params=pltpu.CompilerParams(
            dimension_semantics=("parallel",), vmem_limit_bytes=VMEM_LIMIT),
        name="s5gen",
    )(log_step.reshape(G, 1, 1), dup(a_re), dup(a_im), bt, c_re, c_im, tile4(c_re), tile4(c_im),
      d_skip.reshape(G, P, 1), w_in)


_S5_GB = LANES // S5_GROUP
_S5_NP = _S5_GB // 2


def _s5_kernel(a16r_ref, a16i_ref, u_ref, mt_ref, wt_ref, vt_ref, o_ref,
               ut_scr, yt_scr, e_scr, p_scr):
    P = S5_GROUP
    N2 = 2 * S5_STATE
    cols = S5_COLS
    nblk = cols // LANES
    nt = cols // 8

    def load_body(tau, carry):
        for c in range(nblk):
            x = u_ref[pl.ds(tau + c * LANES * S5_SUB, LANES, stride=S5_SUB), :]
            xt = x.T
            r0 = pl.multiple_of(tau * P, P)
            for k in range(_S5_GB):
                ut_scr[k, pl.ds(r0, P), c * LANES:(c + 1) * LANES] = \
                    xt[k * P:(k + 1) * P, :].astype(BF16)
        return carry

    lax.fori_loop(0, S5_SUB, load_body, 0, unroll=4)

    for pr in range(_S5_NP):
        et = (jnp.dot(wt_ref[2 * pr], ut_scr[2 * pr], preferred_element_type=F32) +
              jnp.dot(wt_ref[2 * pr + 1], ut_scr[2 * pr + 1], preferred_element_type=F32))
        e_scr[pr] = et.T

    sub = lax.broadcasted_iota(jnp.int32, (8, N2), 0)
    lo = sub < BATCH
    first = lax.broadcasted_iota(jnp.int32, (1, N2), 1) < S5_STATE
    a_r, a_i = [], []
    for pr in range(_S5_NP):
        a_r.append(jnp.broadcast_to(
            jnp.where(first, a16r_ref[2 * pr], a16r_ref[2 * pr + 1]), (8, N2)))
        a_i.append(jnp.broadcast_to(
            jnp.where(first, a16i_ref[2 * pr], a16i_ref[2 * pr + 1]), (8, N2)))

    def scan_body(t, carry):
        r0 = pl.multiple_of(t * 8, 8)
        new = []
        for pr in range(_S5_NP):
            s_r, s_i = carry[2 * pr], carry[2 * pr + 1]
            e_r = e_scr[pr, pl.ds(r0, 8), 0:N2]
            e_i = e_scr[pr, pl.ds(r0, 8), N2:2 * N2]
            x_r = pltpu.roll(e_r, BATCH, axis=0)
            x_i = pltpu.roll(e_i, BATCH, axis=0)
            elo_r = jnp.where(lo, e_r, x_r)
            elo_i = jnp.where(lo, e_i, x_i)
            ehi_r = jnp.where(lo, x_r, e_r)
            ehi_i = jnp.where(lo, x_i, e_i)
            t_r = a_r[pr] * s_r - a_i[pr] * s_i + elo_r
            t_i = a_r[pr] * s_i + a_i[pr] * s_r + elo_i
            p_scr[pr, pl.ds(r0, 8), 0:N2] = jnp.where(lo, s_r, t_r)
            p_scr[pr, pl.ds(r0, 8), N2:2 * N2] = jnp.where(lo, s_i, t_i)
            new.append(a_r[pr] * t_r - a_i[pr] * t_i + ehi_r)
            new.append(a_r[pr] * t_i + a_i[pr] * t_r + ehi_i)
        return tuple(new)

    zero = jnp.zeros((8, N2), F32)
    lax.fori_loop(0, nt, scan_body, tuple(zero for _ in range(2 * _S5_NP)))

    for pr in range(_S5_NP):
        pt = p_scr[pr].T.astype(BF16)
        for k in range(2):
            gi = 2 * pr + k
            yt = (jnp.dot(mt_ref[gi], ut_scr[gi], preferred_element_type=F32) +
                  jnp.dot(vt_ref[gi], pt, preferred_element_type=F32))
            for tau in range(S5_SUB):
                yt_scr[tau, gi * P:(gi + 1) * P, :] = yt[tau * P:(tau + 1) * P, :]

    def store_body(tau, carry):
        o_ref[pl.ds(tau, cols, stride=S5_SUB), :] = yt_scr[tau].T
        return carry

    lax.fori_loop(0, S5_SUB, store_body, 0, unroll=4)


def _s5(u_rows, mt, wt, vt, a16r, a16i):
    G, N, R = S5_GROUPS, S5_STATE, S5_ROW
    gb = _S5_GB
    vec = pl.BlockSpec((gb, 1, 2 * N), lambda i: (i, 0, 0))
    mat = pl.BlockSpec((gb, R, R), lambda i: (i, 0, 0))
    slab = pl.BlockSpec((TOK, LANES), lambda i: (0, i))
    return pl.pallas_call(
        _s5_kernel,
        grid=(G // gb,),
        in_specs=[vec, vec, slab, mat, mat, mat],
        out_specs=slab,
        out_shape=jax.ShapeDtypeStruct((TOK, D_S5), F32),
        scratch_shapes=[pltpu.VMEM((gb, R, S5_COLS), BF16),
                        pltpu.VMEM((S5_SUB, LANES, S5_COLS), F32),
                        pltpu.VMEM((_S5_NP, S5_COLS, 4 * N), F32),
                        pltpu.VMEM((_S5_NP, S5_COLS, 4 * N), F32)],
        compiler_params=pltpu.CompilerParams(
            dimension_semantics=("parallel",), vmem_limit_bytes=VMEM_LIMIT),
        name="s5",
    )(a16r, a16i, u_rows, mt, wt, vt)


_MIX_TM = 512
_MIX_NS = _MIX_TM // S5_SUB


def _mix_kernel(ssm_ref, ret_ref, x_ref, g1_ref, sc_ref, sh_ref, nw_ref, wglu_ref, bglu_ref,
                wout_ref, o_ref, h_ref):
    s = ssm_ref[...].reshape(_MIX_TM, D_S5)
    cdf = 0.5 * (1.0 + jnp.tanh(math.sqrt(2.0 / math.pi) * (s + 0.044715 * (s * s * s))))
    sg = s * cdf
    z = jnp.dot(sg.astype(BF16), wglu_ref[...], preferred_element_type=F32) + bglu_ref[...]
    so = sg * jax.nn.sigmoid(z)
    mix = jnp.dot(ret_ref[...], wout_ref[0:D_RET, :], preferred_element_type=F32)
    mix = mix + jnp.dot(so.astype(BF16), wout_ref[D_RET:, :], preferred_element_type=F32)
    x1 = x_ref[...] + g1_ref[...] * mix
    o_ref[...] = x1
    ms = jnp.mean(x1 * x1, axis=-1, keepdims=True)
    y = x1 * lax.rsqrt(ms + EPS) * nw_ref[...]
    h_ref[...] = (y * (1.0 + sc_ref[...]) + sh_ref[...]).astype(BF16)


def _mix(ssm_sb, ret_out, x2d, g1, sc2, sh2, norm_w, w_glu_bf, b_glu, w_out_bf):
    tm = _MIX_TM
    per_b = SEQ // tm
    mod = pl.BlockSpec((None, 1, D_MODEL), lambda i: (i // per_b, 0, 0))
    rows = pl.BlockSpec((tm, D_MODEL), lambda i: (i, 0))
    return pl.pallas_call(
        _mix_kernel,
        grid=(TOK // tm,),
        in_specs=[pl.BlockSpec((_MIX_NS, None, S5_SUB, D_S5),
                               lambda i: (i % per_b, i // per_b, 0, 0)),
                  pl.BlockSpec((tm, D_RET), lambda i: (i, 0)),
                  rows, mod, mod, mod,
                  pl.BlockSpec((1, D_MODEL), lambda i: (0, 0)),
                  pl.BlockSpec((D_S5, D_S5), lambda i: (0, 0)),
                  pl.BlockSpec((1, D_S5), lambda i: (0, 0)),
                  pl.BlockSpec((D_RET + D_S5, D_MODEL), lambda i: (0, 0))],
        out_specs=[rows, rows],
        out_shape=[jax.ShapeDtypeStruct((TOK, D_MODEL), F32),
                   jax.ShapeDtypeStruct((TOK, D_MODEL), BF16)],
        compiler_params=pltpu.CompilerParams(
            dimension_semantics=("parallel",), vmem_limit_bytes=VMEM_LIMIT),
        name="mix",
    )(ssm_sb, ret_out, x2d, g1, sc2, sh2, norm_w, w_glu_bf, b_glu, w_out_bf)


_FFN_UP_TM = 2048
_FFN_UP_SUB = 1024
_FFN_UP_TF = 512
_FFN_DOWN_TM = 512


def _ffn_up_kernel(h_ref, wg_ref, wu_ref, wd_ref, o_ref, wd_bf_ref, wg_scr, wu_scr):
    @pl.when(pl.program_id(1) == 0)
    def _():
        wg_scr[...] = wg_ref[...].astype(BF16)
        wu_scr[...] = wu_ref[...].astype(BF16)

    wd_bf_ref[...] = wd_ref[...].astype(BF16)
    for r in range(_FFN_UP_TM // _FFN_UP_SUB):
        rs = slice(r * _FFN_UP_SUB, (r + 1) * _FFN_UP_SUB)
        h = h_ref[rs, :]
        gate = jnp.dot(h, wg_scr[...], preferred_element_type=F32)
        up = jnp.dot(h, wu_scr[...], preferred_element_type=F32)
        o_ref[rs, :] = (_silu(gate) * up).astype(BF16)


def _ffn_up(h2, w_gate_up, w_down):
    tm, tf = _FFN_UP_TM, _FFN_UP_TF
    nf = D_FF // tf
    nm = TOK // tm
    slab = D_FF // (nf * nm)
    assert slab * nf * nm == D_FF and slab % 16 == 0
    return pl.pallas_call(
        _ffn_up_kernel,
        grid=(nf, nm),
        in_specs=[pl.BlockSpec((tm, D_MODEL), lambda f, i: (i, 0)),
                  pl.BlockSpec((D_MODEL, tf), lambda f, i: (0, f)),
                  pl.BlockSpec((D_MODEL, tf), lambda f, i: (0, nf + f)),
                  pl.BlockSpec((slab, D_MODEL), lambda f, i: (f * nm + i, 0))],
        out_specs=[pl.BlockSpec((tm, tf), lambda f, i: (i, f)),
                   pl.BlockSpec((slab, D_MODEL), lambda f, i: (f * nm + i, 0))],
        out_shape=[jax.ShapeDtypeStruct((TOK, D_FF), BF16),
                   jax.ShapeDtypeStruct((D_FF, D_MODEL), BF16)],
        scratch_shapes=[pltpu.VMEM((D_MODEL, tf), BF16),
                        pltpu.VMEM((D_MODEL, tf), BF16)],
        compiler_params=pltpu.CompilerParams(
            dimension_semantics=("arbitrary", "arbitrary"), vmem_limit_bytes=VMEM_LIMIT),
        name="ffn_up",
    )(h2, w_gate_up, w_gate_up, w_down)


def _ffn_down_kernel(a_ref, x_ref, g2_ref, fw_ref, wd_ref, o_ref):
    down = jnp.dot(a_ref[...], wd_ref[...], preferred_element_type=F32)
    x2 = x_ref[...] + g2_ref[...] * down
    ms = jnp.mean(x2 * x2, axis=-1, keepdims=True)
    o_ref[...] = x2 * lax.rsqrt(ms + EPS) * fw_ref[...]


def _ffn_down(act, x1, g2, final_w, w_down_bf):
    tm = _FFN_DOWN_TM
    per_b = SEQ // tm
    rows = pl.BlockSpec((tm, D_MODEL), lambda i: (i, 0))
    return pl.pallas_call(
        _ffn_down_kernel,
        grid=(TOK // tm,),
        in_specs=[pl.BlockSpec((tm, D_FF), lambda i: (i, 0)),
                  rows,
                  pl.BlockSpec((None, 1, D_MODEL), lambda i: (i // per_b, 0, 0)),
                  pl.BlockSpec((1, D_MODEL), lambda i: (0, 0)),
                  pl.BlockSpec((D_FF, D_MODEL), lambda i: (0, 0))],
        out_specs=rows,
        out_shape=jax.ShapeDtypeStruct((TOK, D_MODEL), F32),
        compiler_params=pltpu.CompilerParams(
            dimension_semantics=("parallel",), vmem_limit_bytes=VMEM_LIMIT),
        name="ffn_down",
    )(act, x1, g2, final_w, w_down_bf)


def kernel(x, c, w_ada, b_ada, norm1_w, w_in, ret_norm_w, s5_a_re, s5_a_im, s5_log_step,
           s5_b_re, s5_b_im, s5_c_re, s5_c_im, s5_d, w_glu, b_glu, w_out, norm2_w,
           w_gate_up, w_down, final_norm_w):
    x2d = x.reshape(TOK, D_MODEL)
    layer = 0
    mod = _adaln(c, w_ada[layer], b_ada[layer])
    sh1, sc1, g1, sh2, sc2, g2 = [m.reshape(BATCH, 1, D_MODEL) for m in jnp.split(mod, 6, axis=-1)]

    mt, wt, vt, a16r, a16i, w_in_bf = _s5gen(
        s5_a_re[layer], s5_a_im[layer], s5_log_step[layer], s5_b_re[layer], s5_b_im[layer],
        s5_c_re[layer], s5_c_im[layer], s5_d[layer], w_in[layer])

    qkvg, u_sb = _inproj(x2d, sc1, sh1, norm1_w[layer].reshape(1, D_MODEL), w_in_bf)
    ret_out, w_out_bf, w_glu_bf = _retention(qkvg, ret_norm_w[layer].reshape(1, D_RET),
                                             w_out[layer], w_glu[layer])
    y_rows = _s5(u_sb.reshape(TOK, D_S5), mt, wt, vt, a16r, a16i)
    ssm_sb = y_rows.reshape(S5_NSUB, BATCH, S5_SUB, D_S5)

    x1, h2 = _mix(ssm_sb, ret_out, x2d, g1, sc2, sh2, norm2_w[layer].reshape(1, D_MODEL),
                  w_glu_bf, b_glu[layer].reshape(1, D_S5), w_out_bf)
    act, w_down_bf = _ffn_up(h2, w_gate_up[layer], w_down[layer])
    out = _ffn_down(act, x1, g2, final_norm_w.reshape(1, D_MODEL), w_down_bf)
    return out.reshape(BATCH, SEQ, D_MODEL)
```

```python
import math

import numpy as np
import jax
import jax.numpy as jnp
from jax import lax
from jax.experimental import pallas as pl
from jax.experimental.pallas import tpu as pltpu

D_MODEL = 2048
BATCH = 4
SEQ = 2048
TOK = BATCH * SEQ
D_RET = 1024
D_S5 = 1024
RET_HEADS = 4
RET_HEAD_DIM = 256
RET_CHUNK = 128
S5_GROUP = 16
S5_GROUPS = 64
S5_STATE = 64
S5_SUB = 16
S5_ROW = S5_SUB * S5_GROUP
S5_NSUB = SEQ // S5_SUB
S5_COLS = S5_NSUB * BATCH
D_FF = 5632
ROPE_BASE = 10000.0
EPS = 1e-6
LANES = 128

F32 = jnp.float32
BF16 = jnp.bfloat16
HI = lax.Precision.HIGHEST
VMEM_LIMIT = 58 * 1024 * 1024


def _silu(v):
    return v * jax.nn.sigmoid(v)


def _adaln_kernel(c_ref, w_ref, b_ref, o_ref):
    cond = _silu(c_ref[...])
    o_ref[...] = jnp.dot(cond.astype(BF16), w_ref[...].astype(BF16),
                         preferred_element_type=F32) + b_ref[...]


def _adaln(c, w_ada, b_ada):
    tn = 1024
    n = w_ada.shape[1]
    return pl.pallas_call(
        _adaln_kernel,
        grid=(n // tn,),
        in_specs=[pl.BlockSpec((BATCH, D_MODEL), lambda j: (0, 0)),
                  pl.BlockSpec((D_MODEL, tn), lambda j: (0, j)),
                  pl.BlockSpec((1, tn), lambda j: (0, j))],
        out_specs=pl.BlockSpec((BATCH, tn), lambda j: (0, j)),
        out_shape=jax.ShapeDtypeStruct((BATCH, n), F32),
        compiler_params=pltpu.CompilerParams(
            dimension_semantics=("parallel",), vmem_limit_bytes=VMEM_LIMIT),
        name="adaln",
    )(c, w_ada, b_ada.reshape(1, n))


_INPROJ_TM = 1024
_INPROJ_TN = 1024
_INPROJ_NS = _INPROJ_TM // S5_SUB


def _inproj_kernel(x_ref, sc_ref, sh_ref, nw_ref, w_ref, o_ref, u_ref, h_scr, cos_scr, sin_scr):
    tm = _INPROJ_TM
    half = RET_HEAD_DIM // 2
    per_b = SEQ // tm
    i = pl.program_id(0)
    j = pl.program_id(1)
    slot = i % per_b

    @pl.when(j == 0)
    def _():
        gain = nw_ref[...] * (1.0 + sc_ref[...])
        shift = sh_ref[...]
        rb = 16

        def norm_body(r, carry):
            r0 = pl.multiple_of(r * rb, rb)
            x = x_ref[pl.ds(r0, rb), :]
            ms = jnp.mean(x * x, axis=-1, keepdims=True)
            h_scr[pl.ds(r0, rb), :] = (x * lax.rsqrt(ms + EPS) * gain + shift).astype(BF16)
            return carry

        lax.fori_loop(0, tm // rb, norm_body, 0, unroll=4)

    @pl.when((j == 0) & (i < per_b))
    def _():
        pos = (lax.broadcasted_iota(jnp.int32, (tm, half), 0) + slot * tm).astype(F32)
        lane = lax.broadcasted_iota(jnp.int32, (tm, half), 1).astype(F32)
        ang = pos * jnp.exp(lane * (-math.log(ROPE_BASE) / half))
        cos_scr[slot] = jnp.cos(ang)
        sin_scr[slot] = jnp.sin(ang)

    acc = jnp.dot(h_scr[...], w_ref[...], preferred_element_type=F32)

    rot = (j < 2).astype(F32)
    cs = cos_scr[slot] * rot + (1.0 - rot)
    sn = sin_scr[slot] * rot
    for hh in range(RET_HEADS):
        c0 = hh * RET_HEAD_DIM
        x1 = acc[:, c0:c0 + half]
        x2 = acc[:, c0 + half:c0 + 2 * half]
        o_ref[:, c0:c0 + half] = (x1 * cs - x2 * sn).astype(BF16)
        o_ref[:, c0 + half:c0 + 2 * half] = (x2 * cs + x1 * sn).astype(BF16)
    u_ref[...] = acc.reshape(_INPROJ_NS, S5_SUB, _INPROJ_TN)


def _inproj(x2d, sc1, sh1, norm_w, w_in_bf):
    tm, tn = _INPROJ_TM, _INPROJ_TN
    per_b = SEQ // tm
    half = RET_HEAD_DIM // 2
    return pl.pallas_call(
        _inproj_kernel,
        grid=(TOK // tm, 5),
        in_specs=[pl.BlockSpec((tm, D_MODEL), lambda i, j: (i, 0)),
                  pl.BlockSpec((None, 1, D_MODEL), lambda i, j: (i // per_b, 0, 0)),
                  pl.BlockSpec((None, 1, D_MODEL), lambda i, j: (i // per_b, 0, 0)),
                  pl.BlockSpec((1, D_MODEL), lambda i, j: (0, 0)),
                  pl.BlockSpec((D_MODEL, tn), lambda i, j: (0, j))],
        out_specs=[pl.BlockSpec((tm, tn), lambda i, j: (i, j)),
                   pl.BlockSpec((_INPROJ_NS, None, S5_SUB, D_S5),
                                lambda i, j: (i % per_b, i // per_b, 0, 0))],
        out_shape=[jax.ShapeDtypeStruct((TOK, 4 * D_RET + D_S5), BF16),
                   jax.ShapeDtypeStruct((S5_NSUB, BATCH, S5_SUB, D_S5), F32)],
        scratch_shapes=[pltpu.VMEM((tm, D_MODEL), BF16),
                        pltpu.VMEM((per_b, tm, half), F32),
                        pltpu.VMEM((per_b, tm, half), F32)],
        compiler_params=pltpu.CompilerParams(
            dimension_semantics=("arbitrary", "arbitrary"), vmem_limit_bytes=VMEM_LIMIT),
        name="inproj",
    )(x2d, sc1, sh1, norm_w, w_in_bf)


_RET_BLK = 256


def _ret_kernel(lg_ref, q_ref, k_ref, v_ref, g_ref, w_ref, wout_ref, wglu_ref,
                o_ref, wout_bf_ref, wglu_bf_ref,
                kv_scr, prev_scr, intra_scr, kdec_scr, qdec_scr):
    wout_bf_ref[...] = wout_ref[...].astype(BF16)
    wglu_bf_ref[...] = wglu_ref[...].astype(BF16)
    C = _RET_BLK
    dh = RET_HEAD_DIM
    nc = SEQ // C
    scale = dh ** -0.5
    lg = lg_ref[0:1, :]
    ii = lax.broadcasted_iota(jnp.int32, (C, C), 0)
    jj = lax.broadcasted_iota(jnp.int32, (C, C), 1)
    diff = (ii - jj).astype(F32)
    intra_scr[...] = jnp.where(diff >= 0.0, jnp.exp(lg * jnp.maximum(diff, 0.0)), 0.0) * scale
    row = lax.broadcasted_iota(jnp.int32, (C, dh), 0).astype(F32)
    kdec_scr[...] = jnp.exp(lg * (C - 1.0 - row)) * scale
    qdec_scr[...] = jnp.exp(lg * (row + 1.0))
    block_decay = jnp.exp(lg * float(C))
    gn_w = w_ref[...]

    for n in range(nc):
        rs = slice(n * C, (n + 1) * C)
        kd = (k_ref[rs, :].astype(F32) * kdec_scr[...]).astype(BF16)
        kv_scr[n] = lax.dot_general(kd, v_ref[rs, :], (((0,), (0,)), ((), ())),
                                    preferred_element_type=F32)

    band = 64
    for rb in range(dh // band):
        bs = slice(rb * band, (rb + 1) * band)
        st = jnp.zeros((band, dh), F32)
        for n in range(nc):
            prev_scr[n, bs, :] = st.astype(BF16)
            if n + 1 < nc:
                st = st * block_decay + kv_scr[n, bs, :]

    for n in range(nc):
        rs = slice(n * C, (n + 1) * C)
        q = q_ref[rs, :]
        s = lax.dot_general(q, k_ref[rs, :], (((1,), (1,)), ((), ())),
                            preferred_element_type=F32) * intra_scr[...]
        y = jnp.dot(s.astype(BF16), v_ref[rs, :], preferred_element_type=F32)
        y = y + jnp.dot(q, prev_scr[n], preferred_element_type=F32) * qdec_scr[...]
        mu = jnp.mean(y, axis=-1, keepdims=True)
        yc = y - mu
        var = jnp.mean(yc * yc, axis=-1, keepdims=True)
        yn = yc * lax.rsqrt(var + EPS) * gn_w
        g = g_ref[rs, :].astype(F32)
        o_ref[rs, :] = (_silu(g) * yn).astype(BF16)


def _retention(qkvg, ret_norm_w, w_out, w_glu):
    dh = RET_HEAD_DIM
    nc = SEQ // _RET_BLK
    steps = BATCH * RET_HEADS
    so = w_out.shape[0] // steps
    sg = w_glu.shape[0] // steps
    assert so * steps == w_out.shape[0] and sg * steps == w_glu.shape[0] and sg % 16 == 0
    lg = np.log1p(-np.exp2(-5.0 - np.arange(RET_HEADS, dtype=np.float64)))
    lg_tab = jnp.asarray(np.broadcast_to(lg[:, None, None], (RET_HEADS, 8, dh)), F32)
    spec = lambda off: pl.BlockSpec((SEQ, dh), lambda b, h: (b, off + h))
    slab_o = pl.BlockSpec((so, w_out.shape[1]), lambda b, h: (b * RET_HEADS + h, 0))
    slab_g = pl.BlockSpec((sg, w_glu.shape[1]), lambda b, h: (b * RET_HEADS + h, 0))
    return pl.pallas_call(
        _ret_kernel,
        grid=(BATCH, RET_HEADS),
        in_specs=[pl.BlockSpec((None, 8, dh), lambda b, h: (h, 0, 0)),
                  spec(0), spec(RET_HEADS), spec(2 * RET_HEADS), spec(3 * RET_HEADS),
                  pl.BlockSpec((1, dh), lambda b, h: (0, h)), slab_o, slab_g],
        out_specs=[pl.BlockSpec((SEQ, dh), lambda b, h: (b, h)), slab_o, slab_g],
        out_shape=[jax.ShapeDtypeStruct((TOK, D_RET), BF16),
                   jax.ShapeDtypeStruct(w_out.shape, BF16),
                   jax.ShapeDtypeStruct(w_glu.shape, BF16)],
        scratch_shapes=[pltpu.VMEM((nc, dh, dh), F32),
                        pltpu.VMEM((nc, dh, dh), BF16),
                        pltpu.VMEM((_RET_BLK, _RET_BLK), F32),
                        pltpu.VMEM((_RET_BLK, dh), F32),
                        pltpu.VMEM((_RET_BLK, dh), F32)],
        compiler_params=pltpu.CompilerParams(
            dimension_semantics=("parallel", "parallel"), vmem_limit_bytes=VMEM_LIMIT),
        name="retention",
    )(lg_tab, qkvg, qkvg, qkvg, qkvg, ret_norm_w, w_out, w_glu)


_S5GEN_GG = 4


def _s5gen_kernel(ls_ref, ar_ref, ai_ref, bt_ref, cr_ref, ci_ref, c4r_ref, c4i_ref, d_ref, win_ref,
                  mt_ref, wt_ref, vt_ref, a16r_ref, a16i_ref, win_bf_ref):
    win_bf_ref[...] = win_ref[...].astype(BF16)
    N, P, R = S5_STATE, S5_GROUP, S5_ROW
    re_half = lax.broadcasted_iota(jnp.int32, (P, 2 * N), 1) < N
    sub_e = lax.broadcasted_iota(jnp.int32, (P, 2 * N), 0)
    lane_p = lax.broadcasted_iota(jnp.int32, (P, R), 1)
    sub_p = lax.broadcasted_iota(jnp.int32, (P, R), 0)
    lag0_diag = lane_p == (R - P) + sub_p
    is_re = lane_p < 2 * N
    lane_slot = lax.shift_right_logical(lane_p, 6) & 1

    def hdot(a, b):
        return jnp.dot(a, b, precision=HI, preferred_element_type=F32)

    def cmul(xr, xi, yr, yi):
        return xr * yr - xi * yi, xr * yi + xi * yr

    for k in range(_S5GEN_GG):
        odd = k % 2
        dt = jnp.exp(ls_ref[k])
        ar = ar_ref[k]
        ai = ai_ref[k]
        mag = jnp.exp(ar * dt)
        lbr = mag * jnp.cos(ai * dt)
        lbi = mag * jnp.sin(ai * dt)
        den = ar * ar + ai * ai
        zr = ((lbr - 1.0) * ar + lbi * ai) / den
        zi = (lbi * ar - (lbr - 1.0) * ai) / den

        pw_r = jnp.ones((P, 2 * N), F32)
        pw_i = jnp.zeros((P, 2 * N), F32)
        sq_r, sq_i = lbr, lbi
        for bit in range(4):
            nr, ni = cmul(pw_r, pw_i, sq_r, sq_i)
            take = (lax.shift_right_logical(sub_e, bit) & 1) == 1
            pw_r = jnp.where(take, nr, pw_r)
            pw_i = jnp.where(take, ni, pw_i)
            if bit < 3:
                sq_r, sq_i = cmul(sq_r, sq_i, sq_r, sq_i)
        p1_r, p1_i = cmul(pw_r, pw_i, lbr, lbi)
        a16r_ref[k] = p1_r[S5_SUB - 1:S5_SUB, :]
        a16i_ref[k] = p1_i[S5_SUB - 1:S5_SUB, :]

        bt = bt_ref[k]
        x = zr * bt + jnp.where(re_half[0:1], -zi, zi) * pltpu.roll(bt, N, axis=1)
        xs = pltpu.roll(x, N, axis=1)
        pw_is = jnp.where(re_half, -pw_i, pw_i)
        w_rows = jnp.concatenate(
            [pw_r[e:e + 1] * x + pw_is[e:e + 1] * xs for e in range(S5_SUB - 1, -1, -1)], axis=0)
        w_t = w_rows.T
        w_r = w_t[0:N]
        w_i = w_t[N:2 * N]
        zero = jnp.zeros_like(w_r)
        wt_ref[k, 0:2 * N, :] = jnp.concatenate(
            [zero, w_r] if odd else [w_r, zero], axis=0).astype(BF16)
        wt_ref[k, 2 * N:4 * N, :] = jnp.concatenate(
            [zero, w_i] if odd else [w_i, zero], axis=0).astype(BF16)

        c_re = cr_ref[k]
        c_im = ci_ref[k]
        krev = hdot(c_re, w_r) - hdot(c_im, w_i)
        krev = krev + jnp.where(lag0_diag, d_ref[k], 0.0)
        for tp in range(S5_SUB):
            width = (tp + 1) * P
            piece = krev if width == R else jnp.where(
                lane_p < width, pltpu.roll(krev, width, axis=1), 0.0)
            mt_ref[k, tp * P:(tp + 1) * P, :] = piece.astype(BF16)

        c4_r = c4r_ref[k]
        c4_i = c4i_ref[k]
        keep = lane_slot == odd
        for tp in range(S5_SUB):
            lr = jnp.concatenate([p1_r[tp:tp + 1], p1_r[tp:tp + 1]], axis=1)
            li = jnp.concatenate([p1_i[tp:tp + 1], p1_i[tp:tp + 1]], axis=1)
            val = jnp.where(is_re, c4_r * lr - c4_i * li, -(c4_r * li + c4_i * lr))
            vt_ref[k, tp * P:(tp + 1) * P, :] = jnp.where(keep, val, 0.0).astype(BF16)


def _s5gen(a_re, a_im, log_step, b_re, b_im, c_re, c_im, d_skip, w_in):
    G, N, P, R = S5_GROUPS, S5_STATE, S5_GROUP, S5_ROW
    gg = _S5GEN_GG
    sw = w_in.shape[0] // (G // gg)
    assert sw * (G // gg) == w_in.shape[0] and sw % 16 == 0
    slab_w = pl.BlockSpec((sw, w_in.shape[1]), lambda g: (g, 0))
    dup = lambda a: jnp.tile(a, (1, 2)).reshape(G, 1, 2 * N)
    bt = jnp.concatenate([jnp.swapaxes(b_re, 1, 2), jnp.swapaxes(b_im, 1, 2)], axis=-1)
    tile4 = lambda a: jnp.tile(a, (1, 1, R // N))
    blk = lambda s: pl.BlockSpec((gg,) + s, lambda g: (g, 0, 0))
    return pl.pallas_call(
        _s5gen_kernel,
        grid=(G // gg,),
        in_specs=[blk((1, 1)), blk((1, 2 * N)), blk((1, 2 * N)), blk((P, 2 * N)),
                  blk((P, N)), blk((P, N)), blk((P, R)), blk((P, R)), blk((P, 1)), slab_w],
        out_specs=[blk((R, R)), blk((R, R)), blk((R, R)), blk((1, 2 * N)), blk((1, 2 * N)),
                   slab_w],
        out_shape=[jax.ShapeDtypeStruct((G, R, R), BF16)] * 3 +
                  [jax.ShapeDtypeStruct((G, 1, 2 * N), F32)] * 2 +
                  [jax.ShapeDtypeStruct(w_in.shape, BF16)],
        compiler_params=pltpu.CompilerParams(
            dimension_semantics=("parallel",), vmem_limit_bytes=VMEM_LIMIT),
        name="s5gen",
    )(log_step.reshape(G, 1, 1), dup(a_re), dup(a_im), bt, c_re, c_im, tile4(c_re), tile4(c_im),
      d_skip.reshape(G, P, 1), w_in)


_S5_GB = LANES // S5_GROUP
_S5_NP = _S5_GB // 2


def _s5_kernel(a16r_ref, a16i_ref, u_ref, mt_ref, wt_ref, vt_ref, o_ref,
               ut_scr, yt_scr, e_scr, p_scr):
    P = S5_GROUP
    N2 = 2 * S5_STATE
    cols = S5_COLS
    nblk = cols // LANES
    nt = cols // 8

    def load_body(tau, carry):
        for c in range(nblk):
            x = u_ref[pl.ds(tau + c * LANES * S5_SUB, LANES, stride=S5_SUB), :]
            xt = x.T
            r0 = pl.multiple_of(tau * P, P)
            for k in range(_S5_GB):
                ut_scr[k, pl.ds(r0, P), c * LANES:(c + 1) * LANES] = \
                    xt[k * P:(k + 1) * P, :].astype(BF16)
        return carry

    lax.fori_loop(0, S5_SUB, load_body, 0, unroll=4)

    for pr in range(_S5_NP):
        et = (jnp.dot(wt_ref[2 * pr], ut_scr[2 * pr], preferred_element_type=F32) +
              jnp.dot(wt_ref[2 * pr + 1], ut_scr[2 * pr + 1], preferred_element_type=F32))
        e_scr[pr] = et.T

    sub = lax.broadcasted_iota(jnp.int32, (8, N2), 0)
    lo = sub < BATCH
    first = lax.broadcasted_iota(jnp.int32, (1, N2), 1) < S5_STATE
    a_r, a_i = [], []
    for pr in range(_S5_NP):
        a_r.append(jnp.broadcast_to(
            jnp.where(first, a16r_ref[2 * pr], a16r_ref[2 * pr + 1]), (8, N2)))
        a_i.append(jnp.broadcast_to(
            jnp.where(first, a16i_ref[2 * pr], a16i_ref[2 * pr + 1]), (8, N2)))

    def scan_body(t, carry):
        r0 = pl.multiple_of(t * 8, 8)
        new = []
        for pr in range(_S5_NP):
            s_r, s_i = carry[2 * pr], carry[2 * pr + 1]
            e_r = e_scr[pr, pl.ds(r0, 8), 0:N2]
            e_i = e_scr[pr, pl.ds(r0, 8), N2:2 * N2]
            x_r = pltpu.roll(e_r, BATCH, axis=0)
            x_i = pltpu.roll(e_i, BATCH, axis=0)
            elo_r = jnp.where(lo, e_r, x_r)
            elo_i = jnp.where(lo, e_i, x_i)
            ehi_r = jnp.where(lo, x_r, e_r)
            ehi_i = jnp.where(lo, x_i, e_i)
            t_r = a_r[pr] * s_r - a_i[pr] * s_i + elo_r
            t_i = a_r[pr] * s_i + a_i[pr] * s_r + elo_i
            p_scr[pr, pl.ds(r0, 8), 0:N2] = jnp.where(lo, s_r, t_r)
            p_scr[pr, pl.ds(r0, 8), N2:2 * N2] = jnp.where(lo, s_i, t_i)
            new.append(a_r[pr] * t_r - a_i[pr] * t_i + ehi_r)
            new.append(a_r[pr] * t_i + a_i[pr] * t_r + ehi_i)
        return tuple(new)

    zero = jnp.zeros((8, N2), F32)
    lax.fori_loop(0, nt, scan_body, tuple(zero for _ in range(2 * _S5_NP)))

    for pr in range(_S5_NP):
        pt = p_scr[pr].T.astype(BF16)
        for k in range(2):
            gi = 2 * pr + k
            yt = (jnp.dot(mt_ref[gi], ut_scr[gi], preferred_element_type=F32) +
                  jnp.dot(vt_ref[gi], pt, preferred_element_type=F32))
            for tau in range(S5_SUB):
                yt_scr[tau, gi * P:(gi + 1) * P, :] = yt[tau * P:(tau + 1) * P, :]

    def store_body(tau, carry):
        o_ref[pl.ds(tau, cols, stride=S5_SUB), :] = yt_scr[tau].T
        return carry

    lax.fori_loop(0, S5_SUB, store_body, 0, unroll=4)


def _s5(u_rows, mt, wt, vt, a16r, a16i):
    G, N, R = S5_GROUPS, S5_STATE, S5_ROW
    gb = _S5_GB
    vec = pl.BlockSpec((gb, 1, 2 * N), lambda i: (i, 0, 0))
    mat = pl.BlockSpec((gb, R, R), lambda i: (i, 0, 0))
    slab = pl.BlockSpec((TOK, LANES), lambda i: (0, i))
    return pl.pallas_call(
        _s5_kernel,
        grid=(G // gb,),
        in_specs=[vec, vec, slab, mat, mat, mat],
        out_specs=slab,
        out_shape=jax.ShapeDtypeStruct((TOK, D_S5), F32),
        scratch_shapes=[pltpu.VMEM((gb, R, S5_COLS), BF16),
                        pltpu.VMEM((S5_SUB, LANES, S5_COLS), F32),
                        pltpu.VMEM((_S5_NP, S5_COLS, 4 * N), F32),
                        pltpu.VMEM((_S5_NP, S5_COLS, 4 * N), F32)],
        compiler_params=pltpu.CompilerParams(
            dimension_semantics=("parallel",), vmem_limit_bytes=VMEM_LIMIT),
        name="s5",
    )(a16r, a16i, u_rows, mt, wt, vt)


_MIX_TM = 512
_MIX_SUB = 256
_MIX_NS = _MIX_TM // S5_SUB


def _mix_kernel(ssm_ref, ret_ref, x_ref, g1_ref, sc_ref, sh_ref, nw_ref, wglu_ref, bglu_ref,
                wout_ref, o_ref, h_ref):
    gain = nw_ref[...] * (1.0 + sc_ref[...])
    for r in range(_MIX_TM // _MIX_SUB):
        rs = slice(r * _MIX_SUB, (r + 1) * _MIX_SUB)
        ns = _MIX_SUB // S5_SUB
        s = ssm_ref[r * ns:(r + 1) * ns].reshape(_MIX_SUB, D_S5)
        cdf = 0.5 * (1.0 + jnp.tanh(math.sqrt(2.0 / math.pi) * (s + 0.044715 * (s * s * s))))
        sg = s * cdf
        z = jnp.dot(sg.astype(BF16), wglu_ref[...], preferred_element_type=F32) + bglu_ref[...]
        so = sg * jax.nn.sigmoid(z)
        mix = jnp.dot(ret_ref[rs, :], wout_ref[0:D_RET, :], preferred_element_type=F32)
        mix = mix + jnp.dot(so.astype(BF16), wout_ref[D_RET:, :], preferred_element_type=F32)
        x1 = x_ref[rs, :] + g1_ref[...] * mix
        o_ref[rs, :] = x1
        ms = jnp.mean(x1 * x1, axis=-1, keepdims=True)
        h_ref[rs, :] = (x1 * lax.rsqrt(ms + EPS) * gain + sh_ref[...]).astype(BF16)


def _mix(ssm_sb, ret_out, x2d, g1, sc2, sh2, norm_w, w_glu_bf, b_glu, w_out_bf):
    tm = _MIX_TM
    per_b = SEQ // tm
    mod = pl.BlockSpec((None, 1, D_MODEL), lambda i: (i // per_b, 0, 0))
    rows = pl.BlockSpec((tm, D_MODEL), lambda i: (i, 0))
    return pl.pallas_call(
        _mix_kernel,
        grid=(TOK // tm,),
        in_specs=[pl.BlockSpec((_MIX_NS, None, S5_SUB, D_S5),
                               lambda i: (i % per_b, i // per_b, 0, 0)),
                  pl.BlockSpec((tm, D_RET), lambda i: (i, 0)),
                  rows, mod, mod, mod,
                  pl.BlockSpec((1, D_MODEL), lambda i: (0, 0)),
                  pl.BlockSpec((D_S5, D_S5), lambda i: (0, 0)),
                  pl.BlockSpec((1, D_S5), lambda i: (0, 0)),
                  pl.BlockSpec((D_RET + D_S5, D_MODEL), lambda i: (0, 0))],
        out_specs=[rows, rows],
        out_shape=[jax.ShapeDtypeStruct((TOK, D_MODEL), F32),
                   jax.ShapeDtypeStruct((TOK, D_MODEL), BF16)],
        compiler_params=pltpu.CompilerParams(
            dimension_semantics=("parallel",), vmem_limit_bytes=VMEM_LIMIT),
        name="mix",
    )(ssm_sb, ret_out, x2d, g1, sc2, sh2, norm_w, w_glu_bf, b_glu, w_out_bf)


_FFN_UP_TM = 2048
_FFN_UP_SUB = 1024
_FFN_UP_TF = 512
_FFN_DOWN_TM = 512
_FFN_DOWN_SUB = 256


def _ffn_up_kernel(h_ref, wg_ref, wu_ref, wd_ref, o_ref, wd_bf_ref, wg_scr, wu_scr):
    @pl.when(pl.program_id(1) == 0)
    def _():
        wg_scr[...] = wg_ref[...].astype(BF16)
        wu_scr[...] = wu_ref[...].astype(BF16)

    wd_bf_ref[...] = wd_ref[...].astype(BF16)
    for r in range(_FFN_UP_TM // _FFN_UP_SUB):
        rs = slice(r * _FFN_UP_SUB, (r + 1) * _FFN_UP_SUB)
        h = h_ref[rs, :]
        gate = jnp.dot(h, wg_scr[...], preferred_element_type=F32)
        up = jnp.dot(h, wu_scr[...], preferred_element_type=F32)
        o_ref[rs, :] = (_silu(gate) * up).astype(BF16)


def _ffn_up(h2, w_gate_up, w_down):
    tm, tf = _FFN_UP_TM, _FFN_UP_TF
    nf = D_FF // tf
    nm = TOK // tm
    slab = D_FF // (nf * nm)
    assert slab * nf * nm == D_FF and slab % 16 == 0
    return pl.pallas_call(
        _ffn_up_kernel,
        grid=(nf, nm),
        in_specs=[pl.BlockSpec((tm, D_MODEL), lambda f, i: (i, 0)),
                  pl.BlockSpec((D_MODEL, tf), lambda f, i: (0, f)),
                  pl.BlockSpec((D_MODEL, tf), lambda f, i: (0, nf + f)),
                  pl.BlockSpec((slab, D_MODEL), lambda f, i: (f * nm + i, 0))],
        out_specs=[pl.BlockSpec((tm, tf), lambda f, i: (i, f)),
                   pl.BlockSpec((slab, D_MODEL), lambda f, i: (f * nm + i, 0))],
        out_shape=[jax.ShapeDtypeStruct((TOK, D_FF), BF16),
                   jax.ShapeDtypeStruct((D_FF, D_MODEL), BF16)],
        scratch_shapes=[pltpu.VMEM((D_MODEL, tf), BF16),
                        pltpu.VMEM((D_MODEL, tf), BF16)],
        compiler_params=pltpu.CompilerParams(
            dimension_semantics=("arbitrary", "arbitrary"), vmem_limit_bytes=VMEM_LIMIT),
        name="ffn_up",
    )(h2, w_gate_up, w_gate_up, w_down)


def _ffn_down_kernel(a_ref, x_ref, g2_ref, fw_ref, wd_ref, o_ref):
    def sub_tile(r, carry):
        rs = pl.ds(pl.multiple_of(r * _FFN_DOWN_SUB, _FFN_DOWN_SUB), _FFN_DOWN_SUB)
        down = jnp.dot(a_ref[rs, :], wd_ref[...], preferred_element_type=F32)
        x2 = x_ref[rs, :] + g2_ref[...] * down
        ms = jnp.mean(x2 * x2, axis=-1, keepdims=True)
        o_ref[rs, :] = x2 * lax.rsqrt(ms + EPS) * fw_ref[...]
        return carry

    lax.fori_loop(0, _FFN_DOWN_TM // _FFN_DOWN_SUB, sub_tile, 0)


def _ffn_down(act, x1, g2, final_w, w_down_bf):
    tm = _FFN_DOWN_TM
    per_b = SEQ // tm
    rows = pl.BlockSpec((tm, D_MODEL), lambda i: (i, 0))
    return pl.pallas_call(
        _ffn_down_kernel,
        grid=(TOK // tm,),
        in_specs=[pl.BlockSpec((tm, D_FF), lambda i: (i, 0)),
                  rows,
                  pl.BlockSpec((None, 1, D_MODEL), lambda i: (i // per_b, 0, 0)),
                  pl.BlockSpec((1, D_MODEL), lambda i: (0, 0)),
                  pl.BlockSpec((D_FF, D_MODEL), lambda i: (0, 0))],
        out_specs=rows,
        out_shape=jax.ShapeDtypeStruct((TOK, D_MODEL), F32),
        compiler_params=pltpu.CompilerParams(
            dimension_semantics=("parallel",), vmem_limit_bytes=VMEM_LIMIT),
        name="ffn_down",
    )(act, x1, g2, final_w, w_down_bf)


def kernel(x, c, w_ada, b_ada, norm1_w, w_in, ret_norm_w, s5_a_re, s5_a_im, s5_log_step,
           s5_b_re, s5_b_im, s5_c_re, s5_c_im, s5_d, w_glu, b_glu, w_out, norm2_w,
           w_gate_up, w_down, final_norm_w):
    x2d = x.reshape(TOK, D_MODEL)
    layer = 0
    mod = _adaln(c, w_ada[layer], b_ada[layer])
    sh1, sc1, g1, sh2, sc2, g2 = [m.reshape(BATCH, 1, D_MODEL) for m in jnp.split(mod, 6, axis=-1)]

    mt, wt, vt, a16r, a16i, w_in_bf = _s5gen(
        s5_a_re[layer], s5_a_im[layer], s5_log_step[layer], s5_b_re[layer], s5_b_im[layer],
        s5_c_re[layer], s5_c_im[layer], s5_d[layer], w_in[layer])

    qkvg, u_sb = _inproj(x2d, sc1, sh1, norm1_w[layer].reshape(1, D_MODEL), w_in_bf)
    ret_out, w_out_bf, w_glu_bf = _retention(qkvg, ret_norm_w[layer].reshape(1, D_RET),
                                             w_out[layer], w_glu[layer])
    y_rows = _s5(u_sb.reshape(TOK, D_S5), mt, wt, vt, a16r, a16i)
    ssm_sb = y_rows.reshape(S5_NSUB, BATCH, S5_SUB, D_S5)

    x1, h2 = _mix(ssm_sb, ret_out, x2d, g1, sc2, sh2, norm2_w[layer].reshape(1, D_MODEL),
                  w_glu_bf, b_glu[layer].reshape(1, D_S5), w_out_bf)
    act, w_down_bf = _ffn_up(h2, w_gate_up[layer], w_down[layer])
    out = _ffn_down(act, x1, g2, final_norm_w.reshape(1, D_MODEL), w_down_bf)
    return out.reshape(BATCH, SEQ, D_MODEL)
```

```python
import math

import numpy as np
import jax
import jax.numpy as jnp
from jax import lax
from jax.experimental import pallas as pl
from jax.experimental.pallas import tpu as pltpu

D_MODEL = 2048
BATCH = 4
SEQ = 2048
TOK = BATCH * SEQ
D_RET = 1024
D_S5 = 1024
RET_HEADS = 4
RET_HEAD_DIM = 256
RET_CHUNK = 128
S5_GROUP = 16
S5_GROUPS = 64
S5_STATE = 64
S5_SUB = 16
S5_ROW = S5_SUB * S5_GROUP
S5_NSUB = SEQ // S5_SUB
S5_COLS = S5_NSUB * BATCH
D_FF = 5632
ROPE_BASE = 10000.0
EPS = 1e-6
LANES = 128

F32 = jnp.float32
BF16 = jnp.bfloat16
HI = lax.Precision.HIGHEST
VMEM_LIMIT = 58 * 1024 * 1024


def _silu(v):
    return v * jax.nn.sigmoid(v)


_INPROJ_TM = 1024
_INPROJ_TN = 1024
_INPROJ_NS = _INPROJ_TM // S5_SUB


def _inproj_kernel(x_ref, sc_ref, sh_ref, nw_ref, w_ref, o_ref, u_ref, h_scr, cos_scr, sin_scr):
    tm = _INPROJ_TM
    half = RET_HEAD_DIM // 2
    per_b = SEQ // tm
    i = pl.program_id(0)
    j = pl.program_id(1)
    slot = i % per_b

    @pl.when(j == 0)
    def _():
        gain = nw_ref[...] * (1.0 + sc_ref[...])
        shift = sh_ref[...]
        rb = 16

        def norm_body(r, carry):
            r0 = pl.multiple_of(r * rb, rb)
            x = x_ref[pl.ds(r0, rb), :]
            ms = jnp.mean(x * x, axis=-1, keepdims=True)
            h_scr[pl.ds(r0, rb), :] = (x * lax.rsqrt(ms + EPS) * gain + shift).astype(BF16)
            return carry

        lax.fori_loop(0, tm // rb, norm_body, 0, unroll=4)

    @pl.when((j == 0) & (i < per_b))
    def _():
        pos = (lax.broadcasted_iota(jnp.int32, (tm, half), 0) + slot * tm).astype(F32)
        lane = lax.broadcasted_iota(jnp.int32, (tm, half), 1).astype(F32)
        ang = pos * jnp.exp(lane * (-math.log(ROPE_BASE) / half))
        cos_scr[slot] = jnp.cos(ang)
        sin_scr[slot] = jnp.sin(ang)

    acc = jnp.dot(h_scr[...], w_ref[...], preferred_element_type=F32)

    rot = (j < 2).astype(F32)
    cs = cos_scr[slot] * rot + (1.0 - rot)
    sn = sin_scr[slot] * rot
    for hh in range(RET_HEADS):
        c0 = hh * RET_HEAD_DIM
        x1 = acc[:, c0:c0 + half]
        x2 = acc[:, c0 + half:c0 + 2 * half]
        o_ref[:, c0:c0 + half] = (x1 * cs - x2 * sn).astype(BF16)
        o_ref[:, c0 + half:c0 + 2 * half] = (x2 * cs + x1 * sn).astype(BF16)
    u_ref[...] = acc.reshape(_INPROJ_NS, S5_SUB, _INPROJ_TN)


def _inproj(x2d, sc1, sh1, norm_w, w_in_bf):
    tm, tn = _INPROJ_TM, _INPROJ_TN
    per_b = SEQ // tm
    half = RET_HEAD_DIM // 2
    return pl.pallas_call(
        _inproj_kernel,
        grid=(TOK // tm, 5),
        in_specs=[pl.BlockSpec((tm, D_MODEL), lambda i, j: (i, 0)),
                  pl.BlockSpec((None, 1, D_MODEL), lambda i, j: (i // per_b, 0, 0)),
                  pl.BlockSpec((None, 1, D_MODEL), lambda i, j: (i // per_b, 0, 0)),
                  pl.BlockSpec((1, D_MODEL), lambda i, j: (0, 0)),
                  pl.BlockSpec((D_MODEL, tn), lambda i, j: (0, j))],
        out_specs=[pl.BlockSpec((tm, tn), lambda i, j: (i, j)),
                   pl.BlockSpec((_INPROJ_NS, None, S5_SUB, D_S5),
                                lambda i, j: (i % per_b, i // per_b, 0, 0))],
        out_shape=[jax.ShapeDtypeStruct((TOK, 4 * D_RET + D_S5), BF16),
                   jax.ShapeDtypeStruct((S5_NSUB, BATCH, S5_SUB, D_S5), F32)],
        scratch_shapes=[pltpu.VMEM((tm, D_MODEL), BF16),
                        pltpu.VMEM((per_b, tm, half), F32),
                        pltpu.VMEM((per_b, tm, half), F32)],
        compiler_params=pltpu.CompilerParams(
            dimension_semantics=("arbitrary", "arbitrary"), vmem_limit_bytes=VMEM_LIMIT),
        name="inproj",
    )(x2d, sc1, sh1, norm_w, w_in_bf)


_RET_BLK = 256


def _ret_kernel(lg_ref, q_ref, k_ref, v_ref, g_ref, w_ref, wout_ref, wglu_ref,
                o_ref, wout_bf_ref, wglu_bf_ref,
                kv_scr, prev_scr, intra_scr, kdec_scr, qdec_scr):
    wout_bf_ref[...] = wout_ref[...].astype(BF16)
    wglu_bf_ref[...] = wglu_ref[...].astype(BF16)
    C = _RET_BLK
    dh = RET_HEAD_DIM
    nc = SEQ // C
    scale = dh ** -0.5
    lg = lg_ref[0:1, :]
    ii = lax.broadcasted_iota(jnp.int32, (C, C), 0)
    jj = lax.broadcasted_iota(jnp.int32, (C, C), 1)
    diff = (ii - jj).astype(F32)
    intra_scr[...] = jnp.where(diff >= 0.0, jnp.exp(lg * jnp.maximum(diff, 0.0)), 0.0) * scale
    row = lax.broadcasted_iota(jnp.int32, (C, dh), 0).astype(F32)
    kdec_scr[...] = jnp.exp(lg * (C - 1.0 - row)) * scale
    qdec_scr[...] = jnp.exp(lg * (row + 1.0))
    block_decay = jnp.exp(lg * float(C))
    gn_w = w_ref[...]

    for n in range(nc):
        rs = slice(n * C, (n + 1) * C)
        kd = (k_ref[rs, :].astype(F32) * kdec_scr[...]).astype(BF16)
        kv_scr[n] = lax.dot_general(kd, v_ref[rs, :], (((0,), (0,)), ((), ())),
                                    preferred_element_type=F32)

    band = 64
    for rb in range(dh // band):
        bs = slice(rb * band, (rb + 1) * band)
        st = jnp.zeros((band, dh), F32)
        for n in range(nc):
            prev_scr[n, bs, :] = st.astype(BF16)
            if n + 1 < nc:
                st = st * block_decay + kv_scr[n, bs, :]

    for n in range(nc):
        rs = slice(n * C, (n + 1) * C)
        q = q_ref[rs, :]
        s = lax.dot_general(q, k_ref[rs, :], (((1,), (1,)), ((), ())),
                            preferred_element_type=F32) * intra_scr[...]
        y = jnp.dot(s.astype(BF16), v_ref[rs, :], preferred_element_type=F32)
        y = y + jnp.dot(q, prev_scr[n], preferred_element_type=F32) * qdec_scr[...]
        mu = jnp.mean(y, axis=-1, keepdims=True)
        yc = y - mu
        var = jnp.mean(yc * yc, axis=-1, keepdims=True)
        yn = yc * lax.rsqrt(var + EPS) * gn_w
        g = g_ref[rs, :].astype(F32)
        o_ref[rs, :] = (_silu(g) * yn).astype(BF16)


def _retention(qkvg, ret_norm_w, w_out, w_glu):
    dh = RET_HEAD_DIM
    nc = SEQ // _RET_BLK
    steps = BATCH * RET_HEADS
    so = w_out.shape[0] // steps
    sg = w_glu.shape[0] // steps
    assert so * steps == w_out.shape[0] and sg * steps == w_glu.shape[0] and sg % 16 == 0
    lg = np.log1p(-np.exp2(-5.0 - np.arange(RET_HEADS, dtype=np.float64)))
    lg_tab = jnp.asarray(np.broadcast_to(lg[:, None, None], (RET_HEADS, 8, dh)), F32)
    spec = lambda off: pl.BlockSpec((SEQ, dh), lambda b, h: (b, off + h))
    slab_o = pl.BlockSpec((so, w_out.shape[1]), lambda b, h: (b * RET_HEADS + h, 0))
    slab_g = pl.BlockSpec((sg, w_glu.shape[1]), lambda b, h: (b * RET_HEADS + h, 0))
    return pl.pallas_call(
        _ret_kernel,
        grid=(BATCH, RET_HEADS),
        in_specs=[pl.BlockSpec((None, 8, dh), lambda b, h: (h, 0, 0)),
                  spec(0), spec(RET_HEADS), spec(2 * RET_HEADS), spec(3 * RET_HEADS),
                  pl.BlockSpec((1, dh), lambda b, h: (0, h)), slab_o, slab_g],
        out_specs=[pl.BlockSpec((SEQ, dh), lambda b, h: (b, h)), slab_o, slab_g],
        out_shape=[jax.ShapeDtypeStruct((TOK, D_RET), BF16),
                   jax.ShapeDtypeStruct(w_out.shape, BF16),
                   jax.ShapeDtypeStruct(w_glu.shape, BF16)],
        scratch_shapes=[pltpu.VMEM((nc, dh, dh), F32),
                        pltpu.VMEM((nc, dh, dh), BF16),
                        pltpu.VMEM((_RET_BLK, _RET_BLK), F32),
                        pltpu.VMEM((_RET_BLK, dh), F32),
                        pltpu.VMEM((_RET_BLK, dh), F32)],
        compiler_params=pltpu.CompilerParams(
            dimension_semantics=("parallel", "parallel"), vmem_limit_bytes=VMEM_LIMIT),
        name="retention",
    )(lg_tab, qkvg, qkvg, qkvg, qkvg, ret_norm_w, w_out, w_glu)


_S5GEN_GG = 4


def _prep_kernel(c_ref, wada_ref, bada_ref,
                 ls_ref, ar_ref, ai_ref, bt_ref, cr_ref, ci_ref, c4r_ref, c4i_ref, d_ref, win_ref,
                 mod_ref, mt_ref, wt_ref, vt_ref, a16r_ref, a16i_ref, win_bf_ref):
    cond = _silu(c_ref[...])
    mod_ref[...] = jnp.dot(cond.astype(BF16), wada_ref[...].astype(BF16),
                           preferred_element_type=F32) + bada_ref[...]
    win_bf_ref[...] = win_ref[...].astype(BF16)
    N, P, R = S5_STATE, S5_GROUP, S5_ROW
    re_half = lax.broadcasted_iota(jnp.int32, (P, 2 * N), 1) < N
    sub_e = lax.broadcasted_iota(jnp.int32, (P, 2 * N), 0)
    lane_p = lax.broadcasted_iota(jnp.int32, (P, R), 1)
    sub_p = lax.broadcasted_iota(jnp.int32, (P, R), 0)
    lag0_diag = lane_p == (R - P) + sub_p
    is_re = lane_p < 2 * N
    lane_slot = lax.shift_right_logical(lane_p, 6) & 1

    def hdot(a, b):
        return jnp.dot(a, b, precision=HI, preferred_element_type=F32)

    def cmul(xr, xi, yr, yi):
        return xr * yr - xi * yi, xr * yi + xi * yr

    for k in range(_S5GEN_GG):
        odd = k % 2
        dt = jnp.exp(ls_ref[k])
        ar = ar_ref[k]
        ai = ai_ref[k]
        mag = jnp.exp(ar * dt)
        lbr = mag * jnp.cos(ai * dt)
        lbi = mag * jnp.sin(ai * dt)
        den = ar * ar + ai * ai
        zr = ((lbr - 1.0) * ar + lbi * ai) / den
        zi = (lbi * ar - (lbr - 1.0) * ai) / den

        pw_r = jnp.ones((P, 2 * N), F32)
        pw_i = jnp.zeros((P, 2 * N), F32)
        sq_r, sq_i = lbr, lbi
        for bit in range(4):
            nr, ni = cmul(pw_r, pw_i, sq_r, sq_i)
            take = (lax.shift_right_logical(sub_e, bit) & 1) == 1
            pw_r = jnp.where(take, nr, pw_r)
            pw_i = jnp.where(take, ni, pw_i)
            if bit < 3:
                sq_r, sq_i = cmul(sq_r, sq_i, sq_r, sq_i)
        p1_r, p1_i = cmul(pw_r, pw_i, lbr, lbi)
        a16r_ref[k] = p1_r[S5_SUB - 1:S5_SUB, :]
        a16i_ref[k] = p1_i[S5_SUB - 1:S5_SUB, :]

        bt = bt_ref[k]
        x = zr * bt + jnp.where(re_half[0:1], -zi, zi) * pltpu.roll(bt, N, axis=1)
        xs = pltpu.roll(x, N, axis=1)
        pw_is = jnp.where(re_half, -pw_i, pw_i)
        w_rows = jnp.concatenate(
            [pw_r[e:e + 1] * x + pw_is[e:e + 1] * xs for e in range(S5_SUB - 1, -1, -1)], axis=0)
        w_t = w_rows.T
        w_r = w_t[0:N]
        w_i = w_t[N:2 * N]
        zero = jnp.zeros_like(w_r)
        wt_ref[k, 0:2 * N, :] = jnp.concatenate(
            [zero, w_r] if odd else [w_r, zero], axis=0).astype(BF16)
        wt_ref[k, 2 * N:4 * N, :] = jnp.concatenate(
            [zero, w_i] if odd else [w_i, zero], axis=0).astype(BF16)

        c_re = cr_ref[k]
        c_im = ci_ref[k]
        krev = hdot(c_re, w_r) - hdot(c_im, w_i)
        krev = krev + jnp.where(lag0_diag, d_ref[k], 0.0)
        for tp in range(S5_SUB):
            width = (tp + 1) * P
            piece = krev if width == R else jnp.where(
                lane_p < width, pltpu.roll(krev, width, axis=1), 0.0)
            mt_ref[k, tp * P:(tp + 1) * P, :] = piece.astype(BF16)

        c4_r = c4r_ref[k]
        c4_i = c4i_ref[k]
        keep = lane_slot == odd
        for tp in range(S5_SUB):
            lr = jnp.concatenate([p1_r[tp:tp + 1], p1_r[tp:tp + 1]], axis=1)
            li = jnp.concatenate([p1_i[tp:tp + 1], p1_i[tp:tp + 1]], axis=1)
            val = jnp.where(is_re, c4_r * lr - c4_i * li, -(c4_r * li + c4_i * lr))
            vt_ref[k, tp * P:(tp + 1) * P, :] = jnp.where(keep, val, 0.0).astype(BF16)


def _prep(c, w_ada, b_ada, a_re, a_im, log_step, b_re, b_im, c_re, c_im, d_skip, w_in):
    G, N, P, R = S5_GROUPS, S5_STATE, S5_GROUP, S5_ROW
    gg = _S5GEN_GG
    steps = G // gg
    sw = w_in.shape[0] // steps
    n_mod = w_ada.shape[1]
    tn = n_mod // steps
    assert sw * steps == w_in.shape[0] and sw % 16 == 0
    assert tn * steps == n_mod and tn % LANES == 0
    slab_w = pl.BlockSpec((sw, w_in.shape[1]), lambda g: (g, 0))
    dup = lambda a: jnp.tile(a, (1, 2)).reshape(G, 1, 2 * N)
    bt = jnp.concatenate([jnp.swapaxes(b_re, 1, 2), jnp.swapaxes(b_im, 1, 2)], axis=-1)
    tile4 = lambda a: jnp.tile(a, (1, 1, R // N))
    blk = lambda s: pl.BlockSpec((gg,) + s, lambda g: (g, 0, 0))
    return pl.pallas_call(
        _prep_kernel,
        grid=(steps,),
        in_specs=[pl.BlockSpec((BATCH, D_MODEL), lambda g: (0, 0)),
                  pl.BlockSpec((D_MODEL, tn), lambda g: (0, g)),
                  pl.BlockSpec((1, tn), lambda g: (0, g)),
                  blk((1, 1)), blk((1, 2 * N)), blk((1, 2 * N)), blk((P, 2 * N)),
                  blk((P, N)), blk((P, N)), blk((P, R)), blk((P, R)), blk((P, 1)), slab_w],
        out_specs=[pl.BlockSpec((BATCH, tn), lambda g: (0, g)),
                   blk((R, R)), blk((R, R)), blk((R, R)), blk((1, 2 * N)), blk((1, 2 * N)),
                   slab_w],
        out_shape=[jax.ShapeDtypeStruct((BATCH, n_mod), F32)] +
                  [jax.ShapeDtypeStruct((G, R, R), BF16)] * 3 +
                  [jax.ShapeDtypeStruct((G, 1, 2 * N), F32)] * 2 +
                  [jax.ShapeDtypeStruct(w_in.shape, BF16)],
        compiler_params=pltpu.CompilerParams(
            dimension_semantics=("parallel",), vmem_limit_bytes=VMEM_LIMIT),
        name="prep",
    )(c, w_ada, b_ada.reshape(1, n_mod),
      log_step.reshape(G, 1, 1), dup(a_re), dup(a_im), bt, c_re, c_im, tile4(c_re), tile4(c_im),
      d_skip.reshape(G, P, 1), w_in)


_S5_GB = LANES // S5_GROUP
_S5_NP = _S5_GB // 2


def _s5_kernel(a16r_ref, a16i_ref, u_ref, mt_ref, wt_ref, vt_ref, o_ref,
               ut_scr, yt_scr, e_scr, p_scr):
    P = S5_GROUP
    N2 = 2 * S5_STATE
    cols = S5_COLS
    nblk = cols // LANES
    nt = cols // 8

    def load_body(tau, carry):
        for c in range(nblk):
            x = u_ref[pl.ds(tau + c * LANES * S5_SUB, LANES, stride=S5_SUB), :]
            xt = x.T
            r0 = pl.multiple_of(tau * P, P)
            for k in range(_S5_GB):
                ut_scr[k, pl.ds(r0, P), c * LANES:(c + 1) * LANES] = \
                    xt[k * P:(k + 1) * P, :].astype(BF16)
        return carry

    lax.fori_loop(0, S5_SUB, load_body, 0, unroll=4)

    for pr in range(_S5_NP):
        et = (jnp.dot(wt_ref[2 * pr], ut_scr[2 * pr], preferred_element_type=F32) +
              jnp.dot(wt_ref[2 * pr + 1], ut_scr[2 * pr + 1], preferred_element_type=F32))
        e_scr[pr] = et.T

    sub = lax.broadcasted_iota(jnp.int32, (8, N2), 0)
    lo = sub < BATCH
    first = lax.broadcasted_iota(jnp.int32, (1, N2), 1) < S5_STATE
    a_r, a_i = [], []
    for pr in range(_S5_NP):
        a_r.append(jnp.broadcast_to(
            jnp.where(first, a16r_ref[2 * pr], a16r_ref[2 * pr + 1]), (8, N2)))
        a_i.append(jnp.broadcast_to(
            jnp.where(first, a16i_ref[2 * pr], a16i_ref[2 * pr + 1]), (8, N2)))

    def scan_body(t, carry):
        r0 = pl.multiple_of(t * 8, 8)
        new = []
        for pr in range(_S5_NP):
            s_r, s_i = carry[2 * pr], carry[2 * pr + 1]
            e_r = e_scr[pr, pl.ds(r0, 8), 0:N2]
            e_i = e_scr[pr, pl.ds(r0, 8), N2:2 * N2]
            x_r = pltpu.roll(e_r, BATCH, axis=0)
            x_i = pltpu.roll(e_i, BATCH, axis=0)
            elo_r = jnp.where(lo, e_r, x_r)
            elo_i = jnp.where(lo, e_i, x_i)
            ehi_r = jnp.where(lo, x_r, e_r)
            ehi_i = jnp.where(lo, x_i, e_i)
            t_r = a_r[pr] * s_r - a_i[pr] * s_i + elo_r
            t_i = a_r[pr] * s_i + a_i[pr] * s_r + elo_i
            p_scr[pr, pl.ds(r0, 8), 0:N2] = jnp.where(lo, s_r, t_r)
            p_scr[pr, pl.ds(r0, 8), N2:2 * N2] = jnp.where(lo, s_i, t_i)
            new.append(a_r[pr] * t_r - a_i[pr] * t_i + ehi_r)
            new.append(a_r[pr] * t_i + a_i[pr] * t_r + ehi_i)
        return tuple(new)

    zero = jnp.zeros((8, N2), F32)
    lax.fori_loop(0, nt, scan_body, tuple(zero for _ in range(2 * _S5_NP)))

    for pr in range(_S5_NP):
        pt = p_scr[pr].T.astype(BF16)
        for k in range(2):
            gi = 2 * pr + k
            yt = (jnp.dot(mt_ref[gi], ut_scr[gi], preferred_element_type=F32) +
                  jnp.dot(vt_ref[gi], pt, preferred_element_type=F32))
            for tau in range(S5_SUB):
                yt_scr[tau, gi * P:(gi + 1) * P, :] = yt[tau * P:(tau + 1) * P, :]

    def store_body(tau, carry):
        o_ref[pl.ds(tau, cols, stride=S5_SUB), :] = yt_scr[tau].T
        return carry

    lax.fori_loop(0, S5_SUB, store_body, 0, unroll=4)


def _s5(u_rows, mt, wt, vt, a16r, a16i):
    G, N, R = S5_GROUPS, S5_STATE, S5_ROW
    gb = _S5_GB
    vec = pl.BlockSpec((gb, 1, 2 * N), lambda i: (i, 0, 0))
    mat = pl.BlockSpec((gb, R, R), lambda i: (i, 0, 0))
    slab = pl.BlockSpec((TOK, LANES), lambda i: (0, i))
    return pl.pallas_call(
        _s5_kernel,
        grid=(G // gb,),
        in_specs=[vec, vec, slab, mat, mat, mat],
        out_specs=slab,
        out_shape=jax.ShapeDtypeStruct((TOK, D_S5), F32),
        scratch_shapes=[pltpu.VMEM((gb, R, S5_COLS), BF16),
                        pltpu.VMEM((S5_SUB, LANES, S5_COLS), F32),
                        pltpu.VMEM((_S5_NP, S5_COLS, 4 * N), F32),
                        pltpu.VMEM((_S5_NP, S5_COLS, 4 * N), F32)],
        compiler_params=pltpu.CompilerParams(
            dimension_semantics=("parallel",), vmem_limit_bytes=VMEM_LIMIT),
        name="s5",
    )(a16r, a16i, u_rows, mt, wt, vt)


_MIX_TM = 512
_MIX_SUB = 256
_MIX_NS = _MIX_TM // S5_SUB


def _mix_kernel(ssm_ref, ret_ref, x_ref, g1_ref, sc_ref, sh_ref, nw_ref, wglu_ref, bglu_ref,
                wout_ref, o_ref, h_ref):
    gain = nw_ref[...] * (1.0 + sc_ref[...])
    for r in range(_MIX_TM // _MIX_SUB):
        rs = slice(r * _MIX_SUB, (r + 1) * _MIX_SUB)
        ns = _MIX_SUB // S5_SUB
        s = ssm_ref[r * ns:(r + 1) * ns].reshape(_MIX_SUB, D_S5)
        cdf = 0.5 * (1.0 + jnp.tanh(math.sqrt(2.0 / math.pi) * (s + 0.044715 * (s * s * s))))
        sg = s * cdf
        z = jnp.dot(sg.astype(BF16), wglu_ref[...], preferred_element_type=F32) + bglu_ref[...]
        so = sg * jax.nn.sigmoid(z)
        mix = jnp.dot(ret_ref[rs, :], wout_ref[0:D_RET, :], preferred_element_type=F32)
        mix = mix + jnp.dot(so.astype(BF16), wout_ref[D_RET:, :], preferred_element_type=F32)
        x1 = x_ref[rs, :] + g1_ref[...] * mix
        o_ref[rs, :] = x1
        ms = jnp.mean(x1 * x1, axis=-1, keepdims=True)
        h_ref[rs, :] = (x1 * lax.rsqrt(ms + EPS) * gain + sh_ref[...]).astype(BF16)


def _mix(ssm_sb, ret_out, x2d, g1, sc2, sh2, norm_w, w_glu_bf, b_glu, w_out_bf):
    tm = _MIX_TM
    per_b = SEQ // tm
    mod = pl.BlockSpec((None, 1, D_MODEL), lambda i: (i // per_b, 0, 0))
    rows = pl.BlockSpec((tm, D_MODEL), lambda i: (i, 0))
    return pl.pallas_call(
        _mix_kernel,
        grid=(TOK // tm,),
        in_specs=[pl.BlockSpec((_MIX_NS, None, S5_SUB, D_S5),
                               lambda i: (i % per_b, i // per_b, 0, 0)),
                  pl.BlockSpec((tm, D_RET), lambda i: (i, 0)),
                  rows, mod, mod, mod,
                  pl.BlockSpec((1, D_MODEL), lambda i: (0, 0)),
                  pl.BlockSpec((D_S5, D_S5), lambda i: (0, 0)),
                  pl.BlockSpec((1, D_S5), lambda i: (0, 0)),
                  pl.BlockSpec((D_RET + D_S5, D_MODEL), lambda i: (0, 0))],
        out_specs=[rows, rows],
        out_shape=[jax.ShapeDtypeStruct((TOK, D_MODEL), F32),
                   jax.ShapeDtypeStruct((TOK, D_MODEL), BF16)],
        compiler_params=pltpu.CompilerParams(
            dimension_semantics=("parallel",), vmem_limit_bytes=VMEM_LIMIT),
        name="mix",
    )(ssm_sb, ret_out, x2d, g1, sc2, sh2, norm_w, w_glu_bf, b_glu, w_out_bf)


_FFN_UP_TM = 2048
_FFN_UP_SUB = 1024
_FFN_UP_TF = 512
_FFN_DOWN_TM = 512
_FFN_DOWN_SUB = 256


def _ffn_up_kernel(h_ref, wg_ref, wu_ref, wd_ref, o_ref, wd_bf_ref, wg_scr, wu_scr):
    @pl.when(pl.program_id(1) == 0)
    def _():
        wg_scr[...] = wg_ref[...].astype(BF16)
        wu_scr[...] = wu_ref[...].astype(BF16)

    wd_bf_ref[...] = wd_ref[...].astype(BF16)
    for r in range(_FFN_UP_TM // _FFN_UP_SUB):
        rs = slice(r * _FFN_UP_SUB, (r + 1) * _FFN_UP_SUB)
        h = h_ref[rs, :]
        gate = jnp.dot(h, wg_scr[...], preferred_element_type=F32)
        up = jnp.dot(h, wu_scr[...], preferred_element_type=F32)
        o_ref[rs, :] = (_silu(gate) * up).astype(BF16)


def _ffn_up(h2, w_gate_up, w_down):
    tm, tf = _FFN_UP_TM, _FFN_UP_TF
    nf = D_FF // tf
    nm = TOK // tm
    slab = D_FF // (nf * nm)
    assert slab * nf * nm == D_FF and slab % 16 == 0
    return pl.pallas_call(
        _ffn_up_kernel,
        grid=(nf, nm),
        in_specs=[pl.BlockSpec((tm, D_MODEL), lambda f, i: (i, 0)),
                  pl.BlockSpec((D_MODEL, tf), lambda f, i: (0, f)),
                  pl.BlockSpec((D_MODEL, tf), lambda f, i: (0, nf + f)),
                  pl.BlockSpec((slab, D_MODEL), lambda f, i: (f * nm + i, 0))],
        out_specs=[pl.BlockSpec((tm, tf), lambda f, i: (i, f)),
                   pl.BlockSpec((slab, D_MODEL), lambda f, i: (f * nm + i, 0))],
        out_shape=[jax.ShapeDtypeStruct((TOK, D_FF), BF16),
                   jax.ShapeDtypeStruct((D_FF, D_MODEL), BF16)],
        scratch_shapes=[pltpu.VMEM((D_MODEL, tf), BF16),
                        pltpu.VMEM((D_MODEL, tf), BF16)],
        compiler_params=pltpu.CompilerParams(
            dimension_semantics=("arbitrary", "arbitrary"), vmem_limit_bytes=VMEM_LIMIT),
        name="ffn_up",
    )(h2, w_gate_up, w_gate_up, w_down)


def _ffn_down_kernel(a_ref, x_ref, g2_ref, fw_ref, wd_ref, o_ref):
    for r in range(_FFN_DOWN_TM // _FFN_DOWN_SUB):
        rs = slice(r * _FFN_DOWN_SUB, (r + 1) * _FFN_DOWN_SUB)
        down = jnp.dot(a_ref[rs, :], wd_ref[...], preferred_element_type=F32)
        x2 = x_ref[rs, :] + g2_ref[...] * down
        ms = jnp.mean(x2 * x2, axis=-1, keepdims=True)
        o_ref[rs, :] = x2 * lax.rsqrt(ms + EPS) * fw_ref[...]


def _ffn_down(act, x1, g2, final_w, w_down_bf):
    tm = _FFN_DOWN_TM
    per_b = SEQ // tm
    rows = pl.BlockSpec((tm, D_MODEL), lambda i: (i, 0))
    return pl.pallas_call(
        _ffn_down_kernel,
        grid=(TOK // tm,),
        in_specs=[pl.BlockSpec((tm, D_FF), lambda i: (i, 0)),
                  rows,
                  pl.BlockSpec((None, 1, D_MODEL), lambda i: (i // per_b, 0, 0)),
                  pl.BlockSpec((1, D_MODEL), lambda i: (0, 0)),
                  pl.BlockSpec((D_FF, D_MODEL), lambda i: (0, 0))],
        out_specs=rows,
        out_shape=jax.ShapeDtypeStruct((TOK, D_MODEL), F32),
        compiler_params=pltpu.CompilerParams(
            dimension_semantics=("parallel",), vmem_limit_bytes=VMEM_LIMIT),
        name="ffn_down",
    )(act, x1, g2, final_w, w_down_bf)


def kernel(x, c, w_ada, b_ada, norm1_w, w_in, ret_norm_w, s5_a_re, s5_a_im, s5_log_step,
           s5_b_re, s5_b_im, s5_c_re, s5_c_im, s5_d, w_glu, b_glu, w_out, norm2_w,
           w_gate_up, w_down, final_norm_w):
    x2d = x.reshape(TOK, D_MODEL)
    layer = 0
    mod, mt, wt, vt, a16r, a16i, w_in_bf = _prep(
        c, w_ada[layer], b_ada[layer],
        s5_a_re[layer], s5_a_im[layer], s5_log_step[layer], s5_b_re[layer], s5_b_im[layer],
        s5_c_re[layer], s5_c_im[layer], s5_d[layer], w_in[layer])
    sh1, sc1, g1, sh2, sc2, g2 = [m.reshape(BATCH, 1, D_MODEL) for m in jnp.split(mod, 6, axis=-1)]

    qkvg, u_sb = _inproj(x2d, sc1, sh1, norm1_w[layer].reshape(1, D_MODEL), w_in_bf)
    ret_out, w_out_bf, w_glu_bf = _retention(qkvg, ret_norm_w[layer].reshape(1, D_RET),
                                             w_out[layer], w_glu[layer])
    y_rows = _s5(u_sb.reshape(TOK, D_S5), mt, wt, vt, a16r, a16i)
    ssm_sb = y_rows.reshape(S5_NSUB, BATCH, S5_SUB, D_S5)

    x1, h2 = _mix(ssm_sb, ret_out, x2d, g1, sc2, sh2, norm2_w[layer].reshape(1, D_MODEL),
                  w_glu_bf, b_glu[layer].reshape(1, D_S5), w_out_bf)
    act, w_down_bf = _ffn_up(h2, w_gate_up[layer], w_down[layer])
    out = _ffn_down(act, x1, g2, final_norm_w.reshape(1, D_MODEL), w_down_bf)
    return out.reshape(BATCH, SEQ, D_MODEL)
```

```python
import math

import numpy as np
import jax
import jax.numpy as jnp
from jax import lax
from jax.experimental import pallas as pl
from jax.experimental.pallas import tpu as pltpu

D_MODEL = 2048
BATCH = 4
SEQ = 2048
TOK = BATCH * SEQ
D_RET = 1024
D_S5 = 1024
RET_HEADS = 4
RET_HEAD_DIM = 256
RET_CHUNK = 128
S5_GROUP = 16
S5_GROUPS = 64
S5_STATE = 64
S5_SUB = 16
S5_ROW = S5_SUB * S5_GROUP
S5_NSUB = SEQ // S5_SUB
S5_COLS = S5_NSUB * BATCH
D_FF = 5632
ROPE_BASE = 10000.0
EPS = 1e-6
LANES = 128

F32 = jnp.float32
BF16 = jnp.bfloat16
HI = lax.Precision.HIGHEST
VMEM_LIMIT = 58 * 1024 * 1024


def _silu(v):
    return v * jax.nn.sigmoid(v)


_INPROJ_TM = 1024
_INPROJ_TN = 1024
_INPROJ_NS = _INPROJ_TM // S5_SUB


def _inproj_kernel(x_ref, sc_ref, sh_ref, nw_ref, w_ref, o_ref, u_ref, h_scr, cos_scr, sin_scr):
    tm = _INPROJ_TM
    half = RET_HEAD_DIM // 2
    per_b = SEQ // tm
    i = pl.program_id(0)
    j = pl.program_id(1)
    slot = i % per_b

    @pl.when(j == 0)
    def _():
        gain = nw_ref[...] * (1.0 + sc_ref[...])
        shift = sh_ref[...]
        rb = 16

        def norm_body(r, carry):
            r0 = pl.multiple_of(r * rb, rb)
            x = x_ref[pl.ds(r0, rb), :]
            ms = jnp.mean(x * x, axis=-1, keepdims=True)
            h_scr[pl.ds(r0, rb), :] = (x * lax.rsqrt(ms + EPS) * gain + shift).astype(BF16)
            return carry

        lax.fori_loop(0, tm // rb, norm_body, 0, unroll=4)

    @pl.when((j == 0) & (i < per_b))
    def _():
        pos = (lax.broadcasted_iota(jnp.int32, (tm, half), 0) + slot * tm).astype(F32)
        lane = lax.broadcasted_iota(jnp.int32, (tm, half), 1).astype(F32)
        ang = pos * jnp.exp(lane * (-math.log(ROPE_BASE) / half))
        cos_scr[slot] = jnp.cos(ang)
        sin_scr[slot] = jnp.sin(ang)

    acc = jnp.dot(h_scr[...], w_ref[...], preferred_element_type=F32)

    rot = (j < 2).astype(F32)
    cs = cos_scr[slot] * rot + (1.0 - rot)
    sn = sin_scr[slot] * rot
    for hh in range(RET_HEADS):
        c0 = hh * RET_HEAD_DIM
        x1 = acc[:, c0:c0 + half]
        x2 = acc[:, c0 + half:c0 + 2 * half]
        o_ref[hh, :, 0:half] = (x1 * cs - x2 * sn).astype(BF16)
        o_ref[hh, :, half:2 * half] = (x2 * cs + x1 * sn).astype(BF16)
    for k in range(_INPROJ_TN // LANES):
        u_ref[k] = acc[:, k * LANES:(k + 1) * LANES].reshape(_INPROJ_NS, S5_SUB, LANES)


def _inproj(x2d, sc1, sh1, norm_w, w_in_bf):
    tm, tn = _INPROJ_TM, _INPROJ_TN
    per_b = SEQ // tm
    half = RET_HEAD_DIM // 2
    return pl.pallas_call(
        _inproj_kernel,
        grid=(TOK // tm, 5),
        in_specs=[pl.BlockSpec((tm, D_MODEL), lambda i, j: (i, 0)),
                  pl.BlockSpec((None, 1, D_MODEL), lambda i, j: (i // per_b, 0, 0)),
                  pl.BlockSpec((None, 1, D_MODEL), lambda i, j: (i // per_b, 0, 0)),
                  pl.BlockSpec((1, D_MODEL), lambda i, j: (0, 0)),
                  pl.BlockSpec((D_MODEL, tn), lambda i, j: (0, j))],
        out_specs=[pl.BlockSpec((RET_HEADS, tm, RET_HEAD_DIM), lambda i, j: (j, i, 0)),
                   pl.BlockSpec((D_S5 // LANES, _INPROJ_NS, None, S5_SUB, LANES),
                                lambda i, j: (0, i % per_b, i // per_b, 0, 0))],
        out_shape=[jax.ShapeDtypeStruct((5 * RET_HEADS, TOK, RET_HEAD_DIM), BF16),
                   jax.ShapeDtypeStruct((D_S5 // LANES, S5_NSUB, BATCH, S5_SUB, LANES), F32)],
        scratch_shapes=[pltpu.VMEM((tm, D_MODEL), BF16),
                        pltpu.VMEM((per_b, tm, half), F32),
                        pltpu.VMEM((per_b, tm, half), F32)],
        compiler_params=pltpu.CompilerParams(
            dimension_semantics=("arbitrary", "arbitrary"), vmem_limit_bytes=VMEM_LIMIT),
        name="inproj",
    )(x2d, sc1, sh1, norm_w, w_in_bf)


_RET_BLK = 256


def _ret_kernel(lg_ref, q_ref, k_ref, v_ref, g_ref, w_ref, wout_ref, wglu_ref,
                o_ref, wout_bf_ref, wglu_bf_ref,
                kv_scr, prev_scr, intra_scr, kdec_scr, qdec_scr):
    wout_bf_ref[...] = wout_ref[...].astype(BF16)
    wglu_bf_ref[...] = wglu_ref[...].astype(BF16)
    C = _RET_BLK
    dh = RET_HEAD_DIM
    nc = SEQ // C
    scale = dh ** -0.5
    lg = lg_ref[0:1, :]
    ii = lax.broadcasted_iota(jnp.int32, (C, C), 0)
    jj = lax.broadcasted_iota(jnp.int32, (C, C), 1)
    diff = (ii - jj).astype(F32)
    intra_scr[...] = jnp.where(diff >= 0.0, jnp.exp(lg * jnp.maximum(diff, 0.0)), 0.0) * scale
    row = lax.broadcasted_iota(jnp.int32, (C, dh), 0).astype(F32)
    kdec_scr[...] = jnp.exp(lg * (C - 1.0 - row)) * scale
    qdec_scr[...] = jnp.exp(lg * (row + 1.0))
    block_decay = jnp.exp(lg * float(C))
    gn_w = w_ref[...]

    for n in range(nc):
        rs = slice(n * C, (n + 1) * C)
        kd = (k_ref[rs, :].astype(F32) * kdec_scr[...]).astype(BF16)
        kv_scr[n] = lax.dot_general(kd, v_ref[rs, :], (((0,), (0,)), ((), ())),
                                    preferred_element_type=F32)

    band = 64
    for rb in range(dh // band):
        bs = slice(rb * band, (rb + 1) * band)
        st = jnp.zeros((band, dh), F32)
        for n in range(nc):
            prev_scr[n, bs, :] = st.astype(BF16)
            if n + 1 < nc:
                st = st * block_decay + kv_scr[n, bs, :]

    for n in range(nc):
        rs = slice(n * C, (n + 1) * C)
        q = q_ref[rs, :]
        s = lax.dot_general(q, k_ref[rs, :], (((1,), (1,)), ((), ())),
                            preferred_element_type=F32) * intra_scr[...]
        y = jnp.dot(s.astype(BF16), v_ref[rs, :], preferred_element_type=F32)
        y = y + jnp.dot(q, prev_scr[n], preferred_element_type=F32) * qdec_scr[...]
        mu = jnp.mean(y, axis=-1, keepdims=True)
        yc = y - mu
        var = jnp.mean(yc * yc, axis=-1, keepdims=True)
        yn = yc * lax.rsqrt(var + EPS) * gn_w
        g = g_ref[rs, :].astype(F32)
        o_ref[rs, :] = (_silu(g) * yn).astype(BF16)


def _retention(qkvg, ret_norm_w, w_out, w_glu):
    dh = RET_HEAD_DIM
    nc = SEQ // _RET_BLK
    steps = BATCH * RET_HEADS
    so = w_out.shape[0] // steps
    sg = w_glu.shape[0] // steps
    assert so * steps == w_out.shape[0] and sg * steps == w_glu.shape[0] and sg % 16 == 0
    lg = np.log1p(-np.exp2(-5.0 - np.arange(RET_HEADS, dtype=np.float64)))
    lg_tab = jnp.asarray(np.broadcast_to(lg[:, None, None], (RET_HEADS, 8, dh)), F32)
    spec = lambda off: pl.BlockSpec((None, SEQ, dh), lambda b, h: (off + h, b, 0))
    slab_o = pl.BlockSpec((so, w_out.shape[1]), lambda b, h: (b * RET_HEADS + h, 0))
    slab_g = pl.BlockSpec((sg, w_glu.shape[1]), lambda b, h: (b * RET_HEADS + h, 0))
    return pl.pallas_call(
        _ret_kernel,
        grid=(BATCH, RET_HEADS),
        in_specs=[pl.BlockSpec((None, 8, dh), lambda b, h: (h, 0, 0)),
                  spec(0), spec(RET_HEADS), spec(2 * RET_HEADS), spec(3 * RET_HEADS),
                  pl.BlockSpec((1, dh), lambda b, h: (0, h)), slab_o, slab_g],
        out_specs=[pl.BlockSpec((None, SEQ, dh), lambda b, h: (h, b, 0)), slab_o, slab_g],
        out_shape=[jax.ShapeDtypeStruct((RET_HEADS, TOK, dh), BF16),
                   jax.ShapeDtypeStruct(w_out.shape, BF16),
                   jax.ShapeDtypeStruct(w_glu.shape, BF16)],
        scratch_shapes=[pltpu.VMEM((nc, dh, dh), F32),
                        pltpu.VMEM((nc, dh, dh), BF16),
                        pltpu.VMEM((_RET_BLK, _RET_BLK), F32),
                        pltpu.VMEM((_RET_BLK, dh), F32),
                        pltpu.VMEM((_RET_BLK, dh), F32)],
        compiler_params=pltpu.CompilerParams(
            dimension_semantics=("parallel", "parallel"), vmem_limit_bytes=VMEM_LIMIT),
        name="retention",
    )(lg_tab, qkvg, qkvg, qkvg, qkvg, ret_norm_w, w_out, w_glu)


_S5GEN_GG = 4


def _prep_kernel(c_ref, wada_ref, bada_ref,
                 ls_ref, ar_ref, ai_ref, bt_ref, cr_ref, ci_ref, c4r_ref, c4i_ref, d_ref, win_ref,
                 mod_ref, mt_ref, wt_ref, vt_ref, a16r_ref, a16i_ref, win_bf_ref):
    cond = _silu(c_ref[...])
    mod_ref[...] = jnp.dot(cond.astype(BF16), wada_ref[...].astype(BF16),
                           preferred_element_type=F32) + bada_ref[...]
    win_bf_ref[...] = win_ref[...].astype(BF16)
    N, P, R = S5_STATE, S5_GROUP, S5_ROW
    re_half = lax.broadcasted_iota(jnp.int32, (P, 2 * N), 1) < N
    sub_e = lax.broadcasted_iota(jnp.int32, (P, 2 * N), 0)
    lane_p = lax.broadcasted_iota(jnp.int32, (P, R), 1)
    sub_p = lax.broadcasted_iota(jnp.int32, (P, R), 0)
    lag0_diag = lane_p == (R - P) + sub_p
    is_re = lane_p < 2 * N
    lane_slot = lax.shift_right_logical(lane_p, 6) & 1

    def hdot(a, b):
        return jnp.dot(a, b, precision=HI, preferred_element_type=F32)

    def cmul(xr, xi, yr, yi):
        return xr * yr - xi * yi, xr * yi + xi * yr

    for k in range(_S5GEN_GG):
        odd = k % 2
        dt = jnp.exp(ls_ref[k])
        ar = ar_ref[k]
        ai = ai_ref[k]
        mag = jnp.exp(ar * dt)
        lbr = mag * jnp.cos(ai * dt)
        lbi = mag * jnp.sin(ai * dt)
        den = ar * ar + ai * ai
        zr = ((lbr - 1.0) * ar + lbi * ai) / den
        zi = (lbi * ar - (lbr - 1.0) * ai) / den

        pw_r = jnp.ones((P, 2 * N), F32)
        pw_i = jnp.zeros((P, 2 * N), F32)
        sq_r, sq_i = lbr, lbi
        for bit in range(4):
            nr, ni = cmul(pw_r, pw_i, sq_r, sq_i)
            take = (lax.shift_right_logical(sub_e, bit) & 1) == 1
            pw_r = jnp.where(take, nr, pw_r)
            pw_i = jnp.where(take, ni, pw_i)
            if bit < 3:
                sq_r, sq_i = cmul(sq_r, sq_i, sq_r, sq_i)
        p1_r, p1_i = cmul(pw_r, pw_i, lbr, lbi)
        a16r_ref[k] = p1_r[S5_SUB - 1:S5_SUB, :]
        a16i_ref[k] = p1_i[S5_SUB - 1:S5_SUB, :]

        bt = bt_ref[k]
        x = zr * bt + jnp.where(re_half[0:1], -zi, zi) * pltpu.roll(bt, N, axis=1)
        xs = pltpu.roll(x, N, axis=1)
        pw_is = jnp.where(re_half, -pw_i, pw_i)
        w_rows = jnp.concatenate(
            [pw_r[e:e + 1] * x + pw_is[e:e + 1] * xs for e in range(S5_SUB - 1, -1, -1)], axis=0)
        w_t = w_rows.T
        w_r = w_t[0:N]
        w_i = w_t[N:2 * N]
        zero = jnp.zeros_like(w_r)
        wt_ref[k, 0:2 * N, :] = jnp.concatenate(
            [zero, w_r] if odd else [w_r, zero], axis=0).astype(BF16)
        wt_ref[k, 2 * N:4 * N, :] = jnp.concatenate(
            [zero, w_i] if odd else [w_i, zero], axis=0).astype(BF16)

        c_re = cr_ref[k]
        c_im = ci_ref[k]
        krev = hdot(c_re, w_r) - hdot(c_im, w_i)
        krev = krev + jnp.where(lag0_diag, d_ref[k], 0.0)
        for tp in range(S5_SUB):
            width = (tp + 1) * P
            piece = krev if width == R else jnp.where(
                lane_p < width, pltpu.roll(krev, width, axis=1), 0.0)
            mt_ref[k, tp * P:(tp + 1) * P, :] = piece.astype(BF16)

        c4_r = c4r_ref[k]
        c4_i = c4i_ref[k]
        keep = lane_slot == odd
        for tp in range(S5_SUB):
            lr = jnp.concatenate([p1_r[tp:tp + 1], p1_r[tp:tp + 1]], axis=1)
            li = jnp.concatenate([p1_i[tp:tp + 1], p1_i[tp:tp + 1]], axis=1)
            val = jnp.where(is_re, c4_r * lr - c4_i * li, -(c4_r * li + c4_i * lr))
            vt_ref[k, tp * P:(tp + 1) * P, :] = jnp.where(keep, val, 0.0).astype(BF16)


def _prep(c, w_ada, b_ada, a_re, a_im, log_step, b_re, b_im, c_re, c_im, d_skip, w_in):
    G, N, P, R = S5_GROUPS, S5_STATE, S5_GROUP, S5_ROW
    gg = _S5GEN_GG
    steps = G // gg
    sw = w_in.shape[0] // steps
    n_mod = w_ada.shape[1]
    tn = n_mod // steps
    assert sw * steps == w_in.shape[0] and sw % 16 == 0
    assert tn * steps == n_mod and tn % LANES == 0
    slab_w = pl.BlockSpec((sw, w_in.shape[1]), lambda g: (g, 0))
    dup = lambda a: jnp.tile(a, (1, 2)).reshape(G, 1, 2 * N)
    bt = jnp.concatenate([jnp.swapaxes(b_re, 1, 2), jnp.swapaxes(b_im, 1, 2)], axis=-1)
    tile4 = lambda a: jnp.tile(a, (1, 1, R // N))
    blk = lambda s: pl.BlockSpec((gg,) + s, lambda g: (g, 0, 0))
    return pl.pallas_call(
        _prep_kernel,
        grid=(steps,),
        in_specs=[pl.BlockSpec((BATCH, D_MODEL), lambda g: (0, 0)),
                  pl.BlockSpec((D_MODEL, tn), lambda g: (0, g)),
                  pl.BlockSpec((1, tn), lambda g: (0, g)),
                  blk((1, 1)), blk((1, 2 * N)), blk((1, 2 * N)), blk((P, 2 * N)),
                  blk((P, N)), blk((P, N)), blk((P, R)), blk((P, R)), blk((P, 1)), slab_w],
        out_specs=[pl.BlockSpec((BATCH, tn), lambda g: (0, g)),
                   blk((R, R)), blk((R, R)), blk((R, R)), blk((1, 2 * N)), blk((1, 2 * N)),
                   slab_w],
        out_shape=[jax.ShapeDtypeStruct((BATCH, n_mod), F32)] +
                  [jax.ShapeDtypeStruct((G, R, R), BF16)] * 3 +
                  [jax.ShapeDtypeStruct((G, 1, 2 * N), F32)] * 2 +
                  [jax.ShapeDtypeStruct(w_in.shape, BF16)],
        compiler_params=pltpu.CompilerParams(
            dimension_semantics=("parallel",), vmem_limit_bytes=VMEM_LIMIT),
        name="prep",
    )(c, w_ada, b_ada.reshape(1, n_mod),
      log_step.reshape(G, 1, 1), dup(a_re), dup(a_im), bt, c_re, c_im, tile4(c_re), tile4(c_im),
      d_skip.reshape(G, P, 1), w_in)


_S5_GB = LANES // S5_GROUP
_S5_NP = _S5_GB // 2


def _s5_kernel(a16r_ref, a16i_ref, u_ref, mt_ref, wt_ref, vt_ref, o_ref,
               ut_scr, yt_scr, e_scr, p_scr):
    P = S5_GROUP
    N2 = 2 * S5_STATE
    cols = S5_COLS
    nblk = cols // LANES
    nt = cols // 8

    def load_body(tau, carry):
        for c in range(nblk):
            x = u_ref[pl.ds(tau + c * LANES * S5_SUB, LANES, stride=S5_SUB), :]
            xt = x.T
            r0 = pl.multiple_of(tau * P, P)
            for k in range(_S5_GB):
                ut_scr[k, pl.ds(r0, P), c * LANES:(c + 1) * LANES] = \
                    xt[k * P:(k + 1) * P, :].astype(BF16)
        return carry

    lax.fori_loop(0, S5_SUB, load_body, 0, unroll=4)

    for pr in range(_S5_NP):
        et = (jnp.dot(wt_ref[2 * pr], ut_scr[2 * pr], preferred_element_type=F32) +
              jnp.dot(wt_ref[2 * pr + 1], ut_scr[2 * pr + 1], preferred_element_type=F32))
        e_scr[pr] = et.T

    sub = lax.broadcasted_iota(jnp.int32, (8, N2), 0)
    lo = sub < BATCH
    first = lax.broadcasted_iota(jnp.int32, (1, N2), 1) < S5_STATE
    a_r, a_i = [], []
    for pr in range(_S5_NP):
        a_r.append(jnp.broadcast_to(
            jnp.where(first, a16r_ref[2 * pr], a16r_ref[2 * pr + 1]), (8, N2)))
        a_i.append(jnp.broadcast_to(
            jnp.where(first, a16i_ref[2 * pr], a16i_ref[2 * pr + 1]), (8, N2)))

    def scan_body(t, carry):
        r0 = pl.multiple_of(t * 8, 8)
        new = []
        for pr in range(_S5_NP):
            s_r, s_i = carry[2 * pr], carry[2 * pr + 1]
            e_r = e_scr[pr, pl.ds(r0, 8), 0:N2]
            e_i = e_scr[pr, pl.ds(r0, 8), N2:2 * N2]
            x_r = pltpu.roll(e_r, BATCH, axis=0)
            x_i = pltpu.roll(e_i, BATCH, axis=0)
            elo_r = jnp.where(lo, e_r, x_r)
            elo_i = jnp.where(lo, e_i, x_i)
            ehi_r = jnp.where(lo, x_r, e_r)
            ehi_i = jnp.where(lo, x_i, e_i)
            t_r = a_r[pr] * s_r - a_i[pr] * s_i + elo_r
            t_i = a_r[pr] * s_i + a_i[pr] * s_r + elo_i
            p_scr[pr, pl.ds(r0, 8), 0:N2] = jnp.where(lo, s_r, t_r)
            p_scr[pr, pl.ds(r0, 8), N2:2 * N2] = jnp.where(lo, s_i, t_i)
            new.append(a_r[pr] * t_r - a_i[pr] * t_i + ehi_r)
            new.append(a_r[pr] * t_i + a_i[pr] * t_r + ehi_i)
        return tuple(new)

    zero = jnp.zeros((8, N2), F32)
    lax.fori_loop(0, nt, scan_body, tuple(zero for _ in range(2 * _S5_NP)))

    for pr in range(_S5_NP):
        pt = p_scr[pr].T.astype(BF16)
        for k in range(2):
            gi = 2 * pr + k
            yt = (jnp.dot(mt_ref[gi], ut_scr[gi], preferred_element_type=F32) +
                  jnp.dot(vt_ref[gi], pt, preferred_element_type=F32))
            for tau in range(S5_SUB):
                yt_scr[tau, gi * P:(gi + 1) * P, :] = yt[tau * P:(tau + 1) * P, :]

    def store_body(tau, carry):
        o_ref[pl.ds(tau, cols, stride=S5_SUB), :] = yt_scr[tau].T
        return carry

    lax.fori_loop(0, S5_SUB, store_body, 0, unroll=4)


def _s5(u_rows, mt, wt, vt, a16r, a16i):
    G, N, R = S5_GROUPS, S5_STATE, S5_ROW
    gb = _S5_GB
    vec = pl.BlockSpec((gb, 1, 2 * N), lambda i: (i, 0, 0))
    mat = pl.BlockSpec((gb, R, R), lambda i: (i, 0, 0))
    slab = pl.BlockSpec((None, TOK, LANES), lambda i: (i, 0, 0))
    return pl.pallas_call(
        _s5_kernel,
        grid=(G // gb,),
        in_specs=[vec, vec, slab, mat, mat, mat],
        out_specs=slab,
        out_shape=jax.ShapeDtypeStruct((D_S5 // LANES, TOK, LANES), F32),
        scratch_shapes=[pltpu.VMEM((gb, R, S5_COLS), BF16),
                        pltpu.VMEM((S5_SUB, LANES, S5_COLS), F32),
                        pltpu.VMEM((_S5_NP, S5_COLS, 4 * N), F32),
                        pltpu.VMEM((_S5_NP, S5_COLS, 4 * N), F32)],
        compiler_params=pltpu.CompilerParams(
            dimension_semantics=("parallel",), vmem_limit_bytes=VMEM_LIMIT),
        name="s5",
    )(a16r, a16i, u_rows, mt, wt, vt)


_MIX_TM = 512
_MIX_SUB = 256
_MIX_NS = _MIX_TM // S5_SUB


def _mix_kernel(ssm_ref, ret_ref, x_ref, g1_ref, sc_ref, sh_ref, nw_ref, wglu_ref, bglu_ref,
                wout_ref, o_ref, h_ref):
    gain = nw_ref[...] * (1.0 + sc_ref[...])
    for r in range(_MIX_TM // _MIX_SUB):
        rs = slice(r * _MIX_SUB, (r + 1) * _MIX_SUB)
        ns = _MIX_SUB // S5_SUB
        s = jnp.concatenate([ssm_ref[k, r * ns:(r + 1) * ns].reshape(_MIX_SUB, LANES)
                             for k in range(D_S5 // LANES)], axis=1)
        cdf = 0.5 * (1.0 + jnp.tanh(math.sqrt(2.0 / math.pi) * (s + 0.044715 * (s * s * s))))
        sg = s * cdf
        z = jnp.dot(sg.astype(BF16), wglu_ref[...], preferred_element_type=F32) + bglu_ref[...]
        so = sg * jax.nn.sigmoid(z)
        mix = None
        for hh in range(RET_HEADS):
            part = jnp.dot(ret_ref[hh, rs, :],
                           wout_ref[hh * RET_HEAD_DIM:(hh + 1) * RET_HEAD_DIM, :],
                           preferred_element_type=F32)
            mix = part if mix is None else mix + part
        mix = mix + jnp.dot(so.astype(BF16), wout_ref[D_RET:, :], preferred_element_type=F32)
        x1 = x_ref[rs, :] + g1_ref[...] * mix
        o_ref[rs, :] = x1
        ms = jnp.mean(x1 * x1, axis=-1, keepdims=True)
        h_ref[rs, :] = (x1 * lax.rsqrt(ms + EPS) * gain + sh_ref[...]).astype(BF16)


def _mix(ssm_sb, ret_out, x2d, g1, sc2, sh2, norm_w, w_glu_bf, b_glu, w_out_bf):
    tm = _MIX_TM
    per_b = SEQ // tm
    mod = pl.BlockSpec((None, 1, D_MODEL), lambda i: (i // per_b, 0, 0))
    rows = pl.BlockSpec((tm, D_MODEL), lambda i: (i, 0))
    return pl.pallas_call(
        _mix_kernel,
        grid=(TOK // tm,),
        in_specs=[pl.BlockSpec((D_S5 // LANES, _MIX_NS, None, S5_SUB, LANES),
                               lambda i: (0, i % per_b, i // per_b, 0, 0)),
                  pl.BlockSpec((RET_HEADS, tm, RET_HEAD_DIM), lambda i: (0, i, 0)),
                  rows, mod, mod, mod,
                  pl.BlockSpec((1, D_MODEL), lambda i: (0, 0)),
                  pl.BlockSpec((D_S5, D_S5), lambda i: (0, 0)),
                  pl.BlockSpec((1, D_S5), lambda i: (0, 0)),
                  pl.BlockSpec((D_RET + D_S5, D_MODEL), lambda i: (0, 0))],
        out_specs=[rows, rows],
        out_shape=[jax.ShapeDtypeStruct((TOK, D_MODEL), F32),
                   jax.ShapeDtypeStruct((TOK, D_MODEL), BF16)],
        compiler_params=pltpu.CompilerParams(
            dimension_semantics=("parallel",), vmem_limit_bytes=VMEM_LIMIT),
        name="mix",
    )(ssm_sb, ret_out, x2d, g1, sc2, sh2, norm_w, w_glu_bf, b_glu, w_out_bf)


_FFN_UP_TM = 2048
_FFN_UP_SUB = 1024
_FFN_UP_TF = 512
_FFN_DOWN_TM = 512
_FFN_DOWN_SUB = 256


def _ffn_up_kernel(h_ref, wg_ref, wu_ref, wd_ref, o_ref, wd_bf_ref, wg_scr, wu_scr):
    @pl.when(pl.program_id(1) == 0)
    def _():
        wg_scr[...] = wg_ref[...].astype(BF16)
        wu_scr[...] = wu_ref[...].astype(BF16)

    wd_bf_ref[...] = wd_ref[...].astype(BF16)
    for r in range(_FFN_UP_TM // _FFN_UP_SUB):
        rs = slice(r * _FFN_UP_SUB, (r + 1) * _FFN_UP_SUB)
        h = h_ref[rs, :]
        gate = jnp.dot(h, wg_scr[...], preferred_element_type=F32)
        up = jnp.dot(h, wu_scr[...], preferred_element_type=F32)
        o_ref[rs, :] = (_silu(gate) * up).astype(BF16)


def _ffn_up(h2, w_gate_up, w_down):
    tm, tf = _FFN_UP_TM, _FFN_UP_TF
    nf = D_FF // tf
    nm = TOK // tm
    slab = D_FF // (nf * nm)
    assert slab * nf * nm == D_FF and slab % 16 == 0
    return pl.pallas_call(
        _ffn_up_kernel,
        grid=(nf, nm),
        in_specs=[pl.BlockSpec((tm, D_MODEL), lambda f, i: (i, 0)),
                  pl.BlockSpec((D_MODEL, tf), lambda f, i: (0, f)),
                  pl.BlockSpec((D_MODEL, tf), lambda f, i: (0, nf + f)),
                  pl.BlockSpec((slab, D_MODEL), lambda f, i: (f * nm + i, 0))],
        out_specs=[pl.BlockSpec((tm, tf), lambda f, i: (i, f)),
                   pl.BlockSpec((slab, D_MODEL), lambda f, i: (f * nm + i, 0))],
        out_shape=[jax.ShapeDtypeStruct((TOK, D_FF), BF16),
                   jax.ShapeDtypeStruct((D_FF, D_MODEL), BF16)],
        scratch_shapes=[pltpu.VMEM((D_MODEL, tf), BF16),
                        pltpu.VMEM((D_MODEL, tf), BF16)],
        compiler_params=pltpu.CompilerParams(
            dimension_semantics=("arbitrary", "arbitrary"), vmem_limit_bytes=VMEM_LIMIT),
        name="ffn_up",
    )(h2, w_gate_up, w_gate_up, w_down)


def _ffn_down_kernel(a_ref, x_ref, g2_ref, fw_ref, wd_ref, o_ref):
    for r in range(_FFN_DOWN_TM // _FFN_DOWN_SUB):
        rs = slice(r * _FFN_DOWN_SUB, (r + 1) * _FFN_DOWN_SUB)
        down = jnp.dot(a_ref[rs, :], wd_ref[...], preferred_element_type=F32)
        x2 = x_ref[rs, :] + g2_ref[...] * down
        ms = jnp.mean(x2 * x2, axis=-1, keepdims=True)
        o_ref[rs, :] = x2 * lax.rsqrt(ms + EPS) * fw_ref[...]


def _ffn_down(act, x1, g2, final_w, w_down_bf):
    tm = _FFN_DOWN_TM
    per_b = SEQ // tm
    rows = pl.BlockSpec((tm, D_MODEL), lambda i: (i, 0))
    return pl.pallas_call(
        _ffn_down_kernel,
        grid=(TOK // tm,),
        in_specs=[pl.BlockSpec((tm, D_FF), lambda i: (i, 0)),
                  rows,
                  pl.BlockSpec((None, 1, D_MODEL), lambda i: (i // per_b, 0, 0)),
                  pl.BlockSpec((1, D_MODEL), lambda i: (0, 0)),
                  pl.BlockSpec((D_FF, D_MODEL), lambda i: (0, 0))],
        out_specs=rows,
        out_shape=jax.ShapeDtypeStruct((TOK, D_MODEL), F32),
        compiler_params=pltpu.CompilerParams(
            dimension_semantics=("parallel",), vmem_limit_bytes=VMEM_LIMIT),
        name="ffn_down",
    )(act, x1, g2, final_w, w_down_bf)


def kernel(x, c, w_ada, b_ada, norm1_w, w_in, ret_norm_w, s5_a_re, s5_a_im, s5_log_step,
           s5_b_re, s5_b_im, s5_c_re, s5_c_im, s5_d, w_glu, b_glu, w_out, norm2_w,
           w_gate_up, w_down, final_norm_w):
    x2d = x.reshape(TOK, D_MODEL)
    layer = 0
    mod, mt, wt, vt, a16r, a16i, w_in_bf = _prep(
        c, w_ada[layer], b_ada[layer],
        s5_a_re[layer], s5_a_im[layer], s5_log_step[layer], s5_b_re[layer], s5_b_im[layer],
        s5_c_re[layer], s5_c_im[layer], s5_d[layer], w_in[layer])
    sh1, sc1, g1, sh2, sc2, g2 = [m.reshape(BATCH, 1, D_MODEL) for m in jnp.split(mod, 6, axis=-1)]

    qkvg, u_sb = _inproj(x2d, sc1, sh1, norm1_w[layer].reshape(1, D_MODEL), w_in_bf)
    ret_out, w_out_bf, w_glu_bf = _retention(qkvg, ret_norm_w[layer].reshape(1, D_RET),
                                             w_out[layer], w_glu[layer])
    y_rows = _s5(u_sb.reshape(D_S5 // LANES, TOK, LANES), mt, wt, vt, a16r, a16i)
    ssm_sb = y_rows.reshape(D_S5 // LANES, S5_NSUB, BATCH, S5_SUB, LANES)

    x1, h2 = _mix(ssm_sb, ret_out, x2d, g1, sc2, sh2, norm2_w[layer].reshape(1, D_MODEL),
                  w_glu_bf, b_glu[layer].reshape(1, D_S5), w_out_bf)
    act, w_down_bf = _ffn_up(h2, w_gate_up[layer], w_down[layer])
    out = _ffn_down(act, x1, g2, final_norm_w.reshape(1, D_MODEL), w_down_bf)
    return out.reshape(BATCH, SEQ, D_MODEL)
```

```python
import math

import numpy as np
import jax
import jax.numpy as jnp
from jax import lax
from jax.experimental import pallas as pl
from jax.experimental.pallas import tpu as pltpu

D_MODEL = 2048
BATCH = 4
SEQ = 2048
TOK = BATCH * SEQ
D_RET = 1024
D_S5 = 1024
RET_HEADS = 4
RET_HEAD_DIM = 256
RET_CHUNK = 128
S5_GROUP = 16
S5_GROUPS = 64
S5_STATE = 64
S5_SUB = 16
S5_ROW = S5_SUB * S5_GROUP
S5_NSUB = SEQ // S5_SUB
S5_COLS = S5_NSUB * BATCH
D_FF = 5632
ROPE_BASE = 10000.0
EPS = 1e-6
LANES = 128

F32 = jnp.float32
BF16 = jnp.bfloat16
HI = lax.Precision.HIGHEST
VMEM_LIMIT = 58 * 1024 * 1024


def _silu(v):
    return v * jax.nn.sigmoid(v)


_INPROJ_TM = 512
_INPROJ_SUB = 256
_INPROJ_NS = _INPROJ_SUB // S5_SUB


def _inproj_kernel(x_ref, sc_ref, sh_ref, nw_ref, w_ref, o_ref, u_ref, h_scr, cos_scr, sin_scr):
    tm = _INPROJ_TM
    half = RET_HEAD_DIM // 2
    per_b = SEQ // tm
    i = pl.program_id(0)
    slot = i % per_b

    @pl.when(i < per_b)
    def _():
        pos = (lax.broadcasted_iota(jnp.int32, (tm, half), 0) + slot * tm).astype(F32)
        lane = lax.broadcasted_iota(jnp.int32, (tm, half), 1).astype(F32)
        ang = pos * jnp.exp(lane * (-math.log(ROPE_BASE) / half))
        cos_scr[slot] = jnp.cos(ang)
        sin_scr[slot] = jnp.sin(ang)

    gain = nw_ref[...] * (1.0 + sc_ref[...])
    shift = sh_ref[...]
    for r in range(tm // _INPROJ_SUB):
        rs = slice(r * _INPROJ_SUB, (r + 1) * _INPROJ_SUB)
        x = x_ref[rs, :]
        ms = jnp.mean(x * x, axis=-1, keepdims=True)
        h_scr[rs, :] = (x * lax.rsqrt(ms + EPS) * gain + shift).astype(BF16)
        cs = cos_scr[slot, rs, :]
        sn = sin_scr[slot, rs, :]
        for t in range(5):
            acc = jnp.dot(h_scr[rs, :], w_ref[:, t * D_RET:(t + 1) * D_RET],
                          preferred_element_type=F32)
            if t < 2:
                for hh in range(RET_HEADS):
                    c0 = hh * RET_HEAD_DIM
                    x1 = acc[:, c0:c0 + half]
                    x2 = acc[:, c0 + half:c0 + 2 * half]
                    o_ref[t * RET_HEADS + hh, rs, 0:half] = (x1 * cs - x2 * sn).astype(BF16)
                    o_ref[t * RET_HEADS + hh, rs, half:2 * half] = (x2 * cs + x1 * sn).astype(BF16)
            elif t < 4:
                for hh in range(RET_HEADS):
                    c0 = hh * RET_HEAD_DIM
                    o_ref[t * RET_HEADS + hh, rs, :] = acc[:, c0:c0 + RET_HEAD_DIM].astype(BF16)
            else:
                for k in range(D_S5 // LANES):
                    u_ref[k, r * _INPROJ_NS:(r + 1) * _INPROJ_NS] = \
                        acc[:, k * LANES:(k + 1) * LANES].reshape(_INPROJ_NS, S5_SUB, LANES)


def _inproj(x2d, sc1, sh1, norm_w, w_in_bf):
    tm = _INPROJ_TM
    per_b = SEQ // tm
    half = RET_HEAD_DIM // 2
    assert w_in_bf.shape == (D_MODEL, 4 * D_RET + D_S5) and D_S5 == D_RET
    return pl.pallas_call(
        _inproj_kernel,
        grid=(TOK // tm,),
        in_specs=[pl.BlockSpec((tm, D_MODEL), lambda i: (i, 0)),
                  pl.BlockSpec((None, 1, D_MODEL), lambda i: (i // per_b, 0, 0)),
                  pl.BlockSpec((None, 1, D_MODEL), lambda i: (i // per_b, 0, 0)),
                  pl.BlockSpec((1, D_MODEL), lambda i: (0, 0)),
                  pl.BlockSpec(w_in_bf.shape, lambda i: (0, 0))],
        out_specs=[pl.BlockSpec((4 * RET_HEADS, tm, RET_HEAD_DIM), lambda i: (0, i, 0)),
                   pl.BlockSpec((D_S5 // LANES, tm // S5_SUB, None, S5_SUB, LANES),
                                lambda i: (0, i % per_b, i // per_b, 0, 0))],
        out_shape=[jax.ShapeDtypeStruct((4 * RET_HEADS, TOK, RET_HEAD_DIM), BF16),
                   jax.ShapeDtypeStruct((D_S5 // LANES, S5_NSUB, BATCH, S5_SUB, LANES), F32)],
        scratch_shapes=[pltpu.VMEM((tm, D_MODEL), BF16),
                        pltpu.VMEM((per_b, tm, half), F32),
                        pltpu.VMEM((per_b, tm, half), F32)],
        compiler_params=pltpu.CompilerParams(
            dimension_semantics=("arbitrary",), vmem_limit_bytes=VMEM_LIMIT),
        name="inproj",
    )(x2d, sc1, sh1, norm_w, w_in_bf)


_RET_BLK = 256


def _ret_kernel(lg_ref, q_ref, k_ref, v_ref, g_ref, w_ref, wout_ref, wglu_ref,
                o_ref, wout_bf_ref, wglu_bf_ref,
                kv_scr, prev_scr, intra_scr, kdec_scr, qdec_scr):
    wout_bf_ref[...] = wout_ref[...].astype(BF16)
    wglu_bf_ref[...] = wglu_ref[...].astype(BF16)
    C = _RET_BLK
    dh = RET_HEAD_DIM
    nc = SEQ // C
    scale = dh ** -0.5
    lg = lg_ref[0:1, :]
    ii = lax.broadcasted_iota(jnp.int32, (C, C), 0)
    jj = lax.broadcasted_iota(jnp.int32, (C, C), 1)
    diff = (ii - jj).astype(F32)
    intra_scr[...] = jnp.where(diff >= 0.0, jnp.exp(lg * jnp.maximum(diff, 0.0)), 0.0) * scale
    row = lax.broadcasted_iota(jnp.int32, (C, dh), 0).astype(F32)
    kdec_scr[...] = jnp.exp(lg * (C - 1.0 - row)) * scale
    qdec_scr[...] = jnp.exp(lg * (row + 1.0))
    block_decay = jnp.exp(lg * float(C))
    gn_w = w_ref[...]

    for n in range(nc):
        rs = slice(n * C, (n + 1) * C)
        kd = (k_ref[rs, :].astype(F32) * kdec_scr[...]).astype(BF16)
        kv_scr[n] = lax.dot_general(kd, v_ref[rs, :], (((0,), (0,)), ((), ())),
                                    preferred_element_type=F32)

    band = 64
    for rb in range(dh // band):
        bs = slice(rb * band, (rb + 1) * band)
        st = jnp.zeros((band, dh), F32)
        for n in range(nc):
            prev_scr[n, bs, :] = st.astype(BF16)
            if n + 1 < nc:
                st = st * block_decay + kv_scr[n, bs, :]

    for n in range(nc):
        rs = slice(n * C, (n + 1) * C)
        q = q_ref[rs, :]
        s = lax.dot_general(q, k_ref[rs, :], (((1,), (1,)), ((), ())),
                            preferred_element_type=F32) * intra_scr[...]
        y = jnp.dot(s.astype(BF16), v_ref[rs, :], preferred_element_type=F32)
        y = y + jnp.dot(q, prev_scr[n], preferred_element_type=F32) * qdec_scr[...]
        mu = jnp.mean(y, axis=-1, keepdims=True)
        yc = y - mu
        var = jnp.mean(yc * yc, axis=-1, keepdims=True)
        yn = yc * lax.rsqrt(var + EPS) * gn_w
        g = g_ref[rs, :].astype(F32)
        o_ref[rs, :] = (_silu(g) * yn).astype(BF16)


def _retention(qkvg, ret_norm_w, w_out, w_glu):
    dh = RET_HEAD_DIM
    nc = SEQ // _RET_BLK
    steps = BATCH * RET_HEADS
    so = w_out.shape[0] // steps
    sg = w_glu.shape[0] // steps
    assert so * steps == w_out.shape[0] and sg * steps == w_glu.shape[0] and sg % 16 == 0
    lg = np.log1p(-np.exp2(-5.0 - np.arange(RET_HEADS, dtype=np.float64)))
    lg_tab = jnp.asarray(np.broadcast_to(lg[:, None, None], (RET_HEADS, 8, dh)), F32)
    spec = lambda off: pl.BlockSpec((None, SEQ, dh), lambda b, h: (off + h, b, 0))
    slab_o = pl.BlockSpec((so, w_out.shape[1]), lambda b, h: (b * RET_HEADS + h, 0))
    slab_g = pl.BlockSpec((sg, w_glu.shape[1]), lambda b, h: (b * RET_HEADS + h, 0))
    return pl.pallas_call(
        _ret_kernel,
        grid=(BATCH, RET_HEADS),
        in_specs=[pl.BlockSpec((None, 8, dh), lambda b, h: (h, 0, 0)),
                  spec(0), spec(RET_HEADS), spec(2 * RET_HEADS), spec(3 * RET_HEADS),
                  pl.BlockSpec((1, dh), lambda b, h: (0, h)), slab_o, slab_g],
        out_specs=[pl.BlockSpec((None, SEQ, dh), lambda b, h: (h, b, 0)), slab_o, slab_g],
        out_shape=[jax.ShapeDtypeStruct((RET_HEADS, TOK, dh), BF16),
                   jax.ShapeDtypeStruct(w_out.shape, BF16),
                   jax.ShapeDtypeStruct(w_glu.shape, BF16)],
        scratch_shapes=[pltpu.VMEM((nc, dh, dh), F32),
                        pltpu.VMEM((nc, dh, dh), BF16),
                        pltpu.VMEM((_RET_BLK, _RET_BLK), F32),
                        pltpu.VMEM((_RET_BLK, dh), F32),
                        pltpu.VMEM((_RET_BLK, dh), F32)],
        compiler_params=pltpu.CompilerParams(
            dimension_semantics=("parallel", "parallel"), vmem_limit_bytes=VMEM_LIMIT),
        name="retention",
    )(lg_tab, qkvg, qkvg, qkvg, qkvg, ret_norm_w, w_out, w_glu)


_S5GEN_GG = 4


def _prep_kernel(c_ref, wada_ref, bada_ref,
                 ls_ref, ar_ref, ai_ref, bt_ref, cr_ref, ci_ref, c4r_ref, c4i_ref, d_ref, win_ref,
                 mod_ref, mt_ref, wt_ref, vt_ref, a16r_ref, a16i_ref, win_bf_ref):
    cond = _silu(c_ref[...])
    mod_ref[...] = jnp.dot(cond.astype(BF16), wada_ref[...].astype(BF16),
                           preferred_element_type=F32) + bada_ref[...]
    win_bf_ref[...] = win_ref[...].astype(BF16)
    N, P, R = S5_STATE, S5_GROUP, S5_ROW
    re_half = lax.broadcasted_iota(jnp.int32, (P, 2 * N), 1) < N
    sub_e = lax.broadcasted_iota(jnp.int32, (P, 2 * N), 0)
    lane_p = lax.broadcasted_iota(jnp.int32, (P, R), 1)
    sub_p = lax.broadcasted_iota(jnp.int32, (P, R), 0)
    lag0_diag = lane_p == (R - P) + sub_p
    is_re = lane_p < 2 * N
    lane_slot = lax.shift_right_logical(lane_p, 6) & 1

    def hdot(a, b):
        return jnp.dot(a, b, precision=HI, preferred_element_type=F32)

    def cmul(xr, xi, yr, yi):
        return xr * yr - xi * yi, xr * yi + xi * yr

    for k in range(_S5GEN_GG):
        odd = k % 2
        dt = jnp.exp(ls_ref[k])
        ar = ar_ref[k]
        ai = ai_ref[k]
        mag = jnp.exp(ar * dt)
        lbr = mag * jnp.cos(ai * dt)
        lbi = mag * jnp.sin(ai * dt)
        den = ar * ar + ai * ai
        zr = ((lbr - 1.0) * ar + lbi * ai) / den
        zi = (lbi * ar - (lbr - 1.0) * ai) / den

        pw_r = jnp.ones((P, 2 * N), F32)
        pw_i = jnp.zeros((P, 2 * N), F32)
        sq_r, sq_i = lbr, lbi
        for bit in range(4):
            nr, ni = cmul(pw_r, pw_i, sq_r, sq_i)
            take = (lax.shift_right_logical(sub_e, bit) & 1) == 1
            pw_r = jnp.where(take, nr, pw_r)
            pw_i = jnp.where(take, ni, pw_i)
            if bit < 3:
                sq_r, sq_i = cmul(sq_r, sq_i, sq_r, sq_i)
        p1_r, p1_i = cmul(pw_r, pw_i, lbr, lbi)
        a16r_ref[k] = p1_r[S5_SUB - 1:S5_SUB, :]
        a16i_ref[k] = p1_i[S5_SUB - 1:S5_SUB, :]

        bt = bt_ref[k]
        x = zr * bt + jnp.where(re_half[0:1], -zi, zi) * pltpu.roll(bt, N, axis=1)
        xs = pltpu.roll(x, N, axis=1)
        pw_is = jnp.where(re_half, -pw_i, pw_i)
        w_rows = jnp.concatenate(
            [pw_r[e:e + 1] * x + pw_is[e:e + 1] * xs for e in range(S5_SUB - 1, -1, -1)], axis=0)
        w_t = w_rows.T
        w_r = w_t[0:N]
        w_i = w_t[N:2 * N]
        zero = jnp.zeros_like(w_r)
        wt_ref[k, 0:2 * N, :] = jnp.concatenate(
            [zero, w_r] if odd else [w_r, zero], axis=0).astype(BF16)
        wt_ref[k, 2 * N:4 * N, :] = jnp.concatenate(
            [zero, w_i] if odd else [w_i, zero], axis=0).astype(BF16)

        c_re = cr_ref[k]
        c_im = ci_ref[k]
        krev = hdot(c_re, w_r) - hdot(c_im, w_i)
        krev = krev + jnp.where(lag0_diag, d_ref[k], 0.0)
        for tp in range(S5_SUB):
            width = (tp + 1) * P
            piece = krev if width == R else jnp.where(
                lane_p < width, pltpu.roll(krev, width, axis=1), 0.0)
            mt_ref[k, tp * P:(tp + 1) * P, :] = piece.astype(BF16)

        c4_r = c4r_ref[k]
        c4_i = c4i_ref[k]
        keep = lane_slot == odd
        for tp in range(S5_SUB):
            lr = jnp.concatenate([p1_r[tp:tp + 1], p1_r[tp:tp + 1]], axis=1)
            li = jnp.concatenate([p1_i[tp:tp + 1], p1_i[tp:tp + 1]], axis=1)
            val = jnp.where(is_re, c4_r * lr - c4_i * li, -(c4_r * li + c4_i * lr))
            vt_ref[k, tp * P:(tp + 1) * P, :] = jnp.where(keep, val, 0.0).astype(BF16)


def _prep(c, w_ada, b_ada, a_re, a_im, log_step, b_re, b_im, c_re, c_im, d_skip, w_in):
    G, N, P, R = S5_GROUPS, S5_STATE, S5_GROUP, S5_ROW
    gg = _S5GEN_GG
    steps = G // gg
    sw = w_in.shape[0] // steps
    n_mod = w_ada.shape[1]
    tn = n_mod // steps
    assert sw * steps == w_in.shape[0] and sw % 16 == 0
    assert tn * steps == n_mod and tn % LANES == 0
    slab_w = pl.BlockSpec((sw, w_in.shape[1]), lambda g: (g, 0))
    dup = lambda a: jnp.tile(a, (1, 2)).reshape(G, 1, 2 * N)
    bt = jnp.concatenate([jnp.swapaxes(b_re, 1, 2), jnp.swapaxes(b_im, 1, 2)], axis=-1)
    tile4 = lambda a: jnp.tile(a, (1, 1, R // N))
    blk = lambda s: pl.BlockSpec((gg,) + s, lambda g: (g, 0, 0))
    return pl.pallas_call(
        _prep_kernel,
        grid=(steps,),
        in_specs=[pl.BlockSpec((BATCH, D_MODEL), lambda g: (0, 0)),
                  pl.BlockSpec((D_MODEL, tn), lambda g: (0, g)),
                  pl.BlockSpec((1, tn), lambda g: (0, g)),
                  blk((1, 1)), blk((1, 2 * N)), blk((1, 2 * N)), blk((P, 2 * N)),
                  blk((P, N)), blk((P, N)), blk((P, R)), blk((P, R)), blk((P, 1)), slab_w],
        out_specs=[pl.BlockSpec((BATCH, tn), lambda g: (0, g)),
                   blk((R, R)), blk((R, R)), blk((R, R)), blk((1, 2 * N)), blk((1, 2 * N)),
                   slab_w],
        out_shape=[jax.ShapeDtypeStruct((BATCH, n_mod), F32)] +
                  [jax.ShapeDtypeStruct((G, R, R), BF16)] * 3 +
                  [jax.ShapeDtypeStruct((G, 1, 2 * N), F32)] * 2 +
                  [jax.ShapeDtypeStruct(w_in.shape, BF16)],
        compiler_params=pltpu.CompilerParams(
            dimension_semantics=("parallel",), vmem_limit_bytes=VMEM_LIMIT),
        name="prep",
    )(c, w_ada, b_ada.reshape(1, n_mod),
      log_step.reshape(G, 1, 1), dup(a_re), dup(a_im), bt, c_re, c_im, tile4(c_re), tile4(c_im),
      d_skip.reshape(G, P, 1), w_in)


_S5_GB = LANES // S5_GROUP
_S5_NP = _S5_GB // 2


def _s5_kernel(a16r_ref, a16i_ref, u_ref, mt_ref, wt_ref, vt_ref, o_ref,
               ut_scr, yt_scr, e_scr, p_scr):
    P = S5_GROUP
    N2 = 2 * S5_STATE
    cols = S5_COLS
    nblk = cols // LANES
    nt = cols // 8

    def load_body(tau, carry):
        for c in range(nblk):
            x = u_ref[pl.ds(tau + c * LANES * S5_SUB, LANES, stride=S5_SUB), :]
            xt = x.T
            r0 = pl.multiple_of(tau * P, P)
            for k in range(_S5_GB):
                ut_scr[k, pl.ds(r0, P), c * LANES:(c + 1) * LANES] = \
                    xt[k * P:(k + 1) * P, :].astype(BF16)
        return carry

    lax.fori_loop(0, S5_SUB, load_body, 0, unroll=4)

    for pr in range(_S5_NP):
        et = (jnp.dot(wt_ref[2 * pr], ut_scr[2 * pr], preferred_element_type=F32) +
              jnp.dot(wt_ref[2 * pr + 1], ut_scr[2 * pr + 1], preferred_element_type=F32))
        e_scr[pr] = et.T

    sub = lax.broadcasted_iota(jnp.int32, (8, N2), 0)
    lo = sub < BATCH
    first = lax.broadcasted_iota(jnp.int32, (1, N2), 1) < S5_STATE
    a_r, a_i = [], []
    for pr in range(_S5_NP):
        a_r.append(jnp.broadcast_to(
            jnp.where(first, a16r_ref[2 * pr], a16r_ref[2 * pr + 1]), (8, N2)))
        a_i.append(jnp.broadcast_to(
            jnp.where(first, a16i_ref[2 * pr], a16i_ref[2 * pr + 1]), (8, N2)))

    def scan_body(t, carry):
        r0 = pl.multiple_of(t * 8, 8)
        new = []
        for pr in range(_S5_NP):
            s_r, s_i = carry[2 * pr], carry[2 * pr + 1]
            e_r = e_scr[pr, pl.ds(r0, 8), 0:N2]
            e_i = e_scr[pr, pl.ds(r0, 8), N2:2 * N2]
            x_r = pltpu.roll(e_r, BATCH, axis=0)
            x_i = pltpu.roll(e_i, BATCH, axis=0)
            elo_r = jnp.where(lo, e_r, x_r)
            elo_i = jnp.where(lo, e_i, x_i)
            ehi_r = jnp.where(lo, x_r, e_r)
            ehi_i = jnp.where(lo, x_i, e_i)
            t_r = a_r[pr] * s_r - a_i[pr] * s_i + elo_r
            t_i = a_r[pr] * s_i + a_i[pr] * s_r + elo_i
            p_scr[pr, pl.ds(r0, 8), 0:N2] = jnp.where(lo, s_r, t_r)
            p_scr[pr, pl.ds(r0, 8), N2:2 * N2] = jnp.where(lo, s_i, t_i)
            new.append(a_r[pr] * t_r - a_i[pr] * t_i + ehi_r)
            new.append(a_r[pr] * t_i + a_i[pr] * t_r + ehi_i)
        return tuple(new)

    zero = jnp.zeros((8, N2), F32)
    lax.fori_loop(0, nt, scan_body, tuple(zero for _ in range(2 * _S5_NP)))

    for pr in range(_S5_NP):
        pt = p_scr[pr].T.astype(BF16)
        for k in range(2):
            gi = 2 * pr + k
            yt = (jnp.dot(mt_ref[gi], ut_scr[gi], preferred_element_type=F32) +
                  jnp.dot(vt_ref[gi], pt, preferred_element_type=F32))
            for tau in range(S5_SUB):
                yt_scr[tau, gi * P:(gi + 1) * P, :] = yt[tau * P:(tau + 1) * P, :]

    def store_body(tau, carry):
        o_ref[pl.ds(tau, cols, stride=S5_SUB), :] = yt_scr[tau].T
        return carry

    lax.fori_loop(0, S5_SUB, store_body, 0, unroll=4)


def _s5(u_rows, mt, wt, vt, a16r, a16i):
    G, N, R = S5_GROUPS, S5_STATE, S5_ROW
    gb = _S5_GB
    vec = pl.BlockSpec((gb, 1, 2 * N), lambda i: (i, 0, 0))
    mat = pl.BlockSpec((gb, R, R), lambda i: (i, 0, 0))
    slab = pl.BlockSpec((None, TOK, LANES), lambda i: (i, 0, 0))
    return pl.pallas_call(
        _s5_kernel,
        grid=(G // gb,),
        in_specs=[vec, vec, slab, mat, mat, mat],
        out_specs=slab,
        out_shape=jax.ShapeDtypeStruct((D_S5 // LANES, TOK, LANES), F32),
        scratch_shapes=[pltpu.VMEM((gb, R, S5_COLS), BF16),
                        pltpu.VMEM((S5_SUB, LANES, S5_COLS), F32),
                        pltpu.VMEM((_S5_NP, S5_COLS, 4 * N), F32),
                        pltpu.VMEM((_S5_NP, S5_COLS, 4 * N), F32)],
        compiler_params=pltpu.CompilerParams(
            dimension_semantics=("parallel",), vmem_limit_bytes=VMEM_LIMIT),
        name="s5",
    )(a16r, a16i, u_rows, mt, wt, vt)


_MIX_TM = 512
_MIX_SUB = 256
_MIX_NS = _MIX_TM // S5_SUB


def _mix_kernel(ssm_ref, ret_ref, x_ref, g1_ref, sc_ref, sh_ref, nw_ref, wglu_ref, bglu_ref,
                wout_ref, o_ref, h_ref):
    gain = nw_ref[...] * (1.0 + sc_ref[...])
    for r in range(_MIX_TM // _MIX_SUB):
        rs = slice(r * _MIX_SUB, (r + 1) * _MIX_SUB)
        ns = _MIX_SUB // S5_SUB
        s = jnp.concatenate([ssm_ref[k, r * ns:(r + 1) * ns].reshape(_MIX_SUB, LANES)
                             for k in range(D_S5 // LANES)], axis=1)
        cdf = 0.5 * (1.0 + jnp.tanh(math.sqrt(2.0 / math.pi) * (s + 0.044715 * (s * s * s))))
        sg = s * cdf
        z = jnp.dot(sg.astype(BF16), wglu_ref[...], preferred_element_type=F32) + bglu_ref[...]
        so = sg * jax.nn.sigmoid(z)
        mix = None
        for hh in range(RET_HEADS):
            part = jnp.dot(ret_ref[hh, rs, :],
                           wout_ref[hh * RET_HEAD_DIM:(hh + 1) * RET_HEAD_DIM, :],
                           preferred_element_type=F32)
            mix = part if mix is None else mix + part
        mix = mix + jnp.dot(so.astype(BF16), wout_ref[D_RET:, :], preferred_element_type=F32)
        x1 = x_ref[rs, :] + g1_ref[...] * mix
        o_ref[rs, :] = x1
        ms = jnp.mean(x1 * x1, axis=-1, keepdims=True)
        h_ref[rs, :] = (x1 * lax.rsqrt(ms + EPS) * gain + sh_ref[...]).astype(BF16)


def _mix(ssm_sb, ret_out, x2d, g1, sc2, sh2, norm_w, w_glu_bf, b_glu, w_out_bf):
    tm = _MIX_TM
    per_b = SEQ // tm
    mod = pl.BlockSpec((None, 1, D_MODEL), lambda i: (i // per_b, 0, 0))
    rows = pl.BlockSpec((tm, D_MODEL), lambda i: (i, 0))
    return pl.pallas_call(
        _mix_kernel,
        grid=(TOK // tm,),
        in_specs=[pl.BlockSpec((D_S5 // LANES, _MIX_NS, None, S5_SUB, LANES),
                               lambda i: (0, i % per_b, i // per_b, 0, 0)),
                  pl.BlockSpec((RET_HEADS, tm, RET_HEAD_DIM), lambda i: (0, i, 0)),
                  rows, mod, mod, mod,
                  pl.BlockSpec((1, D_MODEL), lambda i: (0, 0)),
                  pl.BlockSpec((D_S5, D_S5), lambda i: (0, 0)),
                  pl.BlockSpec((1, D_S5), lambda i: (0, 0)),
                  pl.BlockSpec((D_RET + D_S5, D_MODEL), lambda i: (0, 0))],
        out_specs=[rows, rows],
        out_shape=[jax.ShapeDtypeStruct((TOK, D_MODEL), F32),
                   jax.ShapeDtypeStruct((TOK, D_MODEL), BF16)],
        compiler_params=pltpu.CompilerParams(
            dimension_semantics=("parallel",), vmem_limit_bytes=VMEM_LIMIT),
        name="mix",
    )(ssm_sb, ret_out, x2d, g1, sc2, sh2, norm_w, w_glu_bf, b_glu, w_out_bf)


_FFN_UP_TM = 2048
_FFN_UP_SUB = 1024
_FFN_UP_TF = 512
_FFN_DOWN_TM = 512
_FFN_DOWN_SUB = 256


def _ffn_up_kernel(h_ref, wg_ref, wu_ref, wd_ref, o_ref, wd_bf_ref, wg_scr, wu_scr):
    @pl.when(pl.program_id(1) == 0)
    def _():
        wg_scr[...] = wg_ref[...].astype(BF16)
        wu_scr[...] = wu_ref[...].astype(BF16)

    wd_bf_ref[...] = wd_ref[...].astype(BF16)
    for r in range(_FFN_UP_TM // _FFN_UP_SUB):
        rs = slice(r * _FFN_UP_SUB, (r + 1) * _FFN_UP_SUB)
        h = h_ref[rs, :]
        gate = jnp.dot(h, wg_scr[...], preferred_element_type=F32)
        up = jnp.dot(h, wu_scr[...], preferred_element_type=F32)
        o_ref[rs, :] = (_silu(gate) * up).astype(BF16)


def _ffn_up(h2, w_gate_up, w_down):
    tm, tf = _FFN_UP_TM, _FFN_UP_TF
    nf = D_FF // tf
    nm = TOK // tm
    slab = D_FF // (nf * nm)
    assert slab * nf * nm == D_FF and slab % 16 == 0
    return pl.pallas_call(
        _ffn_up_kernel,
        grid=(nf, nm),
        in_specs=[pl.BlockSpec((tm, D_MODEL), lambda f, i: (i, 0)),
                  pl.BlockSpec((D_MODEL, tf), lambda f, i: (0, f)),
                  pl.BlockSpec((D_MODEL, tf), lambda f, i: (0, nf + f)),
                  pl.BlockSpec((slab, D_MODEL), lambda f, i: (f * nm + i, 0))],
        out_specs=[pl.BlockSpec((tm, tf), lambda f, i: (i, f)),
                   pl.BlockSpec((slab, D_MODEL), lambda f, i: (f * nm + i, 0))],
        out_shape=[jax.ShapeDtypeStruct((TOK, D_FF), BF16),
                   jax.ShapeDtypeStruct((D_FF, D_MODEL), BF16)],
        scratch_shapes=[pltpu.VMEM((D_MODEL, tf), BF16),
                        pltpu.VMEM((D_MODEL, tf), BF16)],
        compiler_params=pltpu.CompilerParams(
            dimension_semantics=("arbitrary", "arbitrary"), vmem_limit_bytes=VMEM_LIMIT),
        name="ffn_up",
    )(h2, w_gate_up, w_gate_up, w_down)


def _ffn_down_kernel(a_ref, x_ref, g2_ref, fw_ref, wd_ref, o_ref):
    for r in range(_FFN_DOWN_TM // _FFN_DOWN_SUB):
        rs = slice(r * _FFN_DOWN_SUB, (r + 1) * _FFN_DOWN_SUB)
        down = jnp.dot(a_ref[rs, :], wd_ref[...], preferred_element_type=F32)
        x2 = x_ref[rs, :] + g2_ref[...] * down
        ms = jnp.mean(x2 * x2, axis=-1, keepdims=True)
        o_ref[rs, :] = x2 * lax.rsqrt(ms + EPS) * fw_ref[...]


def _ffn_down(act, x1, g2, final_w, w_down_bf):
    tm = _FFN_DOWN_TM
    per_b = SEQ // tm
    rows = pl.BlockSpec((tm, D_MODEL), lambda i: (i, 0))
    return pl.pallas_call(
        _ffn_down_kernel,
        grid=(TOK // tm,),
        in_specs=[pl.BlockSpec((tm, D_FF), lambda i: (i, 0)),
                  rows,
                  pl.BlockSpec((None, 1, D_MODEL), lambda i: (i // per_b, 0, 0)),
                  pl.BlockSpec((1, D_MODEL), lambda i: (0, 0)),
                  pl.BlockSpec((D_FF, D_MODEL), lambda i: (0, 0))],
        out_specs=rows,
        out_shape=jax.ShapeDtypeStruct((TOK, D_MODEL), F32),
        compiler_params=pltpu.CompilerParams(
            dimension_semantics=("parallel",), vmem_limit_bytes=VMEM_LIMIT),
        name="ffn_down",
    )(act, x1, g2, final_w, w_down_bf)


def kernel(x, c, w_ada, b_ada, norm1_w, w_in, ret_norm_w, s5_a_re, s5_a_im, s5_log_step,
           s5_b_re, s5_b_im, s5_c_re, s5_c_im, s5_d, w_glu, b_glu, w_out, norm2_w,
           w_gate_up, w_down, final_norm_w):
    x2d = x.reshape(TOK, D_MODEL)
    layer = 0
    mod, mt, wt, vt, a16r, a16i, w_in_bf = _prep(
        c, w_ada[layer], b_ada[layer],
        s5_a_re[layer], s5_a_im[layer], s5_log_step[layer], s5_b_re[layer], s5_b_im[layer],
        s5_c_re[layer], s5_c_im[layer], s5_d[layer], w_in[layer])
    sh1, sc1, g1, sh2, sc2, g2 = [m.reshape(BATCH, 1, D_MODEL) for m in jnp.split(mod, 6, axis=-1)]

    qkvg, u_sb = _inproj(x2d, sc1, sh1, norm1_w[layer].reshape(1, D_MODEL), w_in_bf)
    ret_out, w_out_bf, w_glu_bf = _retention(qkvg, ret_norm_w[layer].reshape(1, D_RET),
                                             w_out[layer], w_glu[layer])
    y_rows = _s5(u_sb.reshape(D_S5 // LANES, TOK, LANES), mt, wt, vt, a16r, a16i)
    ssm_sb = y_rows.reshape(D_S5 // LANES, S5_NSUB, BATCH, S5_SUB, LANES)

    x1, h2 = _mix(ssm_sb, ret_out, x2d, g1, sc2, sh2, norm2_w[layer].reshape(1, D_MODEL),
                  w_glu_bf, b_glu[layer].reshape(1, D_S5), w_out_bf)
    act, w_down_bf = _ffn_up(h2, w_gate_up[layer], w_down[layer])
    out = _ffn_down(act, x1, g2, final_norm_w.reshape(1, D_MODEL), w_down_bf)
    return out.reshape(BATCH, SEQ, D_MODEL)
```

```python
import math

import numpy as np
import jax
import jax.numpy as jnp
from jax import lax
from jax.experimental import pallas as pl
from jax.experimental.pallas import tpu as pltpu

D_MODEL = 2048
BATCH = 4
SEQ = 2048
TOK = BATCH * SEQ
D_RET = 1024
D_S5 = 1024
RET_HEADS = 4
RET_HEAD_DIM = 256
RET_CHUNK = 128
S5_GROUP = 16
S5_GROUPS = 64
S5_STATE = 64
S5_SUB = 16
S5_ROW = S5_SUB * S5_GROUP
S5_NSUB = SEQ // S5_SUB
S5_COLS = S5_NSUB * BATCH
D_FF = 5632
ROPE_BASE = 10000.0
EPS = 1e-6
LANES = 128

F32 = jnp.float32
BF16 = jnp.bfloat16
HI = lax.Precision.HIGHEST
VMEM_LIMIT = 58 * 1024 * 1024


def _silu(v):
    return v * jax.nn.sigmoid(v)


_INPROJ_TM = 512
_INPROJ_SUB = 256
_INPROJ_NS = _INPROJ_SUB // S5_SUB


def _inproj_kernel(x_ref, sc_ref, sh_ref, nw_ref, w_ref, o_ref, u_ref, h_scr, cos_scr, sin_scr):
    tm = _INPROJ_TM
    half = RET_HEAD_DIM // 2
    per_b = SEQ // tm
    i = pl.program_id(0)
    slot = i % per_b

    @pl.when(i < per_b)
    def _():
        pos = (lax.broadcasted_iota(jnp.int32, (tm, half), 0) + slot * tm).astype(F32)
        lane = lax.broadcasted_iota(jnp.int32, (tm, half), 1).astype(F32)
        ang = pos * jnp.exp(lane * (-math.log(ROPE_BASE) / half))
        cos_scr[slot] = jnp.cos(ang)
        sin_scr[slot] = jnp.sin(ang)

    gain = nw_ref[...] * (1.0 + sc_ref[...])
    shift = sh_ref[...]
    for r in range(tm // _INPROJ_SUB):
        rs = slice(r * _INPROJ_SUB, (r + 1) * _INPROJ_SUB)
        x = x_ref[rs, :]
        ms = jnp.mean(x * x, axis=-1, keepdims=True)
        h_scr[rs, :] = (x * lax.rsqrt(ms + EPS) * gain + shift).astype(BF16)
        cs = cos_scr[slot, rs, :]
        sn = sin_scr[slot, rs, :]
        for t in range(5):
            acc = jnp.dot(h_scr[rs, :], w_ref[:, t * D_RET:(t + 1) * D_RET],
                          preferred_element_type=F32)
            if t < 2:
                for hh in range(RET_HEADS):
                    c0 = hh * RET_HEAD_DIM
                    x1 = acc[:, c0:c0 + half]
                    x2 = acc[:, c0 + half:c0 + 2 * half]
                    o_ref[t * RET_HEADS + hh, rs, 0:half] = (x1 * cs - x2 * sn).astype(BF16)
                    o_ref[t * RET_HEADS + hh, rs, half:2 * half] = (x2 * cs + x1 * sn).astype(BF16)
            elif t < 4:
                for hh in range(RET_HEADS):
                    c0 = hh * RET_HEAD_DIM
                    o_ref[t * RET_HEADS + hh, rs, :] = acc[:, c0:c0 + RET_HEAD_DIM].astype(BF16)
            else:
                for k in range(D_S5 // LANES):
                    u_ref[k, r * _INPROJ_NS:(r + 1) * _INPROJ_NS] = \
                        acc[:, k * LANES:(k + 1) * LANES].reshape(_INPROJ_NS, S5_SUB, LANES)


def _inproj(x2d, sc1, sh1, norm_w, w_in_bf):
    tm = _INPROJ_TM
    per_b = SEQ // tm
    half = RET_HEAD_DIM // 2
    assert w_in_bf.shape == (D_MODEL, 4 * D_RET + D_S5) and D_S5 == D_RET
    return pl.pallas_call(
        _inproj_kernel,
        grid=(TOK // tm,),
        in_specs=[pl.BlockSpec((tm, D_MODEL), lambda i: (i, 0)),
                  pl.BlockSpec((None, 1, D_MODEL), lambda i: (i // per_b, 0, 0)),
                  pl.BlockSpec((None, 1, D_MODEL), lambda i: (i // per_b, 0, 0)),
                  pl.BlockSpec((1, D_MODEL), lambda i: (0, 0)),
                  pl.BlockSpec(w_in_bf.shape, lambda i: (0, 0))],
        out_specs=[pl.BlockSpec((4 * RET_HEADS, tm, RET_HEAD_DIM), lambda i: (0, i, 0)),
                   pl.BlockSpec((D_S5 // LANES, tm // S5_SUB, None, S5_SUB, LANES),
                                lambda i: (0, i % per_b, i // per_b, 0, 0))],
        out_shape=[jax.ShapeDtypeStruct((4 * RET_HEADS, TOK, RET_HEAD_DIM), BF16),
                   jax.ShapeDtypeStruct((D_S5 // LANES, S5_NSUB, BATCH, S5_SUB, LANES), F32)],
        scratch_shapes=[pltpu.VMEM((tm, D_MODEL), BF16),
                        pltpu.VMEM((per_b, tm, half), F32),
                        pltpu.VMEM((per_b, tm, half), F32)],
        compiler_params=pltpu.CompilerParams(
            dimension_semantics=("arbitrary",), vmem_limit_bytes=VMEM_LIMIT),
        name="inproj",
    )(x2d, sc1, sh1, norm_w, w_in_bf)


_RET_BLK = 256


def _ret_kernel(lg_ref, q_ref, k_ref, v_ref, g_ref, w_ref, wout_ref, wglu_ref,
                o_ref, wout_bf_ref, wglu_bf_ref,
                kv_scr, prev_scr, intra_scr, kdec_scr, qdec_scr):
    wout_bf_ref[...] = wout_ref[...].astype(BF16)
    wglu_bf_ref[...] = wglu_ref[...].astype(BF16)
    C = _RET_BLK
    dh = RET_HEAD_DIM
    nc = SEQ // C
    scale = dh ** -0.5
    lg = lg_ref[0:1, :]
    ii = lax.broadcasted_iota(jnp.int32, (C, C), 0)
    jj = lax.broadcasted_iota(jnp.int32, (C, C), 1)
    diff = (ii - jj).astype(F32)
    intra_scr[...] = jnp.where(diff >= 0.0, jnp.exp(lg * jnp.maximum(diff, 0.0)), 0.0) * scale
    row = lax.broadcasted_iota(jnp.int32, (C, dh), 0).astype(F32)
    kdec_scr[...] = jnp.exp(lg * (C - 1.0 - row)) * scale
    qdec_scr[...] = jnp.exp(lg * (row + 1.0))
    block_decay = jnp.exp(lg * float(C))
    gn_w = w_ref[...]

    for n in range(nc):
        rs = slice(n * C, (n + 1) * C)
        kd = (k_ref[rs, :].astype(F32) * kdec_scr[...]).astype(BF16)
        kv_scr[n] = lax.dot_general(kd, v_ref[rs, :], (((0,), (0,)), ((), ())),
                                    preferred_element_type=F32)

    band = 64
    for rb in range(dh // band):
        bs = slice(rb * band, (rb + 1) * band)
        st = jnp.zeros((band, dh), F32)
        for n in range(nc):
            prev_scr[n, bs, :] = st.astype(BF16)
            if n + 1 < nc:
                st = st * block_decay + kv_scr[n, bs, :]

    for n in range(nc):
        rs = slice(n * C, (n + 1) * C)
        q = q_ref[rs, :]
        s = lax.dot_general(q, k_ref[rs, :], (((1,), (1,)), ((), ())),
                            preferred_element_type=F32) * intra_scr[...]
        y = jnp.dot(s.astype(BF16), v_ref[rs, :], preferred_element_type=F32)
        y = y + jnp.dot(q, prev_scr[n], preferred_element_type=F32) * qdec_scr[...]
        mu = jnp.mean(y, axis=-1, keepdims=True)
        yc = y - mu
        var = jnp.mean(yc * yc, axis=-1, keepdims=True)
        yn = yc * lax.rsqrt(var + EPS) * gn_w
        g = g_ref[rs, :].astype(F32)
        o_ref[rs, :] = (_silu(g) * yn).astype(BF16)


def _retention(qkvg, ret_norm_w, w_out, w_glu):
    dh = RET_HEAD_DIM
    nc = SEQ // _RET_BLK
    steps = BATCH * RET_HEADS
    so = w_out.shape[0] // steps
    sg = w_glu.shape[0] // steps
    assert so * steps == w_out.shape[0] and sg * steps == w_glu.shape[0] and sg % 16 == 0
    lg = np.log1p(-np.exp2(-5.0 - np.arange(RET_HEADS, dtype=np.float64)))
    lg_tab = jnp.asarray(np.broadcast_to(lg[:, None, None], (RET_HEADS, 8, dh)), F32)
    spec = lambda off: pl.BlockSpec((None, SEQ, dh), lambda b, h: (off + h, b, 0))
    slab_o = pl.BlockSpec((so, w_out.shape[1]), lambda b, h: (b * RET_HEADS + h, 0))
    slab_g = pl.BlockSpec((sg, w_glu.shape[1]), lambda b, h: (b * RET_HEADS + h, 0))
    return pl.pallas_call(
        _ret_kernel,
        grid=(BATCH, RET_HEADS),
        in_specs=[pl.BlockSpec((None, 8, dh), lambda b, h: (h, 0, 0)),
                  spec(0), spec(RET_HEADS), spec(2 * RET_HEADS), spec(3 * RET_HEADS),
                  pl.BlockSpec((1, dh), lambda b, h: (0, h)), slab_o, slab_g],
        out_specs=[pl.BlockSpec((None, SEQ, dh), lambda b, h: (h, b, 0)), slab_o, slab_g],
        out_shape=[jax.ShapeDtypeStruct((RET_HEADS, TOK, dh), BF16),
                   jax.ShapeDtypeStruct(w_out.shape, BF16),
                   jax.ShapeDtypeStruct(w_glu.shape, BF16)],
        scratch_shapes=[pltpu.VMEM((nc, dh, dh), F32),
                        pltpu.VMEM((nc, dh, dh), BF16),
                        pltpu.VMEM((_RET_BLK, _RET_BLK), F32),
                        pltpu.VMEM((_RET_BLK, dh), F32),
                        pltpu.VMEM((_RET_BLK, dh), F32)],
        compiler_params=pltpu.CompilerParams(
            dimension_semantics=("parallel", "parallel"), vmem_limit_bytes=VMEM_LIMIT),
        name="retention",
    )(lg_tab, qkvg, qkvg, qkvg, qkvg, ret_norm_w, w_out, w_glu)


_S5GEN_GG = 4


def _prep_kernel(c_ref, wada_ref, bada_ref,
                 ls_ref, ar_ref, ai_ref, bt_ref, cr_ref, ci_ref, c4r_ref, c4i_ref, d_ref, win_ref,
                 mod_ref, mt_ref, wt_ref, vt_ref, a16r_ref, a16i_ref, win_bf_ref):
    cond = _silu(c_ref[...])
    mod_ref[...] = jnp.dot(cond.astype(BF16), wada_ref[...].astype(BF16),
                           preferred_element_type=F32) + bada_ref[...]
    win_bf_ref[...] = win_ref[...].astype(BF16)
    N, P, R = S5_STATE, S5_GROUP, S5_ROW
    re_half = lax.broadcasted_iota(jnp.int32, (P, 2 * N), 1) < N
    sub_e = lax.broadcasted_iota(jnp.int32, (P, 2 * N), 0)
    lane_p = lax.broadcasted_iota(jnp.int32, (P, R), 1)
    sub_p = lax.broadcasted_iota(jnp.int32, (P, R), 0)
    lag0_diag = lane_p == (R - P) + sub_p
    is_re = lane_p < 2 * N
    lane_slot = lax.shift_right_logical(lane_p, 6) & 1

    def hdot(a, b):
        return jnp.dot(a, b, precision=HI, preferred_element_type=F32)

    def cmul(xr, xi, yr, yi):
        return xr * yr - xi * yi, xr * yi + xi * yr

    for k in range(_S5GEN_GG):
        odd = k % 2
        dt = jnp.exp(ls_ref[k])
        ar = ar_ref[k]
        ai = ai_ref[k]
        mag = jnp.exp(ar * dt)
        lbr = mag * jnp.cos(ai * dt)
        lbi = mag * jnp.sin(ai * dt)
        den = ar * ar + ai * ai
        zr = ((lbr - 1.0) * ar + lbi * ai) / den
        zi = (lbi * ar - (lbr - 1.0) * ai) / den

        pw_r = jnp.ones((P, 2 * N), F32)
        pw_i = jnp.zeros((P, 2 * N), F32)
        sq_r, sq_i = lbr, lbi
        for bit in range(4):
            nr, ni = cmul(pw_r, pw_i, sq_r, sq_i)
            take = (lax.shift_right_logical(sub_e, bit) & 1) == 1
            pw_r = jnp.where(take, nr, pw_r)
            pw_i = jnp.where(take, ni, pw_i)
            if bit < 3:
                sq_r, sq_i = cmul(sq_r, sq_i, sq_r, sq_i)
        p1_r, p1_i = cmul(pw_r, pw_i, lbr, lbi)
        a16r_ref[k] = p1_r[S5_SUB - 1:S5_SUB, :]
        a16i_ref[k] = p1_i[S5_SUB - 1:S5_SUB, :]

        bt = bt_ref[k]
        x = zr * bt + jnp.where(re_half[0:1], -zi, zi) * pltpu.roll(bt, N, axis=1)
        xs = pltpu.roll(x, N, axis=1)
        pw_is = jnp.where(re_half, -pw_i, pw_i)
        w_rows = jnp.concatenate(
            [pw_r[e:e + 1] * x + pw_is[e:e + 1] * xs for e in range(S5_SUB - 1, -1, -1)], axis=0)
        w_t = w_rows.T
        w_r = w_t[0:N]
        w_i = w_t[N:2 * N]
        zero = jnp.zeros_like(w_r)
        wt_ref[k, 0:2 * N, :] = jnp.concatenate(
            [zero, w_r] if odd else [w_r, zero], axis=0).astype(BF16)
        wt_ref[k, 2 * N:4 * N, :] = jnp.concatenate(
            [zero, w_i] if odd else [w_i, zero], axis=0).astype(BF16)

        c_re = cr_ref[k]
        c_im = ci_ref[k]
        krev = hdot(c_re, w_r) - hdot(c_im, w_i)
        krev = krev + jnp.where(lag0_diag, d_ref[k], 0.0)
        for tp in range(S5_SUB):
            width = (tp + 1) * P
            piece = krev if width == R else jnp.where(
                lane_p < width, pltpu.roll(krev, width, axis=1), 0.0)
            mt_ref[k, tp * P:(tp + 1) * P, :] = piece.astype(BF16)

        c4_r = c4r_ref[k]
        c4_i = c4i_ref[k]
        keep = lane_slot == odd
        for tp in range(S5_SUB):
            lr = jnp.concatenate([p1_r[tp:tp + 1], p1_r[tp:tp + 1]], axis=1)
            li = jnp.concatenate([p1_i[tp:tp + 1], p1_i[tp:tp + 1]], axis=1)
            val = jnp.where(is_re, c4_r * lr - c4_i * li, -(c4_r * li + c4_i * lr))
            vt_ref[k, tp * P:(tp + 1) * P, :] = jnp.where(keep, val, 0.0).astype(BF16)


def _prep(c, w_ada, b_ada, a_re, a_im, log_step, b_re, b_im, c_re, c_im, d_skip, w_in):
    G, N, P, R = S5_GROUPS, S5_STATE, S5_GROUP, S5_ROW
    gg = _S5GEN_GG
    steps = G // gg
    sw = w_in.shape[0] // steps
    n_mod = w_ada.shape[1]
    tn = n_mod // steps
    assert sw * steps == w_in.shape[0] and sw % 16 == 0
    assert tn * steps == n_mod and tn % LANES == 0
    slab_w = pl.BlockSpec((sw, w_in.shape[1]), lambda g: (g, 0))
    dup = lambda a: jnp.tile(a, (1, 2)).reshape(G, 1, 2 * N)
    bt = jnp.concatenate([jnp.swapaxes(b_re, 1, 2), jnp.swapaxes(b_im, 1, 2)], axis=-1)
    tile4 = lambda a: jnp.tile(a, (1, 1, R // N))
    blk = lambda s: pl.BlockSpec((gg,) + s, lambda g: (g, 0, 0))
    return pl.pallas_call(
        _prep_kernel,
        grid=(steps,),
        in_specs=[pl.BlockSpec((BATCH, D_MODEL), lambda g: (0, 0)),
                  pl.BlockSpec((D_MODEL, tn), lambda g: (0, g)),
                  pl.BlockSpec((1, tn), lambda g: (0, g)),
                  blk((1, 1)), blk((1, 2 * N)), blk((1, 2 * N)), blk((P, 2 * N)),
                  blk((P, N)), blk((P, N)), blk((P, R)), blk((P, R)), blk((P, 1)), slab_w],
        out_specs=[pl.BlockSpec((BATCH, tn), lambda g: (0, g)),
                   blk((R, R)), blk((R, R)), blk((R, R)), blk((1, 2 * N)), blk((1, 2 * N)),
                   slab_w],
        out_shape=[jax.ShapeDtypeStruct((BATCH, n_mod), F32)] +
                  [jax.ShapeDtypeStruct((G, R, R), BF16)] * 3 +
                  [jax.ShapeDtypeStruct((G, 1, 2 * N), F32)] * 2 +
                  [jax.ShapeDtypeStruct(w_in.shape, BF16)],
        compiler_params=pltpu.CompilerParams(
            dimension_semantics=("parallel",), vmem_limit_bytes=VMEM_LIMIT),
        name="prep",
    )(c, w_ada, b_ada.reshape(1, n_mod),
      log_step.reshape(G, 1, 1), dup(a_re), dup(a_im), bt, c_re, c_im, tile4(c_re), tile4(c_im),
      d_skip.reshape(G, P, 1), w_in)


_S5_GB = LANES // S5_GROUP
_S5_NP = _S5_GB // 2


def _s5_kernel(a16r_ref, a16i_ref, u_ref, mt_ref, wt_ref, vt_ref, o_ref,
               ut_scr, yt_scr, e_scr, p_scr):
    P = S5_GROUP
    N2 = 2 * S5_STATE
    cols = S5_COLS
    nblk = cols // LANES
    nt = cols // 8

    def load_body(tau, carry):
        for c in range(nblk):
            x = u_ref[pl.ds(tau + c * LANES * S5_SUB, LANES, stride=S5_SUB), :]
            xt = x.T
            r0 = pl.multiple_of(tau * P, P)
            for k in range(_S5_GB):
                ut_scr[k, pl.ds(r0, P), c * LANES:(c + 1) * LANES] = \
                    xt[k * P:(k + 1) * P, :].astype(BF16)
        return carry

    lax.fori_loop(0, S5_SUB, load_body, 0, unroll=4)

    for pr in range(_S5_NP):
        et = (jnp.dot(wt_ref[2 * pr], ut_scr[2 * pr], preferred_element_type=F32) +
              jnp.dot(wt_ref[2 * pr + 1], ut_scr[2 * pr + 1], preferred_element_type=F32))
        e_scr[pr] = et.T

    sub = lax.broadcasted_iota(jnp.int32, (8, N2), 0)
    lo = sub < BATCH
    first = lax.broadcasted_iota(jnp.int32, (1, N2), 1) < S5_STATE
    a_r, a_i = [], []
    for pr in range(_S5_NP):
        a_r.append(jnp.broadcast_to(
            jnp.where(first, a16r_ref[2 * pr], a16r_ref[2 * pr + 1]), (8, N2)))
        a_i.append(jnp.broadcast_to(
            jnp.where(first, a16i_ref[2 * pr], a16i_ref[2 * pr + 1]), (8, N2)))

    def scan_body(t, carry):
        r0 = pl.multiple_of(t * 8, 8)
        new = []
        for pr in range(_S5_NP):
            s_r, s_i = carry[2 * pr], carry[2 * pr + 1]
            e_r = e_scr[pr, pl.ds(r0, 8), 0:N2]
            e_i = e_scr[pr, pl.ds(r0, 8), N2:2 * N2]
            x_r = pltpu.roll(e_r, BATCH, axis=0)
            x_i = pltpu.roll(e_i, BATCH, axis=0)
            elo_r = jnp.where(lo, e_r, x_r)
            elo_i = jnp.where(lo, e_i, x_i)
            ehi_r = jnp.where(lo, x_r, e_r)
            ehi_i = jnp.where(lo, x_i, e_i)
            t_r = a_r[pr] * s_r - a_i[pr] * s_i + elo_r
            t_i = a_r[pr] * s_i + a_i[pr] * s_r + elo_i
            p_scr[pr, pl.ds(r0, 8), 0:N2] = jnp.where(lo, s_r, t_r)
            p_scr[pr, pl.ds(r0, 8), N2:2 * N2] = jnp.where(lo, s_i, t_i)
            new.append(a_r[pr] * t_r - a_i[pr] * t_i + ehi_r)
            new.append(a_r[pr] * t_i + a_i[pr] * t_r + ehi_i)
        return tuple(new)

    zero = jnp.zeros((8, N2), F32)
    lax.fori_loop(0, nt, scan_body, tuple(zero for _ in range(2 * _S5_NP)))

    for pr in range(_S5_NP):
        pt = p_scr[pr].T.astype(BF16)
        for k in range(2):
            gi = 2 * pr + k
            yt = (jnp.dot(mt_ref[gi], ut_scr[gi], preferred_element_type=F32) +
                  jnp.dot(vt_ref[gi], pt, preferred_element_type=F32))
            for tau in range(S5_SUB):
                yt_scr[tau, gi * P:(gi + 1) * P, :] = yt[tau * P:(tau + 1) * P, :]

    def store_body(tau, carry):
        o_ref[pl.ds(tau, cols, stride=S5_SUB), :] = yt_scr[tau].T
        return carry

    lax.fori_loop(0, S5_SUB, store_body, 0, unroll=4)


def _s5(u_rows, mt, wt, vt, a16r, a16i):
    G, N, R = S5_GROUPS, S5_STATE, S5_ROW
    gb = _S5_GB
    vec = pl.BlockSpec((gb, 1, 2 * N), lambda i: (i, 0, 0))
    mat = pl.BlockSpec((gb, R, R), lambda i: (i, 0, 0))
    slab = pl.BlockSpec((None, TOK, LANES), lambda i: (i, 0, 0))
    return pl.pallas_call(
        _s5_kernel,
        grid=(G // gb,),
        in_specs=[vec, vec, slab, mat, mat, mat],
        out_specs=slab,
        out_shape=jax.ShapeDtypeStruct((D_S5 // LANES, TOK, LANES), F32),
        scratch_shapes=[pltpu.VMEM((gb, R, S5_COLS), BF16),
                        pltpu.VMEM((S5_SUB, LANES, S5_COLS), F32),
                        pltpu.VMEM((_S5_NP, S5_COLS, 4 * N), F32),
                        pltpu.VMEM((_S5_NP, S5_COLS, 4 * N), F32)],
        compiler_params=pltpu.CompilerParams(
            dimension_semantics=("parallel",), vmem_limit_bytes=VMEM_LIMIT),
        name="s5",
    )(a16r, a16i, u_rows, mt, wt, vt)


_MIX_TM = 512
_MIX_SUB = 256
_MIX_NS = _MIX_TM // S5_SUB


def _mix_kernel(ssm_ref, ret_ref, x_ref, g1_ref, sc_ref, sh_ref, nw_ref, wglu_ref, bglu_ref,
                wout_ref, o_ref, h_ref):
    gain = nw_ref[...] * (1.0 + sc_ref[...])
    nsub = _MIX_TM // _MIX_SUB
    ns = _MIX_SUB // S5_SUB
    gated = []
    for r in range(nsub):
        s = jnp.concatenate([ssm_ref[k, r * ns:(r + 1) * ns].reshape(_MIX_SUB, LANES)
                             for k in range(D_S5 // LANES)], axis=1)
        cdf = 0.5 * (1.0 + jnp.tanh(math.sqrt(2.0 / math.pi) * (s + 0.044715 * (s * s * s))))
        sg = s * cdf
        z = jnp.dot(sg.astype(BF16), wglu_ref[...], preferred_element_type=F32) + bglu_ref[...]
        gated.append((sg * jax.nn.sigmoid(z)).astype(BF16))
    for r in range(nsub):
        rs = slice(r * _MIX_SUB, (r + 1) * _MIX_SUB)
        both = jnp.concatenate([ret_ref[hh, rs, :] for hh in range(RET_HEADS)] + [gated[r]],
                               axis=1)
        mix = jnp.dot(both, wout_ref[...], preferred_element_type=F32)
        x1 = x_ref[rs, :] + g1_ref[...] * mix
        o_ref[rs, :] = x1
        ms = jnp.mean(x1 * x1, axis=-1, keepdims=True)
        h_ref[rs, :] = (x1 * lax.rsqrt(ms + EPS) * gain + sh_ref[...]).astype(BF16)


def _mix(ssm_sb, ret_out, x2d, g1, sc2, sh2, norm_w, w_glu_bf, b_glu, w_out_bf):
    tm = _MIX_TM
    per_b = SEQ // tm
    mod = pl.BlockSpec((None, 1, D_MODEL), lambda i: (i // per_b, 0, 0))
    rows = pl.BlockSpec((tm, D_MODEL), lambda i: (i, 0))
    return pl.pallas_call(
        _mix_kernel,
        grid=(TOK // tm,),
        in_specs=[pl.BlockSpec((D_S5 // LANES, _MIX_NS, None, S5_SUB, LANES),
                               lambda i: (0, i % per_b, i // per_b, 0, 0)),
                  pl.BlockSpec((RET_HEADS, tm, RET_HEAD_DIM), lambda i: (0, i, 0)),
                  rows, mod, mod, mod,
                  pl.BlockSpec((1, D_MODEL), lambda i: (0, 0)),
                  pl.BlockSpec((D_S5, D_S5), lambda i: (0, 0)),
                  pl.BlockSpec((1, D_S5), lambda i: (0, 0)),
                  pl.BlockSpec((D_RET + D_S5, D_MODEL), lambda i: (0, 0))],
        out_specs=[rows, rows],
        out_shape=[jax.ShapeDtypeStruct((TOK, D_MODEL), F32),
                   jax.ShapeDtypeStruct((TOK, D_MODEL), BF16)],
        compiler_params=pltpu.CompilerParams(
            dimension_semantics=("parallel",), vmem_limit_bytes=VMEM_LIMIT),
        name="mix",
    )(ssm_sb, ret_out, x2d, g1, sc2, sh2, norm_w, w_glu_bf, b_glu, w_out_bf)


_FFN_UP_TM = 2048
_FFN_UP_SUB = 1024
_FFN_UP_TF = 512
_FFN_DOWN_TM = 512
_FFN_DOWN_SUB = 256


def _ffn_up_kernel(h_ref, wg_ref, wu_ref, wd_ref, o_ref, wd_bf_ref, wg_scr, wu_scr):
    @pl.when(pl.program_id(1) == 0)
    def _():
        wg_scr[...] = wg_ref[...].astype(BF16)
        wu_scr[...] = wu_ref[...].astype(BF16)

    wd_bf_ref[...] = wd_ref[...].astype(BF16)
    for r in range(_FFN_UP_TM // _FFN_UP_SUB):
        rs = slice(r * _FFN_UP_SUB, (r + 1) * _FFN_UP_SUB)
        h = h_ref[rs, :]
        gate = jnp.dot(h, wg_scr[...], preferred_element_type=F32)
        up = jnp.dot(h, wu_scr[...], preferred_element_type=F32)
        o_ref[rs, :] = (_silu(gate) * up).astype(BF16)


def _ffn_up(h2, w_gate_up, w_down):
    tm, tf = _FFN_UP_TM, _FFN_UP_TF
    nf = D_FF // tf
    nm = TOK // tm
    slab = D_FF // (nf * nm)
    assert slab * nf * nm == D_FF and slab % 16 == 0
    return pl.pallas_call(
        _ffn_up_kernel,
        grid=(nf, nm),
        in_specs=[pl.BlockSpec((tm, D_MODEL), lambda f, i: (i, 0)),
                  pl.BlockSpec((D_MODEL, tf), lambda f, i: (0, f)),
                  pl.BlockSpec((D_MODEL, tf), lambda f, i: (0, nf + f)),
                  pl.BlockSpec((slab, D_MODEL), lambda f, i: (f * nm + i, 0))],
        out_specs=[pl.BlockSpec((tm, tf), lambda f, i: (i, f)),
                   pl.BlockSpec((slab, D_MODEL), lambda f, i: (f * nm + i, 0))],
        out_shape=[jax.ShapeDtypeStruct((TOK, D_FF), BF16),
                   jax.ShapeDtypeStruct((D_FF, D_MODEL), BF16)],
        scratch_shapes=[pltpu.VMEM((D_MODEL, tf), BF16),
                        pltpu.VMEM((D_MODEL, tf), BF16)],
        compiler_params=pltpu.CompilerParams(
            dimension_semantics=("arbitrary", "arbitrary"), vmem_limit_bytes=VMEM_LIMIT),
        name="ffn_up",
    )(h2, w_gate_up, w_gate_up, w_down)


def _ffn_down_kernel(a_ref, x_ref, g2_ref, fw_ref, wd_ref, o_ref):
    for r in range(_FFN_DOWN_TM // _FFN_DOWN_SUB):
        rs = slice(r * _FFN_DOWN_SUB, (r + 1) * _FFN_DOWN_SUB)
        down = jnp.dot(a_ref[rs, :], wd_ref[...], preferred_element_type=F32)
        x2 = x_ref[rs, :] + g2_ref[...] * down
        ms = jnp.mean(x2 * x2, axis=-1, keepdims=True)
        o_ref[rs, :] = x2 * lax.rsqrt(ms + EPS) * fw_ref[...]


def _ffn_down(act, x1, g2, final_w, w_down_bf):
    tm = _FFN_DOWN_TM
    per_b = SEQ // tm
    rows = pl.BlockSpec((tm, D_MODEL), lambda i: (i, 0))
    return pl.pallas_call(
        _ffn_down_kernel,
        grid=(TOK // tm,),
        in_specs=[pl.BlockSpec((tm, D_FF), lambda i: (i, 0)),
                  rows,
                  pl.BlockSpec((None, 1, D_MODEL), lambda i: (i // per_b, 0, 0)),
                  pl.BlockSpec((1, D_MODEL), lambda i: (0, 0)),
                  pl.BlockSpec((D_FF, D_MODEL), lambda i: (0, 0))],
        out_specs=rows,
        out_shape=jax.ShapeDtypeStruct((TOK, D_MODEL), F32),
        compiler_params=pltpu.CompilerParams(
            dimension_semantics=("parallel",), vmem_limit_bytes=VMEM_LIMIT),
        name="ffn_down",
    )(act, x1, g2, final_w, w_down_bf)


def kernel(x, c, w_ada, b_ada, norm1_w, w_in, ret_norm_w, s5_a_re, s5_a_im, s5_log_step,
           s5_b_re, s5_b_im, s5_c_re, s5_c_im, s5_d, w_glu, b_glu, w_out, norm2_w,
           w_gate_up, w_down, final_norm_w):
    x2d = x.reshape(TOK, D_MODEL)
    layer = 0
    mod, mt, wt, vt, a16r, a16i, w_in_bf = _prep(
        c, w_ada[layer], b_ada[layer],
        s5_a_re[layer], s5_a_im[layer], s5_log_step[layer], s5_b_re[layer], s5_b_im[layer],
        s5_c_re[layer], s5_c_im[layer], s5_d[layer], w_in[layer])
    sh1, sc1, g1, sh2, sc2, g2 = [m.reshape(BATCH, 1, D_MODEL) for m in jnp.split(mod, 6, axis=-1)]

    qkvg, u_sb = _inproj(x2d, sc1, sh1, norm1_w[layer].reshape(1, D_MODEL), w_in_bf)
    ret_out, w_out_bf, w_glu_bf = _retention(qkvg, ret_norm_w[layer].reshape(1, D_RET),
                                             w_out[layer], w_glu[layer])
    y_rows = _s5(u_sb.reshape(D_S5 // LANES, TOK, LANES), mt, wt, vt, a16r, a16i)
    ssm_sb = y_rows.reshape(D_S5 // LANES, S5_NSUB, BATCH, S5_SUB, LANES)

    x1, h2 = _mix(ssm_sb, ret_out, x2d, g1, sc2, sh2, norm2_w[layer].reshape(1, D_MODEL),
                  w_glu_bf, b_glu[layer].reshape(1, D_S5), w_out_bf)
    act, w_down_bf = _ffn_up(h2, w_gate_up[layer], w_down[layer])
    out = _ffn_down(act, x1, g2, final_norm_w.reshape(1, D_MODEL), w_down_bf)
    return out.reshape(BATCH, SEQ, D_MODEL)
```

```python
import math

import numpy as np
import jax
import jax.numpy as jnp
from jax import lax
from jax.experimental import pallas as pl
from jax.experimental.pallas import tpu as pltpu

D_MODEL = 2048
BATCH = 4
SEQ = 2048
TOK = BATCH * SEQ
D_RET = 1024
D_S5 = 1024
RET_HEADS = 4
RET_HEAD_DIM = 256
RET_CHUNK = 128
S5_GROUP = 16
S5_GROUPS = 64
S5_STATE = 64
S5_SUB = 16
S5_ROW = S5_SUB * S5_GROUP
S5_NSUB = SEQ // S5_SUB
S5_COLS = S5_NSUB * BATCH
D_FF = 5632
ROPE_BASE = 10000.0
EPS = 1e-6
LANES = 128

F32 = jnp.float32
BF16 = jnp.bfloat16
HI = lax.Precision.HIGHEST
VMEM_LIMIT = 58 * 1024 * 1024


def _silu(v):
    return v * jax.nn.sigmoid(v)


_INPROJ_TM = 512
_INPROJ_SUB = 256
_INPROJ_NS = _INPROJ_SUB // S5_SUB


def _inproj_kernel(x_ref, sc_ref, sh_ref, nw_ref, w_ref, o_ref, u_ref, h_scr, cos_scr, sin_scr):
    tm = _INPROJ_TM
    half = RET_HEAD_DIM // 2
    per_b = SEQ // tm
    i = pl.program_id(0)
    slot = i % per_b

    @pl.when(i < per_b)
    def _():
        pos = (lax.broadcasted_iota(jnp.int32, (tm, half), 0) + slot * tm).astype(F32)
        lane = lax.broadcasted_iota(jnp.int32, (tm, half), 1).astype(F32)
        ang = pos * jnp.exp(lane * (-math.log(ROPE_BASE) / half))
        cos_scr[slot] = jnp.cos(ang)
        sin_scr[slot] = jnp.sin(ang)

    gain = nw_ref[...] * (1.0 + sc_ref[...])
    shift = sh_ref[...]
    for r in range(tm // _INPROJ_SUB):
        rs = slice(r * _INPROJ_SUB, (r + 1) * _INPROJ_SUB)
        x = x_ref[rs, :]
        ms = jnp.mean(x * x, axis=-1, keepdims=True)
        h_scr[rs, :] = (x * lax.rsqrt(ms + EPS) * gain + shift).astype(BF16)
        cs = cos_scr[slot, rs, :]
        sn = sin_scr[slot, rs, :]
        for t in range(5):
            acc = jnp.dot(h_scr[rs, :], w_ref[:, t * D_RET:(t + 1) * D_RET],
                          preferred_element_type=F32)
            if t < 2:
                for hh in range(RET_HEADS):
                    c0 = hh * RET_HEAD_DIM
                    x1 = acc[:, c0:c0 + half]
                    x2 = acc[:, c0 + half:c0 + 2 * half]
                    o_ref[t * RET_HEADS + hh, rs, 0:half] = (x1 * cs - x2 * sn).astype(BF16)
                    o_ref[t * RET_HEADS + hh, rs, half:2 * half] = (x2 * cs + x1 * sn).astype(BF16)
            elif t < 4:
                for hh in range(RET_HEADS):
                    c0 = hh * RET_HEAD_DIM
                    o_ref[t * RET_HEADS + hh, rs, :] = acc[:, c0:c0 + RET_HEAD_DIM].astype(BF16)
            else:
                for k in range(D_S5 // LANES):
                    u_ref[k, r * _INPROJ_NS:(r + 1) * _INPROJ_NS] = \
                        acc[:, k * LANES:(k + 1) * LANES].reshape(_INPROJ_NS, S5_SUB, LANES)


def _inproj(x2d, sc1, sh1, norm_w, w_in_bf):
    tm = _INPROJ_TM
    per_b = SEQ // tm
    half = RET_HEAD_DIM // 2
    assert w_in_bf.shape == (D_MODEL, 4 * D_RET + D_S5) and D_S5 == D_RET
    return pl.pallas_call(
        _inproj_kernel,
        grid=(TOK // tm,),
        in_specs=[pl.BlockSpec((tm, D_MODEL), lambda i: (i, 0)),
                  pl.BlockSpec((None, 1, D_MODEL), lambda i: (i // per_b, 0, 0)),
                  pl.BlockSpec((None, 1, D_MODEL), lambda i: (i // per_b, 0, 0)),
                  pl.BlockSpec((1, D_MODEL), lambda i: (0, 0)),
                  pl.BlockSpec(w_in_bf.shape, lambda i: (0, 0))],
        out_specs=[pl.BlockSpec((4 * RET_HEADS, tm, RET_HEAD_DIM), lambda i: (0, i, 0)),
                   pl.BlockSpec((D_S5 // LANES, tm // S5_SUB, None, S5_SUB, LANES),
                                lambda i: (0, i % per_b, i // per_b, 0, 0))],
        out_shape=[jax.ShapeDtypeStruct((4 * RET_HEADS, TOK, RET_HEAD_DIM), BF16),
                   jax.ShapeDtypeStruct((D_S5 // LANES, S5_NSUB, BATCH, S5_SUB, LANES), F32)],
        scratch_shapes=[pltpu.VMEM((tm, D_MODEL), BF16),
                        pltpu.VMEM((per_b, tm, half), F32),
                        pltpu.VMEM((per_b, tm, half), F32)],
        compiler_params=pltpu.CompilerParams(
            dimension_semantics=("arbitrary",), vmem_limit_bytes=VMEM_LIMIT),
        name="inproj",
    )(x2d, sc1, sh1, norm_w, w_in_bf)


_RET_BLK = 256


def _ret_kernel(lg_ref, q_ref, k_ref, v_ref, g_ref, w_ref, wout_ref, wglu_ref,
                o_ref, wout_bf_ref, wglu_bf_ref,
                kv_scr, prev_scr, intra_scr, kdec_scr, qdec_scr):
    wout_bf_ref[...] = wout_ref[...].astype(BF16)
    wglu_bf_ref[...] = wglu_ref[...].astype(BF16)
    C = _RET_BLK
    dh = RET_HEAD_DIM
    nc = SEQ // C
    scale = dh ** -0.5
    lg = lg_ref[0:1, :]
    ii = lax.broadcasted_iota(jnp.int32, (C, C), 0)
    jj = lax.broadcasted_iota(jnp.int32, (C, C), 1)
    diff = (ii - jj).astype(F32)
    intra_scr[...] = jnp.where(diff >= 0.0, jnp.exp(lg * jnp.maximum(diff, 0.0)), 0.0) * scale
    row = lax.broadcasted_iota(jnp.int32, (C, dh), 0).astype(F32)
    kdec_scr[...] = jnp.exp(lg * (C - 1.0 - row)) * scale
    qdec_scr[...] = jnp.exp(lg * (row + 1.0))
    block_decay = jnp.exp(lg * float(C))
    gn_w = w_ref[...]

    for n in range(nc):
        rs = slice(n * C, (n + 1) * C)
        kd = (k_ref[rs, :].astype(F32) * kdec_scr[...]).astype(BF16)
        kv_scr[n] = lax.dot_general(kd, v_ref[rs, :], (((0,), (0,)), ((), ())),
                                    preferred_element_type=F32)

    band = 64
    for rb in range(dh // band):
        bs = slice(rb * band, (rb + 1) * band)
        st = jnp.zeros((band, dh), F32)
        for n in range(nc):
            prev_scr[n, bs, :] = st.astype(BF16)
            if n + 1 < nc:
                st = st * block_decay + kv_scr[n, bs, :]

    for n in range(nc):
        rs = slice(n * C, (n + 1) * C)
        q = q_ref[rs, :]
        s = lax.dot_general(q, k_ref[rs, :], (((1,), (1,)), ((), ())),
                            preferred_element_type=F32) * intra_scr[...]
        y = jnp.dot(s.astype(BF16), v_ref[rs, :], preferred_element_type=F32)
        y = y + jnp.dot(q, prev_scr[n], preferred_element_type=F32) * qdec_scr[...]
        mu = jnp.mean(y, axis=-1, keepdims=True)
        yc = y - mu
        var = jnp.mean(yc * yc, axis=-1, keepdims=True)
        yn = yc * lax.rsqrt(var + EPS) * gn_w
        g = g_ref[rs, :].astype(F32)
        o_ref[rs, :] = (_silu(g) * yn).astype(BF16)


def _retention(qkvg, ret_norm_w, w_out, w_glu):
    dh = RET_HEAD_DIM
    nc = SEQ // _RET_BLK
    steps = BATCH * RET_HEADS
    so = w_out.shape[0] // steps
    sg = w_glu.shape[0] // steps
    assert so * steps == w_out.shape[0] and sg * steps == w_glu.shape[0] and sg % 16 == 0
    lg = np.log1p(-np.exp2(-5.0 - np.arange(RET_HEADS, dtype=np.float64)))
    lg_tab = jnp.asarray(np.broadcast_to(lg[:, None, None], (RET_HEADS, 8, dh)), F32)
    spec = lambda off: pl.BlockSpec((None, SEQ, dh), lambda b, h: (off + h, b, 0))
    slab_o = pl.BlockSpec((so, w_out.shape[1]), lambda b, h: (b * RET_HEADS + h, 0))
    slab_g = pl.BlockSpec((sg, w_glu.shape[1]), lambda b, h: (b * RET_HEADS + h, 0))
    return pl.pallas_call(
        _ret_kernel,
        grid=(BATCH, RET_HEADS),
        in_specs=[pl.BlockSpec((None, 8, dh), lambda b, h: (h, 0, 0)),
                  spec(0), spec(RET_HEADS), spec(2 * RET_HEADS), spec(3 * RET_HEADS),
                  pl.BlockSpec((1, dh), lambda b, h: (0, h)), slab_o, slab_g],
        out_specs=[pl.BlockSpec((None, SEQ, dh), lambda b, h: (h, b, 0)), slab_o, slab_g],
        out_shape=[jax.ShapeDtypeStruct((RET_HEADS, TOK, dh), BF16),
                   jax.ShapeDtypeStruct(w_out.shape, BF16),
                   jax.ShapeDtypeStruct(w_glu.shape, BF16)],
        scratch_shapes=[pltpu.VMEM((nc, dh, dh), F32),
                        pltpu.VMEM((nc, dh, dh), BF16),
                        pltpu.VMEM((_RET_BLK, _RET_BLK), F32),
                        pltpu.VMEM((_RET_BLK, dh), F32),
                        pltpu.VMEM((_RET_BLK, dh), F32)],
        compiler_params=pltpu.CompilerParams(
            dimension_semantics=("parallel", "parallel"), vmem_limit_bytes=VMEM_LIMIT),
        name="retention",
    )(lg_tab, qkvg, qkvg, qkvg, qkvg, ret_norm_w, w_out, w_glu)


_S5GEN_GG = 4


def _prep_kernel(c_ref, wada_ref, bada_ref,
                 ls_ref, ar_ref, ai_ref, bt_ref, cr_ref, ci_ref, c4r_ref, c4i_ref, d_ref, win_ref,
                 mod_ref, mt_ref, wt_ref, vt_ref, a16r_ref, a16i_ref, win_bf_ref):
    cond = _silu(c_ref[...])
    mod_ref[...] = jnp.dot(cond.astype(BF16), wada_ref[...].astype(BF16),
                           preferred_element_type=F32) + bada_ref[...]
    win_bf_ref[...] = win_ref[...].astype(BF16)
    N, P, R = S5_STATE, S5_GROUP, S5_ROW
    re_half = lax.broadcasted_iota(jnp.int32, (P, 2 * N), 1) < N
    sub_e = lax.broadcasted_iota(jnp.int32, (P, 2 * N), 0)
    lane_p = lax.broadcasted_iota(jnp.int32, (P, R), 1)
    sub_p = lax.broadcasted_iota(jnp.int32, (P, R), 0)
    lag0_diag = lane_p == (R - P) + sub_p
    is_re = lane_p < 2 * N
    lane_slot = lax.shift_right_logical(lane_p, 6) & 1

    def hdot(a, b):
        return jnp.dot(a, b, precision=HI, preferred_element_type=F32)

    def cmul(xr, xi, yr, yi):
        return xr * yr - xi * yi, xr * yi + xi * yr

    for k in range(_S5GEN_GG):
        odd = k % 2
        dt = jnp.exp(ls_ref[k])
        ar = ar_ref[k]
        ai = ai_ref[k]
        mag = jnp.exp(ar * dt)
        lbr = mag * jnp.cos(ai * dt)
        lbi = mag * jnp.sin(ai * dt)
        den = ar * ar + ai * ai
        zr = ((lbr - 1.0) * ar + lbi * ai) / den
        zi = (lbi * ar - (lbr - 1.0) * ai) / den

        pw_r = jnp.ones((P, 2 * N), F32)
        pw_i = jnp.zeros((P, 2 * N), F32)
        sq_r, sq_i = lbr, lbi
        for bit in range(4):
            nr, ni = cmul(pw_r, pw_i, sq_r, sq_i)
            take = (lax.shift_right_logical(sub_e, bit) & 1) == 1
            pw_r = jnp.where(take, nr, pw_r)
            pw_i = jnp.where(take, ni, pw_i)
            if bit < 3:
                sq_r, sq_i = cmul(sq_r, sq_i, sq_r, sq_i)
        p1_r, p1_i = cmul(pw_r, pw_i, lbr, lbi)
        a16r_ref[k] = p1_r[S5_SUB - 1:S5_SUB, :]
        a16i_ref[k] = p1_i[S5_SUB - 1:S5_SUB, :]

        bt = bt_ref[k]
        x = zr * bt + jnp.where(re_half[0:1], -zi, zi) * pltpu.roll(bt, N, axis=1)
        xs = pltpu.roll(x, N, axis=1)
        pw_is = jnp.where(re_half, -pw_i, pw_i)
        w_rows = jnp.concatenate(
            [pw_r[e:e + 1] * x + pw_is[e:e + 1] * xs for e in range(S5_SUB - 1, -1, -1)], axis=0)
        w_t = w_rows.T
        w_r = w_t[0:N]
        w_i = w_t[N:2 * N]
        zero = jnp.zeros_like(w_r)
        wt_ref[k, 0:2 * N, :] = jnp.concatenate(
            [zero, w_r] if odd else [w_r, zero], axis=0).astype(BF16)
        wt_ref[k, 2 * N:4 * N, :] = jnp.concatenate(
            [zero, w_i] if odd else [w_i, zero], axis=0).astype(BF16)

        c_re = cr_ref[k]
        c_im = ci_ref[k]
        krev = hdot(c_re, w_r) - hdot(c_im, w_i)
        krev = krev + jnp.where(lag0_diag, d_ref[k], 0.0)
        for tp in range(S5_SUB):
            width = (tp + 1) * P
            piece = krev if width == R else jnp.where(
                lane_p < width, pltpu.roll(krev, width, axis=1), 0.0)
            mt_ref[k, tp * P:(tp + 1) * P, :] = piece.astype(BF16)

        c4_r = c4r_ref[k]
        c4_i = c4i_ref[k]
        keep = lane_slot == odd
        for tp in range(S5_SUB):
            lr = jnp.concatenate([p1_r[tp:tp + 1], p1_r[tp:tp + 1]], axis=1)
            li = jnp.concatenate([p1_i[tp:tp + 1], p1_i[tp:tp + 1]], axis=1)
            val = jnp.where(is_re, c4_r * lr - c4_i * li, -(c4_r * li + c4_i * lr))
            vt_ref[k, tp * P:(tp + 1) * P, :] = jnp.where(keep, val, 0.0).astype(BF16)


def _prep(c, w_ada, b_ada, a_re, a_im, log_step, b_re, b_im, c_re, c_im, d_skip, w_in):
    G, N, P, R = S5_GROUPS, S5_STATE, S5_GROUP, S5_ROW
    gg = _S5GEN_GG
    steps = G // gg
    sw = w_in.shape[0] // steps
    n_mod = w_ada.shape[1]
    tn = n_mod // steps
    assert sw * steps == w_in.shape[0] and sw % 16 == 0
    assert tn * steps == n_mod and tn % LANES == 0
    slab_w = pl.BlockSpec((sw, w_in.shape[1]), lambda g: (g, 0))
    dup = lambda a: jnp.tile(a, (1, 2)).reshape(G, 1, 2 * N)
    bt = jnp.concatenate([jnp.swapaxes(b_re, 1, 2), jnp.swapaxes(b_im, 1, 2)], axis=-1)
    tile4 = lambda a: jnp.tile(a, (1, 1, R // N))
    blk = lambda s: pl.BlockSpec((gg,) + s, lambda g: (g, 0, 0))
    return pl.pallas_call(
        _prep_kernel,
        grid=(steps,),
        in_specs=[pl.BlockSpec((BATCH, D_MODEL), lambda g: (0, 0)),
                  pl.BlockSpec((D_MODEL, tn), lambda g: (0, g)),
                  pl.BlockSpec((1, tn), lambda g: (0, g)),
                  blk((1, 1)), blk((1, 2 * N)), blk((1, 2 * N)), blk((P, 2 * N)),
                  blk((P, N)), blk((P, N)), blk((P, R)), blk((P, R)), blk((P, 1)), slab_w],
        out_specs=[pl.BlockSpec((BATCH, tn), lambda g: (0, g)),
                   blk((R, R)), blk((R, R)), blk((R, R)), blk((1, 2 * N)), blk((1, 2 * N)),
                   slab_w],
        out_shape=[jax.ShapeDtypeStruct((BATCH, n_mod), F32)] +
                  [jax.ShapeDtypeStruct((G, R, R), BF16)] * 3 +
                  [jax.ShapeDtypeStruct((G, 1, 2 * N), F32)] * 2 +
                  [jax.ShapeDtypeStruct(w_in.shape, BF16)],
        compiler_params=pltpu.CompilerParams(
            dimension_semantics=("parallel",), vmem_limit_bytes=VMEM_LIMIT),
        name="prep",
    )(c, w_ada, b_ada.reshape(1, n_mod),
      log_step.reshape(G, 1, 1), dup(a_re), dup(a_im), bt, c_re, c_im, tile4(c_re), tile4(c_im),
      d_skip.reshape(G, P, 1), w_in)


_S5_GB = LANES // S5_GROUP
_S5_NP = _S5_GB // 2
_S5_NSTEP = S5_GROUPS // _S5_GB


def _s5_kernel(a16r_ref, a16i_ref, u_hbm, mt_ref, wt_ref, vt_ref, y_hbm,
               ubuf, ybuf, in_sem, out_sem, ut_scr, yt_scr, e_scr, p_scr):
    P = S5_GROUP
    N2 = 2 * S5_STATE
    cols = S5_COLS
    nblk = cols // LANES
    nt = cols // 8
    step = pl.program_id(0)
    nstep = pl.num_programs(0)
    slot = step % 2

    def in_copy(slab, sl, tau):
        return pltpu.make_async_copy(u_hbm.at[slab, :, tau, :], ubuf.at[sl, tau], in_sem.at[sl])

    def out_copy(slab, sl, tau):
        return pltpu.make_async_copy(ybuf.at[sl, tau], y_hbm.at[slab, :, tau, :], out_sem.at[sl])

    @pl.when(step == 0)
    def _():
        for tau in range(S5_SUB):
            in_copy(0, 0, tau).start()

    @pl.when(step + 1 < nstep)
    def _():
        for tau in range(S5_SUB):
            in_copy(step + 1, 1 - slot, tau).start()

    for tau in range(S5_SUB):
        in_copy(step, slot, tau).wait()

    def load_body(tau, carry):
        for c in range(nblk):
            xt = ubuf[slot, tau, c * LANES:(c + 1) * LANES, :].T
            r0 = pl.multiple_of(tau * P, P)
            for k in range(_S5_GB):
                ut_scr[k, pl.ds(r0, P), c * LANES:(c + 1) * LANES] = \
                    xt[k * P:(k + 1) * P, :].astype(BF16)
        return carry

    lax.fori_loop(0, S5_SUB, load_body, 0, unroll=4)

    for pr in range(_S5_NP):
        et = (jnp.dot(wt_ref[2 * pr], ut_scr[2 * pr], preferred_element_type=F32) +
              jnp.dot(wt_ref[2 * pr + 1], ut_scr[2 * pr + 1], preferred_element_type=F32))
        e_scr[pr] = et.T

    sub = lax.broadcasted_iota(jnp.int32, (8, N2), 0)
    lo = sub < BATCH
    first = lax.broadcasted_iota(jnp.int32, (1, N2), 1) < S5_STATE
    a_r, a_i = [], []
    for pr in range(_S5_NP):
        a_r.append(jnp.broadcast_to(
            jnp.where(first, a16r_ref[2 * pr], a16r_ref[2 * pr + 1]), (8, N2)))
        a_i.append(jnp.broadcast_to(
            jnp.where(first, a16i_ref[2 * pr], a16i_ref[2 * pr + 1]), (8, N2)))

    def scan_body(t, carry):
        r0 = pl.multiple_of(t * 8, 8)
        new = []
        for pr in range(_S5_NP):
            s_r, s_i = carry[2 * pr], carry[2 * pr + 1]
            e_r = e_scr[pr, pl.ds(r0, 8), 0:N2]
            e_i = e_scr[pr, pl.ds(r0, 8), N2:2 * N2]
            x_r = pltpu.roll(e_r, BATCH, axis=0)
            x_i = pltpu.roll(e_i, BATCH, axis=0)
            elo_r = jnp.where(lo, e_r, x_r)
            elo_i = jnp.where(lo, e_i, x_i)
            ehi_r = jnp.where(lo, x_r, e_r)
            ehi_i = jnp.where(lo, x_i, e_i)
            t_r = a_r[pr] * s_r - a_i[pr] * s_i + elo_r
            t_i = a_r[pr] * s_i + a_i[pr] * s_r + elo_i
            p_scr[pr, pl.ds(r0, 8), 0:N2] = jnp.where(lo, s_r, t_r)
            p_scr[pr, pl.ds(r0, 8), N2:2 * N2] = jnp.where(lo, s_i, t_i)
            new.append(a_r[pr] * t_r - a_i[pr] * t_i + ehi_r)
            new.append(a_r[pr] * t_i + a_i[pr] * t_r + ehi_i)
        return tuple(new)

    zero = jnp.zeros((8, N2), F32)
    lax.fori_loop(0, nt, scan_body, tuple(zero for _ in range(2 * _S5_NP)))

    for pr in range(_S5_NP):
        pt = p_scr[pr].T.astype(BF16)
        for k in range(2):
            gi = 2 * pr + k
            yt = (jnp.dot(mt_ref[gi], ut_scr[gi], preferred_element_type=F32) +
                  jnp.dot(vt_ref[gi], pt, preferred_element_type=F32))
            for tau in range(S5_SUB):
                yt_scr[tau, gi * P:(gi + 1) * P, :] = yt[tau * P:(tau + 1) * P, :]

    @pl.when(step >= 2)
    def _():
        for tau in range(S5_SUB):
            out_copy(step - 2, slot, tau).wait()

    def store_body(tau, carry):
        ybuf[slot, tau] = yt_scr[tau].T
        return carry

    lax.fori_loop(0, S5_SUB, store_body, 0, unroll=4)

    for tau in range(S5_SUB):
        out_copy(step, slot, tau).start()

    @pl.when(step == nstep - 1)
    def _():
        if _S5_NSTEP > 1:
            for tau in range(S5_SUB):
                out_copy(step - 1, 1 - slot, tau).wait()
        for tau in range(S5_SUB):
            out_copy(step, slot, tau).wait()


def _s5(u_rows, mt, wt, vt, a16r, a16i):
    G, N, R = S5_GROUPS, S5_STATE, S5_ROW
    gb = _S5_GB
    vec = pl.BlockSpec((gb, 1, 2 * N), lambda i: (i, 0, 0))
    mat = pl.BlockSpec((gb, R, R), lambda i: (i, 0, 0))
    hbm = pl.BlockSpec(memory_space=pl.ANY)
    plane = pltpu.VMEM((2, S5_SUB, S5_COLS, LANES), F32)
    return pl.pallas_call(
        _s5_kernel,
        grid=(_S5_NSTEP,),
        in_specs=[vec, vec, hbm, mat, mat, mat],
        out_specs=hbm,
        out_shape=jax.ShapeDtypeStruct((D_S5 // LANES, S5_COLS, S5_SUB, LANES), F32),
        scratch_shapes=[plane, plane,
                        pltpu.SemaphoreType.DMA((2,)), pltpu.SemaphoreType.DMA((2,)),
                        pltpu.VMEM((gb, R, S5_COLS), BF16),
                        pltpu.VMEM((S5_SUB, LANES, S5_COLS), F32),
                        pltpu.VMEM((_S5_NP, S5_COLS, 4 * N), F32),
                        pltpu.VMEM((_S5_NP, S5_COLS, 4 * N), F32)],
        compiler_params=pltpu.CompilerParams(
            dimension_semantics=("arbitrary",), vmem_limit_bytes=VMEM_LIMIT),
        name="s5",
    )(a16r, a16i, u_rows, mt, wt, vt)


_MIX_TM = 512
_MIX_SUB = 256
_MIX_NS = _MIX_TM // S5_SUB


def _mix_kernel(ssm_ref, ret_ref, x_ref, g1_ref, sc_ref, sh_ref, nw_ref, wglu_ref, bglu_ref,
                wout_ref, o_ref, h_ref):
    gain = nw_ref[...] * (1.0 + sc_ref[...])
    nsub = _MIX_TM // _MIX_SUB
    ns = _MIX_SUB // S5_SUB
    gated = []
    for r in range(nsub):
        s = jnp.concatenate([ssm_ref[k, r * ns:(r + 1) * ns].reshape(_MIX_SUB, LANES)
                             for k in range(D_S5 // LANES)], axis=1)
        cdf = 0.5 * (1.0 + jnp.tanh(math.sqrt(2.0 / math.pi) * (s + 0.044715 * (s * s * s))))
        sg = s * cdf
        z = jnp.dot(sg.astype(BF16), wglu_ref[...], preferred_element_type=F32) + bglu_ref[...]
        gated.append((sg * jax.nn.sigmoid(z)).astype(BF16))
    for r in range(nsub):
        rs = slice(r * _MIX_SUB, (r + 1) * _MIX_SUB)
        both = jnp.concatenate([ret_ref[hh, rs, :] for hh in range(RET_HEADS)] + [gated[r]],
                               axis=1)
        mix = jnp.dot(both, wout_ref[...], preferred_element_type=F32)
        x1 = x_ref[rs, :] + g1_ref[...] * mix
        o_ref[rs, :] = x1
        ms = jnp.mean(x1 * x1, axis=-1, keepdims=True)
        h_ref[rs, :] = (x1 * lax.rsqrt(ms + EPS) * gain + sh_ref[...]).astype(BF16)


def _mix(ssm_sb, ret_out, x2d, g1, sc2, sh2, norm_w, w_glu_bf, b_glu, w_out_bf):
    tm = _MIX_TM
    per_b = SEQ // tm
    mod = pl.BlockSpec((None, 1, D_MODEL), lambda i: (i // per_b, 0, 0))
    rows = pl.BlockSpec((tm, D_MODEL), lambda i: (i, 0))
    return pl.pallas_call(
        _mix_kernel,
        grid=(TOK // tm,),
        in_specs=[pl.BlockSpec((D_S5 // LANES, _MIX_NS, None, S5_SUB, LANES),
                               lambda i: (0, i % per_b, i // per_b, 0, 0)),
                  pl.BlockSpec((RET_HEADS, tm, RET_HEAD_DIM), lambda i: (0, i, 0)),
                  rows, mod, mod, mod,
                  pl.BlockSpec((1, D_MODEL), lambda i: (0, 0)),
                  pl.BlockSpec((D_S5, D_S5), lambda i: (0, 0)),
                  pl.BlockSpec((1, D_S5), lambda i: (0, 0)),
                  pl.BlockSpec((D_RET + D_S5, D_MODEL), lambda i: (0, 0))],
        out_specs=[rows, rows],
        out_shape=[jax.ShapeDtypeStruct((TOK, D_MODEL), F32),
                   jax.ShapeDtypeStruct((TOK, D_MODEL), BF16)],
        compiler_params=pltpu.CompilerParams(
            dimension_semantics=("parallel",), vmem_limit_bytes=VMEM_LIMIT),
        name="mix",
    )(ssm_sb, ret_out, x2d, g1, sc2, sh2, norm_w, w_glu_bf, b_glu, w_out_bf)


_FFN_UP_TM = 2048
_FFN_UP_SUB = 1024
_FFN_UP_TF = 512
_FFN_DOWN_TM = 512
_FFN_DOWN_SUB = 256


def _ffn_up_kernel(h_ref, wg_ref, wu_ref, wd_ref, o_ref, wd_bf_ref, wg_scr, wu_scr):
    @pl.when(pl.program_id(1) == 0)
    def _():
        wg_scr[...] = wg_ref[...].astype(BF16)
        wu_scr[...] = wu_ref[...].astype(BF16)

    wd_bf_ref[...] = wd_ref[...].astype(BF16)
    for r in range(_FFN_UP_TM // _FFN_UP_SUB):
        rs = slice(r * _FFN_UP_SUB, (r + 1) * _FFN_UP_SUB)
        h = h_ref[rs, :]
        gate = jnp.dot(h, wg_scr[...], preferred_element_type=F32)
        up = jnp.dot(h, wu_scr[...], preferred_element_type=F32)
        o_ref[rs, :] = (_silu(gate) * up).astype(BF16)


def _ffn_up(h2, w_gate_up, w_down):
    tm, tf = _FFN_UP_TM, _FFN_UP_TF
    nf = D_FF // tf
    nm = TOK // tm
    slab = D_FF // (nf * nm)
    assert slab * nf * nm == D_FF and slab % 16 == 0
    return pl.pallas_call(
        _ffn_up_kernel,
        grid=(nf, nm),
        in_specs=[pl.BlockSpec((tm, D_MODEL), lambda f, i: (i, 0)),
                  pl.BlockSpec((D_MODEL, tf), lambda f, i: (0, f)),
                  pl.BlockSpec((D_MODEL, tf), lambda f, i: (0, nf + f)),
                  pl.BlockSpec((slab, D_MODEL), lambda f, i: (f * nm + i, 0))],
        out_specs=[pl.BlockSpec((tm, tf), lambda f, i: (i, f)),
                   pl.BlockSpec((slab, D_MODEL), lambda f, i: (f * nm + i, 0))],
        out_shape=[jax.ShapeDtypeStruct((TOK, D_FF), BF16),
                   jax.ShapeDtypeStruct((D_FF, D_MODEL), BF16)],
        scratch_shapes=[pltpu.VMEM((D_MODEL, tf), BF16),
                        pltpu.VMEM((D_MODEL, tf), BF16)],
        compiler_params=pltpu.CompilerParams(
            dimension_semantics=("arbitrary", "arbitrary"), vmem_limit_bytes=VMEM_LIMIT),
        name="ffn_up",
    )(h2, w_gate_up, w_gate_up, w_down)


def _ffn_down_kernel(a_ref, x_ref, g2_ref, fw_ref, wd_ref, o_ref):
    for r in range(_FFN_DOWN_TM // _FFN_DOWN_SUB):
        rs = slice(r * _FFN_DOWN_SUB, (r + 1) * _FFN_DOWN_SUB)
        down = jnp.dot(a_ref[rs, :], wd_ref[...], preferred_element_type=F32)
        x2 = x_ref[rs, :] + g2_ref[...] * down
        ms = jnp.mean(x2 * x2, axis=-1, keepdims=True)
        o_ref[rs, :] = x2 * lax.rsqrt(ms + EPS) * fw_ref[...]


def _ffn_down(act, x1, g2, final_w, w_down_bf):
    tm = _FFN_DOWN_TM
    per_b = SEQ // tm
    rows = pl.BlockSpec((tm, D_MODEL), lambda i: (i, 0))
    return pl.pallas_call(
        _ffn_down_kernel,
        grid=(TOK // tm,),
        in_specs=[pl.BlockSpec((tm, D_FF), lambda i: (i, 0)),
                  rows,
                  pl.BlockSpec((None, 1, D_MODEL), lambda i: (i // per_b, 0, 0)),
                  pl.BlockSpec((1, D_MODEL), lambda i: (0, 0)),
                  pl.BlockSpec((D_FF, D_MODEL), lambda i: (0, 0))],
        out_specs=rows,
        out_shape=jax.ShapeDtypeStruct((TOK, D_MODEL), F32),
        compiler_params=pltpu.CompilerParams(
            dimension_semantics=("parallel",), vmem_limit_bytes=VMEM_LIMIT),
        name="ffn_down",
    )(act, x1, g2, final_w, w_down_bf)


def kernel(x, c, w_ada, b_ada, norm1_w, w_in, ret_norm_w, s5_a_re, s5_a_im, s5_log_step,
           s5_b_re, s5_b_im, s5_c_re, s5_c_im, s5_d, w_glu, b_glu, w_out, norm2_w,
           w_gate_up, w_down, final_norm_w):
    x2d = x.reshape(TOK, D_MODEL)
    layer = 0
    mod, mt, wt, vt, a16r, a16i, w_in_bf = _prep(
        c, w_ada[layer], b_ada[layer],
        s5_a_re[layer], s5_a_im[layer], s5_log_step[layer], s5_b_re[layer], s5_b_im[layer],
        s5_c_re[layer], s5_c_im[layer], s5_d[layer], w_in[layer])
    sh1, sc1, g1, sh2, sc2, g2 = [m.reshape(BATCH, 1, D_MODEL) for m in jnp.split(mod, 6, axis=-1)]

    qkvg, u_sb = _inproj(x2d, sc1, sh1, norm1_w[layer].reshape(1, D_MODEL), w_in_bf)
    ret_out, w_out_bf, w_glu_bf = _retention(qkvg, ret_norm_w[layer].reshape(1, D_RET),
                                             w_out[layer], w_glu[layer])
    y_rows = _s5(u_sb.reshape(D_S5 // LANES, S5_COLS, S5_SUB, LANES), mt, wt, vt, a16r, a16i)
    ssm_sb = y_rows.reshape(D_S5 // LANES, S5_NSUB, BATCH, S5_SUB, LANES)

    x1, h2 = _mix(ssm_sb, ret_out, x2d, g1, sc2, sh2, norm2_w[layer].reshape(1, D_MODEL),
                  w_glu_bf, b_glu[layer].reshape(1, D_S5), w_out_bf)
    act, w_down_bf = _ffn_up(h2, w_gate_up[layer], w_down[layer])
    out = _ffn_down(act, x1, g2, final_norm_w.reshape(1, D_MODEL), w_down_bf)
    return out.reshape(BATCH, SEQ, D_MODEL)
```

```python
import math

import numpy as np
import jax
import jax.numpy as jnp
from jax import lax
from jax.experimental import pallas as pl
from jax.experimental.pallas import tpu as pltpu

D_MODEL = 2048
BATCH = 4
SEQ = 2048
TOK = BATCH * SEQ
D_RET = 1024
D_S5 = 1024
RET_HEADS = 4
RET_HEAD_DIM = 256
RET_CHUNK = 128
S5_GROUP = 16
S5_GROUPS = 64
S5_STATE = 64
S5_SUB = 16
S5_ROW = S5_SUB * S5_GROUP
S5_NSUB = SEQ // S5_SUB
S5_COLS = S5_NSUB * BATCH
D_FF = 5632
ROPE_BASE = 10000.0
EPS = 1e-6
LANES = 128

F32 = jnp.float32
BF16 = jnp.bfloat16
HI = lax.Precision.HIGHEST
VMEM_LIMIT = 58 * 1024 * 1024


def _silu(v):
    return v * jax.nn.sigmoid(v)


_INPROJ_TM = 512
_INPROJ_SUB = 256
_INPROJ_NS = _INPROJ_SUB // S5_SUB


def _inproj_kernel(x_ref, sc_ref, sh_ref, nw_ref, w_ref, wout_ref, wglu_ref,
                   o_ref, u_ref, wout_bf_ref, wglu_bf_ref, h_scr, cos_scr, sin_scr):
    wout_bf_ref[...] = wout_ref[...].astype(BF16)
    wglu_bf_ref[...] = wglu_ref[...].astype(BF16)
    tm = _INPROJ_TM
    half = RET_HEAD_DIM // 2
    per_b = SEQ // tm
    i = pl.program_id(0)
    slot = i % per_b

    @pl.when(i < per_b)
    def _():
        pos = (lax.broadcasted_iota(jnp.int32, (tm, half), 0) + slot * tm).astype(F32)
        lane = lax.broadcasted_iota(jnp.int32, (tm, half), 1).astype(F32)
        ang = pos * jnp.exp(lane * (-math.log(ROPE_BASE) / half))
        cos_scr[slot] = jnp.cos(ang)
        sin_scr[slot] = jnp.sin(ang)

    gain = nw_ref[...] * (1.0 + sc_ref[...])
    shift = sh_ref[...]
    for r in range(tm // _INPROJ_SUB):
        rs = slice(r * _INPROJ_SUB, (r + 1) * _INPROJ_SUB)
        x = x_ref[rs, :]
        ms = jnp.mean(x * x, axis=-1, keepdims=True)
        h_scr[rs, :] = (x * lax.rsqrt(ms + EPS) * gain + shift).astype(BF16)
        cs = cos_scr[slot, rs, :]
        sn = sin_scr[slot, rs, :]
        for t in range(5):
            acc = jnp.dot(h_scr[rs, :], w_ref[:, t * D_RET:(t + 1) * D_RET],
                          preferred_element_type=F32)
            if t < 2:
                for hh in range(RET_HEADS):
                    c0 = hh * RET_HEAD_DIM
                    x1 = acc[:, c0:c0 + half]
                    x2 = acc[:, c0 + half:c0 + 2 * half]
                    o_ref[t * RET_HEADS + hh, rs, 0:half] = (x1 * cs - x2 * sn).astype(BF16)
                    o_ref[t * RET_HEADS + hh, rs, half:2 * half] = (x2 * cs + x1 * sn).astype(BF16)
            elif t < 4:
                for hh in range(RET_HEADS):
                    c0 = hh * RET_HEAD_DIM
                    o_ref[t * RET_HEADS + hh, rs, :] = acc[:, c0:c0 + RET_HEAD_DIM].astype(BF16)
            else:
                for k in range(D_S5 // LANES):
                    u_ref[k, r * _INPROJ_NS:(r + 1) * _INPROJ_NS] = \
                        acc[:, k * LANES:(k + 1) * LANES].reshape(_INPROJ_NS, S5_SUB, LANES)


def _inproj(x2d, sc1, sh1, norm_w, w_in_bf, w_out, w_glu):
    tm = _INPROJ_TM
    per_b = SEQ // tm
    half = RET_HEAD_DIM // 2
    steps = TOK // tm
    so = w_out.shape[0] // steps
    sg = w_glu.shape[0] // steps
    assert w_in_bf.shape == (D_MODEL, 4 * D_RET + D_S5) and D_S5 == D_RET
    assert so * steps == w_out.shape[0] and sg * steps == w_glu.shape[0] and sg % 16 == 0
    slab_o = pl.BlockSpec((so, w_out.shape[1]), lambda i: (i, 0))
    slab_g = pl.BlockSpec((sg, w_glu.shape[1]), lambda i: (i, 0))
    return pl.pallas_call(
        _inproj_kernel,
        grid=(steps,),
        in_specs=[pl.BlockSpec((tm, D_MODEL), lambda i: (i, 0)),
                  pl.BlockSpec((None, 1, D_MODEL), lambda i: (i // per_b, 0, 0)),
                  pl.BlockSpec((None, 1, D_MODEL), lambda i: (i // per_b, 0, 0)),
                  pl.BlockSpec((1, D_MODEL), lambda i: (0, 0)),
                  pl.BlockSpec(w_in_bf.shape, lambda i: (0, 0)), slab_o, slab_g],
        out_specs=[pl.BlockSpec((4 * RET_HEADS, tm, RET_HEAD_DIM), lambda i: (0, i, 0)),
                   pl.BlockSpec((D_S5 // LANES, tm // S5_SUB, None, S5_SUB, LANES),
                                lambda i: (0, i % per_b, i // per_b, 0, 0)),
                   slab_o, slab_g],
        out_shape=[jax.ShapeDtypeStruct((4 * RET_HEADS, TOK, RET_HEAD_DIM), BF16),
                   jax.ShapeDtypeStruct((D_S5 // LANES, S5_NSUB, BATCH, S5_SUB, LANES), F32),
                   jax.ShapeDtypeStruct(w_out.shape, BF16),
                   jax.ShapeDtypeStruct(w_glu.shape, BF16)],
        scratch_shapes=[pltpu.VMEM((tm, D_MODEL), BF16),
                        pltpu.VMEM((per_b, tm, half), F32),
                        pltpu.VMEM((per_b, tm, half), F32)],
        compiler_params=pltpu.CompilerParams(
            dimension_semantics=("arbitrary",), vmem_limit_bytes=VMEM_LIMIT),
        name="inproj",
    )(x2d, sc1, sh1, norm_w, w_in_bf, w_out, w_glu)


_RET_BLK = 256


def _ret_kernel(lg_ref, q_ref, k_ref, v_ref, g_ref, w_ref, o_ref,
                kv_scr, prev_scr, intra_scr, kdec_scr, qdec_scr):
    C = _RET_BLK
    dh = RET_HEAD_DIM
    nc = SEQ // C
    scale = dh ** -0.5
    lg = lg_ref[0:1, :]
    ii = lax.broadcasted_iota(jnp.int32, (C, C), 0)
    jj = lax.broadcasted_iota(jnp.int32, (C, C), 1)
    diff = (ii - jj).astype(F32)
    intra_scr[...] = jnp.where(diff >= 0.0, jnp.exp(lg * jnp.maximum(diff, 0.0)), 0.0) * scale
    row = lax.broadcasted_iota(jnp.int32, (C, dh), 0).astype(F32)
    kdec_scr[...] = jnp.exp(lg * (C - 1.0 - row)) * scale
    qdec_scr[...] = jnp.exp(lg * (row + 1.0))
    block_decay = jnp.exp(lg * float(C))
    gn_w = w_ref[...]

    for n in range(nc):
        rs = slice(n * C, (n + 1) * C)
        kd = (k_ref[rs, :].astype(F32) * kdec_scr[...]).astype(BF16)
        kv_scr[n] = lax.dot_general(kd, v_ref[rs, :], (((0,), (0,)), ((), ())),
                                    preferred_element_type=F32)

    band = 64
    for rb in range(dh // band):
        bs = slice(rb * band, (rb + 1) * band)
        st = jnp.zeros((band, dh), F32)
        for n in range(nc):
            prev_scr[n, bs, :] = st.astype(BF16)
            if n + 1 < nc:
                st = st * block_decay + kv_scr[n, bs, :]

    for n in range(nc):
        rs = slice(n * C, (n + 1) * C)
        q = q_ref[rs, :]
        s = lax.dot_general(q, k_ref[rs, :], (((1,), (1,)), ((), ())),
                            preferred_element_type=F32) * intra_scr[...]
        y = jnp.dot(s.astype(BF16), v_ref[rs, :], preferred_element_type=F32)
        y = y + jnp.dot(q, prev_scr[n], preferred_element_type=F32) * qdec_scr[...]
        mu = jnp.mean(y, axis=-1, keepdims=True)
        yc = y - mu
        var = jnp.mean(yc * yc, axis=-1, keepdims=True)
        yn = yc * lax.rsqrt(var + EPS) * gn_w
        g = g_ref[rs, :].astype(F32)
        o_ref[rs, :] = (_silu(g) * yn).astype(BF16)


def _retention(qkvg, ret_norm_w):
    dh = RET_HEAD_DIM
    nc = SEQ // _RET_BLK
    lg = np.log1p(-np.exp2(-5.0 - np.arange(RET_HEADS, dtype=np.float64)))
    lg_tab = jnp.asarray(np.broadcast_to(lg[:, None, None], (RET_HEADS, 8, dh)), F32)
    spec = lambda off: pl.BlockSpec((None, SEQ, dh), lambda b, h: (off + h, b, 0))
    return pl.pallas_call(
        _ret_kernel,
        grid=(BATCH, RET_HEADS),
        in_specs=[pl.BlockSpec((None, 8, dh), lambda b, h: (h, 0, 0)),
                  spec(0), spec(RET_HEADS), spec(2 * RET_HEADS), spec(3 * RET_HEADS),
                  pl.BlockSpec((1, dh), lambda b, h: (0, h))],
        out_specs=pl.BlockSpec((None, SEQ, dh), lambda b, h: (h, b, 0)),
        out_shape=jax.ShapeDtypeStruct((RET_HEADS, TOK, dh), BF16),
        scratch_shapes=[pltpu.VMEM((nc, dh, dh), F32),
                        pltpu.VMEM((nc, dh, dh), BF16),
                        pltpu.VMEM((_RET_BLK, _RET_BLK), F32),
                        pltpu.VMEM((_RET_BLK, dh), F32),
                        pltpu.VMEM((_RET_BLK, dh), F32)],
        compiler_params=pltpu.CompilerParams(
            dimension_semantics=("parallel", "parallel"), vmem_limit_bytes=VMEM_LIMIT),
        name="retention",
    )(lg_tab, qkvg, qkvg, qkvg, qkvg, ret_norm_w)


_S5GEN_GG = 4


def _prep_kernel(c_ref, wada_ref, bada_ref,
                 ls_ref, ar_ref, ai_ref, bt_ref, cr_ref, ci_ref, c4r_ref, c4i_ref, d_ref, win_ref,
                 mod_ref, mt_ref, wt_ref, vt_ref, a16r_ref, a16i_ref, win_bf_ref):
    cond = _silu(c_ref[...])
    mod_ref[...] = jnp.dot(cond.astype(BF16), wada_ref[...].astype(BF16),
                           preferred_element_type=F32) + bada_ref[...]
    win_bf_ref[...] = win_ref[...].astype(BF16)
    N, P, R = S5_STATE, S5_GROUP, S5_ROW
    re_half = lax.broadcasted_iota(jnp.int32, (P, 2 * N), 1) < N
    sub_e = lax.broadcasted_iota(jnp.int32, (P, 2 * N), 0)
    lane_p = lax.broadcasted_iota(jnp.int32, (P, R), 1)
    sub_p = lax.broadcasted_iota(jnp.int32, (P, R), 0)
    lag0_diag = lane_p == (R - P) + sub_p
    is_re = lane_p < 2 * N
    lane_slot = lax.shift_right_logical(lane_p, 6) & 1

    def hdot(a, b):
        return jnp.dot(a, b, precision=HI, preferred_element_type=F32)

    def cmul(xr, xi, yr, yi):
        return xr * yr - xi * yi, xr * yi + xi * yr

    for k in range(_S5GEN_GG):
        odd = k % 2
        dt = jnp.exp(ls_ref[k])
        ar = ar_ref[k]
        ai = ai_ref[k]
        mag = jnp.exp(ar * dt)
        lbr = mag * jnp.cos(ai * dt)
        lbi = mag * jnp.sin(ai * dt)
        den = ar * ar + ai * ai
        zr = ((lbr - 1.0) * ar + lbi * ai) / den
        zi = (lbi * ar - (lbr - 1.0) * ai) / den

        pw_r = jnp.ones((P, 2 * N), F32)
        pw_i = jnp.zeros((P, 2 * N), F32)
        sq_r, sq_i = lbr, lbi
        for bit in range(4):
            nr, ni = cmul(pw_r, pw_i, sq_r, sq_i)
            take = (lax.shift_right_logical(sub_e, bit) & 1) == 1
            pw_r = jnp.where(take, nr, pw_r)
            pw_i = jnp.where(take, ni, pw_i)
            if bit < 3:
                sq_r, sq_i = cmul(sq_r, sq_i, sq_r, sq_i)
        p1_r, p1_i = cmul(pw_r, pw_i, lbr, lbi)
        a16r_ref[k] = p1_r[S5_SUB - 1:S5_SUB, :]
        a16i_ref[k] = p1_i[S5_SUB - 1:S5_SUB, :]

        bt = bt_ref[k]
        x = zr * bt + jnp.where(re_half[0:1], -zi, zi) * pltpu.roll(bt, N, axis=1)
        xs = pltpu.roll(x, N, axis=1)
        pw_is = jnp.where(re_half, -pw_i, pw_i)
        w_rows = jnp.concatenate(
            [pw_r[e:e + 1] * x + pw_is[e:e + 1] * xs for e in range(S5_SUB - 1, -1, -1)], axis=0)
        w_t = w_rows.T
        w_r = w_t[0:N]
        w_i = w_t[N:2 * N]
        zero = jnp.zeros_like(w_r)
        wt_ref[k, 0:2 * N, :] = jnp.concatenate(
            [zero, w_r] if odd else [w_r, zero], axis=0).astype(BF16)
        wt_ref[k, 2 * N:4 * N, :] = jnp.concatenate(
            [zero, w_i] if odd else [w_i, zero], axis=0).astype(BF16)

        c_re = cr_ref[k]
        c_im = ci_ref[k]
        krev = hdot(c_re, w_r) - hdot(c_im, w_i)
        krev = krev + jnp.where(lag0_diag, d_ref[k], 0.0)
        for tp in range(S5_SUB):
            width = (tp + 1) * P
            piece = krev if width == R else jnp.where(
                lane_p < width, pltpu.roll(krev, width, axis=1), 0.0)
            mt_ref[k, tp * P:(tp + 1) * P, :] = piece.astype(BF16)

        c4_r = c4r_ref[k]
        c4_i = c4i_ref[k]
        keep = lane_slot == odd
        for tp in range(S5_SUB):
            lr = jnp.concatenate([p1_r[tp:tp + 1], p1_r[tp:tp + 1]], axis=1)
            li = jnp.concatenate([p1_i[tp:tp + 1], p1_i[tp:tp + 1]], axis=1)
            val = jnp.where(is_re, c4_r * lr - c4_i * li, -(c4_r * li + c4_i * lr))
            vt_ref[k, tp * P:(tp + 1) * P, :] = jnp.where(keep, val, 0.0).astype(BF16)


def _prep(c, w_ada, b_ada, a_re, a_im, log_step, b_re, b_im, c_re, c_im, d_skip, w_in):
    G, N, P, R = S5_GROUPS, S5_STATE, S5_GROUP, S5_ROW
    gg = _S5GEN_GG
    steps = G // gg
    sw = w_in.shape[0] // steps
    n_mod = w_ada.shape[1]
    tn = n_mod // steps
    assert sw * steps == w_in.shape[0] and sw % 16 == 0
    assert tn * steps == n_mod and tn % LANES == 0
    slab_w = pl.BlockSpec((sw, w_in.shape[1]), lambda g: (g, 0))
    dup = lambda a: jnp.tile(a, (1, 2)).reshape(G, 1, 2 * N)
    bt = jnp.concatenate([jnp.swapaxes(b_re, 1, 2), jnp.swapaxes(b_im, 1, 2)], axis=-1)
    tile4 = lambda a: jnp.tile(a, (1, 1, R // N))
    blk = lambda s: pl.BlockSpec((gg,) + s, lambda g: (g, 0, 0))
    return pl.pallas_call(
        _prep_kernel,
        grid=(steps,),
        in_specs=[pl.BlockSpec((BATCH, D_MODEL), lambda g: (0, 0)),
                  pl.BlockSpec((D_MODEL, tn), lambda g: (0, g)),
                  pl.BlockSpec((1, tn), lambda g: (0, g)),
                  blk((1, 1)), blk((1, 2 * N)), blk((1, 2 * N)), blk((P, 2 * N)),
                  blk((P, N)), blk((P, N)), blk((P, R)), blk((P, R)), blk((P, 1)), slab_w],
        out_specs=[pl.BlockSpec((BATCH, tn), lambda g: (0, g)),
                   blk((R, R)), blk((R, R)), blk((R, R)), blk((1, 2 * N)), blk((1, 2 * N)),
                   slab_w],
        out_shape=[jax.ShapeDtypeStruct((BATCH, n_mod), F32)] +
                  [jax.ShapeDtypeStruct((G, R, R), BF16)] * 3 +
                  [jax.ShapeDtypeStruct((G, 1, 2 * N), F32)] * 2 +
                  [jax.ShapeDtypeStruct(w_in.shape, BF16)],
        compiler_params=pltpu.CompilerParams(
            dimension_semantics=("parallel",), vmem_limit_bytes=VMEM_LIMIT),
        name="prep",
    )(c, w_ada, b_ada.reshape(1, n_mod),
      log_step.reshape(G, 1, 1), dup(a_re), dup(a_im), bt, c_re, c_im, tile4(c_re), tile4(c_im),
      d_skip.reshape(G, P, 1), w_in)


_S5_GB = LANES // S5_GROUP
_S5_NP = _S5_GB // 2
_S5_NSTEP = S5_GROUPS // _S5_GB


def _s5_kernel(a16r_ref, a16i_ref, u_hbm, mt_ref, wt_ref, vt_ref, y_hbm,
               ubuf, ybuf, in_sem, out_sem, ut_scr, yt_scr, e_scr, p_scr):
    P = S5_GROUP
    N2 = 2 * S5_STATE
    cols = S5_COLS
    nblk = cols // LANES
    nt = cols // 8
    step = pl.program_id(0)
    nstep = pl.num_programs(0)
    slot = step % 2

    def in_copy(slab, sl, tau):
        return pltpu.make_async_copy(u_hbm.at[slab, :, tau, :], ubuf.at[sl, tau], in_sem.at[sl])

    def out_copy(slab, sl, tau):
        return pltpu.make_async_copy(ybuf.at[sl, tau], y_hbm.at[slab, :, tau, :], out_sem.at[sl])

    @pl.when(step == 0)
    def _():
        for tau in range(S5_SUB):
            in_copy(0, 0, tau).start()

    @pl.when(step + 1 < nstep)
    def _():
        for tau in range(S5_SUB):
            in_copy(step + 1, 1 - slot, tau).start()

    for tau in range(S5_SUB):
        in_copy(step, slot, tau).wait()

    def load_body(tau, carry):
        for c in range(nblk):
            xt = ubuf[slot, tau, c * LANES:(c + 1) * LANES, :].T
            r0 = pl.multiple_of(tau * P, P)
            for k in range(_S5_GB):
                ut_scr[k, pl.ds(r0, P), c * LANES:(c + 1) * LANES] = \
                    xt[k * P:(k + 1) * P, :].astype(BF16)
        return carry

    lax.fori_loop(0, S5_SUB, load_body, 0, unroll=4)

    for pr in range(_S5_NP):
        et = (jnp.dot(wt_ref[2 * pr], ut_scr[2 * pr], preferred_element_type=F32) +
              jnp.dot(wt_ref[2 * pr + 1], ut_scr[2 * pr + 1], preferred_element_type=F32))
        e_scr[pr] = et.T

    sub = lax.broadcasted_iota(jnp.int32, (8, N2), 0)
    lo = sub < BATCH
    first = lax.broadcasted_iota(jnp.int32, (1, N2), 1) < S5_STATE
    a_r, a_i = [], []
    for pr in range(_S5_NP):
        a_r.append(jnp.broadcast_to(
            jnp.where(first, a16r_ref[2 * pr], a16r_ref[2 * pr + 1]), (8, N2)))
        a_i.append(jnp.broadcast_to(
            jnp.where(first, a16i_ref[2 * pr], a16i_ref[2 * pr + 1]), (8, N2)))

    def scan_body(t, carry):
        r0 = pl.multiple_of(t * 8, 8)
        new = []
        for pr in range(_S5_NP):
            s_r, s_i = carry[2 * pr], carry[2 * pr + 1]
            e_r = e_scr[pr, pl.ds(r0, 8), 0:N2]
            e_i = e_scr[pr, pl.ds(r0, 8), N2:2 * N2]
            x_r = pltpu.roll(e_r, BATCH, axis=0)
            x_i = pltpu.roll(e_i, BATCH, axis=0)
            elo_r = jnp.where(lo, e_r, x_r)
            elo_i = jnp.where(lo, e_i, x_i)
            ehi_r = jnp.where(lo, x_r, e_r)
            ehi_i = jnp.where(lo, x_i, e_i)
            t_r = a_r[pr] * s_r - a_i[pr] * s_i + elo_r
            t_i = a_r[pr] * s_i + a_i[pr] * s_r + elo_i
            p_scr[pr, pl.ds(r0, 8), 0:N2] = jnp.where(lo, s_r, t_r)
            p_scr[pr, pl.ds(r0, 8), N2:2 * N2] = jnp.where(lo, s_i, t_i)
            new.append(a_r[pr] * t_r - a_i[pr] * t_i + ehi_r)
            new.append(a_r[pr] * t_i + a_i[pr] * t_r + ehi_i)
        return tuple(new)

    zero = jnp.zeros((8, N2), F32)
    lax.fori_loop(0, nt, scan_body, tuple(zero for _ in range(2 * _S5_NP)))

    for pr in range(_S5_NP):
        pt = p_scr[pr].T.astype(BF16)
        for k in range(2):
            gi = 2 * pr + k
            yt = (jnp.dot(mt_ref[gi], ut_scr[gi], preferred_element_type=F32) +
                  jnp.dot(vt_ref[gi], pt, preferred_element_type=F32))
            for tau in range(S5_SUB):
                yt_scr[tau, gi * P:(gi + 1) * P, :] = yt[tau * P:(tau + 1) * P, :]

    @pl.when(step >= 2)
    def _():
        for tau in range(S5_SUB):
            out_copy(step - 2, slot, tau).wait()

    def store_body(tau, carry):
        ybuf[slot, tau] = yt_scr[tau].T
        return carry

    lax.fori_loop(0, S5_SUB, store_body, 0, unroll=4)

    for tau in range(S5_SUB):
        out_copy(step, slot, tau).start()

    @pl.when(step == nstep - 1)
    def _():
        if _S5_NSTEP > 1:
            for tau in range(S5_SUB):
                out_copy(step - 1, 1 - slot, tau).wait()
        for tau in range(S5_SUB):
            out_copy(step, slot, tau).wait()


def _s5(u_rows, mt, wt, vt, a16r, a16i):
    G, N, R = S5_GROUPS, S5_STATE, S5_ROW
    gb = _S5_GB
    vec = pl.BlockSpec((gb, 1, 2 * N), lambda i: (i, 0, 0))
    mat = pl.BlockSpec((gb, R, R), lambda i: (i, 0, 0))
    hbm = pl.BlockSpec(memory_space=pl.ANY)
    plane = pltpu.VMEM((2, S5_SUB, S5_COLS, LANES), F32)
    return pl.pallas_call(
        _s5_kernel,
        grid=(_S5_NSTEP,),
        in_specs=[vec, vec, hbm, mat, mat, mat],
        out_specs=hbm,
        out_shape=jax.ShapeDtypeStruct((D_S5 // LANES, S5_COLS, S5_SUB, LANES), F32),
        scratch_shapes=[plane, plane,
                        pltpu.SemaphoreType.DMA((2,)), pltpu.SemaphoreType.DMA((2,)),
                        pltpu.VMEM((gb, R, S5_COLS), BF16),
                        pltpu.VMEM((S5_SUB, LANES, S5_COLS), F32),
                        pltpu.VMEM((_S5_NP, S5_COLS, 4 * N), F32),
                        pltpu.VMEM((_S5_NP, S5_COLS, 4 * N), F32)],
        compiler_params=pltpu.CompilerParams(
            dimension_semantics=("arbitrary",), vmem_limit_bytes=VMEM_LIMIT),
        name="s5",
    )(a16r, a16i, u_rows, mt, wt, vt)


_MIX_TM = 512
_MIX_SUB = 256
_MIX_NS = _MIX_TM // S5_SUB


def _mix_kernel(ssm_ref, ret_ref, x_ref, g1_ref, sc_ref, sh_ref, nw_ref, wglu_ref, bglu_ref,
                wout_ref, o_ref, h_ref):
    gain = nw_ref[...] * (1.0 + sc_ref[...])
    nsub = _MIX_TM // _MIX_SUB
    ns = _MIX_SUB // S5_SUB
    gated = []
    for r in range(nsub):
        s = jnp.concatenate([ssm_ref[k, r * ns:(r + 1) * ns].reshape(_MIX_SUB, LANES)
                             for k in range(D_S5 // LANES)], axis=1)
        cdf = 0.5 * (1.0 + jnp.tanh(math.sqrt(2.0 / math.pi) * (s + 0.044715 * (s * s * s))))
        sg = s * cdf
        z = jnp.dot(sg.astype(BF16), wglu_ref[...], preferred_element_type=F32) + bglu_ref[...]
        gated.append((sg * jax.nn.sigmoid(z)).astype(BF16))
    for r in range(nsub):
        rs = slice(r * _MIX_SUB, (r + 1) * _MIX_SUB)
        both = jnp.concatenate([ret_ref[hh, rs, :] for hh in range(RET_HEADS)] + [gated[r]],
                               axis=1)
        mix = jnp.dot(both, wout_ref[...], preferred_element_type=F32)
        x1 = x_ref[rs, :] + g1_ref[...] * mix
        o_ref[rs, :] = x1
        ms = jnp.mean(x1 * x1, axis=-1, keepdims=True)
        h_ref[rs, :] = (x1 * lax.rsqrt(ms + EPS) * gain + sh_ref[...]).astype(BF16)


def _mix(ssm_sb, ret_out, x2d, g1, sc2, sh2, norm_w, w_glu_bf, b_glu, w_out_bf):
    tm = _MIX_TM
    per_b = SEQ // tm
    mod = pl.BlockSpec((None, 1, D_MODEL), lambda i: (i // per_b, 0, 0))
    rows = pl.BlockSpec((tm, D_MODEL), lambda i: (i, 0))
    return pl.pallas_call(
        _mix_kernel,
        grid=(TOK // tm,),
        in_specs=[pl.BlockSpec((D_S5 // LANES, _MIX_NS, None, S5_SUB, LANES),
                               lambda i: (0, i % per_b, i // per_b, 0, 0)),
                  pl.BlockSpec((RET_HEADS, tm, RET_HEAD_DIM), lambda i: (0, i, 0)),
                  rows, mod, mod, mod,
                  pl.BlockSpec((1, D_MODEL), lambda i: (0, 0)),
                  pl.BlockSpec((D_S5, D_S5), lambda i: (0, 0)),
                  pl.BlockSpec((1, D_S5), lambda i: (0, 0)),
                  pl.BlockSpec((D_RET + D_S5, D_MODEL), lambda i: (0, 0))],
        out_specs=[rows, rows],
        out_shape=[jax.ShapeDtypeStruct((TOK, D_MODEL), F32),
                   jax.ShapeDtypeStruct((TOK, D_MODEL), BF16)],
        compiler_params=pltpu.CompilerParams(
            dimension_semantics=("parallel",), vmem_limit_bytes=VMEM_LIMIT),
        name="mix",
    )(ssm_sb, ret_out, x2d, g1, sc2, sh2, norm_w, w_glu_bf, b_glu, w_out_bf)


_FFN_UP_TM = 2048
_FFN_UP_SUB = 1024
_FFN_UP_TF = 512
_FFN_DOWN_TM = 512
_FFN_DOWN_SUB = 256


def _ffn_up_kernel(h_ref, wg_ref, wu_ref, wd_ref, o_ref, wd_bf_ref, wg_scr, wu_scr):
    @pl.when(pl.program_id(1) == 0)
    def _():
        wg_scr[...] = wg_ref[...].astype(BF16)
        wu_scr[...] = wu_ref[...].astype(BF16)

    wd_bf_ref[...] = wd_ref[...].astype(BF16)
    for r in range(_FFN_UP_TM // _FFN_UP_SUB):
        rs = slice(r * _FFN_UP_SUB, (r + 1) * _FFN_UP_SUB)
        h = h_ref[rs, :]
        gate = jnp.dot(h, wg_scr[...], preferred_element_type=F32)
        up = jnp.dot(h, wu_scr[...], preferred_element_type=F32)
        o_ref[rs, :] = (_silu(gate) * up).astype(BF16)


def _ffn_up(h2, w_gate_up, w_down):
    tm, tf = _FFN_UP_TM, _FFN_UP_TF
    nf = D_FF // tf
    nm = TOK // tm
    slab = D_FF // (nf * nm)
    assert slab * nf * nm == D_FF and slab % 16 == 0
    return pl.pallas_call(
        _ffn_up_kernel,
        grid=(nf, nm),
        in_specs=[pl.BlockSpec((tm, D_MODEL), lambda f, i: (i, 0)),
                  pl.BlockSpec((D_MODEL, tf), lambda f, i: (0, f)),
                  pl.BlockSpec((D_MODEL, tf), lambda f, i: (0, nf + f)),
                  pl.BlockSpec((slab, D_MODEL), lambda f, i: (f * nm + i, 0))],
        out_specs=[pl.BlockSpec((tm, tf), lambda f, i: (i, f)),
                   pl.BlockSpec((slab, D_MODEL), lambda f, i: (f * nm + i, 0))],
        out_shape=[jax.ShapeDtypeStruct((TOK, D_FF), BF16),
                   jax.ShapeDtypeStruct((D_FF, D_MODEL), BF16)],
        scratch_shapes=[pltpu.VMEM((D_MODEL, tf), BF16),
                        pltpu.VMEM((D_MODEL, tf), BF16)],
        compiler_params=pltpu.CompilerParams(
            dimension_semantics=("arbitrary", "arbitrary"), vmem_limit_bytes=VMEM_LIMIT),
        name="ffn_up",
    )(h2, w_gate_up, w_gate_up, w_down)


def _ffn_down_kernel(a_ref, x_ref, g2_ref, fw_ref, wd_ref, o_ref):
    for r in range(_FFN_DOWN_TM // _FFN_DOWN_SUB):
        rs = slice(r * _FFN_DOWN_SUB, (r + 1) * _FFN_DOWN_SUB)
        down = jnp.dot(a_ref[rs, :], wd_ref[...], preferred_element_type=F32)
        x2 = x_ref[rs, :] + g2_ref[...] * down
        ms = jnp.mean(x2 * x2, axis=-1, keepdims=True)
        o_ref[rs, :] = x2 * lax.rsqrt(ms + EPS) * fw_ref[...]


def _ffn_down(act, x1, g2, final_w, w_down_bf):
    tm = _FFN_DOWN_TM
    per_b = SEQ // tm
    rows = pl.BlockSpec((tm, D_MODEL), lambda i: (i, 0))
    return pl.pallas_call(
        _ffn_down_kernel,
        grid=(TOK // tm,),
        in_specs=[pl.BlockSpec((tm, D_FF), lambda i: (i, 0)),
                  rows,
                  pl.BlockSpec((None, 1, D_MODEL), lambda i: (i // per_b, 0, 0)),
                  pl.BlockSpec((1, D_MODEL), lambda i: (0, 0)),
                  pl.BlockSpec((D_FF, D_MODEL), lambda i: (0, 0))],
        out_specs=rows,
        out_shape=jax.ShapeDtypeStruct((TOK, D_MODEL), F32),
        compiler_params=pltpu.CompilerParams(
            dimension_semantics=("parallel",), vmem_limit_bytes=VMEM_LIMIT),
        name="ffn_down",
    )(act, x1, g2, final_w, w_down_bf)


def kernel(x, c, w_ada, b_ada, norm1_w, w_in, ret_norm_w, s5_a_re, s5_a_im, s5_log_step,
           s5_b_re, s5_b_im, s5_c_re, s5_c_im, s5_d, w_glu, b_glu, w_out, norm2_w,
           w_gate_up, w_down, final_norm_w):
    x2d = x.reshape(TOK, D_MODEL)
    layer = 0
    mod, mt, wt, vt, a16r, a16i, w_in_bf = _prep(
        c, w_ada[layer], b_ada[layer],
        s5_a_re[layer], s5_a_im[layer], s5_log_step[layer], s5_b_re[layer], s5_b_im[layer],
        s5_c_re[layer], s5_c_im[layer], s5_d[layer], w_in[layer])
    sh1, sc1, g1, sh2, sc2, g2 = [m.reshape(BATCH, 1, D_MODEL) for m in jnp.split(mod, 6, axis=-1)]

    qkvg, u_sb, w_out_bf, w_glu_bf = _inproj(x2d, sc1, sh1, norm1_w[layer].reshape(1, D_MODEL),
                                             w_in_bf, w_out[layer], w_glu[layer])
    ret_out = _retention(qkvg, ret_norm_w[layer].reshape(1, D_RET))
    y_rows = _s5(u_sb.reshape(D_S5 // LANES, S5_COLS, S5_SUB, LANES), mt, wt, vt, a16r, a16i)
    ssm_sb = y_rows.reshape(D_S5 // LANES, S5_NSUB, BATCH, S5_SUB, LANES)

    x1, h2 = _mix(ssm_sb, ret_out, x2d, g1, sc2, sh2, norm2_w[layer].reshape(1, D_MODEL),
                  w_glu_bf, b_glu[layer].reshape(1, D_S5), w_out_bf)
    act, w_down_bf = _ffn_up(h2, w_gate_up[layer], w_down[layer])
    out = _ffn_down(act, x1, g2, final_norm_w.reshape(1, D_MODEL), w_down_bf)
    return out.reshape(BATCH, SEQ, D_MODEL)
```

```python
import math

import numpy as np
import jax
import jax.numpy as jnp
from jax import lax
from jax.experimental import pallas as pl
from jax.experimental.pallas import tpu as pltpu

D_MODEL = 2048
BATCH = 4
SEQ = 2048
TOK = BATCH * SEQ
D_RET = 1024
D_S5 = 1024
RET_HEADS = 4
RET_HEAD_DIM = 256
RET_CHUNK = 128
S5_GROUP = 16
S5_GROUPS = 64
S5_STATE = 64
S5_SUB = 16
S5_ROW = S5_SUB * S5_GROUP
S5_NSUB = SEQ // S5_SUB
S5_COLS = S5_NSUB * BATCH
D_FF = 5632
ROPE_BASE = 10000.0
EPS = 1e-6
LANES = 128

F32 = jnp.float32
BF16 = jnp.bfloat16
HI = lax.Precision.HIGHEST
VMEM_LIMIT = 58 * 1024 * 1024


def _silu(v):
    return v * jax.nn.sigmoid(v)


_INPROJ_TM = 512
_INPROJ_SUB = 256
_INPROJ_NS = _INPROJ_SUB // S5_SUB


def _inproj_kernel(x_ref, sc_ref, sh_ref, nw_ref, w_ref, wout_ref, wglu_ref,
                   o_ref, u_ref, wout_bf_ref, wglu_bf_ref, h_scr, cos_scr, sin_scr):
    wout_bf_ref[...] = wout_ref[...].astype(BF16)
    wglu_bf_ref[...] = wglu_ref[...].astype(BF16)
    tm = _INPROJ_TM
    half = RET_HEAD_DIM // 2
    per_b = SEQ // tm
    i = pl.program_id(0)
    slot = i % per_b

    @pl.when(i < per_b)
    def _():
        pos = (lax.broadcasted_iota(jnp.int32, (tm, half), 0) + slot * tm).astype(F32)
        lane = lax.broadcasted_iota(jnp.int32, (tm, half), 1).astype(F32)
        ang = pos * jnp.exp(lane * (-math.log(ROPE_BASE) / half))
        cos_scr[slot] = jnp.cos(ang)
        sin_scr[slot] = jnp.sin(ang)

    gain = nw_ref[...] * (1.0 + sc_ref[...])
    shift = sh_ref[...]
    for r in range(tm // _INPROJ_SUB):
        rs = slice(r * _INPROJ_SUB, (r + 1) * _INPROJ_SUB)
        x = x_ref[rs, :]
        ms = jnp.mean(x * x, axis=-1, keepdims=True)
        h_scr[rs, :] = (x * lax.rsqrt(ms + EPS) * gain + shift).astype(BF16)
        cs = cos_scr[slot, rs, :]
        sn = sin_scr[slot, rs, :]
        for t in range(5):
            acc = jnp.dot(h_scr[rs, :], w_ref[:, t * D_RET:(t + 1) * D_RET],
                          preferred_element_type=F32)
            if t < 2:
                for hh in range(RET_HEADS):
                    c0 = hh * RET_HEAD_DIM
                    x1 = acc[:, c0:c0 + half]
                    x2 = acc[:, c0 + half:c0 + 2 * half]
                    o_ref[t * RET_HEADS + hh, rs, 0:half] = (x1 * cs - x2 * sn).astype(BF16)
                    o_ref[t * RET_HEADS + hh, rs, half:2 * half] = (x2 * cs + x1 * sn).astype(BF16)
            elif t < 4:
                for hh in range(RET_HEADS):
                    c0 = hh * RET_HEAD_DIM
                    o_ref[t * RET_HEADS + hh, rs, :] = acc[:, c0:c0 + RET_HEAD_DIM].astype(BF16)
            else:
                for k in range(D_S5 // LANES):
                    u_ref[k, r * _INPROJ_NS:(r + 1) * _INPROJ_NS] = \
                        acc[:, k * LANES:(k + 1) * LANES].reshape(_INPROJ_NS, S5_SUB, LANES)


def _inproj(x2d, sc1, sh1, norm_w, w_in_bf, w_out, w_glu):
    tm = _INPROJ_TM
    per_b = SEQ // tm
    half = RET_HEAD_DIM // 2
    steps = TOK // tm
    so = w_out.shape[0] // steps
    sg = w_glu.shape[0] // steps
    assert w_in_bf.shape == (D_MODEL, 4 * D_RET + D_S5) and D_S5 == D_RET
    assert so * steps == w_out.shape[0] and sg * steps == w_glu.shape[0] and sg % 16 == 0
    slab_o = pl.BlockSpec((so, w_out.shape[1]), lambda i: (i, 0))
    slab_g = pl.BlockSpec((sg, w_glu.shape[1]), lambda i: (i, 0))
    return pl.pallas_call(
        _inproj_kernel,
        grid=(steps,),
        in_specs=[pl.BlockSpec((tm, D_MODEL), lambda i: (i, 0)),
                  pl.BlockSpec((None, 1, D_MODEL), lambda i: (i // per_b, 0, 0)),
                  pl.BlockSpec((None, 1, D_MODEL), lambda i: (i // per_b, 0, 0)),
                  pl.BlockSpec((1, D_MODEL), lambda i: (0, 0)),
                  pl.BlockSpec(w_in_bf.shape, lambda i: (0, 0)), slab_o, slab_g],
        out_specs=[pl.BlockSpec((4 * RET_HEADS, tm, RET_HEAD_DIM), lambda i: (0, i, 0)),
                   pl.BlockSpec((D_S5 // LANES, tm // S5_SUB, None, S5_SUB, LANES),
                                lambda i: (0, i % per_b, i // per_b, 0, 0)),
                   slab_o, slab_g],
        out_shape=[jax.ShapeDtypeStruct((4 * RET_HEADS, TOK, RET_HEAD_DIM), BF16),
                   jax.ShapeDtypeStruct((D_S5 // LANES, S5_NSUB, BATCH, S5_SUB, LANES), F32),
                   jax.ShapeDtypeStruct(w_out.shape, BF16),
                   jax.ShapeDtypeStruct(w_glu.shape, BF16)],
        scratch_shapes=[pltpu.VMEM((tm, D_MODEL), BF16),
                        pltpu.VMEM((per_b, tm, half), F32),
                        pltpu.VMEM((per_b, tm, half), F32)],
        compiler_params=pltpu.CompilerParams(
            dimension_semantics=("arbitrary",), vmem_limit_bytes=VMEM_LIMIT),
        name="inproj",
    )(x2d, sc1, sh1, norm_w, w_in_bf, w_out, w_glu)


_RET_BLK = 256


def _ret_kernel(lg_ref, q_ref, k_ref, v_ref, g_ref, w_ref, o_ref,
                kv_scr, prev_scr, intra_scr, kdec_scr, qdec_scr):
    C = _RET_BLK
    dh = RET_HEAD_DIM
    nc = SEQ // C
    scale = dh ** -0.5
    lg = lg_ref[0:1, :]
    ii = lax.broadcasted_iota(jnp.int32, (C, C), 0)
    jj = lax.broadcasted_iota(jnp.int32, (C, C), 1)
    diff = (ii - jj).astype(F32)
    intra_scr[...] = jnp.where(diff >= 0.0, jnp.exp(lg * jnp.maximum(diff, 0.0)), 0.0) * scale
    row = lax.broadcasted_iota(jnp.int32, (C, dh), 0).astype(F32)
    kdec_scr[...] = jnp.exp(lg * (C - 1.0 - row)) * scale
    qdec_scr[...] = jnp.exp(lg * (row + 1.0))
    block_decay = jnp.exp(lg * float(C))
    gn_w = w_ref[...]

    for n in range(nc):
        rs = slice(n * C, (n + 1) * C)
        kd = (k_ref[rs, :].astype(F32) * kdec_scr[...]).astype(BF16)
        kv_scr[n] = lax.dot_general(kd, v_ref[rs, :], (((0,), (0,)), ((), ())),
                                    preferred_element_type=F32)

    band = 64
    for rb in range(dh // band):
        bs = slice(rb * band, (rb + 1) * band)
        st = jnp.zeros((band, dh), F32)
        for n in range(nc):
            prev_scr[n, bs, :] = st.astype(BF16)
            if n + 1 < nc:
                st = st * block_decay + kv_scr[n, bs, :]

    for n in range(nc):
        rs = slice(n * C, (n + 1) * C)
        q = q_ref[rs, :]
        s = lax.dot_general(q, k_ref[rs, :], (((1,), (1,)), ((), ())),
                            preferred_element_type=F32) * intra_scr[...]
        y = jnp.dot(s.astype(BF16), v_ref[rs, :], preferred_element_type=F32)
        y = y + jnp.dot(q, prev_scr[n], preferred_element_type=F32) * qdec_scr[...]
        mu = jnp.mean(y, axis=-1, keepdims=True)
        yc = y - mu
        var = jnp.mean(yc * yc, axis=-1, keepdims=True)
        yn = yc * lax.rsqrt(var + EPS) * gn_w
        g = g_ref[rs, :].astype(F32)
        o_ref[rs, :] = (_silu(g) * yn).astype(BF16)


def _retention(qkvg, ret_norm_w):
    dh = RET_HEAD_DIM
    nc = SEQ // _RET_BLK
    lg = np.log1p(-np.exp2(-5.0 - np.arange(RET_HEADS, dtype=np.float64)))
    lg_tab = jnp.asarray(np.broadcast_to(lg[:, None, None], (RET_HEADS, 8, dh)), F32)
    spec = lambda off: pl.BlockSpec((None, SEQ, dh), lambda b, h: (off + h, b, 0))
    return pl.pallas_call(
        _ret_kernel,
        grid=(BATCH, RET_HEADS),
        in_specs=[pl.BlockSpec((None, 8, dh), lambda b, h: (h, 0, 0)),
                  spec(0), spec(RET_HEADS), spec(2 * RET_HEADS), spec(3 * RET_HEADS),
                  pl.BlockSpec((1, dh), lambda b, h: (0, h))],
        out_specs=pl.BlockSpec((None, SEQ, dh), lambda b, h: (h, b, 0)),
        out_shape=jax.ShapeDtypeStruct((RET_HEADS, TOK, dh), BF16),
        scratch_shapes=[pltpu.VMEM((nc, dh, dh), F32),
                        pltpu.VMEM((nc, dh, dh), BF16),
                        pltpu.VMEM((_RET_BLK, _RET_BLK), F32),
                        pltpu.VMEM((_RET_BLK, dh), F32),
                        pltpu.VMEM((_RET_BLK, dh), F32)],
        compiler_params=pltpu.CompilerParams(
            dimension_semantics=("parallel", "parallel"), vmem_limit_bytes=VMEM_LIMIT),
        name="retention",
    )(lg_tab, qkvg, qkvg, qkvg, qkvg, ret_norm_w)


_S5GEN_GG = 4


def _prep_kernel(c_ref, wada_ref, bada_ref,
                 ls_ref, ar_ref, ai_ref, bt_ref, cr_ref, ci_ref, cc_ref, d_ref, win_ref,
                 mod_ref, mt_ref, wt_ref, vt_ref, a16r_ref, a16i_ref, win_bf_ref):
    cond = _silu(c_ref[...])
    mod_ref[...] = jnp.dot(cond.astype(BF16), wada_ref[...].astype(BF16),
                           preferred_element_type=F32) + bada_ref[...]
    win_bf_ref[...] = win_ref[...].astype(BF16)
    N, P, R = S5_STATE, S5_GROUP, S5_ROW
    re_half = lax.broadcasted_iota(jnp.int32, (P, 2 * N), 1) < N
    sub_e = lax.broadcasted_iota(jnp.int32, (P, 2 * N), 0)
    lane_p = lax.broadcasted_iota(jnp.int32, (P, R), 1)
    sub_p = lax.broadcasted_iota(jnp.int32, (P, R), 0)
    lag0_diag = lane_p == (R - P) + sub_p

    def hdot(a, b):
        return jnp.dot(a, b, precision=HI, preferred_element_type=F32)

    def cmul(xr, xi, yr, yi):
        return xr * yr - xi * yi, xr * yi + xi * yr

    for k in range(_S5GEN_GG):
        dt = jnp.exp(ls_ref[k])
        ar = ar_ref[k]
        ai = ai_ref[k]
        mag = jnp.exp(ar * dt)
        lbr = mag * jnp.cos(ai * dt)
        lbi = mag * jnp.sin(ai * dt)
        den = ar * ar + ai * ai
        zr = ((lbr - 1.0) * ar + lbi * ai) / den
        zi = (lbi * ar - (lbr - 1.0) * ai) / den

        pw_r = jnp.ones((P, 2 * N), F32)
        pw_i = jnp.zeros((P, 2 * N), F32)
        sq_r, sq_i = lbr, lbi
        for bit in range(4):
            nr, ni = cmul(pw_r, pw_i, sq_r, sq_i)
            take = (lax.shift_right_logical(sub_e, bit) & 1) == 1
            pw_r = jnp.where(take, nr, pw_r)
            pw_i = jnp.where(take, ni, pw_i)
            if bit < 3:
                sq_r, sq_i = cmul(sq_r, sq_i, sq_r, sq_i)
        p1_r, p1_i = cmul(pw_r, pw_i, lbr, lbi)
        a16r_ref[k] = p1_r[S5_SUB - 1:S5_SUB, :]
        a16i_ref[k] = p1_i[S5_SUB - 1:S5_SUB, :]

        bt = bt_ref[k]
        x = zr * bt + jnp.where(re_half[0:1], -zi, zi) * pltpu.roll(bt, N, axis=1)
        xs = pltpu.roll(x, N, axis=1)
        pw_is = jnp.where(re_half, -pw_i, pw_i)
        w_rows = jnp.concatenate(
            [pw_r[e:e + 1] * x + pw_is[e:e + 1] * xs for e in range(S5_SUB - 1, -1, -1)], axis=0)
        w_t = w_rows.T
        w_r = w_t[0:N]
        w_i = w_t[N:2 * N]
        wt_ref[k] = w_t.astype(BF16)

        c_re = cr_ref[k]
        c_im = ci_ref[k]
        krev = hdot(c_re, w_r) - hdot(c_im, w_i)
        krev = krev + jnp.where(lag0_diag, d_ref[k], 0.0)
        for tp in range(S5_SUB):
            width = (tp + 1) * P
            piece = krev if width == R else jnp.where(
                lane_p < width, pltpu.roll(krev, width, axis=1), 0.0)
            mt_ref[k, tp * P:(tp + 1) * P, :] = piece.astype(BF16)

        cc = cc_ref[k]
        ccs = pltpu.roll(cc, N, axis=1)
        p1_rs = jnp.where(re_half, p1_r, -p1_r)
        for tp in range(S5_SUB):
            val = cc * p1_rs[tp:tp + 1] - ccs * p1_i[tp:tp + 1]
            vt_ref[k, tp * P:(tp + 1) * P, :] = val.astype(BF16)


def _prep(c, w_ada, b_ada, a_re, a_im, log_step, b_re, b_im, c_re, c_im, d_skip, w_in):
    G, N, P, R = S5_GROUPS, S5_STATE, S5_GROUP, S5_ROW
    gg = _S5GEN_GG
    steps = G // gg
    sw = w_in.shape[0] // steps
    n_mod = w_ada.shape[1]
    tn = n_mod // steps
    assert sw * steps == w_in.shape[0] and sw % 16 == 0
    assert tn * steps == n_mod and tn % LANES == 0
    slab_w = pl.BlockSpec((sw, w_in.shape[1]), lambda g: (g, 0))
    dup = lambda a: jnp.tile(a, (1, 2)).reshape(G, 1, 2 * N)
    bt = jnp.concatenate([jnp.swapaxes(b_re, 1, 2), jnp.swapaxes(b_im, 1, 2)], axis=-1)
    cc = jnp.concatenate([c_re, c_im], axis=-1)
    blk = lambda s: pl.BlockSpec((gg,) + s, lambda g: (g, 0, 0))
    return pl.pallas_call(
        _prep_kernel,
        grid=(steps,),
        in_specs=[pl.BlockSpec((BATCH, D_MODEL), lambda g: (0, 0)),
                  pl.BlockSpec((D_MODEL, tn), lambda g: (0, g)),
                  pl.BlockSpec((1, tn), lambda g: (0, g)),
                  blk((1, 1)), blk((1, 2 * N)), blk((1, 2 * N)), blk((P, 2 * N)),
                  blk((P, N)), blk((P, N)), blk((P, 2 * N)), blk((P, 1)), slab_w],
        out_specs=[pl.BlockSpec((BATCH, tn), lambda g: (0, g)),
                   blk((R, R)), blk((2 * N, R)), blk((R, 2 * N)), blk((1, 2 * N)), blk((1, 2 * N)),
                   slab_w],
        out_shape=[jax.ShapeDtypeStruct((BATCH, n_mod), F32),
                   jax.ShapeDtypeStruct((G, R, R), BF16),
                   jax.ShapeDtypeStruct((G, 2 * N, R), BF16),
                   jax.ShapeDtypeStruct((G, R, 2 * N), BF16)] +
                  [jax.ShapeDtypeStruct((G, 1, 2 * N), F32)] * 2 +
                  [jax.ShapeDtypeStruct(w_in.shape, BF16)],
        compiler_params=pltpu.CompilerParams(
            dimension_semantics=("parallel",), vmem_limit_bytes=VMEM_LIMIT),
        name="prep",
    )(c, w_ada, b_ada.reshape(1, n_mod),
      log_step.reshape(G, 1, 1), dup(a_re), dup(a_im), bt, c_re, c_im, cc,
      d_skip.reshape(G, P, 1), w_in)


_S5_GB = LANES // S5_GROUP
_S5_NP = _S5_GB // 2
_S5_NSTEP = S5_GROUPS // _S5_GB


def _s5_kernel(a16r_ref, a16i_ref, u_hbm, mt_ref, wt_ref, vt_ref, y_hbm,
               ubuf, ybuf, in_sem, out_sem, ut_scr, yt_scr, e_scr, p_scr):
    P = S5_GROUP
    N2 = 2 * S5_STATE
    cols = S5_COLS
    nblk = cols // LANES
    nt = cols // 8
    step = pl.program_id(0)
    nstep = pl.num_programs(0)
    slot = step % 2

    def in_copy(slab, sl, tau):
        return pltpu.make_async_copy(u_hbm.at[slab, :, tau, :], ubuf.at[sl, tau], in_sem.at[sl])

    def out_copy(slab, sl, tau):
        return pltpu.make_async_copy(ybuf.at[sl, tau], y_hbm.at[slab, :, tau, :], out_sem.at[sl])

    @pl.when(step == 0)
    def _():
        for tau in range(S5_SUB):
            in_copy(0, 0, tau).start()

    @pl.when(step + 1 < nstep)
    def _():
        for tau in range(S5_SUB):
            in_copy(step + 1, 1 - slot, tau).start()

    for tau in range(S5_SUB):
        in_copy(step, slot, tau).wait()

    def load_body(tau, carry):
        for c in range(nblk):
            xt = ubuf[slot, tau, c * LANES:(c + 1) * LANES, :].T
            r0 = pl.multiple_of(tau * P, P)
            for k in range(_S5_GB):
                ut_scr[k, pl.ds(r0, P), c * LANES:(c + 1) * LANES] = \
                    xt[k * P:(k + 1) * P, :].astype(BF16)
        return carry

    lax.fori_loop(0, S5_SUB, load_body, 0, unroll=4)

    NS = S5_STATE
    for pr in range(_S5_NP):
        e0 = jnp.dot(wt_ref[2 * pr], ut_scr[2 * pr], preferred_element_type=F32)
        e1 = jnp.dot(wt_ref[2 * pr + 1], ut_scr[2 * pr + 1], preferred_element_type=F32)
        et = jnp.concatenate([e0[0:NS], e1[0:NS], e0[NS:2 * NS], e1[NS:2 * NS]], axis=0)
        e_scr[pr] = et.T

    sub = lax.broadcasted_iota(jnp.int32, (8, N2), 0)
    lo = sub < BATCH
    first = lax.broadcasted_iota(jnp.int32, (1, N2), 1) < S5_STATE
    a_r, a_i = [], []
    for pr in range(_S5_NP):
        a_r.append(jnp.broadcast_to(
            jnp.where(first, a16r_ref[2 * pr], a16r_ref[2 * pr + 1]), (8, N2)))
        a_i.append(jnp.broadcast_to(
            jnp.where(first, a16i_ref[2 * pr], a16i_ref[2 * pr + 1]), (8, N2)))

    def scan_body(t, carry):
        r0 = pl.multiple_of(t * 8, 8)
        new = []
        for pr in range(_S5_NP):
            s_r, s_i = carry[2 * pr], carry[2 * pr + 1]
            e_r = e_scr[pr, pl.ds(r0, 8), 0:N2]
            e_i = e_scr[pr, pl.ds(r0, 8), N2:2 * N2]
            x_r = pltpu.roll(e_r, BATCH, axis=0)
            x_i = pltpu.roll(e_i, BATCH, axis=0)
            elo_r = jnp.where(lo, e_r, x_r)
            elo_i = jnp.where(lo, e_i, x_i)
            ehi_r = jnp.where(lo, x_r, e_r)
            ehi_i = jnp.where(lo, x_i, e_i)
            t_r = a_r[pr] * s_r - a_i[pr] * s_i + elo_r
            t_i = a_r[pr] * s_i + a_i[pr] * s_r + elo_i
            p_scr[pr, pl.ds(r0, 8), 0:N2] = jnp.where(lo, s_r, t_r)
            p_scr[pr, pl.ds(r0, 8), N2:2 * N2] = jnp.where(lo, s_i, t_i)
            new.append(a_r[pr] * t_r - a_i[pr] * t_i + ehi_r)
            new.append(a_r[pr] * t_i + a_i[pr] * t_r + ehi_i)
        return tuple(new)

    zero = jnp.zeros((8, N2), F32)
    lax.fori_loop(0, nt, scan_body, tuple(zero for _ in range(2 * _S5_NP)))

    for pr in range(_S5_NP):
        pt = p_scr[pr].T.astype(BF16)
        for k in range(2):
            gi = 2 * pr + k
            pk = jnp.concatenate([pt[k * NS:(k + 1) * NS],
                                  pt[(2 + k) * NS:(3 + k) * NS]], axis=0)
            yt = (jnp.dot(mt_ref[gi], ut_scr[gi], preferred_element_type=F32) +
                  jnp.dot(vt_ref[gi], pk, preferred_element_type=F32))
            for tau in range(S5_SUB):
                yt_scr[tau, gi * P:(gi + 1) * P, :] = yt[tau * P:(tau + 1) * P, :]

    @pl.when(step >= 2)
    def _():
        for tau in range(S5_SUB):
            out_copy(step - 2, slot, tau).wait()

    def store_body(tau, carry):
        ybuf[slot, tau] = yt_scr[tau].T
        return carry

    lax.fori_loop(0, S5_SUB, store_body, 0, unroll=4)

    for tau in range(S5_SUB):
        out_copy(step, slot, tau).start()

    @pl.when(step == nstep - 1)
    def _():
        if _S5_NSTEP > 1:
            for tau in range(S5_SUB):
                out_copy(step - 1, 1 - slot, tau).wait()
        for tau in range(S5_SUB):
            out_copy(step, slot, tau).wait()


def _s5(u_rows, mt, wt, vt, a16r, a16i):
    G, N, R = S5_GROUPS, S5_STATE, S5_ROW
    gb = _S5_GB
    vec = pl.BlockSpec((gb, 1, 2 * N), lambda i: (i, 0, 0))
    mat = lambda r, c: pl.BlockSpec((gb, r, c), lambda i: (i, 0, 0))
    hbm = pl.BlockSpec(memory_space=pl.ANY)
    plane = pltpu.VMEM((2, S5_SUB, S5_COLS, LANES), F32)
    return pl.pallas_call(
        _s5_kernel,
        grid=(_S5_NSTEP,),
        in_specs=[vec, vec, hbm, mat(R, R), mat(2 * N, R), mat(R, 2 * N)],
        out_specs=hbm,
        out_shape=jax.ShapeDtypeStruct((D_S5 // LANES, S5_COLS, S5_SUB, LANES), F32),
        scratch_shapes=[plane, plane,
                        pltpu.SemaphoreType.DMA((2,)), pltpu.SemaphoreType.DMA((2,)),
                        pltpu.VMEM((gb, R, S5_COLS), BF16),
                        pltpu.VMEM((S5_SUB, LANES, S5_COLS), F32),
                        pltpu.VMEM((_S5_NP, S5_COLS, 4 * N), F32),
                        pltpu.VMEM((_S5_NP, S5_COLS, 4 * N), F32)],
        compiler_params=pltpu.CompilerParams(
            dimension_semantics=("arbitrary",), vmem_limit_bytes=VMEM_LIMIT),
        name="s5",
    )(a16r, a16i, u_rows, mt, wt, vt)


_MIX_TM = 512
_MIX_SUB = 256
_MIX_NS = _MIX_TM // S5_SUB


def _mix_kernel(ssm_ref, ret_ref, x_ref, g1_ref, sc_ref, sh_ref, nw_ref, wglu_ref, bglu_ref,
                wout_ref, o_ref, h_ref):
    gain = nw_ref[...] * (1.0 + sc_ref[...])
    nsub = _MIX_TM // _MIX_SUB
    ns = _MIX_SUB // S5_SUB
    gated = []
    for r in range(nsub):
        s = jnp.concatenate([ssm_ref[k, r * ns:(r + 1) * ns].reshape(_MIX_SUB, LANES)
                             for k in range(D_S5 // LANES)], axis=1)
        cdf = 0.5 * (1.0 + jnp.tanh(math.sqrt(2.0 / math.pi) * (s + 0.044715 * (s * s * s))))
        sg = s * cdf
        z = jnp.dot(sg.astype(BF16), wglu_ref[...], preferred_element_type=F32) + bglu_ref[...]
        gated.append((sg * jax.nn.sigmoid(z)).astype(BF16))
    for r in range(nsub):
        rs = slice(r * _MIX_SUB, (r + 1) * _MIX_SUB)
        both = jnp.concatenate([ret_ref[hh, rs, :] for hh in range(RET_HEADS)] + [gated[r]],
                               axis=1)
        mix = jnp.dot(both, wout_ref[...], preferred_element_type=F32)
        x1 = x_ref[rs, :] + g1_ref[...] * mix
        o_ref[rs, :] = x1
        ms = jnp.mean(x1 * x1, axis=-1, keepdims=True)
        h_ref[rs, :] = (x1 * lax.rsqrt(ms + EPS) * gain + sh_ref[...]).astype(BF16)


def _mix(ssm_sb, ret_out, x2d, g1, sc2, sh2, norm_w, w_glu_bf, b_glu, w_out_bf):
    tm = _MIX_TM
    per_b = SEQ // tm
    mod = pl.BlockSpec((None, 1, D_MODEL), lambda i: (i // per_b, 0, 0))
    rows = pl.BlockSpec((tm, D_MODEL), lambda i: (i, 0))
    return pl.pallas_call(
        _mix_kernel,
        grid=(TOK // tm,),
        in_specs=[pl.BlockSpec((D_S5 // LANES, _MIX_NS, None, S5_SUB, LANES),
                               lambda i: (0, i % per_b, i // per_b, 0, 0)),
                  pl.BlockSpec((RET_HEADS, tm, RET_HEAD_DIM), lambda i: (0, i, 0)),
                  rows, mod, mod, mod,
                  pl.BlockSpec((1, D_MODEL), lambda i: (0, 0)),
                  pl.BlockSpec((D_S5, D_S5), lambda i: (0, 0)),
                  pl.BlockSpec((1, D_S5), lambda i: (0, 0)),
                  pl.BlockSpec((D_RET + D_S5, D_MODEL), lambda i: (0, 0))],
        out_specs=[rows, rows],
        out_shape=[jax.ShapeDtypeStruct((TOK, D_MODEL), F32),
                   jax.ShapeDtypeStruct((TOK, D_MODEL), BF16)],
        compiler_params=pltpu.CompilerParams(
            dimension_semantics=("parallel",), vmem_limit_bytes=VMEM_LIMIT),
        name="mix",
    )(ssm_sb, ret_out, x2d, g1, sc2, sh2, norm_w, w_glu_bf, b_glu, w_out_bf)


_FFN_UP_TM = 2048
_FFN_UP_SUB = 1024
_FFN_UP_TF = 512
_FFN_DOWN_TM = 512
_FFN_DOWN_SUB = 256


def _ffn_up_kernel(h_ref, wg_ref, wu_ref, wd_ref, o_ref, wd_bf_ref, wg_scr, wu_scr):
    @pl.when(pl.program_id(1) == 0)
    def _():
        wg_scr[...] = wg_ref[...].astype(BF16)
        wu_scr[...] = wu_ref[...].astype(BF16)

    wd_bf_ref[...] = wd_ref[...].astype(BF16)
    for r in range(_FFN_UP_TM // _FFN_UP_SUB):
        rs = slice(r * _FFN_UP_SUB, (r + 1) * _FFN_UP_SUB)
        h = h_ref[rs, :]
        gate = jnp.dot(h, wg_scr[...], preferred_element_type=F32)
        up = jnp.dot(h, wu_scr[...], preferred_element_type=F32)
        o_ref[rs, :] = (_silu(gate) * up).astype(BF16)


def _ffn_up(h2, w_gate_up, w_down):
    tm, tf = _FFN_UP_TM, _FFN_UP_TF
    nf = D_FF // tf
    nm = TOK // tm
    slab = D_FF // (nf * nm)
    assert slab * nf * nm == D_FF and slab % 16 == 0
    return pl.pallas_call(
        _ffn_up_kernel,
        grid=(nf, nm),
        in_specs=[pl.BlockSpec((tm, D_MODEL), lambda f, i: (i, 0)),
                  pl.BlockSpec((D_MODEL, tf), lambda f, i: (0, f)),
                  pl.BlockSpec((D_MODEL, tf), lambda f, i: (0, nf + f)),
                  pl.BlockSpec((slab, D_MODEL), lambda f, i: (f * nm + i, 0))],
        out_specs=[pl.BlockSpec((tm, tf), lambda f, i: (i, f)),
                   pl.BlockSpec((slab, D_MODEL), lambda f, i: (f * nm + i, 0))],
        out_shape=[jax.ShapeDtypeStruct((TOK, D_FF), BF16),
                   jax.ShapeDtypeStruct((D_FF, D_MODEL), BF16)],
        scratch_shapes=[pltpu.VMEM((D_MODEL, tf), BF16),
                        pltpu.VMEM((D_MODEL, tf), BF16)],
        compiler_params=pltpu.CompilerParams(
            dimension_semantics=("arbitrary", "arbitrary"), vmem_limit_bytes=VMEM_LIMIT),
        name="ffn_up",
    )(h2, w_gate_up, w_gate_up, w_down)


def _ffn_down_kernel(a_ref, x_ref, g2_ref, fw_ref, wd_ref, o_ref):
    for r in range(_FFN_DOWN_TM // _FFN_DOWN_SUB):
        rs = slice(r * _FFN_DOWN_SUB, (r + 1) * _FFN_DOWN_SUB)
        down = jnp.dot(a_ref[rs, :], wd_ref[...], preferred_element_type=F32)
        x2 = x_ref[rs, :] + g2_ref[...] * down
        ms = jnp.mean(x2 * x2, axis=-1, keepdims=True)
        o_ref[rs, :] = x2 * lax.rsqrt(ms + EPS) * fw_ref[...]


def _ffn_down(act, x1, g2, final_w, w_down_bf):
    tm = _FFN_DOWN_TM
    per_b = SEQ // tm
    rows = pl.BlockSpec((tm, D_MODEL), lambda i: (i, 0))
    return pl.pallas_call(
        _ffn_down_kernel,
        grid=(TOK // tm,),
        in_specs=[pl.BlockSpec((tm, D_FF), lambda i: (i, 0)),
                  rows,
                  pl.BlockSpec((None, 1, D_MODEL), lambda i: (i // per_b, 0, 0)),
                  pl.BlockSpec((1, D_MODEL), lambda i: (0, 0)),
                  pl.BlockSpec((D_FF, D_MODEL), lambda i: (0, 0))],
        out_specs=rows,
        out_shape=jax.ShapeDtypeStruct((TOK, D_MODEL), F32),
        compiler_params=pltpu.CompilerParams(
            dimension_semantics=("parallel",), vmem_limit_bytes=VMEM_LIMIT),
        name="ffn_down",
    )(act, x1, g2, final_w, w_down_bf)


def kernel(x, c, w_ada, b_ada, norm1_w, w_in, ret_norm_w, s5_a_re, s5_a_im, s5_log_step,
           s5_b_re, s5_b_im, s5_c_re, s5_c_im, s5_d, w_glu, b_glu, w_out, norm2_w,
           w_gate_up, w_down, final_norm_w):
    x2d = x.reshape(TOK, D_MODEL)
    layer = 0
    mod, mt, wt, vt, a16r, a16i, w_in_bf = _prep(
        c, w_ada[layer], b_ada[layer],
        s5_a_re[layer], s5_a_im[layer], s5_log_step[layer], s5_b_re[layer], s5_b_im[layer],
        s5_c_re[layer], s5_c_im[layer], s5_d[layer], w_in[layer])
    sh1, sc1, g1, sh2, sc2, g2 = [m.reshape(BATCH, 1, D_MODEL) for m in jnp.split(mod, 6, axis=-1)]

    qkvg, u_sb, w_out_bf, w_glu_bf = _inproj(x2d, sc1, sh1, norm1_w[layer].reshape(1, D_MODEL),
                                             w_in_bf, w_out[layer], w_glu[layer])
    ret_out = _retention(qkvg, ret_norm_w[layer].reshape(1, D_RET))
    y_rows = _s5(u_sb.reshape(D_S5 // LANES, S5_COLS, S5_SUB, LANES), mt, wt, vt, a16r, a16i)
    ssm_sb = y_rows.reshape(D_S5 // LANES, S5_NSUB, BATCH, S5_SUB, LANES)

    x1, h2 = _mix(ssm_sb, ret_out, x2d, g1, sc2, sh2, norm2_w[layer].reshape(1, D_MODEL),
                  w_glu_bf, b_glu[layer].reshape(1, D_S5), w_out_bf)
    act, w_down_bf = _ffn_up(h2, w_gate_up[layer], w_down[layer])
    out = _ffn_down(act, x1, g2, final_norm_w.reshape(1, D_MODEL), w_down_bf)
    return out.reshape(BATCH, SEQ, D_MODEL)
```

```python
import math

import numpy as np
import jax
import jax.numpy as jnp
from jax import lax
from jax.experimental import pallas as pl
from jax.experimental.pallas import tpu as pltpu

D_MODEL = 2048
BATCH = 4
SEQ = 2048
TOK = BATCH * SEQ
D_RET = 1024
D_S5 = 1024
RET_HEADS = 4
RET_HEAD_DIM = 256
RET_CHUNK = 128
S5_GROUP = 16
S5_GROUPS = 64
S5_STATE = 64
S5_SUB = 16
S5_ROW = S5_SUB * S5_GROUP
S5_NSUB = SEQ // S5_SUB
S5_COLS = S5_NSUB * BATCH
D_FF = 5632
ROPE_BASE = 10000.0
EPS = 1e-6
LANES = 128

F32 = jnp.float32
BF16 = jnp.bfloat16
HI = lax.Precision.HIGHEST
VMEM_LIMIT = 58 * 1024 * 1024


def _silu(v):
    return v * jax.nn.sigmoid(v)


_SH1, _SC1, _G1, _SH2, _SC2, _G2 = range(6)


def _mod_spec(which, rows_per_step):
    per_b = SEQ // rows_per_step
    return pl.BlockSpec((None, 1, D_MODEL), lambda i: (i // per_b, 0, which))


_INPROJ_TM = 512
_INPROJ_SUB = 256
_INPROJ_NS = _INPROJ_SUB // S5_SUB


def _inproj_kernel(x_ref, sc_ref, sh_ref, nw_ref, w_ref, wout_ref, wglu_ref,
                   o_ref, u_ref, wout_bf_ref, wglu_bf_ref, h_scr, cos_scr, sin_scr):
    wout_bf_ref[...] = wout_ref[...].astype(BF16)
    wglu_bf_ref[...] = wglu_ref[...].astype(BF16)
    tm = _INPROJ_TM
    half = RET_HEAD_DIM // 2
    per_b = SEQ // tm
    i = pl.program_id(0)
    slot = i % per_b

    @pl.when(i < per_b)
    def _():
        pos = (lax.broadcasted_iota(jnp.int32, (tm, half), 0) + slot * tm).astype(F32)
        lane = lax.broadcasted_iota(jnp.int32, (tm, half), 1).astype(F32)
        ang = pos * jnp.exp(lane * (-math.log(ROPE_BASE) / half))
        cos_scr[slot] = jnp.cos(ang)
        sin_scr[slot] = jnp.sin(ang)

    gain = nw_ref[...] * (1.0 + sc_ref[...])
    shift = sh_ref[...]
    for r in range(tm // _INPROJ_SUB):
        rs = slice(r * _INPROJ_SUB, (r + 1) * _INPROJ_SUB)
        x = x_ref[rs, :]
        ms = jnp.mean(x * x, axis=-1, keepdims=True)
        h_scr[rs, :] = (x * lax.rsqrt(ms + EPS) * gain + shift).astype(BF16)
        cs = cos_scr[slot, rs, :]
        sn = sin_scr[slot, rs, :]
        for t in range(5):
            acc = jnp.dot(h_scr[rs, :], w_ref[:, t * D_RET:(t + 1) * D_RET],
                          preferred_element_type=F32)
            if t < 2:
                for hh in range(RET_HEADS):
                    c0 = hh * RET_HEAD_DIM
                    x1 = acc[:, c0:c0 + half]
                    x2 = acc[:, c0 + half:c0 + 2 * half]
                    o_ref[t * RET_HEADS + hh, rs, 0:half] = (x1 * cs - x2 * sn).astype(BF16)
                    o_ref[t * RET_HEADS + hh, rs, half:2 * half] = (x2 * cs + x1 * sn).astype(BF16)
            elif t < 4:
                for hh in range(RET_HEADS):
                    c0 = hh * RET_HEAD_DIM
                    o_ref[t * RET_HEADS + hh, rs, :] = acc[:, c0:c0 + RET_HEAD_DIM].astype(BF16)
            else:
                for k in range(D_S5 // LANES):
                    u_ref[k, r * _INPROJ_NS:(r + 1) * _INPROJ_NS] = \
                        acc[:, k * LANES:(k + 1) * LANES].reshape(_INPROJ_NS, S5_SUB, LANES)


def _inproj(x2d, mod, norm_w, w_in_bf, w_out, w_glu):
    tm = _INPROJ_TM
    per_b = SEQ // tm
    half = RET_HEAD_DIM // 2
    steps = TOK // tm
    so = w_out.shape[0] // steps
    sg = w_glu.shape[0] // steps
    assert w_in_bf.shape == (D_MODEL, 4 * D_RET + D_S5) and D_S5 == D_RET
    assert so * steps == w_out.shape[0] and sg * steps == w_glu.shape[0] and sg % 16 == 0
    slab_o = pl.BlockSpec((so, w_out.shape[1]), lambda i: (i, 0))
    slab_g = pl.BlockSpec((sg, w_glu.shape[1]), lambda i: (i, 0))
    return pl.pallas_call(
        _inproj_kernel,
        grid=(steps,),
        in_specs=[pl.BlockSpec((tm, D_MODEL), lambda i: (i, 0)),
                  _mod_spec(_SC1, tm), _mod_spec(_SH1, tm),
                  pl.BlockSpec((1, D_MODEL), lambda i: (0, 0)),
                  pl.BlockSpec(w_in_bf.shape, lambda i: (0, 0)), slab_o, slab_g],
        out_specs=[pl.BlockSpec((4 * RET_HEADS, tm, RET_HEAD_DIM), lambda i: (0, i, 0)),
                   pl.BlockSpec((D_S5 // LANES, tm // S5_SUB, None, S5_SUB, LANES),
                                lambda i: (0, i % per_b, i // per_b, 0, 0)),
                   slab_o, slab_g],
        out_shape=[jax.ShapeDtypeStruct((4 * RET_HEADS, TOK, RET_HEAD_DIM), BF16),
                   jax.ShapeDtypeStruct((D_S5 // LANES, S5_NSUB, BATCH, S5_SUB, LANES), F32),
                   jax.ShapeDtypeStruct(w_out.shape, BF16),
                   jax.ShapeDtypeStruct(w_glu.shape, BF16)],
        scratch_shapes=[pltpu.VMEM((tm, D_MODEL), BF16),
                        pltpu.VMEM((per_b, tm, half), F32),
                        pltpu.VMEM((per_b, tm, half), F32)],
        compiler_params=pltpu.CompilerParams(
            dimension_semantics=("arbitrary",), vmem_limit_bytes=VMEM_LIMIT),
        name="inproj",
    )(x2d, mod, mod, norm_w, w_in_bf, w_out, w_glu)


_RET_BLK = 256


def _ret_kernel(lg_ref, q_ref, k_ref, v_ref, g_ref, w_ref, o_ref,
                kv_scr, prev_scr, intra_scr, kdec_scr, qdec_scr):
    C = _RET_BLK
    dh = RET_HEAD_DIM
    nc = SEQ // C
    scale = dh ** -0.5
    lg = lg_ref[0:1, :]
    ii = lax.broadcasted_iota(jnp.int32, (C, C), 0)
    jj = lax.broadcasted_iota(jnp.int32, (C, C), 1)
    diff = (ii - jj).astype(F32)
    intra_scr[...] = jnp.where(diff >= 0.0, jnp.exp(lg * jnp.maximum(diff, 0.0)), 0.0) * scale
    row = lax.broadcasted_iota(jnp.int32, (C, dh), 0).astype(F32)
    kdec_scr[...] = jnp.exp(lg * (C - 1.0 - row)) * scale
    qdec_scr[...] = jnp.exp(lg * (row + 1.0))
    block_decay = jnp.exp(lg * float(C))
    gn_w = w_ref[...]

    for n in range(nc):
        rs = slice(n * C, (n + 1) * C)
        kd = (k_ref[rs, :].astype(F32) * kdec_scr[...]).astype(BF16)
        kv_scr[n] = lax.dot_general(kd, v_ref[rs, :], (((0,), (0,)), ((), ())),
                                    preferred_element_type=F32)

    band = 64
    for rb in range(dh // band):
        bs = slice(rb * band, (rb + 1) * band)
        st = jnp.zeros((band, dh), F32)
        for n in range(nc):
            prev_scr[n, bs, :] = st.astype(BF16)
            if n + 1 < nc:
                st = st * block_decay + kv_scr[n, bs, :]

    for n in range(nc):
        rs = slice(n * C, (n + 1) * C)
        q = q_ref[rs, :]
        s = lax.dot_general(q, k_ref[rs, :], (((1,), (1,)), ((), ())),
                            preferred_element_type=F32) * intra_scr[...]
        y = jnp.dot(s.astype(BF16), v_ref[rs, :], preferred_element_type=F32)
        y = y + jnp.dot(q, prev_scr[n], preferred_element_type=F32) * qdec_scr[...]
        mu = jnp.mean(y, axis=-1, keepdims=True)
        yc = y - mu
        var = jnp.mean(yc * yc, axis=-1, keepdims=True)
        yn = yc * lax.rsqrt(var + EPS) * gn_w
        g = g_ref[rs, :].astype(F32)
        o_ref[rs, :] = (_silu(g) * yn).astype(BF16)


def _retention(qkvg, ret_norm_w):
    dh = RET_HEAD_DIM
    nc = SEQ // _RET_BLK
    lg = np.log1p(-np.exp2(-5.0 - np.arange(RET_HEADS, dtype=np.float64)))
    lg_tab = jnp.asarray(np.broadcast_to(lg[:, None, None], (RET_HEADS, 8, dh)), F32)
    spec = lambda off: pl.BlockSpec((None, SEQ, dh), lambda b, h: (off + h, b, 0))
    return pl.pallas_call(
        _ret_kernel,
        grid=(BATCH, RET_HEADS),
        in_specs=[pl.BlockSpec((None, 8, dh), lambda b, h: (h, 0, 0)),
                  spec(0), spec(RET_HEADS), spec(2 * RET_HEADS), spec(3 * RET_HEADS),
                  pl.BlockSpec((1, dh), lambda b, h: (0, h))],
        out_specs=pl.BlockSpec((None, SEQ, dh), lambda b, h: (h, b, 0)),
        out_shape=jax.ShapeDtypeStruct((RET_HEADS, TOK, dh), BF16),
        scratch_shapes=[pltpu.VMEM((nc, dh, dh), F32),
                        pltpu.VMEM((nc, dh, dh), BF16),
                        pltpu.VMEM((_RET_BLK, _RET_BLK), F32),
                        pltpu.VMEM((_RET_BLK, dh), F32),
                        pltpu.VMEM((_RET_BLK, dh), F32)],
        compiler_params=pltpu.CompilerParams(
            dimension_semantics=("parallel", "parallel"), vmem_limit_bytes=VMEM_LIMIT),
        name="retention",
    )(lg_tab, qkvg, qkvg, qkvg, qkvg, ret_norm_w)


_S5GEN_GG = 4


def _prep_kernel(c_ref, wada_ref, bada_ref,
                 ls_ref, ar_ref, ai_ref, br_ref, bi_ref, cr_ref, ci_ref, d_ref, win_ref,
                 mod_ref, mt_ref, wt_ref, vt_ref, a16r_ref, a16i_ref, win_bf_ref):
    cond = _silu(c_ref[...])
    mod = jnp.dot(cond.astype(BF16), wada_ref[...].astype(BF16),
                  preferred_element_type=F32) + bada_ref[...]
    for b in range(BATCH):
        mod_ref[b] = mod[b:b + 1, :]
    win_bf_ref[...] = win_ref[...].astype(BF16)
    N, P, R = S5_STATE, S5_GROUP, S5_ROW
    re_half = lax.broadcasted_iota(jnp.int32, (P, 2 * N), 1) < N
    sub_e = lax.broadcasted_iota(jnp.int32, (P, 2 * N), 0)
    lane_p = lax.broadcasted_iota(jnp.int32, (P, R), 1)
    sub_p = lax.broadcasted_iota(jnp.int32, (P, R), 0)
    lag0_diag = lane_p == (R - P) + sub_p
    lane_n = lax.broadcasted_iota(jnp.int32, (N, 2 * N), 1)
    sub_n = lax.broadcasted_iota(jnp.int32, (N, 2 * N), 0)
    put_lo = (lane_n == sub_n).astype(F32)
    put_hi = (lane_n == sub_n + N).astype(F32)
    eye_p = (lax.broadcasted_iota(jnp.int32, (P, P), 0) ==
             lax.broadcasted_iota(jnp.int32, (P, P), 1)).astype(F32)

    def hdot(a, b):
        return jnp.dot(a, b, precision=HI, preferred_element_type=F32)

    def cmul(xr, xi, yr, yi):
        return xr * yr - xi * yi, xr * yi + xi * yr

    gg = _S5GEN_GG
    a2_r = hdot(ar_ref[...], put_lo + put_hi)
    a2_i = hdot(ai_ref[...], put_lo + put_hi)
    b_cat = jnp.concatenate([m for k in range(gg) for m in (br_ref[k], bi_ref[k])], axis=0)
    bt_all = lax.dot_general(eye_p, b_cat, (((1,), (1,)), ((), ())), precision=HI,
                             preferred_element_type=F32)
    cc_all = (hdot(jnp.concatenate([cr_ref[k] for k in range(gg)], axis=0), put_lo) +
              hdot(jnp.concatenate([ci_ref[k] for k in range(gg)], axis=0), put_hi))

    for k in range(gg):
        dt = jnp.exp(ls_ref[k])
        ar = a2_r[k:k + 1]
        ai = a2_i[k:k + 1]
        mag = jnp.exp(ar * dt)
        lbr = mag * jnp.cos(ai * dt)
        lbi = mag * jnp.sin(ai * dt)
        den = ar * ar + ai * ai
        zr = ((lbr - 1.0) * ar + lbi * ai) / den
        zi = (lbi * ar - (lbr - 1.0) * ai) / den

        pw_r = jnp.ones((P, 2 * N), F32)
        pw_i = jnp.zeros((P, 2 * N), F32)
        sq_r, sq_i = lbr, lbi
        for bit in range(4):
            nr, ni = cmul(pw_r, pw_i, sq_r, sq_i)
            take = (lax.shift_right_logical(sub_e, bit) & 1) == 1
            pw_r = jnp.where(take, nr, pw_r)
            pw_i = jnp.where(take, ni, pw_i)
            if bit < 3:
                sq_r, sq_i = cmul(sq_r, sq_i, sq_r, sq_i)
        p1_r, p1_i = cmul(pw_r, pw_i, lbr, lbi)
        a16r_ref[k] = p1_r[S5_SUB - 1:S5_SUB, :]
        a16i_ref[k] = p1_i[S5_SUB - 1:S5_SUB, :]

        bt = bt_all[:, k * 2 * N:(k + 1) * 2 * N]
        x = zr * bt + jnp.where(re_half[0:1], -zi, zi) * pltpu.roll(bt, N, axis=1)
        xs = pltpu.roll(x, N, axis=1)
        pw_is = jnp.where(re_half, -pw_i, pw_i)
        w_rows = jnp.concatenate(
            [pw_r[e:e + 1] * x + pw_is[e:e + 1] * xs for e in range(S5_SUB - 1, -1, -1)], axis=0)
        w_t = w_rows.T
        w_r = w_t[0:N]
        w_i = w_t[N:2 * N]
        wt_ref[k] = w_t.astype(BF16)

        c_re = cr_ref[k]
        c_im = ci_ref[k]
        krev = hdot(c_re, w_r) - hdot(c_im, w_i)
        krev = krev + jnp.where(lag0_diag, d_ref[k], 0.0)
        for tp in range(S5_SUB):
            width = (tp + 1) * P
            piece = krev if width == R else jnp.where(
                lane_p < width, pltpu.roll(krev, width, axis=1), 0.0)
            mt_ref[k, tp * P:(tp + 1) * P, :] = piece.astype(BF16)

        cc = cc_all[k * P:(k + 1) * P]
        ccs = pltpu.roll(cc, N, axis=1)
        p1_rs = jnp.where(re_half, p1_r, -p1_r)
        for tp in range(S5_SUB):
            val = cc * p1_rs[tp:tp + 1] - ccs * p1_i[tp:tp + 1]
            vt_ref[k, tp * P:(tp + 1) * P, :] = val.astype(BF16)


def _prep(c, w_ada, b_ada, a_re, a_im, log_step, b_re, b_im, c_re, c_im, d_skip, w_in):
    G, N, P, R = S5_GROUPS, S5_STATE, S5_GROUP, S5_ROW
    gg = _S5GEN_GG
    steps = G // gg
    sw = w_in.shape[0] // steps
    n_mod = w_ada.shape[1]
    tn = n_mod // steps
    assert sw * steps == w_in.shape[0] and sw % 16 == 0
    assert tn * steps == n_mod and tn % LANES == 0
    slab_w = pl.BlockSpec((sw, w_in.shape[1]), lambda g: (g, 0))
    blk = lambda s: pl.BlockSpec((gg,) + s, lambda g: (g, 0, 0))
    rows_a = pl.BlockSpec((None, gg, N), lambda g: (g, 0, 0))
    return pl.pallas_call(
        _prep_kernel,
        grid=(steps,),
        in_specs=[pl.BlockSpec((BATCH, D_MODEL), lambda g: (0, 0)),
                  pl.BlockSpec((D_MODEL, tn), lambda g: (0, g)),
                  pl.BlockSpec((1, tn), lambda g: (0, g)),
                  blk((1, 1)), rows_a, rows_a, blk((N, P)), blk((N, P)),
                  blk((P, N)), blk((P, N)), blk((P, 1)), slab_w],
        out_specs=[pl.BlockSpec((BATCH, 1, tn), lambda g: (0, 0, g)),
                   blk((R, R)), blk((2 * N, R)), blk((R, 2 * N)), blk((1, 2 * N)), blk((1, 2 * N)),
                   slab_w],
        out_shape=[jax.ShapeDtypeStruct((BATCH, 1, n_mod), F32),
                   jax.ShapeDtypeStruct((G, R, R), BF16),
                   jax.ShapeDtypeStruct((G, 2 * N, R), BF16),
                   jax.ShapeDtypeStruct((G, R, 2 * N), BF16)] +
                  [jax.ShapeDtypeStruct((G, 1, 2 * N), F32)] * 2 +
                  [jax.ShapeDtypeStruct(w_in.shape, BF16)],
        compiler_params=pltpu.CompilerParams(
            dimension_semantics=("parallel",), vmem_limit_bytes=VMEM_LIMIT),
        name="prep",
    )(c, w_ada, b_ada.reshape(1, n_mod),
      log_step.reshape(G, 1, 1), a_re.reshape(steps, gg, N), a_im.reshape(steps, gg, N), b_re, b_im,
      c_re, c_im, d_skip.reshape(G, P, 1), w_in)


_S5_GB = LANES // S5_GROUP
_S5_NP = _S5_GB // 2
_S5_NSTEP = S5_GROUPS // _S5_GB


def _s5_kernel(a16r_ref, a16i_ref, u_hbm, mt_ref, wt_ref, vt_ref, y_hbm,
               ubuf, ybuf, in_sem, out_sem, ut_scr, yt_scr, e_scr, p_scr):
    P = S5_GROUP
    N2 = 2 * S5_STATE
    cols = S5_COLS
    nblk = cols // LANES
    nt = cols // 8
    step = pl.program_id(0)
    nstep = pl.num_programs(0)
    slot = step % 2

    def in_copy(slab, sl, tau):
        return pltpu.make_async_copy(u_hbm.at[slab, :, tau, :], ubuf.at[sl, tau], in_sem.at[sl])

    def out_copy(slab, sl, tau):
        return pltpu.make_async_copy(ybuf.at[sl, tau], y_hbm.at[slab, :, tau, :], out_sem.at[sl])

    @pl.when(step == 0)
    def _():
        for tau in range(S5_SUB):
            in_copy(0, 0, tau).start()

    @pl.when(step + 1 < nstep)
    def _():
        for tau in range(S5_SUB):
            in_copy(step + 1, 1 - slot, tau).start()

    for tau in range(S5_SUB):
        in_copy(step, slot, tau).wait()

    def load_body(tau, carry):
        for c in range(nblk):
            xt = ubuf[slot, tau, c * LANES:(c + 1) * LANES, :].T
            r0 = pl.multiple_of(tau * P, P)
            for k in range(_S5_GB):
                ut_scr[k, pl.ds(r0, P), c * LANES:(c + 1) * LANES] = \
                    xt[k * P:(k + 1) * P, :].astype(BF16)
        return carry

    lax.fori_loop(0, S5_SUB, load_body, 0, unroll=4)

    NS = S5_STATE
    for pr in range(_S5_NP):
        e0 = jnp.dot(wt_ref[2 * pr], ut_scr[2 * pr], preferred_element_type=F32)
        e1 = jnp.dot(wt_ref[2 * pr + 1], ut_scr[2 * pr + 1], preferred_element_type=F32)
        et = jnp.concatenate([e0[0:NS], e1[0:NS], e0[NS:2 * NS], e1[NS:2 * NS]], axis=0)
        e_scr[pr] = et.T

    sub = lax.broadcasted_iota(jnp.int32, (8, N2), 0)
    lo = sub < BATCH
    first = lax.broadcasted_iota(jnp.int32, (1, N2), 1) < S5_STATE
    a_r, a_i = [], []
    for pr in range(_S5_NP):
        a_r.append(jnp.broadcast_to(
            jnp.where(first, a16r_ref[2 * pr], a16r_ref[2 * pr + 1]), (8, N2)))
        a_i.append(jnp.broadcast_to(
            jnp.where(first, a16i_ref[2 * pr], a16i_ref[2 * pr + 1]), (8, N2)))

    def scan_body(t, carry):
        r0 = pl.multiple_of(t * 8, 8)
        new = []
        for pr in range(_S5_NP):
            s_r, s_i = carry[2 * pr], carry[2 * pr + 1]
            e_r = e_scr[pr, pl.ds(r0, 8), 0:N2]
            e_i = e_scr[pr, pl.ds(r0, 8), N2:2 * N2]
            x_r = pltpu.roll(e_r, BATCH, axis=0)
            x_i = pltpu.roll(e_i, BATCH, axis=0)
            elo_r = jnp.where(lo, e_r, x_r)
            elo_i = jnp.where(lo, e_i, x_i)
            ehi_r = jnp.where(lo, x_r, e_r)
            ehi_i = jnp.where(lo, x_i, e_i)
            t_r = a_r[pr] * s_r - a_i[pr] * s_i + elo_r
            t_i = a_r[pr] * s_i + a_i[pr] * s_r + elo_i
            p_scr[pr, pl.ds(r0, 8), 0:N2] = jnp.where(lo, s_r, t_r)
            p_scr[pr, pl.ds(r0, 8), N2:2 * N2] = jnp.where(lo, s_i, t_i)
            new.append(a_r[pr] * t_r - a_i[pr] * t_i + ehi_r)
            new.append(a_r[pr] * t_i + a_i[pr] * t_r + ehi_i)
        return tuple(new)

    zero = jnp.zeros((8, N2), F32)
    lax.fori_loop(0, nt, scan_body, tuple(zero for _ in range(2 * _S5_NP)))

    for pr in range(_S5_NP):
        pt = p_scr[pr].T.astype(BF16)
        for k in range(2):
            gi = 2 * pr + k
            pk = jnp.concatenate([pt[k * NS:(k + 1) * NS],
                                  pt[(2 + k) * NS:(3 + k) * NS]], axis=0)
            yt = (jnp.dot(mt_ref[gi], ut_scr[gi], preferred_element_type=F32) +
                  jnp.dot(vt_ref[gi], pk, preferred_element_type=F32))
            for tau in range(S5_SUB):
                yt_scr[tau, gi * P:(gi + 1) * P, :] = yt[tau * P:(tau + 1) * P, :]

    @pl.when(step >= 2)
    def _():
        for tau in range(S5_SUB):
            out_copy(step - 2, slot, tau).wait()

    def store_body(tau, carry):
        ybuf[slot, tau] = yt_scr[tau].T
        return carry

    lax.fori_loop(0, S5_SUB, store_body, 0, unroll=4)

    for tau in range(S5_SUB):
        out_copy(step, slot, tau).start()

    @pl.when(step == nstep - 1)
    def _():
        if _S5_NSTEP > 1:
            for tau in range(S5_SUB):
                out_copy(step - 1, 1 - slot, tau).wait()
        for tau in range(S5_SUB):
            out_copy(step, slot, tau).wait()


def _s5(u_rows, mt, wt, vt, a16r, a16i):
    G, N, R = S5_GROUPS, S5_STATE, S5_ROW
    gb = _S5_GB
    vec = pl.BlockSpec((gb, 1, 2 * N), lambda i: (i, 0, 0))
    mat = lambda r, c: pl.BlockSpec((gb, r, c), lambda i: (i, 0, 0))
    hbm = pl.BlockSpec(memory_space=pl.ANY)
    plane = pltpu.VMEM((2, S5_SUB, S5_COLS, LANES), F32)
    return pl.pallas_call(
        _s5_kernel,
        grid=(_S5_NSTEP,),
        in_specs=[vec, vec, hbm, mat(R, R), mat(2 * N, R), mat(R, 2 * N)],
        out_specs=hbm,
        out_shape=jax.ShapeDtypeStruct((D_S5 // LANES, S5_COLS, S5_SUB, LANES), F32),
        scratch_shapes=[plane, plane,
                        pltpu.SemaphoreType.DMA((2,)), pltpu.SemaphoreType.DMA((2,)),
                        pltpu.VMEM((gb, R, S5_COLS), BF16),
                        pltpu.VMEM((S5_SUB, LANES, S5_COLS), F32),
                        pltpu.VMEM((_S5_NP, S5_COLS, 4 * N), F32),
                        pltpu.VMEM((_S5_NP, S5_COLS, 4 * N), F32)],
        compiler_params=pltpu.CompilerParams(
            dimension_semantics=("arbitrary",), vmem_limit_bytes=VMEM_LIMIT),
        name="s5",
    )(a16r, a16i, u_rows, mt, wt, vt)


_MIX_TM = 512
_MIX_SUB = 256
_MIX_NS = _MIX_TM // S5_SUB


def _mix_kernel(ssm_ref, ret_ref, x_ref, g1_ref, sc_ref, sh_ref, nw_ref, wglu_ref, bglu_ref,
                wout_ref, o_ref, h_ref):
    gain = nw_ref[...] * (1.0 + sc_ref[...])
    nsub = _MIX_TM // _MIX_SUB
    ns = _MIX_SUB // S5_SUB
    gated = []
    for r in range(nsub):
        s = jnp.concatenate([ssm_ref[k, r * ns:(r + 1) * ns].reshape(_MIX_SUB, LANES)
                             for k in range(D_S5 // LANES)], axis=1)
        cdf = 0.5 * (1.0 + jnp.tanh(math.sqrt(2.0 / math.pi) * (s + 0.044715 * (s * s * s))))
        sg = s * cdf
        z = jnp.dot(sg.astype(BF16), wglu_ref[...], preferred_element_type=F32) + bglu_ref[...]
        gated.append((sg * jax.nn.sigmoid(z)).astype(BF16))
    for r in range(nsub):
        rs = slice(r * _MIX_SUB, (r + 1) * _MIX_SUB)
        both = jnp.concatenate([ret_ref[hh, rs, :] for hh in range(RET_HEADS)] + [gated[r]],
                               axis=1)
        mix = jnp.dot(both, wout_ref[...], preferred_element_type=F32)
        x1 = x_ref[rs, :] + g1_ref[...] * mix
        o_ref[rs, :] = x1
        ms = jnp.mean(x1 * x1, axis=-1, keepdims=True)
        h_ref[rs, :] = (x1 * lax.rsqrt(ms + EPS) * gain + sh_ref[...]).astype(BF16)


def _mix(ssm_sb, ret_out, x2d, mod, norm_w, w_glu_bf, b_glu, w_out_bf):
    tm = _MIX_TM
    per_b = SEQ // tm
    rows = pl.BlockSpec((tm, D_MODEL), lambda i: (i, 0))
    return pl.pallas_call(
        _mix_kernel,
        grid=(TOK // tm,),
        in_specs=[pl.BlockSpec((D_S5 // LANES, _MIX_NS, None, S5_SUB, LANES),
                               lambda i: (0, i % per_b, i // per_b, 0, 0)),
                  pl.BlockSpec((RET_HEADS, tm, RET_HEAD_DIM), lambda i: (0, i, 0)),
                  rows, _mod_spec(_G1, tm), _mod_spec(_SC2, tm), _mod_spec(_SH2, tm),
                  pl.BlockSpec((1, D_MODEL), lambda i: (0, 0)),
                  pl.BlockSpec((D_S5, D_S5), lambda i: (0, 0)),
                  pl.BlockSpec((1, D_S5), lambda i: (0, 0)),
                  pl.BlockSpec((D_RET + D_S5, D_MODEL), lambda i: (0, 0))],
        out_specs=[rows, rows],
        out_shape=[jax.ShapeDtypeStruct((TOK, D_MODEL), F32),
                   jax.ShapeDtypeStruct((TOK, D_MODEL), BF16)],
        compiler_params=pltpu.CompilerParams(
            dimension_semantics=("parallel",), vmem_limit_bytes=VMEM_LIMIT),
        name="mix",
    )(ssm_sb, ret_out, x2d, mod, mod, mod, norm_w, w_glu_bf, b_glu, w_out_bf)


_FFN_UP_TM = 2048
_FFN_UP_SUB = 1024
_FFN_UP_TF = 512
_FFN_DOWN_TM = 512
_FFN_DOWN_SUB = 256


def _ffn_up_kernel(h_ref, wg_ref, wu_ref, wd_ref, o_ref, wd_bf_ref, wg_scr, wu_scr):
    @pl.when(pl.program_id(1) == 0)
    def _():
        wg_scr[...] = wg_ref[...].astype(BF16)
        wu_scr[...] = wu_ref[...].astype(BF16)

    wd_bf_ref[...] = wd_ref[...].astype(BF16)
    for r in range(_FFN_UP_TM // _FFN_UP_SUB):
        rs = slice(r * _FFN_UP_SUB, (r + 1) * _FFN_UP_SUB)
        h = h_ref[rs, :]
        gate = jnp.dot(h, wg_scr[...], preferred_element_type=F32)
        up = jnp.dot(h, wu_scr[...], preferred_element_type=F32)
        o_ref[rs, :] = (_silu(gate) * up).astype(BF16)


def _ffn_up(h2, w_gate_up, w_down):
    tm, tf = _FFN_UP_TM, _FFN_UP_TF
    nf = D_FF // tf
    nm = TOK // tm
    slab = D_FF // (nf * nm)
    assert slab * nf * nm == D_FF and slab % 16 == 0
    return pl.pallas_call(
        _ffn_up_kernel,
        grid=(nf, nm),
        in_specs=[pl.BlockSpec((tm, D_MODEL), lambda f, i: (i, 0)),
                  pl.BlockSpec((D_MODEL, tf), lambda f, i: (0, f)),
                  pl.BlockSpec((D_MODEL, tf), lambda f, i: (0, nf + f)),
                  pl.BlockSpec((slab, D_MODEL), lambda f, i: (f * nm + i, 0))],
        out_specs=[pl.BlockSpec((tm, tf), lambda f, i: (i, f)),
                   pl.BlockSpec((slab, D_MODEL), lambda f, i: (f * nm + i, 0))],
        out_shape=[jax.ShapeDtypeStruct((TOK, D_FF), BF16),
                   jax.ShapeDtypeStruct((D_FF, D_MODEL), BF16)],
        scratch_shapes=[pltpu.VMEM((D_MODEL, tf), BF16),
                        pltpu.VMEM((D_MODEL, tf), BF16)],
        compiler_params=pltpu.CompilerParams(
            dimension_semantics=("arbitrary", "arbitrary"), vmem_limit_bytes=VMEM_LIMIT),
        name="ffn_up",
    )(h2, w_gate_up, w_gate_up, w_down)


def _ffn_down_kernel(a_ref, x_ref, g2_ref, fw_ref, wd_ref, o_ref):
    for r in range(_FFN_DOWN_TM // _FFN_DOWN_SUB):
        rs = slice(r * _FFN_DOWN_SUB, (r + 1) * _FFN_DOWN_SUB)
        down = jnp.dot(a_ref[rs, :], wd_ref[...], preferred_element_type=F32)
        x2 = x_ref[rs, :] + g2_ref[...] * down
        ms = jnp.mean(x2 * x2, axis=-1, keepdims=True)
        o_ref[rs, :] = x2 * lax.rsqrt(ms + EPS) * fw_ref[...]


def _ffn_down(act, x1, mod, final_w, w_down_bf):
    tm = _FFN_DOWN_TM
    rows = pl.BlockSpec((tm, D_MODEL), lambda i: (i, 0))
    return pl.pallas_call(
        _ffn_down_kernel,
        grid=(TOK // tm,),
        in_specs=[pl.BlockSpec((tm, D_FF), lambda i: (i, 0)),
                  rows,
                  _mod_spec(_G2, tm),
                  pl.BlockSpec((1, D_MODEL), lambda i: (0, 0)),
                  pl.BlockSpec((D_FF, D_MODEL), lambda i: (0, 0))],
        out_specs=rows,
        out_shape=jax.ShapeDtypeStruct((TOK, D_MODEL), F32),
        compiler_params=pltpu.CompilerParams(
            dimension_semantics=("parallel",), vmem_limit_bytes=VMEM_LIMIT),
        name="ffn_down",
    )(act, x1, mod, final_w, w_down_bf)


def kernel(x, c, w_ada, b_ada, norm1_w, w_in, ret_norm_w, s5_a_re, s5_a_im, s5_log_step,
           s5_b_re, s5_b_im, s5_c_re, s5_c_im, s5_d, w_glu, b_glu, w_out, norm2_w,
           w_gate_up, w_down, final_norm_w):
    x2d = x.reshape(TOK, D_MODEL)
    layer = 0
    mod, mt, wt, vt, a16r, a16i, w_in_bf = _prep(
        c, w_ada[layer], b_ada[layer],
        s5_a_re[layer], s5_a_im[layer], s5_log_step[layer], s5_b_re[layer], s5_b_im[layer],
        s5_c_re[layer], s5_c_im[layer], s5_d[layer], w_in[layer])

    qkvg, u_sb, w_out_bf, w_glu_bf = _inproj(x2d, mod, norm1_w[layer].reshape(1, D_MODEL),
                                             w_in_bf, w_out[layer], w_glu[layer])
    ret_out = _retention(qkvg, ret_norm_w[layer].reshape(1, D_RET))
    y_rows = _s5(u_sb.reshape(D_S5 // LANES, S5_COLS, S5_SUB, LANES), mt, wt, vt, a16r, a16i)
    ssm_sb = y_rows.reshape(D_S5 // LANES, S5_NSUB, BATCH, S5_SUB, LANES)

    x1, h2 = _mix(ssm_sb, ret_out, x2d, mod, norm2_w[layer].reshape(1, D_MODEL),
                  w_glu_bf, b_glu[layer].reshape(1, D_S5), w_out_bf)
    act, w_down_bf = _ffn_up(h2, w_gate_up[layer], w_down[layer])
    out = _ffn_down(act, x1, mod, final_norm_w.reshape(1, D_MODEL), w_down_bf)
    return out.reshape(BATCH, SEQ, D_MODEL)
```

```python
import math

import numpy as np
import jax
import jax.numpy as jnp
from jax import lax
from jax.experimental import pallas as pl
from jax.experimental.pallas import tpu as pltpu

D_MODEL = 2048
BATCH = 4
SEQ = 2048
TOK = BATCH * SEQ
D_RET = 1024
D_S5 = 1024
RET_HEADS = 4
RET_HEAD_DIM = 256
RET_CHUNK = 128
S5_GROUP = 16
S5_GROUPS = 64
S5_STATE = 64
S5_SUB = 16
S5_ROW = S5_SUB * S5_GROUP
S5_NSUB = SEQ // S5_SUB
S5_COLS = S5_NSUB * BATCH
D_FF = 5632
ROPE_BASE = 10000.0
EPS = 1e-6
LANES = 128

F32 = jnp.float32
BF16 = jnp.bfloat16
HI = lax.Precision.HIGHEST
VMEM_LIMIT = 58 * 1024 * 1024


def _silu(v):
    return v * jax.nn.sigmoid(v)


_SH1, _SC1, _G1, _SH2, _SC2, _G2 = range(6)


def _mod_spec(which, rows_per_step):
    per_b = SEQ // rows_per_step
    return pl.BlockSpec((None, 1, D_MODEL), lambda i: (i // per_b, 0, which))


_INPROJ_TM = 512
_INPROJ_SUB = 256
_INPROJ_NS = _INPROJ_SUB // S5_SUB


def _inproj_kernel(x_ref, sc_ref, sh_ref, nw_ref, w_ref, wout_ref, wglu_ref,
                   o_ref, u_ref, wout_bf_ref, wglu_bf_ref, h_scr, cos_scr, sin_scr):
    wout_bf_ref[...] = wout_ref[...].astype(BF16)
    wglu_bf_ref[...] = wglu_ref[...].astype(BF16)
    tm = _INPROJ_TM
    half = RET_HEAD_DIM // 2
    per_b = SEQ // tm
    i = pl.program_id(0)
    slot = i % per_b

    @pl.when(i < per_b)
    def _():
        pos = (lax.broadcasted_iota(jnp.int32, (tm, half), 0) + slot * tm).astype(F32)
        lane = lax.broadcasted_iota(jnp.int32, (tm, half), 1).astype(F32)
        ang = pos * jnp.exp(lane * (-math.log(ROPE_BASE) / half))
        cos_scr[slot] = jnp.cos(ang)
        sin_scr[slot] = jnp.sin(ang)

    gain = nw_ref[...] * (1.0 + sc_ref[...])
    shift = sh_ref[...]
    for r in range(tm // _INPROJ_SUB):
        rs = slice(r * _INPROJ_SUB, (r + 1) * _INPROJ_SUB)
        x = x_ref[rs, :]
        ms = jnp.mean(x * x, axis=-1, keepdims=True)
        h_scr[rs, :] = (x * lax.rsqrt(ms + EPS) * gain + shift).astype(BF16)
        cs = cos_scr[slot, rs, :]
        sn = sin_scr[slot, rs, :]
        for t in range(5):
            acc = jnp.dot(h_scr[rs, :], w_ref[:, t * D_RET:(t + 1) * D_RET],
                          preferred_element_type=F32)
            if t < 2:
                for hh in range(RET_HEADS):
                    c0 = hh * RET_HEAD_DIM
                    x1 = acc[:, c0:c0 + half]
                    x2 = acc[:, c0 + half:c0 + 2 * half]
                    o_ref[t * RET_HEADS + hh, rs, 0:half] = (x1 * cs - x2 * sn).astype(BF16)
                    o_ref[t * RET_HEADS + hh, rs, half:2 * half] = (x2 * cs + x1 * sn).astype(BF16)
            elif t < 4:
                for hh in range(RET_HEADS):
                    c0 = hh * RET_HEAD_DIM
                    o_ref[t * RET_HEADS + hh, rs, :] = acc[:, c0:c0 + RET_HEAD_DIM].astype(BF16)
            else:
                for k in range(D_S5 // LANES):
                    u_ref[k, r * _INPROJ_NS:(r + 1) * _INPROJ_NS] = \
                        acc[:, k * LANES:(k + 1) * LANES].reshape(_INPROJ_NS, S5_SUB, LANES)


def _inproj(x2d, mod, norm_w, w_in_bf, w_out, w_glu):
    tm = _INPROJ_TM
    per_b = SEQ // tm
    half = RET_HEAD_DIM // 2
    steps = TOK // tm
    so = w_out.shape[0] // steps
    sg = w_glu.shape[0] // steps
    assert w_in_bf.shape == (D_MODEL, 4 * D_RET + D_S5) and D_S5 == D_RET
    assert so * steps == w_out.shape[0] and sg * steps == w_glu.shape[0] and sg % 16 == 0
    slab_o = pl.BlockSpec((so, w_out.shape[1]), lambda i: (i, 0))
    slab_g = pl.BlockSpec((sg, w_glu.shape[1]), lambda i: (i, 0))
    return pl.pallas_call(
        _inproj_kernel,
        grid=(steps,),
        in_specs=[pl.BlockSpec((tm, D_MODEL), lambda i: (i, 0)),
                  _mod_spec(_SC1, tm), _mod_spec(_SH1, tm),
                  pl.BlockSpec((1, D_MODEL), lambda i: (0, 0)),
                  pl.BlockSpec(w_in_bf.shape, lambda i: (0, 0)), slab_o, slab_g],
        out_specs=[pl.BlockSpec((4 * RET_HEADS, tm, RET_HEAD_DIM), lambda i: (0, i, 0)),
                   pl.BlockSpec((D_S5 // LANES, tm // S5_SUB, None, S5_SUB, LANES),
                                lambda i: (0, i % per_b, i // per_b, 0, 0)),
                   slab_o, slab_g],
        out_shape=[jax.ShapeDtypeStruct((4 * RET_HEADS, TOK, RET_HEAD_DIM), BF16),
                   jax.ShapeDtypeStruct((D_S5 // LANES, S5_NSUB, BATCH, S5_SUB, LANES), F32),
                   jax.ShapeDtypeStruct(w_out.shape, BF16),
                   jax.ShapeDtypeStruct(w_glu.shape, BF16)],
        scratch_shapes=[pltpu.VMEM((tm, D_MODEL), BF16),
                        pltpu.VMEM((per_b, tm, half), F32),
                        pltpu.VMEM((per_b, tm, half), F32)],
        compiler_params=pltpu.CompilerParams(
            dimension_semantics=("arbitrary",), vmem_limit_bytes=VMEM_LIMIT),
        name="inproj",
    )(x2d, mod, mod, norm_w, w_in_bf, w_out, w_glu)


_RET_BLK = 256


def _ret_kernel(lg_ref, q_ref, k_ref, v_ref, g_ref, w_ref, o_ref,
                kv_scr, prev_scr, intra_scr, kdec_scr, qdec_scr):
    C = _RET_BLK
    dh = RET_HEAD_DIM
    nc = SEQ // C
    scale = dh ** -0.5
    lg = lg_ref[0:1, :]
    ii = lax.broadcasted_iota(jnp.int32, (C, C), 0)
    jj = lax.broadcasted_iota(jnp.int32, (C, C), 1)
    diff = (ii - jj).astype(F32)
    intra_scr[...] = jnp.where(diff >= 0.0, jnp.exp(lg * jnp.maximum(diff, 0.0)), 0.0) * scale
    row = lax.broadcasted_iota(jnp.int32, (C, dh), 0).astype(F32)
    kdec_scr[...] = jnp.exp(lg * (C - 1.0 - row)) * scale
    qdec_scr[...] = jnp.exp(lg * (row + 1.0))
    block_decay = jnp.exp(lg * float(C))
    gn_w = w_ref[...]

    for n in range(nc):
        rs = slice(n * C, (n + 1) * C)
        kd = (k_ref[rs, :].astype(F32) * kdec_scr[...]).astype(BF16)
        kv_scr[n] = lax.dot_general(kd, v_ref[rs, :], (((0,), (0,)), ((), ())),
                                    preferred_element_type=F32)

    band = 64
    for rb in range(dh // band):
        bs = slice(rb * band, (rb + 1) * band)
        st = jnp.zeros((band, dh), F32)
        for n in range(nc):
            prev_scr[n, bs, :] = st.astype(BF16)
            if n + 1 < nc:
                st = st * block_decay + kv_scr[n, bs, :]

    for n in range(nc):
        rs = slice(n * C, (n + 1) * C)
        q = q_ref[rs, :]
        s = lax.dot_general(q, k_ref[rs, :], (((1,), (1,)), ((), ())),
                            preferred_element_type=F32) * intra_scr[...]
        y = jnp.dot(s.astype(BF16), v_ref[rs, :], preferred_element_type=F32)
        y = y + jnp.dot(q, prev_scr[n], preferred_element_type=F32) * qdec_scr[...]
        mu = jnp.mean(y, axis=-1, keepdims=True)
        yc = y - mu
        var = jnp.mean(yc * yc, axis=-1, keepdims=True)
        yn = yc * lax.rsqrt(var + EPS) * gn_w
        g = g_ref[rs, :].astype(F32)
        o_ref[rs, :] = (_silu(g) * yn).astype(BF16)


def _retention(qkvg, ret_norm_w):
    dh = RET_HEAD_DIM
    nc = SEQ // _RET_BLK
    lg = np.log1p(-np.exp2(-5.0 - np.arange(RET_HEADS, dtype=np.float64)))
    lg_tab = jnp.asarray(np.broadcast_to(lg[:, None, None], (RET_HEADS, 8, dh)), F32)
    spec = lambda off: pl.BlockSpec((None, SEQ, dh), lambda b, h: (off + h, b, 0))
    return pl.pallas_call(
        _ret_kernel,
        grid=(BATCH, RET_HEADS),
        in_specs=[pl.BlockSpec((None, 8, dh), lambda b, h: (h, 0, 0)),
                  spec(0), spec(RET_HEADS), spec(2 * RET_HEADS), spec(3 * RET_HEADS),
                  pl.BlockSpec((1, dh), lambda b, h: (0, h))],
        out_specs=pl.BlockSpec((None, SEQ, dh), lambda b, h: (h, b, 0)),
        out_shape=jax.ShapeDtypeStruct((RET_HEADS, TOK, dh), BF16),
        scratch_shapes=[pltpu.VMEM((nc, dh, dh), F32),
                        pltpu.VMEM((nc, dh, dh), BF16),
                        pltpu.VMEM((_RET_BLK, _RET_BLK), F32),
                        pltpu.VMEM((_RET_BLK, dh), F32),
                        pltpu.VMEM((_RET_BLK, dh), F32)],
        compiler_params=pltpu.CompilerParams(
            dimension_semantics=("parallel", "parallel"), vmem_limit_bytes=VMEM_LIMIT),
        name="retention",
    )(lg_tab, qkvg, qkvg, qkvg, qkvg, ret_norm_w)


_S5GEN_GG = 4


def _prep_kernel(c_ref, wada_ref, bada_ref,
                 ls_ref, ar_ref, ai_ref, bt_ref, cr_ref, ci_ref, cc_ref, d_ref, win_ref,
                 mod_ref, mt_ref, wt_ref, vt_ref, a16r_ref, a16i_ref, win_bf_ref):
    cond = _silu(c_ref[...])
    mod = jnp.dot(cond.astype(BF16), wada_ref[...].astype(BF16),
                  preferred_element_type=F32) + bada_ref[...]
    for b in range(BATCH):
        mod_ref[b] = mod[b:b + 1, :]
    win_bf_ref[...] = win_ref[...].astype(BF16)
    N, P, R = S5_STATE, S5_GROUP, S5_ROW
    re_half = lax.broadcasted_iota(jnp.int32, (P, 2 * N), 1) < N
    sub_e = lax.broadcasted_iota(jnp.int32, (P, 2 * N), 0)
    lane_p = lax.broadcasted_iota(jnp.int32, (P, R), 1)
    sub_p = lax.broadcasted_iota(jnp.int32, (P, R), 0)
    lag0_diag = lane_p == (R - P) + sub_p

    def hdot(a, b):
        return jnp.dot(a, b, precision=HI, preferred_element_type=F32)

    def cmul(xr, xi, yr, yi):
        return xr * yr - xi * yi, xr * yi + xi * yr

    for k in range(_S5GEN_GG):
        dt = jnp.exp(ls_ref[k])
        ar = ar_ref[k]
        ai = ai_ref[k]
        mag = jnp.exp(ar * dt)
        lbr = mag * jnp.cos(ai * dt)
        lbi = mag * jnp.sin(ai * dt)
        den = ar * ar + ai * ai
        zr = ((lbr - 1.0) * ar + lbi * ai) / den
        zi = (lbi * ar - (lbr - 1.0) * ai) / den

        pw_r = jnp.ones((P, 2 * N), F32)
        pw_i = jnp.zeros((P, 2 * N), F32)
        sq_r, sq_i = lbr, lbi
        for bit in range(4):
            nr, ni = cmul(pw_r, pw_i, sq_r, sq_i)
            take = (lax.shift_right_logical(sub_e, bit) & 1) == 1
            pw_r = jnp.where(take, nr, pw_r)
            pw_i = jnp.where(take, ni, pw_i)
            if bit < 3:
                sq_r, sq_i = cmul(sq_r, sq_i, sq_r, sq_i)
        p1_r, p1_i = cmul(pw_r, pw_i, lbr, lbi)
        a16r_ref[k] = p1_r[S5_SUB - 1:S5_SUB, :]
        a16i_ref[k] = p1_i[S5_SUB - 1:S5_SUB, :]

        bt = bt_ref[k]
        x = zr * bt + jnp.where(re_half[0:1], -zi, zi) * pltpu.roll(bt, N, axis=1)
        xs = pltpu.roll(x, N, axis=1)
        pw_is = jnp.where(re_half, -pw_i, pw_i)
        w_rows = jnp.concatenate(
            [pw_r[e:e + 1] * x + pw_is[e:e + 1] * xs for e in range(S5_SUB - 1, -1, -1)], axis=0)
        w_t = w_rows.T
        w_r = w_t[0:N]
        w_i = w_t[N:2 * N]
        wt_ref[k] = w_t.astype(BF16)

        c_re = cr_ref[k]
        c_im = ci_ref[k]
        krev = hdot(c_re, w_r) - hdot(c_im, w_i)
        krev = krev + jnp.where(lag0_diag, d_ref[k], 0.0)
        for tp in range(S5_SUB):
            width = (tp + 1) * P
            piece = krev if width == R else jnp.where(
                lane_p < width, pltpu.roll(krev, width, axis=1), 0.0)
            mt_ref[k, tp * P:(tp + 1) * P, :] = piece.astype(BF16)

        cc = cc_ref[k]
        ccs = pltpu.roll(cc, N, axis=1)
        p1_rs = jnp.where(re_half, p1_r, -p1_r)
        for tp in range(S5_SUB):
            val = cc * p1_rs[tp:tp + 1] - ccs * p1_i[tp:tp + 1]
            vt_ref[k, tp * P:(tp + 1) * P, :] = val.astype(BF16)


def _prep(c, w_ada, b_ada, a_re, a_im, log_step, b_re, b_im, c_re, c_im, d_skip, w_in):
    G, N, P, R = S5_GROUPS, S5_STATE, S5_GROUP, S5_ROW
    gg = _S5GEN_GG
    steps = G // gg
    sw = w_in.shape[0] // steps
    n_mod = w_ada.shape[1]
    tn = n_mod // steps
    assert sw * steps == w_in.shape[0] and sw % 16 == 0
    assert tn * steps == n_mod and tn % LANES == 0
    slab_w = pl.BlockSpec((sw, w_in.shape[1]), lambda g: (g, 0))
    dup = lambda a: jnp.tile(a, (1, 2)).reshape(G, 1, 2 * N)
    bt = jnp.concatenate([jnp.swapaxes(b_re, 1, 2), jnp.swapaxes(b_im, 1, 2)], axis=-1)
    cc = jnp.concatenate([c_re, c_im], axis=-1)
    blk = lambda s: pl.BlockSpec((gg,) + s, lambda g: (g, 0, 0))
    return pl.pallas_call(
        _prep_kernel,
        grid=(steps,),
        in_specs=[pl.BlockSpec((BATCH, D_MODEL), lambda g: (0, 0)),
                  pl.BlockSpec((D_MODEL, tn), lambda g: (0, g)),
                  pl.BlockSpec((1, tn), lambda g: (0, g)),
                  blk((1, 1)), blk((1, 2 * N)), blk((1, 2 * N)), blk((P, 2 * N)),
                  blk((P, N)), blk((P, N)), blk((P, 2 * N)), blk((P, 1)), slab_w],
        out_specs=[pl.BlockSpec((BATCH, 1, tn), lambda g: (0, 0, g)),
                   blk((R, R)), blk((2 * N, R)), blk((R, 2 * N)), blk((1, 2 * N)), blk((1, 2 * N)),
                   slab_w],
        out_shape=[jax.ShapeDtypeStruct((BATCH, 1, n_mod), F32),
                   jax.ShapeDtypeStruct((G, R, R), BF16),
                   jax.ShapeDtypeStruct((G, 2 * N, R), BF16),
                   jax.ShapeDtypeStruct((G, R, 2 * N), BF16)] +
                  [jax.ShapeDtypeStruct((G, 1, 2 * N), F32)] * 2 +
                  [jax.ShapeDtypeStruct(w_in.shape, BF16)],
        compiler_params=pltpu.CompilerParams(
            dimension_semantics=("parallel",), vmem_limit_bytes=VMEM_LIMIT),
        name="prep",
    )(c, w_ada, b_ada.reshape(1, n_mod),
      log_step.reshape(G, 1, 1), dup(a_re), dup(a_im), bt, c_re, c_im, cc,
      d_skip.reshape(G, P, 1), w_in)


_S5_GB = LANES // S5_GROUP
_S5_NP = _S5_GB // 2
_S5_NSTEP = S5_GROUPS // _S5_GB


def _s5_kernel(a16r_ref, a16i_ref, u_hbm, mt_ref, wt_ref, vt_ref, y_hbm,
               ubuf, ybuf, in_sem, out_sem, ut_scr, yt_scr, e_scr, p_scr):
    P = S5_GROUP
    N2 = 2 * S5_STATE
    cols = S5_COLS
    nblk = cols // LANES
    nt = cols // 8
    step = pl.program_id(0)
    nstep = pl.num_programs(0)
    slot = step % 2

    def in_copy(slab, sl, tau):
        return pltpu.make_async_copy(u_hbm.at[slab, :, tau, :], ubuf.at[sl, tau], in_sem.at[sl])

    def out_copy(slab, sl, tau):
        return pltpu.make_async_copy(ybuf.at[sl, tau], y_hbm.at[slab, :, tau, :], out_sem.at[sl])

    @pl.when(step == 0)
    def _():
        for tau in range(S5_SUB):
            in_copy(0, 0, tau).start()

    @pl.when(step + 1 < nstep)
    def _():
        for tau in range(S5_SUB):
            in_copy(step + 1, 1 - slot, tau).start()

    for tau in range(S5_SUB):
        in_copy(step, slot, tau).wait()

    def load_body(tau, carry):
        for c in range(nblk):
            xt = ubuf[slot, tau, c * LANES:(c + 1) * LANES, :].T
            r0 = pl.multiple_of(tau * P, P)
            for k in range(_S5_GB):
                ut_scr[k, pl.ds(r0, P), c * LANES:(c + 1) * LANES] = \
                    xt[k * P:(k + 1) * P, :].astype(BF16)
        return carry

    lax.fori_loop(0, S5_SUB, load_body, 0, unroll=4)

    NS = S5_STATE
    for pr in range(_S5_NP):
        e0 = jnp.dot(wt_ref[2 * pr], ut_scr[2 * pr], preferred_element_type=F32)
        e1 = jnp.dot(wt_ref[2 * pr + 1], ut_scr[2 * pr + 1], preferred_element_type=F32)
        et = jnp.concatenate([e0[0:NS], e1[0:NS], e0[NS:2 * NS], e1[NS:2 * NS]], axis=0)
        e_scr[pr] = et.T

    sub = lax.broadcasted_iota(jnp.int32, (8, N2), 0)
    lo = sub < BATCH
    first = lax.broadcasted_iota(jnp.int32, (1, N2), 1) < S5_STATE
    a_r, a_i = [], []
    for pr in range(_S5_NP):
        a_r.append(jnp.broadcast_to(
            jnp.where(first, a16r_ref[2 * pr], a16r_ref[2 * pr + 1]), (8, N2)))
        a_i.append(jnp.broadcast_to(
            jnp.where(first, a16i_ref[2 * pr], a16i_ref[2 * pr + 1]), (8, N2)))

    def scan_body(t, carry):
        r0 = pl.multiple_of(t * 8, 8)
        new = []
        for pr in range(_S5_NP):
            s_r, s_i = carry[2 * pr], carry[2 * pr + 1]
            e_r = e_scr[pr, pl.ds(r0, 8), 0:N2]
            e_i = e_scr[pr, pl.ds(r0, 8), N2:2 * N2]
            x_r = pltpu.roll(e_r, BATCH, axis=0)
            x_i = pltpu.roll(e_i, BATCH, axis=0)
            elo_r = jnp.where(lo, e_r, x_r)
            elo_i = jnp.where(lo, e_i, x_i)
            ehi_r = jnp.where(lo, x_r, e_r)
            ehi_i = jnp.where(lo, x_i, e_i)
            t_r = a_r[pr] * s_r - a_i[pr] * s_i + elo_r
            t_i = a_r[pr] * s_i + a_i[pr] * s_r + elo_i
            p_scr[pr, pl.ds(r0, 8), 0:N2] = jnp.where(lo, s_r, t_r)
            p_scr[pr, pl.ds(r0, 8), N2:2 * N2] = jnp.where(lo, s_i, t_i)
            new.append(a_r[pr] * t_r - a_i[pr] * t_i + ehi_r)
            new.append(a_r[pr] * t_i + a_i[pr] * t_r + ehi_i)
        return tuple(new)

    zero = jnp.zeros((8, N2), F32)
    lax.fori_loop(0, nt, scan_body, tuple(zero for _ in range(2 * _S5_NP)))

    for pr in range(_S5_NP):
        pt = p_scr[pr].T.astype(BF16)
        for k in range(2):
            gi = 2 * pr + k
            pk = jnp.concatenate([pt[k * NS:(k + 1) * NS],
                                  pt[(2 + k) * NS:(3 + k) * NS]], axis=0)
            yt = (jnp.dot(mt_ref[gi], ut_scr[gi], preferred_element_type=F32) +
                  jnp.dot(vt_ref[gi], pk, preferred_element_type=F32))
            for tau in range(S5_SUB):
                yt_scr[tau, gi * P:(gi + 1) * P, :] = yt[tau * P:(tau + 1) * P, :]

    @pl.when(step >= 2)
    def _():
        for tau in range(S5_SUB):
            out_copy(step - 2, slot, tau).wait()

    def store_body(tau, carry):
        ybuf[slot, tau] = yt_scr[tau].T
        return carry

    lax.fori_loop(0, S5_SUB, store_body, 0, unroll=4)

    for tau in range(S5_SUB):
        out_copy(step, slot, tau).start()

    @pl.when(step == nstep - 1)
    def _():
        if _S5_NSTEP > 1:
            for tau in range(S5_SUB):
                out_copy(step - 1, 1 - slot, tau).wait()
        for tau in range(S5_SUB):
            out_copy(step, slot, tau).wait()


def _s5(u_rows, mt, wt, vt, a16r, a16i):
    G, N, R = S5_GROUPS, S5_STATE, S5_ROW
    gb = _S5_GB
    vec = pl.BlockSpec((gb, 1, 2 * N), lambda i: (i, 0, 0))
    mat = lambda r, c: pl.BlockSpec((gb, r, c), lambda i: (i, 0, 0))
    hbm = pl.BlockSpec(memory_space=pl.ANY)
    plane = pltpu.VMEM((2, S5_SUB, S5_COLS, LANES), F32)
    return pl.pallas_call(
        _s5_kernel,
        grid=(_S5_NSTEP,),
        in_specs=[vec, vec, hbm, mat(R, R), mat(2 * N, R), mat(R, 2 * N)],
        out_specs=hbm,
        out_shape=jax.ShapeDtypeStruct((D_S5 // LANES, S5_COLS, S5_SUB, LANES), F32),
        scratch_shapes=[plane, plane,
                        pltpu.SemaphoreType.DMA((2,)), pltpu.SemaphoreType.DMA((2,)),
                        pltpu.VMEM((gb, R, S5_COLS), BF16),
                        pltpu.VMEM((S5_SUB, LANES, S5_COLS), F32),
                        pltpu.VMEM((_S5_NP, S5_COLS, 4 * N), F32),
                        pltpu.VMEM((_S5_NP, S5_COLS, 4 * N), F32)],
        compiler_params=pltpu.CompilerParams(
            dimension_semantics=("arbitrary",), vmem_limit_bytes=VMEM_LIMIT),
        name="s5",
    )(a16r, a16i, u_rows, mt, wt, vt)


_MIX_TM = 512
_MIX_SUB = 256
_MIX_NS = _MIX_TM // S5_SUB


def _mix_kernel(ssm_ref, ret_ref, x_ref, g1_ref, sc_ref, sh_ref, nw_ref, wglu_ref, bglu_ref,
                wout_ref, o_ref, h_ref):
    gain = nw_ref[...] * (1.0 + sc_ref[...])
    nsub = _MIX_TM // _MIX_SUB
    ns = _MIX_SUB // S5_SUB
    gated = []
    for r in range(nsub):
        s = jnp.concatenate([ssm_ref[k, r * ns:(r + 1) * ns].reshape(_MIX_SUB, LANES)
                             for k in range(D_S5 // LANES)], axis=1)
        cdf = 0.5 * (1.0 + jnp.tanh(math.sqrt(2.0 / math.pi) * (s + 0.044715 * (s * s * s))))
        sg = s * cdf
        z = jnp.dot(sg.astype(BF16), wglu_ref[...], preferred_element_type=F32) + bglu_ref[...]
        gated.append((sg * jax.nn.sigmoid(z)).astype(BF16))
    for r in range(nsub):
        rs = slice(r * _MIX_SUB, (r + 1) * _MIX_SUB)
        both = jnp.concatenate([ret_ref[hh, rs, :] for hh in range(RET_HEADS)] + [gated[r]],
                               axis=1)
        mix = jnp.dot(both, wout_ref[...], preferred_element_type=F32)
        x1 = x_ref[rs, :] + g1_ref[...] * mix
        o_ref[rs, :] = x1
        ms = jnp.mean(x1 * x1, axis=-1, keepdims=True)
        h_ref[rs, :] = (x1 * lax.rsqrt(ms + EPS) * gain + sh_ref[...]).astype(BF16)


def _mix(ssm_sb, ret_out, x2d, mod, norm_w, w_glu_bf, b_glu, w_out_bf):
    tm = _MIX_TM
    per_b = SEQ // tm
    rows = pl.BlockSpec((tm, D_MODEL), lambda i: (i, 0))
    return pl.pallas_call(
        _mix_kernel,
        grid=(TOK // tm,),
        in_specs=[pl.BlockSpec((D_S5 // LANES, _MIX_NS, None, S5_SUB, LANES),
                               lambda i: (0, i % per_b, i // per_b, 0, 0)),
                  pl.BlockSpec((RET_HEADS, tm, RET_HEAD_DIM), lambda i: (0, i, 0)),
                  rows, _mod_spec(_G1, tm), _mod_spec(_SC2, tm), _mod_spec(_SH2, tm),
                  pl.BlockSpec((1, D_MODEL), lambda i: (0, 0)),
                  pl.BlockSpec((D_S5, D_S5), lambda i: (0, 0)),
                  pl.BlockSpec((1, D_S5), lambda i: (0, 0)),
                  pl.BlockSpec((D_RET + D_S5, D_MODEL), lambda i: (0, 0))],
        out_specs=[rows, rows],
        out_shape=[jax.ShapeDtypeStruct((TOK, D_MODEL), F32),
                   jax.ShapeDtypeStruct((TOK, D_MODEL), BF16)],
        compiler_params=pltpu.CompilerParams(
            dimension_semantics=("parallel",), vmem_limit_bytes=VMEM_LIMIT),
        name="mix",
    )(ssm_sb, ret_out, x2d, mod, mod, mod, norm_w, w_glu_bf, b_glu, w_out_bf)


_FFN_UP_TM = 2048
_FFN_UP_SUB = 1024
_FFN_UP_TF = 512
_FFN_DOWN_TM = 512
_FFN_DOWN_SUB = 256


def _ffn_up_kernel(h_ref, wg_ref, wu_ref, wd_ref, o_ref, wd_bf_ref, wg_scr, wu_scr):
    @pl.when(pl.program_id(1) == 0)
    def _():
        wg_scr[...] = wg_ref[...].astype(BF16)
        wu_scr[...] = wu_ref[...].astype(BF16)

    wd_bf_ref[...] = wd_ref[...].astype(BF16)
    for r in range(_FFN_UP_TM // _FFN_UP_SUB):
        rs = slice(r * _FFN_UP_SUB, (r + 1) * _FFN_UP_SUB)
        h = h_ref[rs, :]
        gate = jnp.dot(h, wg_scr[...], preferred_element_type=F32)
        up = jnp.dot(h, wu_scr[...], preferred_element_type=F32)
        o_ref[rs, :] = (_silu(gate) * up).astype(BF16)


def _ffn_up(h2, w_gate_up, w_down):
    tm, tf = _FFN_UP_TM, _FFN_UP_TF
    nf = D_FF // tf
    nm = TOK // tm
    slab = D_FF // (nf * nm)
    assert slab * nf * nm == D_FF and slab % 16 == 0
    return pl.pallas_call(
        _ffn_up_kernel,
        grid=(nf, nm),
        in_specs=[pl.BlockSpec((tm, D_MODEL), lambda f, i: (i, 0)),
                  pl.BlockSpec((D_MODEL, tf), lambda f, i: (0, f)),
                  pl.BlockSpec((D_MODEL, tf), lambda f, i: (0, nf + f)),
                  pl.BlockSpec((slab, D_MODEL), lambda f, i: (f * nm + i, 0))],
        out_specs=[pl.BlockSpec((tm, tf), lambda f, i: (i, f)),
                   pl.BlockSpec((slab, D_MODEL), lambda f, i: (f * nm + i, 0))],
        out_shape=[jax.ShapeDtypeStruct((TOK, D_FF), BF16),
                   jax.ShapeDtypeStruct((D_FF, D_MODEL), BF16)],
        scratch_shapes=[pltpu.VMEM((D_MODEL, tf), BF16),
                        pltpu.VMEM((D_MODEL, tf), BF16)],
        compiler_params=pltpu.CompilerParams(
            dimension_semantics=("arbitrary", "arbitrary"), vmem_limit_bytes=VMEM_LIMIT),
        name="ffn_up",
    )(h2, w_gate_up, w_gate_up, w_down)


def _ffn_down_kernel(a_ref, x_ref, g2_ref, fw_ref, wd_ref, o_ref):
    for r in range(_FFN_DOWN_TM // _FFN_DOWN_SUB):
        rs = slice(r * _FFN_DOWN_SUB, (r + 1) * _FFN_DOWN_SUB)
        down = jnp.dot(a_ref[rs, :], wd_ref[...], preferred_element_type=F32)
        x2 = x_ref[rs, :] + g2_ref[...] * down
        ms = jnp.mean(x2 * x2, axis=-1, keepdims=True)
        o_ref[rs, :] = x2 * lax.rsqrt(ms + EPS) * fw_ref[...]


def _ffn_down(act, x1, mod, final_w, w_down_bf):
    tm = _FFN_DOWN_TM
    rows = pl.BlockSpec((tm, D_MODEL), lambda i: (i, 0))
    return pl.pallas_call(
        _ffn_down_kernel,
        grid=(TOK // tm,),
        in_specs=[pl.BlockSpec((tm, D_FF), lambda i: (i, 0)),
                  rows,
                  _mod_spec(_G2, tm),
                  pl.BlockSpec((1, D_MODEL), lambda i: (0, 0)),
                  pl.BlockSpec((D_FF, D_MODEL), lambda i: (0, 0))],
        out_specs=rows,
        out_shape=jax.ShapeDtypeStruct((TOK, D_MODEL), F32),
        compiler_params=pltpu.CompilerParams(
            dimension_semantics=("parallel",), vmem_limit_bytes=VMEM_LIMIT),
        name="ffn_down",
    )(act, x1, mod, final_w, w_down_bf)


def kernel(x, c, w_ada, b_ada, norm1_w, w_in, ret_norm_w, s5_a_re, s5_a_im, s5_log_step,
           s5_b_re, s5_b_im, s5_c_re, s5_c_im, s5_d, w_glu, b_glu, w_out, norm2_w,
           w_gate_up, w_down, final_norm_w):
    x2d = x.reshape(TOK, D_MODEL)
    layer = 0
    mod, mt, wt, vt, a16r, a16i, w_in_bf = _prep(
        c, w_ada[layer], b_ada[layer],
        s5_a_re[layer], s5_a_im[layer], s5_log_step[layer], s5_b_re[layer], s5_b_im[layer],
        s5_c_re[layer], s5_c_im[layer], s5_d[layer], w_in[layer])

    qkvg, u_sb, w_out_bf, w_glu_bf = _inproj(x2d, mod, norm1_w[layer].reshape(1, D_MODEL),
                                             w_in_bf, w_out[layer], w_glu[layer])
    ret_out = _retention(qkvg, ret_norm_w[layer].reshape(1, D_RET))
    y_rows = _s5(u_sb.reshape(D_S5 // LANES, S5_COLS, S5_SUB, LANES), mt, wt, vt, a16r, a16i)
    ssm_sb = y_rows.reshape(D_S5 // LANES, S5_NSUB, BATCH, S5_SUB, LANES)

    x1, h2 = _mix(ssm_sb, ret_out, x2d, mod, norm2_w[layer].reshape(1, D_MODEL),
                  w_glu_bf, b_glu[layer].reshape(1, D_S5), w_out_bf)
    act, w_down_bf = _ffn_up(h2, w_gate_up[layer], w_down[layer])
    out = _ffn_down(act, x1, mod, final_norm_w.reshape(1, D_MODEL), w_down_bf)
    return out.reshape(BATCH, SEQ, D_MODEL)
```

```python
import math

import numpy as np
import jax
import jax.numpy as jnp
from jax import lax
from jax.experimental import pallas as pl
from jax.experimental.pallas import tpu as pltpu

D_MODEL = 2048
BATCH = 4
SEQ = 2048
TOK = BATCH * SEQ
D_RET = 1024
D_S5 = 1024
RET_HEADS = 4
RET_HEAD_DIM = 256
RET_CHUNK = 128
S5_GROUP = 16
S5_GROUPS = 64
S5_STATE = 64
S5_SUB = 16
S5_ROW = S5_SUB * S5_GROUP
S5_NSUB = SEQ // S5_SUB
S5_COLS = S5_NSUB * BATCH
D_FF = 5632
ROPE_BASE = 10000.0
EPS = 1e-6
LANES = 128

F32 = jnp.float32
BF16 = jnp.bfloat16
HI = lax.Precision.HIGHEST
VMEM_LIMIT = 58 * 1024 * 1024


def _silu(v):
    return v * jax.nn.sigmoid(v)


_SH1, _SC1, _G1, _SH2, _SC2, _G2 = range(6)


def _mod_spec(which, rows_per_step):
    per_b = SEQ // rows_per_step
    return pl.BlockSpec((None, 1, D_MODEL), lambda i: (i // per_b, 0, which))


_INPROJ_TM = 512
_INPROJ_SUB = 256
_INPROJ_NS = _INPROJ_SUB // S5_SUB


_INPROJ_WT = 512
_INPROJ_NW = (4 * D_RET + D_S5) // _INPROJ_WT


def _inproj_kernel(x_ref, sc_ref, sh_ref, nw_ref, w_ref, wout_ref, wglu_ref,
                   o_ref, u_ref, wout_bf_ref, wglu_bf_ref, w_scr, h_scr, cos_scr, sin_scr):
    step = pl.program_id(0)

    @pl.when(step < _INPROJ_NW)
    def _():
        w_scr[step] = w_ref[...].astype(BF16)

    @pl.when(step >= _INPROJ_NW)
    def _():
        _inproj_rows(step - _INPROJ_NW, x_ref, sc_ref, sh_ref, nw_ref, wout_ref, wglu_ref,
                     o_ref, u_ref, wout_bf_ref, wglu_bf_ref, w_scr, h_scr, cos_scr, sin_scr)


def _inproj_rows(i, x_ref, sc_ref, sh_ref, nw_ref, wout_ref, wglu_ref,
                 o_ref, u_ref, wout_bf_ref, wglu_bf_ref, w_scr, h_scr, cos_scr, sin_scr):
    wout_bf_ref[...] = wout_ref[...].astype(BF16)
    wglu_bf_ref[...] = wglu_ref[...].astype(BF16)
    tm = _INPROJ_TM
    half = RET_HEAD_DIM // 2
    per_b = SEQ // tm
    hpt = _INPROJ_WT // RET_HEAD_DIM
    slot = i % per_b

    @pl.when(i < per_b)
    def _():
        pos = (lax.broadcasted_iota(jnp.int32, (tm, half), 0) + slot * tm).astype(F32)
        lane = lax.broadcasted_iota(jnp.int32, (tm, half), 1).astype(F32)
        ang = pos * jnp.exp(lane * (-math.log(ROPE_BASE) / half))
        cos_scr[slot] = jnp.cos(ang)
        sin_scr[slot] = jnp.sin(ang)

    gain = nw_ref[...] * (1.0 + sc_ref[...])
    shift = sh_ref[...]
    for r in range(tm // _INPROJ_SUB):
        rs = slice(r * _INPROJ_SUB, (r + 1) * _INPROJ_SUB)
        x = x_ref[rs, :]
        ms = jnp.mean(x * x, axis=-1, keepdims=True)
        h_scr[rs, :] = (x * lax.rsqrt(ms + EPS) * gain + shift).astype(BF16)
        cs = cos_scr[slot, rs, :]
        sn = sin_scr[slot, rs, :]
        for t in range(_INPROJ_NW):
            acc = jnp.dot(h_scr[rs, :], w_scr[t], preferred_element_type=F32)
            if t < 2 * RET_HEADS // hpt:
                for hh in range(hpt):
                    c0 = hh * RET_HEAD_DIM
                    x1 = acc[:, c0:c0 + half]
                    x2 = acc[:, c0 + half:c0 + 2 * half]
                    o_ref[t * hpt + hh, rs, 0:half] = (x1 * cs - x2 * sn).astype(BF16)
                    o_ref[t * hpt + hh, rs, half:2 * half] = (x2 * cs + x1 * sn).astype(BF16)
            elif t < 4 * RET_HEADS // hpt:
                for hh in range(hpt):
                    c0 = hh * RET_HEAD_DIM
                    o_ref[t * hpt + hh, rs, :] = acc[:, c0:c0 + RET_HEAD_DIM].astype(BF16)
            else:
                k0 = (t - 4 * RET_HEADS // hpt) * (_INPROJ_WT // LANES)
                for k in range(_INPROJ_WT // LANES):
                    u_ref[k0 + k, r * _INPROJ_NS:(r + 1) * _INPROJ_NS] = \
                        acc[:, k * LANES:(k + 1) * LANES].reshape(_INPROJ_NS, S5_SUB, LANES)


def _inproj(x2d, mod, norm_w, w_in, w_out, w_glu):
    tm = _INPROJ_TM
    per_b = SEQ // tm
    half = RET_HEAD_DIM // 2
    nw = _INPROJ_NW
    steps = TOK // tm
    so = w_out.shape[0] // steps
    sg = w_glu.shape[0] // steps
    assert w_in.shape == (D_MODEL, 4 * D_RET + D_S5) and D_S5 == D_RET
    assert so * steps == w_out.shape[0] and sg * steps == w_glu.shape[0] and sg % 16 == 0
    row = lambda s: jnp.maximum(s - nw, 0)
    slab_o = pl.BlockSpec((so, w_out.shape[1]), lambda s: (row(s), 0))
    slab_g = pl.BlockSpec((sg, w_glu.shape[1]), lambda s: (row(s), 0))
    mod_spec = lambda which: pl.BlockSpec((None, 1, D_MODEL),
                                          lambda s: (row(s) // per_b, 0, which))
    return pl.pallas_call(
        _inproj_kernel,
        grid=(nw + steps,),
        in_specs=[pl.BlockSpec((tm, D_MODEL), lambda s: (row(s), 0)),
                  mod_spec(_SC1), mod_spec(_SH1),
                  pl.BlockSpec((1, D_MODEL), lambda s: (0, 0)),
                  pl.BlockSpec((D_MODEL, _INPROJ_WT), lambda s: (0, jnp.minimum(s, nw - 1))),
                  slab_o, slab_g],
        out_specs=[pl.BlockSpec((4 * RET_HEADS, tm, RET_HEAD_DIM), lambda s: (0, row(s), 0)),
                   pl.BlockSpec((D_S5 // LANES, tm // S5_SUB, None, S5_SUB, LANES),
                                lambda s: (0, row(s) % per_b, row(s) // per_b, 0, 0)),
                   slab_o, slab_g],
        out_shape=[jax.ShapeDtypeStruct((4 * RET_HEADS, TOK, RET_HEAD_DIM), BF16),
                   jax.ShapeDtypeStruct((D_S5 // LANES, S5_NSUB, BATCH, S5_SUB, LANES), F32),
                   jax.ShapeDtypeStruct(w_out.shape, BF16),
                   jax.ShapeDtypeStruct(w_glu.shape, BF16)],
        scratch_shapes=[pltpu.VMEM((nw, D_MODEL, _INPROJ_WT), BF16),
                        pltpu.VMEM((tm, D_MODEL), BF16),
                        pltpu.VMEM((per_b, tm, half), F32),
                        pltpu.VMEM((per_b, tm, half), F32)],
        compiler_params=pltpu.CompilerParams(
            dimension_semantics=("arbitrary",), vmem_limit_bytes=VMEM_LIMIT),
        name="inproj",
    )(x2d, mod, mod, norm_w, w_in, w_out, w_glu)


_RET_BLK = 256


def _ret_kernel(lg_ref, q_ref, k_ref, v_ref, g_ref, w_ref, o_ref,
                kv_scr, prev_scr, intra_scr, kdec_scr, qdec_scr):
    C = _RET_BLK
    dh = RET_HEAD_DIM
    nc = SEQ // C
    scale = dh ** -0.5
    lg = lg_ref[0:1, :]
    ii = lax.broadcasted_iota(jnp.int32, (C, C), 0)
    jj = lax.broadcasted_iota(jnp.int32, (C, C), 1)
    diff = (ii - jj).astype(F32)
    intra_scr[...] = jnp.where(diff >= 0.0, jnp.exp(lg * jnp.maximum(diff, 0.0)), 0.0) * scale
    row = lax.broadcasted_iota(jnp.int32, (C, dh), 0).astype(F32)
    kdec_scr[...] = jnp.exp(lg * (C - 1.0 - row)) * scale
    qdec_scr[...] = jnp.exp(lg * (row + 1.0))
    block_decay = jnp.exp(lg * float(C))
    gn_w = w_ref[...]

    for n in range(nc):
        rs = slice(n * C, (n + 1) * C)
        kd = (k_ref[rs, :].astype(F32) * kdec_scr[...]).astype(BF16)
        kv_scr[n] = lax.dot_general(kd, v_ref[rs, :], (((0,), (0,)), ((), ())),
                                    preferred_element_type=F32)

    band = 64
    for rb in range(dh // band):
        bs = slice(rb * band, (rb + 1) * band)
        st = jnp.zeros((band, dh), F32)
        for n in range(nc):
            prev_scr[n, bs, :] = st.astype(BF16)
            if n + 1 < nc:
                st = st * block_decay + kv_scr[n, bs, :]

    for n in range(nc):
        rs = slice(n * C, (n + 1) * C)
        q = q_ref[rs, :]
        s = lax.dot_general(q, k_ref[rs, :], (((1,), (1,)), ((), ())),
                            preferred_element_type=F32) * intra_scr[...]
        y = jnp.dot(s.astype(BF16), v_ref[rs, :], preferred_element_type=F32)
        y = y + jnp.dot(q, prev_scr[n], preferred_element_type=F32) * qdec_scr[...]
        mu = jnp.mean(y, axis=-1, keepdims=True)
        yc = y - mu
        var = jnp.mean(yc * yc, axis=-1, keepdims=True)
        yn = yc * lax.rsqrt(var + EPS) * gn_w
        g = g_ref[rs, :].astype(F32)
        o_ref[rs, :] = (_silu(g) * yn).astype(BF16)


def _retention(qkvg, ret_norm_w):
    dh = RET_HEAD_DIM
    nc = SEQ // _RET_BLK
    lg = np.log1p(-np.exp2(-5.0 - np.arange(RET_HEADS, dtype=np.float64)))
    lg_tab = jnp.asarray(np.broadcast_to(lg[:, None, None], (RET_HEADS, 8, dh)), F32)
    spec = lambda off: pl.BlockSpec((None, SEQ, dh), lambda b, h: (off + h, b, 0))
    return pl.pallas_call(
        _ret_kernel,
        grid=(BATCH, RET_HEADS),
        in_specs=[pl.BlockSpec((None, 8, dh), lambda b, h: (h, 0, 0)),
                  spec(0), spec(RET_HEADS), spec(2 * RET_HEADS), spec(3 * RET_HEADS),
                  pl.BlockSpec((1, dh), lambda b, h: (0, h))],
        out_specs=pl.BlockSpec((None, SEQ, dh), lambda b, h: (h, b, 0)),
        out_shape=jax.ShapeDtypeStruct((RET_HEADS, TOK, dh), BF16),
        scratch_shapes=[pltpu.VMEM((nc, dh, dh), F32),
                        pltpu.VMEM((nc, dh, dh), BF16),
                        pltpu.VMEM((_RET_BLK, _RET_BLK), F32),
                        pltpu.VMEM((_RET_BLK, dh), F32),
                        pltpu.VMEM((_RET_BLK, dh), F32)],
        compiler_params=pltpu.CompilerParams(
            dimension_semantics=("parallel", "parallel"), vmem_limit_bytes=VMEM_LIMIT),
        name="retention",
    )(lg_tab, qkvg, qkvg, qkvg, qkvg, ret_norm_w)


_S5GEN_GG = 4


def _prep_kernel(c_ref, wada_ref, bada_ref,
                 ls_ref, ar_ref, ai_ref, bt_ref, cr_ref, ci_ref, cc_ref, d_ref,
                 mod_ref, mt_ref, wt_ref, vt_ref, a16r_ref, a16i_ref):
    cond = _silu(c_ref[...])
    mod = jnp.dot(cond.astype(BF16), wada_ref[...].astype(BF16),
                  preferred_element_type=F32) + bada_ref[...]
    for b in range(BATCH):
        mod_ref[b] = mod[b:b + 1, :]
    N, P, R = S5_STATE, S5_GROUP, S5_ROW
    re_half = lax.broadcasted_iota(jnp.int32, (P, 2 * N), 1) < N
    sub_e = lax.broadcasted_iota(jnp.int32, (P, 2 * N), 0)
    lane_p = lax.broadcasted_iota(jnp.int32, (P, R), 1)
    sub_p = lax.broadcasted_iota(jnp.int32, (P, R), 0)
    lag0_diag = lane_p == (R - P) + sub_p

    def hdot(a, b):
        return jnp.dot(a, b, precision=HI, preferred_element_type=F32)

    def cmul(xr, xi, yr, yi):
        return xr * yr - xi * yi, xr * yi + xi * yr

    for k in range(_S5GEN_GG):
        dt = jnp.exp(ls_ref[k])
        ar = ar_ref[k]
        ai = ai_ref[k]
        mag = jnp.exp(ar * dt)
        lbr = mag * jnp.cos(ai * dt)
        lbi = mag * jnp.sin(ai * dt)
        den = ar * ar + ai * ai
        zr = ((lbr - 1.0) * ar + lbi * ai) / den
        zi = (lbi * ar - (lbr - 1.0) * ai) / den

        pw_r = jnp.ones((P, 2 * N), F32)
        pw_i = jnp.zeros((P, 2 * N), F32)
        sq_r, sq_i = lbr, lbi
        for bit in range(4):
            nr, ni = cmul(pw_r, pw_i, sq_r, sq_i)
            take = (lax.shift_right_logical(sub_e, bit) & 1) == 1
            pw_r = jnp.where(take, nr, pw_r)
            pw_i = jnp.where(take, ni, pw_i)
            if bit < 3:
                sq_r, sq_i = cmul(sq_r, sq_i, sq_r, sq_i)
        p1_r, p1_i = cmul(pw_r, pw_i, lbr, lbi)
        a16r_ref[k] = p1_r[S5_SUB - 1:S5_SUB, :]
        a16i_ref[k] = p1_i[S5_SUB - 1:S5_SUB, :]

        bt = bt_ref[k]
        x = zr * bt + jnp.where(re_half[0:1], -zi, zi) * pltpu.roll(bt, N, axis=1)
        xs = pltpu.roll(x, N, axis=1)
        pw_is = jnp.where(re_half, -pw_i, pw_i)
        w_rows = jnp.concatenate(
            [pw_r[e:e + 1] * x + pw_is[e:e + 1] * xs for e in range(S5_SUB - 1, -1, -1)], axis=0)
        w_t = w_rows.T
        w_r = w_t[0:N]
        w_i = w_t[N:2 * N]
        wt_ref[k] = w_t.astype(BF16)

        c_re = cr_ref[k]
        c_im = ci_ref[k]
        krev = hdot(c_re, w_r) - hdot(c_im, w_i)
        krev = krev + jnp.where(lag0_diag, d_ref[k], 0.0)
        for tp in range(S5_SUB):
            width = (tp + 1) * P
            piece = krev if width == R else jnp.where(
                lane_p < width, pltpu.roll(krev, width, axis=1), 0.0)
            mt_ref[k, tp * P:(tp + 1) * P, :] = piece.astype(BF16)

        cc = cc_ref[k]
        ccs = pltpu.roll(cc, N, axis=1)
        p1_rs = jnp.where(re_half, p1_r, -p1_r)
        for tp in range(S5_SUB):
            val = cc * p1_rs[tp:tp + 1] - ccs * p1_i[tp:tp + 1]
            vt_ref[k, tp * P:(tp + 1) * P, :] = val.astype(BF16)


def _prep(c, w_ada, b_ada, a_re, a_im, log_step, b_re, b_im, c_re, c_im, d_skip):
    G, N, P, R = S5_GROUPS, S5_STATE, S5_GROUP, S5_ROW
    gg = _S5GEN_GG
    steps = G // gg
    n_mod = w_ada.shape[1]
    tn = n_mod // steps
    assert tn * steps == n_mod and tn % LANES == 0
    dup = lambda a: jnp.tile(a, (1, 2)).reshape(G, 1, 2 * N)
    bt = jnp.concatenate([jnp.swapaxes(b_re, 1, 2), jnp.swapaxes(b_im, 1, 2)], axis=-1)
    cc = jnp.concatenate([c_re, c_im], axis=-1)
    blk = lambda s: pl.BlockSpec((gg,) + s, lambda g: (g, 0, 0))
    return pl.pallas_call(
        _prep_kernel,
        grid=(steps,),
        in_specs=[pl.BlockSpec((BATCH, D_MODEL), lambda g: (0, 0)),
                  pl.BlockSpec((D_MODEL, tn), lambda g: (0, g)),
                  pl.BlockSpec((1, tn), lambda g: (0, g)),
                  blk((1, 1)), blk((1, 2 * N)), blk((1, 2 * N)), blk((P, 2 * N)),
                  blk((P, N)), blk((P, N)), blk((P, 2 * N)), blk((P, 1))],
        out_specs=[pl.BlockSpec((BATCH, 1, tn), lambda g: (0, 0, g)),
                   blk((R, R)), blk((2 * N, R)), blk((R, 2 * N)), blk((1, 2 * N)), blk((1, 2 * N))],
        out_shape=[jax.ShapeDtypeStruct((BATCH, 1, n_mod), F32),
                   jax.ShapeDtypeStruct((G, R, R), BF16),
                   jax.ShapeDtypeStruct((G, 2 * N, R), BF16),
                   jax.ShapeDtypeStruct((G, R, 2 * N), BF16)] +
                  [jax.ShapeDtypeStruct((G, 1, 2 * N), F32)] * 2,
        compiler_params=pltpu.CompilerParams(
            dimension_semantics=("parallel",), vmem_limit_bytes=VMEM_LIMIT),
        name="prep",
    )(c, w_ada, b_ada.reshape(1, n_mod),
      log_step.reshape(G, 1, 1), dup(a_re), dup(a_im), bt, c_re, c_im, cc,
      d_skip.reshape(G, P, 1))


_S5_GB = LANES // S5_GROUP
_S5_NP = _S5_GB // 2
_S5_NSTEP = S5_GROUPS // _S5_GB


def _s5_kernel(a16r_ref, a16i_ref, u_hbm, mt_ref, wt_ref, vt_ref, y_hbm,
               ubuf, ybuf, in_sem, out_sem, ut_scr, yt_scr, e_scr, p_scr):
    P = S5_GROUP
    N2 = 2 * S5_STATE
    cols = S5_COLS
    nblk = cols // LANES
    nt = cols // 8
    step = pl.program_id(0)
    nstep = pl.num_programs(0)
    slot = step % 2

    def in_copy(slab, sl, tau):
        return pltpu.make_async_copy(u_hbm.at[slab, :, tau, :], ubuf.at[sl, tau], in_sem.at[sl])

    def out_copy(slab, sl, tau):
        return pltpu.make_async_copy(ybuf.at[sl, tau], y_hbm.at[slab, :, tau, :], out_sem.at[sl])

    @pl.when(step == 0)
    def _():
        for tau in range(S5_SUB):
            in_copy(0, 0, tau).start()

    @pl.when(step + 1 < nstep)
    def _():
        for tau in range(S5_SUB):
            in_copy(step + 1, 1 - slot, tau).start()

    for tau in range(S5_SUB):
        in_copy(step, slot, tau).wait()

    def load_body(tau, carry):
        for c in range(nblk):
            xt = ubuf[slot, tau, c * LANES:(c + 1) * LANES, :].T
            r0 = pl.multiple_of(tau * P, P)
            for k in range(_S5_GB):
                ut_scr[k, pl.ds(r0, P), c * LANES:(c + 1) * LANES] = \
                    xt[k * P:(k + 1) * P, :].astype(BF16)
        return carry

    lax.fori_loop(0, S5_SUB, load_body, 0, unroll=4)

    NS = S5_STATE
    for pr in range(_S5_NP):
        e0 = jnp.dot(wt_ref[2 * pr], ut_scr[2 * pr], preferred_element_type=F32)
        e1 = jnp.dot(wt_ref[2 * pr + 1], ut_scr[2 * pr + 1], preferred_element_type=F32)
        et = jnp.concatenate([e0[0:NS], e1[0:NS], e0[NS:2 * NS], e1[NS:2 * NS]], axis=0)
        e_scr[pr] = et.T

    sub = lax.broadcasted_iota(jnp.int32, (8, N2), 0)
    lo = sub < BATCH
    first = lax.broadcasted_iota(jnp.int32, (1, N2), 1) < S5_STATE
    a_r, a_i = [], []
    for pr in range(_S5_NP):
        a_r.append(jnp.broadcast_to(
            jnp.where(first, a16r_ref[2 * pr], a16r_ref[2 * pr + 1]), (8, N2)))
        a_i.append(jnp.broadcast_to(
            jnp.where(first, a16i_ref[2 * pr], a16i_ref[2 * pr + 1]), (8, N2)))

    def scan_body(t, carry):
        r0 = pl.multiple_of(t * 8, 8)
        new = []
        for pr in range(_S5_NP):
            s_r, s_i = carry[2 * pr], carry[2 * pr + 1]
            e_r = e_scr[pr, pl.ds(r0, 8), 0:N2]
            e_i = e_scr[pr, pl.ds(r0, 8), N2:2 * N2]
            x_r = pltpu.roll(e_r, BATCH, axis=0)
            x_i = pltpu.roll(e_i, BATCH, axis=0)
            elo_r = jnp.where(lo, e_r, x_r)
            elo_i = jnp.where(lo, e_i, x_i)
            ehi_r = jnp.where(lo, x_r, e_r)
            ehi_i = jnp.where(lo, x_i, e_i)
            t_r = a_r[pr] * s_r - a_i[pr] * s_i + elo_r
            t_i = a_r[pr] * s_i + a_i[pr] * s_r + elo_i
            p_scr[pr, pl.ds(r0, 8), 0:N2] = jnp.where(lo, s_r, t_r)
            p_scr[pr, pl.ds(r0, 8), N2:2 * N2] = jnp.where(lo, s_i, t_i)
            new.append(a_r[pr] * t_r - a_i[pr] * t_i + ehi_r)
            new.append(a_r[pr] * t_i + a_i[pr] * t_r + ehi_i)
        return tuple(new)

    zero = jnp.zeros((8, N2), F32)
    lax.fori_loop(0, nt, scan_body, tuple(zero for _ in range(2 * _S5_NP)))

    for pr in range(_S5_NP):
        pt = p_scr[pr].T.astype(BF16)
        for k in range(2):
            gi = 2 * pr + k
            pk = jnp.concatenate([pt[k * NS:(k + 1) * NS],
                                  pt[(2 + k) * NS:(3 + k) * NS]], axis=0)
            yt = (jnp.dot(mt_ref[gi], ut_scr[gi], preferred_element_type=F32) +
                  jnp.dot(vt_ref[gi], pk, preferred_element_type=F32))
            for tau in range(S5_SUB):
                yt_scr[tau, gi * P:(gi + 1) * P, :] = yt[tau * P:(tau + 1) * P, :]

    @pl.when(step >= 2)
    def _():
        for tau in range(S5_SUB):
            out_copy(step - 2, slot, tau).wait()

    def store_body(tau, carry):
        ybuf[slot, tau] = yt_scr[tau].T
        return carry

    lax.fori_loop(0, S5_SUB, store_body, 0, unroll=4)

    for tau in range(S5_SUB):
        out_copy(step, slot, tau).start()

    @pl.when(step == nstep - 1)
    def _():
        if _S5_NSTEP > 1:
            for tau in range(S5_SUB):
                out_copy(step - 1, 1 - slot, tau).wait()
        for tau in range(S5_SUB):
            out_copy(step, slot, tau).wait()


def _s5(u_rows, mt, wt, vt, a16r, a16i):
    G, N, R = S5_GROUPS, S5_STATE, S5_ROW
    gb = _S5_GB
    vec = pl.BlockSpec((gb, 1, 2 * N), lambda i: (i, 0, 0))
    mat = lambda r, c: pl.BlockSpec((gb, r, c), lambda i: (i, 0, 0))
    hbm = pl.BlockSpec(memory_space=pl.ANY)
    plane = pltpu.VMEM((2, S5_SUB, S5_COLS, LANES), F32)
    return pl.pallas_call(
        _s5_kernel,
        grid=(_S5_NSTEP,),
        in_specs=[vec, vec, hbm, mat(R, R), mat(2 * N, R), mat(R, 2 * N)],
        out_specs=hbm,
        out_shape=jax.ShapeDtypeStruct((D_S5 // LANES, S5_COLS, S5_SUB, LANES), F32),
        scratch_shapes=[plane, plane,
                        pltpu.SemaphoreType.DMA((2,)), pltpu.SemaphoreType.DMA((2,)),
                        pltpu.VMEM((gb, R, S5_COLS), BF16),
                        pltpu.VMEM((S5_SUB, LANES, S5_COLS), F32),
                        pltpu.VMEM((_S5_NP, S5_COLS, 4 * N), F32),
                        pltpu.VMEM((_S5_NP, S5_COLS, 4 * N), F32)],
        compiler_params=pltpu.CompilerParams(
            dimension_semantics=("arbitrary",), vmem_limit_bytes=VMEM_LIMIT),
        name="s5",
    )(a16r, a16i, u_rows, mt, wt, vt)


_MIX_TM = 512
_MIX_SUB = 256
_MIX_NS = _MIX_TM // S5_SUB


def _mix_kernel(ssm_ref, ret_ref, x_ref, g1_ref, sc_ref, sh_ref, nw_ref, wglu_ref, bglu_ref,
                wout_ref, o_ref, h_ref):
    gain = nw_ref[...] * (1.0 + sc_ref[...])
    nsub = _MIX_TM // _MIX_SUB
    ns = _MIX_SUB // S5_SUB
    gated = []
    for r in range(nsub):
        s = jnp.concatenate([ssm_ref[k, r * ns:(r + 1) * ns].reshape(_MIX_SUB, LANES)
                             for k in range(D_S5 // LANES)], axis=1)
        cdf = 0.5 * (1.0 + jnp.tanh(math.sqrt(2.0 / math.pi) * (s + 0.044715 * (s * s * s))))
        sg = s * cdf
        z = jnp.dot(sg.astype(BF16), wglu_ref[...], preferred_element_type=F32) + bglu_ref[...]
        gated.append((sg * jax.nn.sigmoid(z)).astype(BF16))
    for r in range(nsub):
        rs = slice(r * _MIX_SUB, (r + 1) * _MIX_SUB)
        both = jnp.concatenate([ret_ref[hh, rs, :] for hh in range(RET_HEADS)] + [gated[r]],
                               axis=1)
        mix = jnp.dot(both, wout_ref[...], preferred_element_type=F32)
        x1 = x_ref[rs, :] + g1_ref[...] * mix
        o_ref[rs, :] = x1
        ms = jnp.mean(x1 * x1, axis=-1, keepdims=True)
        h_ref[rs, :] = (x1 * lax.rsqrt(ms + EPS) * gain + sh_ref[...]).astype(BF16)


def _mix(ssm_sb, ret_out, x2d, mod, norm_w, w_glu_bf, b_glu, w_out_bf):
    tm = _MIX_TM
    per_b = SEQ // tm
    rows = pl.BlockSpec((tm, D_MODEL), lambda i: (i, 0))
    return pl.pallas_call(
        _mix_kernel,
        grid=(TOK // tm,),
        in_specs=[pl.BlockSpec((D_S5 // LANES, _MIX_NS, None, S5_SUB, LANES),
                               lambda i: (0, i % per_b, i // per_b, 0, 0)),
                  pl.BlockSpec((RET_HEADS, tm, RET_HEAD_DIM), lambda i: (0, i, 0)),
                  rows, _mod_spec(_G1, tm), _mod_spec(_SC2, tm), _mod_spec(_SH2, tm),
                  pl.BlockSpec((1, D_MODEL), lambda i: (0, 0)),
                  pl.BlockSpec((D_S5, D_S5), lambda i: (0, 0)),
                  pl.BlockSpec((1, D_S5), lambda i: (0, 0)),
                  pl.BlockSpec((D_RET + D_S5, D_MODEL), lambda i: (0, 0))],
        out_specs=[rows, rows],
        out_shape=[jax.ShapeDtypeStruct((TOK, D_MODEL), F32),
                   jax.ShapeDtypeStruct((TOK, D_MODEL), BF16)],
        compiler_params=pltpu.CompilerParams(
            dimension_semantics=("parallel",), vmem_limit_bytes=VMEM_LIMIT),
        name="mix",
    )(ssm_sb, ret_out, x2d, mod, mod, mod, norm_w, w_glu_bf, b_glu, w_out_bf)


_FFN_UP_TM = 2048
_FFN_UP_SUB = 1024
_FFN_UP_TF = 512
_FFN_DOWN_TM = 512
_FFN_DOWN_SUB = 256


def _ffn_up_kernel(h_ref, wg_ref, wu_ref, wd_ref, o_ref, wd_bf_ref, wg_scr, wu_scr):
    @pl.when(pl.program_id(1) == 0)
    def _():
        wg_scr[...] = wg_ref[...].astype(BF16)
        wu_scr[...] = wu_ref[...].astype(BF16)

    wd_bf_ref[...] = wd_ref[...].astype(BF16)
    for r in range(_FFN_UP_TM // _FFN_UP_SUB):
        rs = slice(r * _FFN_UP_SUB, (r + 1) * _FFN_UP_SUB)
        h = h_ref[rs, :]
        gate = jnp.dot(h, wg_scr[...], preferred_element_type=F32)
        up = jnp.dot(h, wu_scr[...], preferred_element_type=F32)
        o_ref[rs, :] = (_silu(gate) * up).astype(BF16)


def _ffn_up(h2, w_gate_up, w_down):
    tm, tf = _FFN_UP_TM, _FFN_UP_TF
    nf = D_FF // tf
    nm = TOK // tm
    slab = D_FF // (nf * nm)
    assert slab * nf * nm == D_FF and slab % 16 == 0
    return pl.pallas_call(
        _ffn_up_kernel,
        grid=(nf, nm),
        in_specs=[pl.BlockSpec((tm, D_MODEL), lambda f, i: (i, 0)),
                  pl.BlockSpec((D_MODEL, tf), lambda f, i: (0, f)),
                  pl.BlockSpec((D_MODEL, tf), lambda f, i: (0, nf + f)),
                  pl.BlockSpec((slab, D_MODEL), lambda f, i: (f * nm + i, 0))],
        out_specs=[pl.BlockSpec((tm, tf), lambda f, i: (i, f)),
                   pl.BlockSpec((slab, D_MODEL), lambda f, i: (f * nm + i, 0))],
        out_shape=[jax.ShapeDtypeStruct((TOK, D_FF), BF16),
                   jax.ShapeDtypeStruct((D_FF, D_MODEL), BF16)],
        scratch_shapes=[pltpu.VMEM((D_MODEL, tf), BF16),
                        pltpu.VMEM((D_MODEL, tf), BF16)],
        compiler_params=pltpu.CompilerParams(
            dimension_semantics=("arbitrary", "arbitrary"), vmem_limit_bytes=VMEM_LIMIT),
        name="ffn_up",
    )(h2, w_gate_up, w_gate_up, w_down)


def _ffn_down_kernel(a_ref, x_ref, g2_ref, fw_ref, wd_ref, o_ref):
    for r in range(_FFN_DOWN_TM // _FFN_DOWN_SUB):
        rs = slice(r * _FFN_DOWN_SUB, (r + 1) * _FFN_DOWN_SUB)
        down = jnp.dot(a_ref[rs, :], wd_ref[...], preferred_element_type=F32)
        x2 = x_ref[rs, :] + g2_ref[...] * down
        ms = jnp.mean(x2 * x2, axis=-1, keepdims=True)
        o_ref[rs, :] = x2 * lax.rsqrt(ms + EPS) * fw_ref[...]


def _ffn_down(act, x1, mod, final_w, w_down_bf):
    tm = _FFN_DOWN_TM
    rows = pl.BlockSpec((tm, D_MODEL), lambda i: (i, 0))
    return pl.pallas_call(
        _ffn_down_kernel,
        grid=(TOK // tm,),
        in_specs=[pl.BlockSpec((tm, D_FF), lambda i: (i, 0)),
                  rows,
                  _mod_spec(_G2, tm),
                  pl.BlockSpec((1, D_MODEL), lambda i: (0, 0)),
                  pl.BlockSpec((D_FF, D_MODEL), lambda i: (0, 0))],
        out_specs=rows,
        out_shape=jax.ShapeDtypeStruct((TOK, D_MODEL), F32),
        compiler_params=pltpu.CompilerParams(
            dimension_semantics=("parallel",), vmem_limit_bytes=VMEM_LIMIT),
        name="ffn_down",
    )(act, x1, mod, final_w, w_down_bf)


def kernel(x, c, w_ada, b_ada, norm1_w, w_in, ret_norm_w, s5_a_re, s5_a_im, s5_log_step,
           s5_b_re, s5_b_im, s5_c_re, s5_c_im, s5_d, w_glu, b_glu, w_out, norm2_w,
           w_gate_up, w_down, final_norm_w):
    x2d = x.reshape(TOK, D_MODEL)
    layer = 0
    mod, mt, wt, vt, a16r, a16i = _prep(
        c, w_ada[layer], b_ada[layer],
        s5_a_re[layer], s5_a_im[layer], s5_log_step[layer], s5_b_re[layer], s5_b_im[layer],
        s5_c_re[layer], s5_c_im[layer], s5_d[layer])

    qkvg, u_sb, w_out_bf, w_glu_bf = _inproj(x2d, mod, norm1_w[layer].reshape(1, D_MODEL),
                                             w_in[layer], w_out[layer], w_glu[layer])
    ret_out = _retention(qkvg, ret_norm_w[layer].reshape(1, D_RET))
    y_rows = _s5(u_sb.reshape(D_S5 // LANES, S5_COLS, S5_SUB, LANES), mt, wt, vt, a16r, a16i)
    ssm_sb = y_rows.reshape(D_S5 // LANES, S5_NSUB, BATCH, S5_SUB, LANES)

    x1, h2 = _mix(ssm_sb, ret_out, x2d, mod, norm2_w[layer].reshape(1, D_MODEL),
                  w_glu_bf, b_glu[layer].reshape(1, D_S5), w_out_bf)
    act, w_down_bf = _ffn_up(h2, w_gate_up[layer], w_down[layer])
    out = _ffn_down(act, x1, mod, final_norm_w.reshape(1, D_MODEL), w_down_bf)
    return out.reshape(BATCH, SEQ, D_MODEL)
```

```python
import math

import numpy as np
import jax
import jax.numpy as jnp
from jax import lax
from jax.experimental import pallas as pl
from jax.experimental.pallas import tpu as pltpu

D_MODEL = 2048
BATCH = 4
SEQ = 2048
TOK = BATCH * SEQ
D_RET = 1024
D_S5 = 1024
RET_HEADS = 4
RET_HEAD_DIM = 256
RET_CHUNK = 128
S5_GROUP = 16
S5_GROUPS = 64
S5_STATE = 64
S5_SUB = 16
S5_ROW = S5_SUB * S5_GROUP
S5_NSUB = SEQ // S5_SUB
S5_COLS = S5_NSUB * BATCH
D_FF = 5632
ROPE_BASE = 10000.0
EPS = 1e-6
LANES = 128

F32 = jnp.float32
BF16 = jnp.bfloat16
HI = lax.Precision.HIGHEST
VMEM_LIMIT = 58 * 1024 * 1024


def _silu(v):
    return v * jax.nn.sigmoid(v)


_SH1, _SC1, _G1, _SH2, _SC2, _G2 = range(6)


def _mod_spec(which, rows_per_step):
    per_b = SEQ // rows_per_step
    return pl.BlockSpec((None, 1, D_MODEL), lambda i: (i // per_b, 0, which))


_INPROJ_TM = 512
_INPROJ_SUB = 256
_INPROJ_NS = _INPROJ_SUB // S5_SUB


_INPROJ_WT = 512
_INPROJ_NW = (4 * D_RET + D_S5) // _INPROJ_WT


def _inproj_kernel(x_ref, sc_ref, sh_ref, nw_ref, w_ref, wout_ref, wglu_ref,
                   o_ref, u_ref, wout_bf_ref, wglu_bf_ref, w_scr, h_scr, cos_scr, sin_scr):
    step = pl.program_id(0)

    @pl.when(step < _INPROJ_NW)
    def _():
        w_scr[step] = w_ref[...].astype(BF16)

        @pl.when(step < SEQ // _INPROJ_TM)
        def _():
            tm = _INPROJ_TM
            half = RET_HEAD_DIM // 2
            pos = (lax.broadcasted_iota(jnp.int32, (tm, half), 0) + step * tm).astype(F32)
            lane = lax.broadcasted_iota(jnp.int32, (tm, half), 1).astype(F32)
            ang = pos * jnp.exp(lane * (-math.log(ROPE_BASE) / half))
            cos_scr[step] = jnp.cos(ang)
            sin_scr[step] = jnp.sin(ang)

    @pl.when(step >= _INPROJ_NW)
    def _():
        _inproj_rows(step - _INPROJ_NW, x_ref, sc_ref, sh_ref, nw_ref, wout_ref, wglu_ref,
                     o_ref, u_ref, wout_bf_ref, wglu_bf_ref, w_scr, h_scr, cos_scr, sin_scr)


def _inproj_rows(i, x_ref, sc_ref, sh_ref, nw_ref, wout_ref, wglu_ref,
                 o_ref, u_ref, wout_bf_ref, wglu_bf_ref, w_scr, h_scr, cos_scr, sin_scr):
    wout_bf_ref[...] = wout_ref[...].astype(BF16)
    wglu_bf_ref[...] = wglu_ref[...].astype(BF16)
    tm = _INPROJ_TM
    half = RET_HEAD_DIM // 2
    per_b = SEQ // tm
    hpt = _INPROJ_WT // RET_HEAD_DIM
    slot = i % per_b

    gain = nw_ref[...] * (1.0 + sc_ref[...])
    shift = sh_ref[...]
    for r in range(tm // _INPROJ_SUB):
        rs = slice(r * _INPROJ_SUB, (r + 1) * _INPROJ_SUB)
        x = x_ref[rs, :]
        ms = jnp.mean(x * x, axis=-1, keepdims=True)
        h_scr[rs, :] = (x * lax.rsqrt(ms + EPS) * gain + shift).astype(BF16)
        cs = cos_scr[slot, rs, :]
        sn = sin_scr[slot, rs, :]
        for t in range(_INPROJ_NW):
            acc = jnp.dot(h_scr[rs, :], w_scr[t], preferred_element_type=F32)
            if t < 2 * RET_HEADS // hpt:
                for hh in range(hpt):
                    c0 = hh * RET_HEAD_DIM
                    x1 = acc[:, c0:c0 + half]
                    x2 = acc[:, c0 + half:c0 + 2 * half]
                    o_ref[t * hpt + hh, rs, 0:half] = (x1 * cs - x2 * sn).astype(BF16)
                    o_ref[t * hpt + hh, rs, half:2 * half] = (x2 * cs + x1 * sn).astype(BF16)
            elif t < 4 * RET_HEADS // hpt:
                for hh in range(hpt):
                    c0 = hh * RET_HEAD_DIM
                    o_ref[t * hpt + hh, rs, :] = acc[:, c0:c0 + RET_HEAD_DIM].astype(BF16)
            else:
                k0 = (t - 4 * RET_HEADS // hpt) * (_INPROJ_WT // LANES)
                for k in range(_INPROJ_WT // LANES):
                    u_ref[k0 + k, r * _INPROJ_NS:(r + 1) * _INPROJ_NS] = \
                        acc[:, k * LANES:(k + 1) * LANES].reshape(_INPROJ_NS, S5_SUB, LANES)


def _inproj(x2d, mod, norm_w, w_in, w_out, w_glu):
    tm = _INPROJ_TM
    per_b = SEQ // tm
    half = RET_HEAD_DIM // 2
    nw = _INPROJ_NW
    steps = TOK // tm
    so = w_out.shape[0] // steps
    sg = w_glu.shape[0] // steps
    assert w_in.shape == (D_MODEL, 4 * D_RET + D_S5) and D_S5 == D_RET and nw >= per_b
    assert so * steps == w_out.shape[0] and sg * steps == w_glu.shape[0] and sg % 16 == 0
    row = lambda s: jnp.maximum(s - nw, 0)
    slab_o = pl.BlockSpec((so, w_out.shape[1]), lambda s: (row(s), 0))
    slab_g = pl.BlockSpec((sg, w_glu.shape[1]), lambda s: (row(s), 0))
    mod_spec = lambda which: pl.BlockSpec((None, 1, D_MODEL),
                                          lambda s: (row(s) // per_b, 0, which))
    return pl.pallas_call(
        _inproj_kernel,
        grid=(nw + steps,),
        in_specs=[pl.BlockSpec((tm, D_MODEL), lambda s: (row(s), 0)),
                  mod_spec(_SC1), mod_spec(_SH1),
                  pl.BlockSpec((1, D_MODEL), lambda s: (0, 0)),
                  pl.BlockSpec((D_MODEL, _INPROJ_WT), lambda s: (0, jnp.minimum(s, nw - 1))),
                  slab_o, slab_g],
        out_specs=[pl.BlockSpec((4 * RET_HEADS, tm, RET_HEAD_DIM), lambda s: (0, row(s), 0)),
                   pl.BlockSpec((D_S5 // LANES, tm // S5_SUB, None, S5_SUB, LANES),
                                lambda s: (0, row(s) % per_b, row(s) // per_b, 0, 0)),
                   slab_o, slab_g],
        out_shape=[jax.ShapeDtypeStruct((4 * RET_HEADS, TOK, RET_HEAD_DIM), BF16),
                   jax.ShapeDtypeStruct((D_S5 // LANES, S5_NSUB, BATCH, S5_SUB, LANES), F32),
                   jax.ShapeDtypeStruct(w_out.shape, BF16),
                   jax.ShapeDtypeStruct(w_glu.shape, BF16)],
        scratch_shapes=[pltpu.VMEM((nw, D_MODEL, _INPROJ_WT), BF16),
                        pltpu.VMEM((tm, D_MODEL), BF16),
                        pltpu.VMEM((per_b, tm, half), F32),
                        pltpu.VMEM((per_b, tm, half), F32)],
        compiler_params=pltpu.CompilerParams(
            dimension_semantics=("arbitrary",), vmem_limit_bytes=VMEM_LIMIT),
        name="inproj",
    )(x2d, mod, mod, norm_w, w_in, w_out, w_glu)


_RET_BLK = 256


def _ret_kernel(lg_ref, q_ref, k_ref, v_ref, g_ref, w_ref, o_ref,
                kv_scr, prev_scr, intra_scr, kdec_scr, qdec_scr):
    C = _RET_BLK
    dh = RET_HEAD_DIM
    nc = SEQ // C
    scale = dh ** -0.5
    lg = lg_ref[0:1, :]
    ii = lax.broadcasted_iota(jnp.int32, (C, C), 0)
    jj = lax.broadcasted_iota(jnp.int32, (C, C), 1)
    diff = (ii - jj).astype(F32)
    intra_scr[...] = jnp.where(diff >= 0.0, jnp.exp(lg * jnp.maximum(diff, 0.0)), 0.0) * scale
    row = lax.broadcasted_iota(jnp.int32, (C, dh), 0).astype(F32)
    kdec_scr[...] = jnp.exp(lg * (C - 1.0 - row)) * scale
    qdec_scr[...] = jnp.exp(lg * (row + 1.0))
    block_decay = jnp.exp(lg * float(C))
    gn_w = w_ref[...]

    for n in range(nc):
        rs = slice(n * C, (n + 1) * C)
        kd = (k_ref[rs, :].astype(F32) * kdec_scr[...]).astype(BF16)
        kv_scr[n] = lax.dot_general(kd, v_ref[rs, :], (((0,), (0,)), ((), ())),
                                    preferred_element_type=F32)

    band = 64
    for rb in range(dh // band):
        bs = slice(rb * band, (rb + 1) * band)
        st = jnp.zeros((band, dh), F32)
        for n in range(nc):
            prev_scr[n, bs, :] = st.astype(BF16)
            if n + 1 < nc:
                st = st * block_decay + kv_scr[n, bs, :]

    for n in range(nc):
        rs = slice(n * C, (n + 1) * C)
        q = q_ref[rs, :]
        s = lax.dot_general(q, k_ref[rs, :], (((1,), (1,)), ((), ())),
                            preferred_element_type=F32) * intra_scr[...]
        y = jnp.dot(s.astype(BF16), v_ref[rs, :], preferred_element_type=F32)
        y = y + jnp.dot(q, prev_scr[n], preferred_element_type=F32) * qdec_scr[...]
        mu = jnp.mean(y, axis=-1, keepdims=True)
        yc = y - mu
        var = jnp.mean(yc * yc, axis=-1, keepdims=True)
        yn = yc * lax.rsqrt(var + EPS) * gn_w
        g = g_ref[rs, :].astype(F32)
        o_ref[rs, :] = (_silu(g) * yn).astype(BF16)


def _retention(qkvg, ret_norm_w):
    dh = RET_HEAD_DIM
    nc = SEQ // _RET_BLK
    lg = np.log1p(-np.exp2(-5.0 - np.arange(RET_HEADS, dtype=np.float64)))
    lg_tab = jnp.asarray(np.broadcast_to(lg[:, None, None], (RET_HEADS, 8, dh)), F32)
    spec = lambda off: pl.BlockSpec((None, SEQ, dh), lambda b, h: (off + h, b, 0))
    return pl.pallas_call(
        _ret_kernel,
        grid=(BATCH, RET_HEADS),
        in_specs=[pl.BlockSpec((None, 8, dh), lambda b, h: (h, 0, 0)),
                  spec(0), spec(RET_HEADS), spec(2 * RET_HEADS), spec(3 * RET_HEADS),
                  pl.BlockSpec((1, dh), lambda b, h: (0, h))],
        out_specs=pl.BlockSpec((None, SEQ, dh), lambda b, h: (h, b, 0)),
        out_shape=jax.ShapeDtypeStruct((RET_HEADS, TOK, dh), BF16),
        scratch_shapes=[pltpu.VMEM((nc, dh, dh), F32),
                        pltpu.VMEM((nc, dh, dh), BF16),
                        pltpu.VMEM((_RET_BLK, _RET_BLK), F32),
                        pltpu.VMEM((_RET_BLK, dh), F32),
                        pltpu.VMEM((_RET_BLK, dh), F32)],
        compiler_params=pltpu.CompilerParams(
            dimension_semantics=("parallel", "parallel"), vmem_limit_bytes=VMEM_LIMIT),
        name="retention",
    )(lg_tab, qkvg, qkvg, qkvg, qkvg, ret_norm_w)


_S5GEN_GG = 4


def _prep_kernel(c_ref, wada_ref, bada_ref,
                 ls_ref, ar_ref, ai_ref, bt_ref, cr_ref, ci_ref, cc_ref, d_ref,
                 mod_ref, mt_ref, wt_ref, vt_ref, a16r_ref, a16i_ref):
    cond = _silu(c_ref[...])
    mod = jnp.dot(cond.astype(BF16), wada_ref[...].astype(BF16),
                  preferred_element_type=F32) + bada_ref[...]
    for b in range(BATCH):
        mod_ref[b] = mod[b:b + 1, :]
    N, P, R = S5_STATE, S5_GROUP, S5_ROW
    re_half = lax.broadcasted_iota(jnp.int32, (P, 2 * N), 1) < N
    sub_e = lax.broadcasted_iota(jnp.int32, (P, 2 * N), 0)
    lane_p = lax.broadcasted_iota(jnp.int32, (P, R), 1)
    sub_p = lax.broadcasted_iota(jnp.int32, (P, R), 0)
    lag0_diag = lane_p == (R - P) + sub_p

    def hdot(a, b):
        return jnp.dot(a, b, precision=HI, preferred_element_type=F32)

    def cmul(xr, xi, yr, yi):
        return xr * yr - xi * yi, xr * yi + xi * yr

    for k in range(_S5GEN_GG):
        dt = jnp.exp(ls_ref[k])
        ar = ar_ref[k]
        ai = ai_ref[k]
        mag = jnp.exp(ar * dt)
        lbr = mag * jnp.cos(ai * dt)
        lbi = mag * jnp.sin(ai * dt)
        den = ar * ar + ai * ai
        zr = ((lbr - 1.0) * ar + lbi * ai) / den
        zi = (lbi * ar - (lbr - 1.0) * ai) / den

        pw_r = jnp.ones((P, 2 * N), F32)
        pw_i = jnp.zeros((P, 2 * N), F32)
        sq_r, sq_i = lbr, lbi
        for bit in range(4):
            nr, ni = cmul(pw_r, pw_i, sq_r, sq_i)
            take = (lax.shift_right_logical(sub_e, bit) & 1) == 1
            pw_r = jnp.where(take, nr, pw_r)
            pw_i = jnp.where(take, ni, pw_i)
            if bit < 3:
                sq_r, sq_i = cmul(sq_r, sq_i, sq_r, sq_i)
        p1_r, p1_i = cmul(pw_r, pw_i, lbr, lbi)
        a16r_ref[k] = p1_r[S5_SUB - 1:S5_SUB, :]
        a16i_ref[k] = p1_i[S5_SUB - 1:S5_SUB, :]

        bt = bt_ref[k]
        x = zr * bt + jnp.where(re_half[0:1], -zi, zi) * pltpu.roll(bt, N, axis=1)
        xs = pltpu.roll(x, N, axis=1)
        pw_is = jnp.where(re_half, -pw_i, pw_i)
        w_rows = jnp.concatenate(
            [pw_r[e:e + 1] * x + pw_is[e:e + 1] * xs for e in range(S5_SUB - 1, -1, -1)], axis=0)
        w_t = w_rows.T
        w_r = w_t[0:N]
        w_i = w_t[N:2 * N]
        wt_ref[k] = w_t.astype(BF16)

        c_re = cr_ref[k]
        c_im = ci_ref[k]
        krev = hdot(c_re, w_r) - hdot(c_im, w_i)
        krev = krev + jnp.where(lag0_diag, d_ref[k], 0.0)
        for tp in range(S5_SUB):
            width = (tp + 1) * P
            piece = krev if width == R else jnp.where(
                lane_p < width, pltpu.roll(krev, width, axis=1), 0.0)
            mt_ref[k, tp * P:(tp + 1) * P, :] = piece.astype(BF16)

        cc = cc_ref[k]
        ccs = pltpu.roll(cc, N, axis=1)
        p1_rs = jnp.where(re_half, p1_r, -p1_r)
        for tp in range(S5_SUB):
            val = cc * p1_rs[tp:tp + 1] - ccs * p1_i[tp:tp + 1]
            vt_ref[k, tp * P:(tp + 1) * P, :] = val.astype(BF16)


def _prep(c, w_ada, b_ada, a_re, a_im, log_step, b_re, b_im, c_re, c_im, d_skip):
    G, N, P, R = S5_GROUPS, S5_STATE, S5_GROUP, S5_ROW
    gg = _S5GEN_GG
    steps = G // gg
    n_mod = w_ada.shape[1]
    tn = n_mod // steps
    assert tn * steps == n_mod and tn % LANES == 0
    dup = lambda a: jnp.tile(a, (1, 2)).reshape(G, 1, 2 * N)
    bt = jnp.concatenate([jnp.swapaxes(b_re, 1, 2), jnp.swapaxes(b_im, 1, 2)], axis=-1)
    cc = jnp.concatenate([c_re, c_im], axis=-1)
    blk = lambda s: pl.BlockSpec((gg,) + s, lambda g: (g, 0, 0))
    return pl.pallas_call(
        _prep_kernel,
        grid=(steps,),
        in_specs=[pl.BlockSpec((BATCH, D_MODEL), lambda g: (0, 0)),
                  pl.BlockSpec((D_MODEL, tn), lambda g: (0, g)),
                  pl.BlockSpec((1, tn), lambda g: (0, g)),
                  blk((1, 1)), blk((1, 2 * N)), blk((1, 2 * N)), blk((P, 2 * N)),
                  blk((P, N)), blk((P, N)), blk((P, 2 * N)), blk((P, 1))],
        out_specs=[pl.BlockSpec((BATCH, 1, tn), lambda g: (0, 0, g)),
                   blk((R, R)), blk((2 * N, R)), blk((R, 2 * N)), blk((1, 2 * N)), blk((1, 2 * N))],
        out_shape=[jax.ShapeDtypeStruct((BATCH, 1, n_mod), F32),
                   jax.ShapeDtypeStruct((G, R, R), BF16),
                   jax.ShapeDtypeStruct((G, 2 * N, R), BF16),
                   jax.ShapeDtypeStruct((G, R, 2 * N), BF16)] +
                  [jax.ShapeDtypeStruct((G, 1, 2 * N), F32)] * 2,
        compiler_params=pltpu.CompilerParams(
            dimension_semantics=("parallel",), vmem_limit_bytes=VMEM_LIMIT,
            allow_input_fusion=[False, False, False, True, True, True, True, False, False, True,
                                True]),
        name="prep",
    )(c, w_ada, b_ada.reshape(1, n_mod),
      log_step.reshape(G, 1, 1), dup(a_re), dup(a_im), bt, c_re, c_im, cc,
      d_skip.reshape(G, P, 1))


_S5_GB = LANES // S5_GROUP
_S5_NP = _S5_GB // 2
_S5_NSTEP = S5_GROUPS // _S5_GB


def _s5_kernel(a16r_ref, a16i_ref, u_hbm, mt_ref, wt_ref, vt_ref, y_hbm,
               ubuf, ybuf, in_sem, out_sem, ut_scr, yt_scr, e_scr, p_scr):
    P = S5_GROUP
    N2 = 2 * S5_STATE
    cols = S5_COLS
    nblk = cols // LANES
    nt = cols // 8
    step = pl.program_id(0)
    nstep = pl.num_programs(0)
    slot = step % 2

    def in_copy(slab, sl, tau):
        return pltpu.make_async_copy(u_hbm.at[slab, :, tau, :], ubuf.at[sl, tau], in_sem.at[sl])

    def out_copy(slab, sl, tau):
        return pltpu.make_async_copy(ybuf.at[sl, tau], y_hbm.at[slab, :, tau, :], out_sem.at[sl])

    @pl.when(step == 0)
    def _():
        for tau in range(S5_SUB):
            in_copy(0, 0, tau).start()

    @pl.when(step + 1 < nstep)
    def _():
        for tau in range(S5_SUB):
            in_copy(step + 1, 1 - slot, tau).start()

    for tau in range(S5_SUB):
        in_copy(step, slot, tau).wait()

    def load_body(tau, carry):
        for c in range(nblk):
            xt = ubuf[slot, tau, c * LANES:(c + 1) * LANES, :].T
            r0 = pl.multiple_of(tau * P, P)
            for k in range(_S5_GB):
                ut_scr[k, pl.ds(r0, P), c * LANES:(c + 1) * LANES] = \
                    xt[k * P:(k + 1) * P, :].astype(BF16)
        return carry

    lax.fori_loop(0, S5_SUB, load_body, 0, unroll=4)

    NS = S5_STATE
    for pr in range(_S5_NP):
        e0 = jnp.dot(wt_ref[2 * pr], ut_scr[2 * pr], preferred_element_type=F32)
        e1 = jnp.dot(wt_ref[2 * pr + 1], ut_scr[2 * pr + 1], preferred_element_type=F32)
        et = jnp.concatenate([e0[0:NS], e1[0:NS], e0[NS:2 * NS], e1[NS:2 * NS]], axis=0)
        e_scr[pr] = et.T

    sub = lax.broadcasted_iota(jnp.int32, (8, N2), 0)
    lo = sub < BATCH
    first = lax.broadcasted_iota(jnp.int32, (1, N2), 1) < S5_STATE
    a_r, a_i = [], []
    for pr in range(_S5_NP):
        a_r.append(jnp.broadcast_to(
            jnp.where(first, a16r_ref[2 * pr], a16r_ref[2 * pr + 1]), (8, N2)))
        a_i.append(jnp.broadcast_to(
            jnp.where(first, a16i_ref[2 * pr], a16i_ref[2 * pr + 1]), (8, N2)))

    def scan_body(t, carry):
        r0 = pl.multiple_of(t * 8, 8)
        new = []
        for pr in range(_S5_NP):
            s_r, s_i = carry[2 * pr], carry[2 * pr + 1]
            e_r = e_scr[pr, pl.ds(r0, 8), 0:N2]
            e_i = e_scr[pr, pl.ds(r0, 8), N2:2 * N2]
            x_r = pltpu.roll(e_r, BATCH, axis=0)
            x_i = pltpu.roll(e_i, BATCH, axis=0)
            elo_r = jnp.where(lo, e_r, x_r)
            elo_i = jnp.where(lo, e_i, x_i)
            ehi_r = jnp.where(lo, x_r, e_r)
            ehi_i = jnp.where(lo, x_i, e_i)
            t_r = a_r[pr] * s_r - a_i[pr] * s_i + elo_r
            t_i = a_r[pr] * s_i + a_i[pr] * s_r + elo_i
            p_scr[pr, pl.ds(r0, 8), 0:N2] = jnp.where(lo, s_r, t_r)
            p_scr[pr, pl.ds(r0, 8), N2:2 * N2] = jnp.where(lo, s_i, t_i)
            new.append(a_r[pr] * t_r - a_i[pr] * t_i + ehi_r)
            new.append(a_r[pr] * t_i + a_i[pr] * t_r + ehi_i)
        return tuple(new)

    zero = jnp.zeros((8, N2), F32)
    lax.fori_loop(0, nt, scan_body, tuple(zero for _ in range(2 * _S5_NP)))

    for pr in range(_S5_NP):
        pt = p_scr[pr].T.astype(BF16)
        for k in range(2):
            gi = 2 * pr + k
            pk = jnp.concatenate([pt[k * NS:(k + 1) * NS],
                                  pt[(2 + k) * NS:(3 + k) * NS]], axis=0)
            yt = (jnp.dot(mt_ref[gi], ut_scr[gi], preferred_element_type=F32) +
                  jnp.dot(vt_ref[gi], pk, preferred_element_type=F32))
            for tau in range(S5_SUB):
                yt_scr[tau, gi * P:(gi + 1) * P, :] = yt[tau * P:(tau + 1) * P, :]

    @pl.when(step >= 2)
    def _():
        for tau in range(S5_SUB):
            out_copy(step - 2, slot, tau).wait()

    def store_body(tau, carry):
        ybuf[slot, tau] = yt_scr[tau].T
        return carry

    lax.fori_loop(0, S5_SUB, store_body, 0, unroll=4)

    for tau in range(S5_SUB):
        out_copy(step, slot, tau).start()

    @pl.when(step == nstep - 1)
    def _():
        if _S5_NSTEP > 1:
            for tau in range(S5_SUB):
                out_copy(step - 1, 1 - slot, tau).wait()
        for tau in range(S5_SUB):
            out_copy(step, slot, tau).wait()


def _s5(u_rows, mt, wt, vt, a16r, a16i):
    G, N, R = S5_GROUPS, S5_STATE, S5_ROW
    gb = _S5_GB
    vec = pl.BlockSpec((gb, 1, 2 * N), lambda i: (i, 0, 0))
    mat = lambda r, c: pl.BlockSpec((gb, r, c), lambda i: (i, 0, 0))
    hbm = pl.BlockSpec(memory_space=pl.ANY)
    plane = pltpu.VMEM((2, S5_SUB, S5_COLS, LANES), F32)
    return pl.pallas_call(
        _s5_kernel,
        grid=(_S5_NSTEP,),
        in_specs=[vec, vec, hbm, mat(R, R), mat(2 * N, R), mat(R, 2 * N)],
        out_specs=hbm,
        out_shape=jax.ShapeDtypeStruct((D_S5 // LANES, S5_COLS, S5_SUB, LANES), F32),
        scratch_shapes=[plane, plane,
                        pltpu.SemaphoreType.DMA((2,)), pltpu.SemaphoreType.DMA((2,)),
                        pltpu.VMEM((gb, R, S5_COLS), BF16),
                        pltpu.VMEM((S5_SUB, LANES, S5_COLS), F32),
                        pltpu.VMEM((_S5_NP, S5_COLS, 4 * N), F32),
                        pltpu.VMEM((_S5_NP, S5_COLS, 4 * N), F32)],
        compiler_params=pltpu.CompilerParams(
            dimension_semantics=("arbitrary",), vmem_limit_bytes=VMEM_LIMIT),
        name="s5",
    )(a16r, a16i, u_rows, mt, wt, vt)


_MIX_TM = 512
_MIX_SUB = 256
_MIX_NS = _MIX_TM // S5_SUB


def _mix_kernel(ssm_ref, ret_ref, x_ref, g1_ref, sc_ref, sh_ref, nw_ref, wglu_ref, bglu_ref,
                wout_ref, o_ref, h_ref):
    gain = nw_ref[...] * (1.0 + sc_ref[...])
    nsub = _MIX_TM // _MIX_SUB
    ns = _MIX_SUB // S5_SUB
    gated = []
    for r in range(nsub):
        s = jnp.concatenate([ssm_ref[k, r * ns:(r + 1) * ns].reshape(_MIX_SUB, LANES)
                             for k in range(D_S5 // LANES)], axis=1)
        cdf = 0.5 * (1.0 + jnp.tanh(math.sqrt(2.0 / math.pi) * (s + 0.044715 * (s * s * s))))
        sg = s * cdf
        z = jnp.dot(sg.astype(BF16), wglu_ref[...], preferred_element_type=F32) + bglu_ref[...]
        gated.append((sg * jax.nn.sigmoid(z)).astype(BF16))
    for r in range(nsub):
        rs = slice(r * _MIX_SUB, (r + 1) * _MIX_SUB)
        both = jnp.concatenate([ret_ref[hh, rs, :] for hh in range(RET_HEADS)] + [gated[r]],
                               axis=1)
        mix = jnp.dot(both, wout_ref[...], preferred_element_type=F32)
        x1 = x_ref[rs, :] + g1_ref[...] * mix
        o_ref[rs, :] = x1
        ms = jnp.mean(x1 * x1, axis=-1, keepdims=True)
        h_ref[rs, :] = (x1 * lax.rsqrt(ms + EPS) * gain + sh_ref[...]).astype(BF16)


def _mix(ssm_sb, ret_out, x2d, mod, norm_w, w_glu_bf, b_glu, w_out_bf):
    tm = _MIX_TM
    per_b = SEQ // tm
    rows = pl.BlockSpec((tm, D_MODEL), lambda i: (i, 0))
    return pl.pallas_call(
        _mix_kernel,
        grid=(TOK // tm,),
        in_specs=[pl.BlockSpec((D_S5 // LANES, _MIX_NS, None, S5_SUB, LANES),
                               lambda i: (0, i % per_b, i // per_b, 0, 0)),
                  pl.BlockSpec((RET_HEADS, tm, RET_HEAD_DIM), lambda i: (0, i, 0)),
                  rows, _mod_spec(_G1, tm), _mod_spec(_SC2, tm), _mod_spec(_SH2, tm),
                  pl.BlockSpec((1, D_MODEL), lambda i: (0, 0)),
                  pl.BlockSpec((D_S5, D_S5), lambda i: (0, 0)),
                  pl.BlockSpec((1, D_S5), lambda i: (0, 0)),
                  pl.BlockSpec((D_RET + D_S5, D_MODEL), lambda i: (0, 0))],
        out_specs=[rows, rows],
        out_shape=[jax.ShapeDtypeStruct((TOK, D_MODEL), F32),
                   jax.ShapeDtypeStruct((TOK, D_MODEL), BF16)],
        compiler_params=pltpu.CompilerParams(
            dimension_semantics=("parallel",), vmem_limit_bytes=VMEM_LIMIT),
        name="mix",
    )(ssm_sb, ret_out, x2d, mod, mod, mod, norm_w, w_glu_bf, b_glu, w_out_bf)


_FFN_UP_TM = 2048
_FFN_UP_SUB = 1024
_FFN_UP_TF = 512
_FFN_DOWN_TM = 512
_FFN_DOWN_SUB = 256


def _ffn_up_kernel(h_ref, wg_ref, wu_ref, wd_ref, o_ref, wd_bf_ref, wg_scr, wu_scr):
    @pl.when(pl.program_id(1) == 0)
    def _():
        wg_scr[...] = wg_ref[...].astype(BF16)
        wu_scr[...] = wu_ref[...].astype(BF16)

    wd_bf_ref[...] = wd_ref[...].astype(BF16)
    for r in range(_FFN_UP_TM // _FFN_UP_SUB):
        rs = slice(r * _FFN_UP_SUB, (r + 1) * _FFN_UP_SUB)
        h = h_ref[rs, :]
        gate = jnp.dot(h, wg_scr[...], preferred_element_type=F32)
        up = jnp.dot(h, wu_scr[...], preferred_element_type=F32)
        o_ref[rs, :] = (_silu(gate) * up).astype(BF16)


def _ffn_up(h2, w_gate_up, w_down):
    tm, tf = _FFN_UP_TM, _FFN_UP_TF
    nf = D_FF // tf
    nm = TOK // tm
    slab = D_FF // (nf * nm)
    assert slab * nf * nm == D_FF and slab % 16 == 0
    return pl.pallas_call(
        _ffn_up_kernel,
        grid=(nf, nm),
        in_specs=[pl.BlockSpec((tm, D_MODEL), lambda f, i: (i, 0)),
                  pl.BlockSpec((D_MODEL, tf), lambda f, i: (0, f)),
                  pl.BlockSpec((D_MODEL, tf), lambda f, i: (0, nf + f)),
                  pl.BlockSpec((slab, D_MODEL), lambda f, i: (f * nm + i, 0))],
        out_specs=[pl.BlockSpec((tm, tf), lambda f, i: (i, f)),
                   pl.BlockSpec((slab, D_MODEL), lambda f, i: (f * nm + i, 0))],
        out_shape=[jax.ShapeDtypeStruct((TOK, D_FF), BF16),
                   jax.ShapeDtypeStruct((D_FF, D_MODEL), BF16)],
        scratch_shapes=[pltpu.VMEM((D_MODEL, tf), BF16),
                        pltpu.VMEM((D_MODEL, tf), BF16)],
        compiler_params=pltpu.CompilerParams(
            dimension_semantics=("arbitrary", "arbitrary"), vmem_limit_bytes=VMEM_LIMIT),
        name="ffn_up",
    )(h2, w_gate_up, w_gate_up, w_down)


def _ffn_down_kernel(a_ref, x_ref, g2_ref, fw_ref, wd_ref, o_ref):
    for r in range(_FFN_DOWN_TM // _FFN_DOWN_SUB):
        rs = slice(r * _FFN_DOWN_SUB, (r + 1) * _FFN_DOWN_SUB)
        down = jnp.dot(a_ref[rs, :], wd_ref[...], preferred_element_type=F32)
        x2 = x_ref[rs, :] + g2_ref[...] * down
        ms = jnp.mean(x2 * x2, axis=-1, keepdims=True)
        o_ref[rs, :] = x2 * lax.rsqrt(ms + EPS) * fw_ref[...]


def _ffn_down(act, x1, mod, final_w, w_down_bf):
    tm = _FFN_DOWN_TM
    rows = pl.BlockSpec((tm, D_MODEL), lambda i: (i, 0))
    return pl.pallas_call(
        _ffn_down_kernel,
        grid=(TOK // tm,),
        in_specs=[pl.BlockSpec((tm, D_FF), lambda i: (i, 0)),
                  rows,
                  _mod_spec(_G2, tm),
                  pl.BlockSpec((1, D_MODEL), lambda i: (0, 0)),
                  pl.BlockSpec((D_FF, D_MODEL), lambda i: (0, 0))],
        out_specs=rows,
        out_shape=jax.ShapeDtypeStruct((TOK, D_MODEL), F32),
        compiler_params=pltpu.CompilerParams(
            dimension_semantics=("parallel",), vmem_limit_bytes=VMEM_LIMIT),
        name="ffn_down",
    )(act, x1, mod, final_w, w_down_bf)


def kernel(x, c, w_ada, b_ada, norm1_w, w_in, ret_norm_w, s5_a_re, s5_a_im, s5_log_step,
           s5_b_re, s5_b_im, s5_c_re, s5_c_im, s5_d, w_glu, b_glu, w_out, norm2_w,
           w_gate_up, w_down, final_norm_w):
    x2d = x.reshape(TOK, D_MODEL)
    layer = 0
    mod, mt, wt, vt, a16r, a16i = _prep(
        c, w_ada[layer], b_ada[layer],
        s5_a_re[layer], s5_a_im[layer], s5_log_step[layer], s5_b_re[layer], s5_b_im[layer],
        s5_c_re[layer], s5_c_im[layer], s5_d[layer])

    qkvg, u_sb, w_out_bf, w_glu_bf = _inproj(x2d, mod, norm1_w[layer].reshape(1, D_MODEL),
                                             w_in[layer], w_out[layer], w_glu[layer])
    ret_out = _retention(qkvg, ret_norm_w[layer].reshape(1, D_RET))
    y_rows = _s5(u_sb.reshape(D_S5 // LANES, S5_COLS, S5_SUB, LANES), mt, wt, vt, a16r, a16i)
    ssm_sb = y_rows.reshape(D_S5 // LANES, S5_NSUB, BATCH, S5_SUB, LANES)

    x1, h2 = _mix(ssm_sb, ret_out, x2d, mod, norm2_w[layer].reshape(1, D_MODEL),
                  w_glu_bf, b_glu[layer].reshape(1, D_S5), w_out_bf)
    act, w_down_bf = _ffn_up(h2, w_gate_up[layer], w_down[layer])
    out = _ffn_down(act, x1, mod, final_norm_w.reshape(1, D_MODEL), w_down_bf)
    return out.reshape(BATCH, SEQ, D_MODEL)
```

```python
import math

import numpy as np
import jax
import jax.numpy as jnp
from jax import lax
from jax.experimental import pallas as pl
from jax.experimental.pallas import tpu as pltpu

D_MODEL = 2048
BATCH = 4
SEQ = 2048
TOK = BATCH * SEQ
D_RET = 1024
D_S5 = 1024
RET_HEADS = 4
RET_HEAD_DIM = 256
S5_GROUP = 16
S5_GROUPS = 64
S5_STATE = 64
S5_SUB = 16
S5_ROW = S5_SUB * S5_GROUP
S5_NSUB = SEQ // S5_SUB
S5_COLS = S5_NSUB * BATCH
D_FF = 5632
ROPE_BASE = 10000.0
EPS = 1e-6
LANES = 128

F32 = jnp.float32
BF16 = jnp.bfloat16
HI = lax.Precision.HIGHEST
VMEM_LIMIT = 58 * 1024 * 1024


def _silu(v):
    return v * jax.nn.sigmoid(v)


_SH1, _SC1, _G1, _SH2, _SC2, _G2 = range(6)


def _mod_spec(which, rows_per_step):
    per_b = SEQ // rows_per_step
    return pl.BlockSpec((None, 1, D_MODEL), lambda i: (i // per_b, 0, which))


_INPROJ_TM = 512
_INPROJ_SUB = 256
_INPROJ_NS = _INPROJ_SUB // S5_SUB


_INPROJ_WT = 512
_INPROJ_NW = (4 * D_RET + D_S5) // _INPROJ_WT


def _inproj_kernel(x_ref, sc_ref, sh_ref, nw_ref, w_ref, wout_ref, wglu_ref,
                   o_ref, u_ref, wout_bf_ref, wglu_bf_ref, w_scr, h_scr, cos_scr, sin_scr):
    step = pl.program_id(0)

    @pl.when(step < _INPROJ_NW)
    def _():
        w_scr[step] = w_ref[...].astype(BF16)

        @pl.when(step < SEQ // _INPROJ_TM)
        def _():
            tm = _INPROJ_TM
            half = RET_HEAD_DIM // 2
            pos = (lax.broadcasted_iota(jnp.int32, (tm, half), 0) + step * tm).astype(F32)
            lane = lax.broadcasted_iota(jnp.int32, (tm, half), 1).astype(F32)
            ang = pos * jnp.exp(lane * (-math.log(ROPE_BASE) / half))
            cos_scr[step] = jnp.cos(ang)
            sin_scr[step] = jnp.sin(ang)

    @pl.when(step >= _INPROJ_NW)
    def _():
        _inproj_rows(step - _INPROJ_NW, x_ref, sc_ref, sh_ref, nw_ref, wout_ref, wglu_ref,
                     o_ref, u_ref, wout_bf_ref, wglu_bf_ref, w_scr, h_scr, cos_scr, sin_scr)


def _inproj_rows(i, x_ref, sc_ref, sh_ref, nw_ref, wout_ref, wglu_ref,
                 o_ref, u_ref, wout_bf_ref, wglu_bf_ref, w_scr, h_scr, cos_scr, sin_scr):
    wout_bf_ref[...] = wout_ref[...].astype(BF16)
    wglu_bf_ref[...] = wglu_ref[...].astype(BF16)
    tm = _INPROJ_TM
    half = RET_HEAD_DIM // 2
    per_b = SEQ // tm
    hpt = _INPROJ_WT // RET_HEAD_DIM
    slot = i % per_b

    gain = nw_ref[...] * (1.0 + sc_ref[...])
    shift = sh_ref[...]
    for r in range(tm // _INPROJ_SUB):
        rs = slice(r * _INPROJ_SUB, (r + 1) * _INPROJ_SUB)
        x = x_ref[rs, :]
        ms = jnp.mean(x * x, axis=-1, keepdims=True)
        h_scr[rs, :] = (x * lax.rsqrt(ms + EPS) * gain + shift).astype(BF16)
        cs = cos_scr[slot, rs, :]
        sn = sin_scr[slot, rs, :]
        for t in range(_INPROJ_NW):
            acc = jnp.dot(h_scr[rs, :], w_scr[t], preferred_element_type=F32)
            if t < 2 * RET_HEADS // hpt:
                for hh in range(hpt):
                    c0 = hh * RET_HEAD_DIM
                    x1 = acc[:, c0:c0 + half]
                    x2 = acc[:, c0 + half:c0 + 2 * half]
                    o_ref[t * hpt + hh, rs, 0:half] = (x1 * cs - x2 * sn).astype(BF16)
                    o_ref[t * hpt + hh, rs, half:2 * half] = (x2 * cs + x1 * sn).astype(BF16)
            elif t < 4 * RET_HEADS // hpt:
                for hh in range(hpt):
                    c0 = hh * RET_HEAD_DIM
                    o_ref[t * hpt + hh, rs, :] = acc[:, c0:c0 + RET_HEAD_DIM].astype(BF16)
            else:
                k0 = (t - 4 * RET_HEADS // hpt) * (_INPROJ_WT // LANES)
                for k in range(_INPROJ_WT // LANES):
                    u_ref[k0 + k, r * _INPROJ_NS:(r + 1) * _INPROJ_NS] = \
                        acc[:, k * LANES:(k + 1) * LANES].reshape(_INPROJ_NS, S5_SUB, LANES)


def _inproj(x2d, mod, norm_w, w_in, w_out, w_glu):
    tm = _INPROJ_TM
    per_b = SEQ // tm
    half = RET_HEAD_DIM // 2
    nw = _INPROJ_NW
    steps = TOK // tm
    so = w_out.shape[0] // steps
    sg = w_glu.shape[0] // steps
    assert w_in.shape == (D_MODEL, 4 * D_RET + D_S5) and D_S5 == D_RET and nw >= per_b
    assert so * steps == w_out.shape[0] and sg * steps == w_glu.shape[0] and sg % 16 == 0
    row = lambda s: jnp.maximum(s - nw, 0)
    slab_o = pl.BlockSpec((so, w_out.shape[1]), lambda s: (row(s), 0))
    slab_g = pl.BlockSpec((sg, w_glu.shape[1]), lambda s: (row(s), 0))
    mod_spec = lambda which: pl.BlockSpec((None, 1, D_MODEL),
                                          lambda s: (row(s) // per_b, 0, which))
    return pl.pallas_call(
        _inproj_kernel,
        grid=(nw + steps,),
        in_specs=[pl.BlockSpec((tm, D_MODEL), lambda s: (row(s), 0)),
                  mod_spec(_SC1), mod_spec(_SH1),
                  pl.BlockSpec((1, D_MODEL), lambda s: (0, 0)),
                  pl.BlockSpec((D_MODEL, _INPROJ_WT), lambda s: (0, jnp.minimum(s, nw - 1))),
                  slab_o, slab_g],
        out_specs=[pl.BlockSpec((4 * RET_HEADS, tm, RET_HEAD_DIM), lambda s: (0, row(s), 0)),
                   pl.BlockSpec((D_S5 // LANES, tm // S5_SUB, None, S5_SUB, LANES),
                                lambda s: (0, row(s) % per_b, row(s) // per_b, 0, 0)),
                   slab_o, slab_g],
        out_shape=[jax.ShapeDtypeStruct((4 * RET_HEADS, TOK, RET_HEAD_DIM), BF16),
                   jax.ShapeDtypeStruct((D_S5 // LANES, S5_NSUB, BATCH, S5_SUB, LANES), F32),
                   jax.ShapeDtypeStruct(w_out.shape, BF16),
                   jax.ShapeDtypeStruct(w_glu.shape, BF16)],
        scratch_shapes=[pltpu.VMEM((nw, D_MODEL, _INPROJ_WT), BF16),
                        pltpu.VMEM((tm, D_MODEL), BF16),
                        pltpu.VMEM((per_b, tm, half), F32),
                        pltpu.VMEM((per_b, tm, half), F32)],
        compiler_params=pltpu.CompilerParams(
            dimension_semantics=("arbitrary",), vmem_limit_bytes=VMEM_LIMIT),
        name="inproj",
    )(x2d, mod, mod, norm_w, w_in, w_out, w_glu)


_RET_BLK = 256


def _ret_kernel(lg_ref, q_ref, k_ref, v_ref, g_ref, w_ref, o_ref,
                kv_scr, prev_scr, intra_scr, kdec_scr, qdec_scr):
    C = _RET_BLK
    dh = RET_HEAD_DIM
    nc = SEQ // C
    scale = dh ** -0.5
    lg = lg_ref[0:1, :]

    @pl.when(pl.program_id(1) == 0)
    def _():
        ii = lax.broadcasted_iota(jnp.int32, (C, C), 0)
        jj = lax.broadcasted_iota(jnp.int32, (C, C), 1)
        diff = (ii - jj).astype(F32)
        intra_scr[...] = jnp.where(diff >= 0.0, jnp.exp(lg * jnp.maximum(diff, 0.0)), 0.0) * scale
        row = lax.broadcasted_iota(jnp.int32, (C, dh), 0).astype(F32)
        kdec_scr[...] = jnp.exp(lg * (C - 1.0 - row)) * scale
        qdec_scr[...] = jnp.exp(lg * (row + 1.0))

    block_decay = jnp.exp(lg * float(C))
    gn_w = w_ref[...]

    for n in range(nc):
        rs = slice(n * C, (n + 1) * C)
        kd = (k_ref[rs, :].astype(F32) * kdec_scr[...]).astype(BF16)
        kv_scr[n] = lax.dot_general(kd, v_ref[rs, :], (((0,), (0,)), ((), ())),
                                    preferred_element_type=F32)

    band = 64
    for rb in range(dh // band):
        bs = slice(rb * band, (rb + 1) * band)
        st = jnp.zeros((band, dh), F32)
        for n in range(nc):
            prev_scr[n, bs, :] = st.astype(BF16)
            if n + 1 < nc:
                st = st * block_decay + kv_scr[n, bs, :]

    for n in range(nc):
        rs = slice(n * C, (n + 1) * C)
        q = q_ref[rs, :]
        s = lax.dot_general(q, k_ref[rs, :], (((1,), (1,)), ((), ())),
                            preferred_element_type=F32) * intra_scr[...]
        y = jnp.dot(s.astype(BF16), v_ref[rs, :], preferred_element_type=F32)
        y = y + jnp.dot(q, prev_scr[n], preferred_element_type=F32) * qdec_scr[...]
        mu = jnp.mean(y, axis=-1, keepdims=True)
        yc = y - mu
        var = jnp.mean(yc * yc, axis=-1, keepdims=True)
        yn = yc * lax.rsqrt(var + EPS) * gn_w
        g = g_ref[rs, :].astype(F32)
        o_ref[rs, :] = (_silu(g) * yn).astype(BF16)


def _retention(qkvg, ret_norm_w):
    dh = RET_HEAD_DIM
    nc = SEQ // _RET_BLK
    lg = np.log1p(-np.exp2(-5.0 - np.arange(RET_HEADS, dtype=np.float64)))
    lg_tab = jnp.asarray(np.broadcast_to(lg[:, None, None], (RET_HEADS, 8, dh)), F32)
    spec = lambda off: pl.BlockSpec((None, SEQ, dh), lambda h, b: (off + h, b, 0))
    return pl.pallas_call(
        _ret_kernel,
        grid=(RET_HEADS, BATCH),
        in_specs=[pl.BlockSpec((None, 8, dh), lambda h, b: (h, 0, 0)),
                  spec(0), spec(RET_HEADS), spec(2 * RET_HEADS), spec(3 * RET_HEADS),
                  pl.BlockSpec((1, dh), lambda h, b: (0, h))],
        out_specs=pl.BlockSpec((None, SEQ, dh), lambda h, b: (h, b, 0)),
        out_shape=jax.ShapeDtypeStruct((RET_HEADS, TOK, dh), BF16),
        scratch_shapes=[pltpu.VMEM((nc, dh, dh), F32),
                        pltpu.VMEM((nc, dh, dh), BF16),
                        pltpu.VMEM((_RET_BLK, _RET_BLK), F32),
                        pltpu.VMEM((_RET_BLK, dh), F32),
                        pltpu.VMEM((_RET_BLK, dh), F32)],
        compiler_params=pltpu.CompilerParams(
            dimension_semantics=("arbitrary", "arbitrary"), vmem_limit_bytes=VMEM_LIMIT),
        name="retention",
    )(lg_tab, qkvg, qkvg, qkvg, qkvg, ret_norm_w)


_S5GEN_GG = 4


def _prep_kernel(c_ref, wada_ref, bada_ref,
                 ls_ref, ar_ref, ai_ref, bt_ref, cr_ref, ci_ref, cc_ref, d_ref,
                 mod_ref, mt_ref, wt_ref, vt_ref, a16r_ref, a16i_ref):
    cond = _silu(c_ref[...])
    mod = jnp.dot(cond.astype(BF16), wada_ref[...].astype(BF16),
                  preferred_element_type=F32) + bada_ref[...]
    for b in range(BATCH):
        mod_ref[b] = mod[b:b + 1, :]
    N, P, R = S5_STATE, S5_GROUP, S5_ROW
    re_half = lax.broadcasted_iota(jnp.int32, (P, 2 * N), 1) < N
    sub_e = lax.broadcasted_iota(jnp.int32, (P, 2 * N), 0)
    lane_p = lax.broadcasted_iota(jnp.int32, (P, R), 1)
    sub_p = lax.broadcasted_iota(jnp.int32, (P, R), 0)
    lag0_diag = lane_p == (R - P) + sub_p

    def hdot(a, b):
        return jnp.dot(a, b, precision=HI, preferred_element_type=F32)

    def cmul(xr, xi, yr, yi):
        return xr * yr - xi * yi, xr * yi + xi * yr

    for k in range(_S5GEN_GG):
        dt = jnp.exp(ls_ref[k])
        ar = ar_ref[k]
        ai = ai_ref[k]
        mag = jnp.exp(ar * dt)
        lbr = mag * jnp.cos(ai * dt)
        lbi = mag * jnp.sin(ai * dt)
        den = ar * ar + ai * ai
        zr = ((lbr - 1.0) * ar + lbi * ai) / den
        zi = (lbi * ar - (lbr - 1.0) * ai) / den

        pw_r = jnp.ones((P, 2 * N), F32)
        pw_i = jnp.zeros((P, 2 * N), F32)
        sq_r, sq_i = lbr, lbi
        for bit in range(4):
            nr, ni = cmul(pw_r, pw_i, sq_r, sq_i)
            take = (lax.shift_right_logical(sub_e, bit) & 1) == 1
            pw_r = jnp.where(take, nr, pw_r)
            pw_i = jnp.where(take, ni, pw_i)
            if bit < 3:
                sq_r, sq_i = cmul(sq_r, sq_i, sq_r, sq_i)
        p1_r, p1_i = cmul(pw_r, pw_i, lbr, lbi)
        a16r_ref[k] = p1_r[S5_SUB - 1:S5_SUB, :]
        a16i_ref[k] = p1_i[S5_SUB - 1:S5_SUB, :]

        bt = bt_ref[k]
        x = zr * bt + jnp.where(re_half[0:1], -zi, zi) * pltpu.roll(bt, N, axis=1)
        xs = pltpu.roll(x, N, axis=1)
        pw_is = jnp.where(re_half, -pw_i, pw_i)
        w_rows = jnp.concatenate(
            [pw_r[e:e + 1] * x + pw_is[e:e + 1] * xs for e in range(S5_SUB - 1, -1, -1)], axis=0)
        w_t = w_rows.T
        w_r = w_t[0:N]
        w_i = w_t[N:2 * N]
        wt_ref[k] = w_t.astype(BF16)

        c_re = cr_ref[k]
        c_im = ci_ref[k]
        krev = hdot(c_re, w_r) - hdot(c_im, w_i)
        krev = krev + jnp.where(lag0_diag, d_ref[k], 0.0)
        for tp in range(S5_SUB):
            width = (tp + 1) * P
            piece = krev if width == R else jnp.where(
                lane_p < width, pltpu.roll(krev, width, axis=1), 0.0)
            mt_ref[k, tp * P:(tp + 1) * P, :] = piece.astype(BF16)

        cc = cc_ref[k]
        ccs = pltpu.roll(cc, N, axis=1)
        p1_rs = jnp.where(re_half, p1_r, -p1_r)
        for tp in range(S5_SUB):
            val = cc * p1_rs[tp:tp + 1] - ccs * p1_i[tp:tp + 1]
            vt_ref[k, tp * P:(tp + 1) * P, :] = val.astype(BF16)


def _prep(c, w_ada, b_ada, a_re, a_im, log_step, b_re, b_im, c_re, c_im, d_skip):
    G, N, P, R = S5_GROUPS, S5_STATE, S5_GROUP, S5_ROW
    gg = _S5GEN_GG
    steps = G // gg
    n_mod = w_ada.shape[1]
    tn = n_mod // steps
    assert tn * steps == n_mod and tn % LANES == 0
    dup = lambda a: jnp.concatenate([a, a], axis=-1).reshape(G, 1, 2 * N)
    bt = jnp.concatenate([jnp.swapaxes(b_re, 1, 2), jnp.swapaxes(b_im, 1, 2)], axis=-1)
    cc = jnp.concatenate([c_re, c_im], axis=-1)
    blk = lambda s: pl.BlockSpec((gg,) + s, lambda g: (g, 0, 0))
    return pl.pallas_call(
        _prep_kernel,
        grid=(steps,),
        in_specs=[pl.BlockSpec((BATCH, D_MODEL), lambda g: (0, 0)),
                  pl.BlockSpec((D_MODEL, tn), lambda g: (0, g)),
                  pl.BlockSpec((1, tn), lambda g: (0, g)),
                  blk((1, 1)), blk((1, 2 * N)), blk((1, 2 * N)), blk((P, 2 * N)),
                  blk((P, N)), blk((P, N)), blk((P, 2 * N)), blk((P, 1))],
        out_specs=[pl.BlockSpec((BATCH, 1, tn), lambda g: (0, 0, g)),
                   blk((R, R)), blk((2 * N, R)), blk((R, 2 * N)), blk((1, 2 * N)), blk((1, 2 * N))],
        out_shape=[jax.ShapeDtypeStruct((BATCH, 1, n_mod), F32),
                   jax.ShapeDtypeStruct((G, R, R), BF16),
                   jax.ShapeDtypeStruct((G, 2 * N, R), BF16),
                   jax.ShapeDtypeStruct((G, R, 2 * N), BF16)] +
                  [jax.ShapeDtypeStruct((G, 1, 2 * N), F32)] * 2,
        compiler_params=pltpu.CompilerParams(
            dimension_semantics=("parallel",), vmem_limit_bytes=VMEM_LIMIT,
            allow_input_fusion=[False, False, False, True, True, True, True, False, False, True,
                                True]),
        name="prep",
    )(c, w_ada, b_ada.reshape(1, n_mod),
      log_step.reshape(G, 1, 1), dup(a_re), dup(a_im), bt, c_re, c_im, cc,
      d_skip.reshape(G, P, 1))


_S5_GB = LANES // S5_GROUP
_S5_NP = _S5_GB // 2
_S5_NSTEP = S5_GROUPS // _S5_GB


def _s5_kernel(a16r_ref, a16i_ref, u_hbm, mt_ref, wt_ref, vt_ref, y_hbm,
               ubuf, ybuf, in_sem, out_sem, ut_scr, yt_scr, e_scr, p_scr):
    P = S5_GROUP
    N2 = 2 * S5_STATE
    cols = S5_COLS
    nblk = cols // LANES
    nt = cols // 8
    step = pl.program_id(0)
    nstep = pl.num_programs(0)
    slot = step % 2

    def in_copy(slab, sl, tau):
        return pltpu.make_async_copy(u_hbm.at[slab, :, tau, :], ubuf.at[sl, tau], in_sem.at[sl])

    def out_copy(slab, sl, tau):
        return pltpu.make_async_copy(ybuf.at[sl, tau], y_hbm.at[slab, :, tau, :], out_sem.at[sl])

    @pl.when(step == 0)
    def _():
        for tau in range(S5_SUB):
            in_copy(0, 0, tau).start()

    @pl.when(step + 1 < nstep)
    def _():
        for tau in range(S5_SUB):
            in_copy(step + 1, 1 - slot, tau).start()

    for tau in range(S5_SUB):
        in_copy(step, slot, tau).wait()

    def load_body(tau, carry):
        for c in range(nblk):
            xt = ubuf[slot, tau, c * LANES:(c + 1) * LANES, :].T
            r0 = pl.multiple_of(tau * P, P)
            for k in range(_S5_GB):
                ut_scr[k, pl.ds(r0, P), c * LANES:(c + 1) * LANES] = \
                    xt[k * P:(k + 1) * P, :].astype(BF16)
        return carry

    lax.fori_loop(0, S5_SUB, load_body, 0, unroll=4)

    NS = S5_STATE
    for pr in range(_S5_NP):
        e0 = jnp.dot(wt_ref[2 * pr], ut_scr[2 * pr], preferred_element_type=F32)
        e1 = jnp.dot(wt_ref[2 * pr + 1], ut_scr[2 * pr + 1], preferred_element_type=F32)
        et = jnp.concatenate([e0[0:NS], e1[0:NS], e0[NS:2 * NS], e1[NS:2 * NS]], axis=0)
        e_scr[pr] = et.T

    sub = lax.broadcasted_iota(jnp.int32, (8, N2), 0)
    lo = sub < BATCH
    first = lax.broadcasted_iota(jnp.int32, (1, N2), 1) < S5_STATE
    a_r, a_i = [], []
    for pr in range(_S5_NP):
        a_r.append(jnp.broadcast_to(
            jnp.where(first, a16r_ref[2 * pr], a16r_ref[2 * pr + 1]), (8, N2)))
        a_i.append(jnp.broadcast_to(
            jnp.where(first, a16i_ref[2 * pr], a16i_ref[2 * pr + 1]), (8, N2)))

    def scan_body(t, carry):
        r0 = pl.multiple_of(t * 8, 8)
        new = []
        for pr in range(_S5_NP):
            s_r, s_i = carry[2 * pr], carry[2 * pr + 1]
            e_r = e_scr[pr, pl.ds(r0, 8), 0:N2]
            e_i = e_scr[pr, pl.ds(r0, 8), N2:2 * N2]
            x_r = pltpu.roll(e_r, BATCH, axis=0)
            x_i = pltpu.roll(e_i, BATCH, axis=0)
            elo_r = jnp.where(lo, e_r, x_r)
            elo_i = jnp.where(lo, e_i, x_i)
            ehi_r = jnp.where(lo, x_r, e_r)
            ehi_i = jnp.where(lo, x_i, e_i)
            t_r = a_r[pr] * s_r - a_i[pr] * s_i + elo_r
            t_i = a_r[pr] * s_i + a_i[pr] * s_r + elo_i
            p_scr[pr, pl.ds(r0, 8), 0:N2] = jnp.where(lo, s_r, t_r)
            p_scr[pr, pl.ds(r0, 8), N2:2 * N2] = jnp.where(lo, s_i, t_i)
            new.append(a_r[pr] * t_r - a_i[pr] * t_i + ehi_r)
            new.append(a_r[pr] * t_i + a_i[pr] * t_r + ehi_i)
        return tuple(new)

    zero = jnp.zeros((8, N2), F32)
    lax.fori_loop(0, nt, scan_body, tuple(zero for _ in range(2 * _S5_NP)), unroll=2)

    for pr in range(_S5_NP):
        pt = p_scr[pr].astype(BF16).T
        for k in range(2):
            gi = 2 * pr + k
            pk = jnp.concatenate([pt[k * NS:(k + 1) * NS],
                                  pt[(2 + k) * NS:(3 + k) * NS]], axis=0)
            yt = (jnp.dot(mt_ref[gi], ut_scr[gi], preferred_element_type=F32) +
                  jnp.dot(vt_ref[gi], pk, preferred_element_type=F32))
            for tau in range(S5_SUB):
                yt_scr[tau, gi * P:(gi + 1) * P, :] = yt[tau * P:(tau + 1) * P, :]

    @pl.when(step >= 2)
    def _():
        for tau in range(S5_SUB):
            out_copy(step - 2, slot, tau).wait()

    def store_body(tau, carry):
        ybuf[slot, tau] = yt_scr[tau].T
        return carry

    lax.fori_loop(0, S5_SUB, store_body, 0, unroll=4)

    for tau in range(S5_SUB):
        out_copy(step, slot, tau).start()

    @pl.when(step == nstep - 1)
    def _():
        if _S5_NSTEP > 1:
            for tau in range(S5_SUB):
                out_copy(step - 1, 1 - slot, tau).wait()
        for tau in range(S5_SUB):
            out_copy(step, slot, tau).wait()


def _s5(u_rows, mt, wt, vt, a16r, a16i):
    G, N, R = S5_GROUPS, S5_STATE, S5_ROW
    gb = _S5_GB
    vec = pl.BlockSpec((gb, 1, 2 * N), lambda i: (i, 0, 0))
    mat = lambda r, c: pl.BlockSpec((gb, r, c), lambda i: (i, 0, 0))
    hbm = pl.BlockSpec(memory_space=pl.ANY)
    plane = pltpu.VMEM((2, S5_SUB, S5_COLS, LANES), F32)
    return pl.pallas_call(
        _s5_kernel,
        grid=(_S5_NSTEP,),
        in_specs=[vec, vec, hbm, mat(R, R), mat(2 * N, R), mat(R, 2 * N)],
        out_specs=hbm,
        out_shape=jax.ShapeDtypeStruct((D_S5 // LANES, S5_COLS, S5_SUB, LANES), F32),
        scratch_shapes=[plane, plane,
                        pltpu.SemaphoreType.DMA((2,)), pltpu.SemaphoreType.DMA((2,)),
                        pltpu.VMEM((gb, R, S5_COLS), BF16),
                        pltpu.VMEM((S5_SUB, LANES, S5_COLS), F32),
                        pltpu.VMEM((_S5_NP, S5_COLS, 4 * N), F32),
                        pltpu.VMEM((_S5_NP, S5_COLS, 4 * N), F32)],
        compiler_params=pltpu.CompilerParams(
            dimension_semantics=("arbitrary",), vmem_limit_bytes=VMEM_LIMIT),
        name="s5",
    )(a16r, a16i, u_rows, mt, wt, vt)


_MIX_TM = 512
_MIX_SUB = 256
_MIX_NS = _MIX_TM // S5_SUB


def _mix_kernel(ssm_ref, ret_ref, x_ref, g1_ref, sc_ref, sh_ref, nw_ref, wglu_ref, bglu_ref,
                wout_ref, o_ref, h_ref):
    gain = nw_ref[...] * (1.0 + sc_ref[...])
    nsub = _MIX_TM // _MIX_SUB
    ns = _MIX_SUB // S5_SUB
    gated = []
    for r in range(nsub):
        s = jnp.concatenate([ssm_ref[k, r * ns:(r + 1) * ns].reshape(_MIX_SUB, LANES)
                             for k in range(D_S5 // LANES)], axis=1)
        cdf = 0.5 * (1.0 + jnp.tanh(math.sqrt(2.0 / math.pi) * (s + 0.044715 * (s * s * s))))
        sg = s * cdf
        z = jnp.dot(sg.astype(BF16), wglu_ref[...], preferred_element_type=F32) + bglu_ref[...]
        gated.append((sg * jax.nn.sigmoid(z)).astype(BF16))
    for r in range(nsub):
        rs = slice(r * _MIX_SUB, (r + 1) * _MIX_SUB)
        both = jnp.concatenate([ret_ref[hh, rs, :] for hh in range(RET_HEADS)] + [gated[r]],
                               axis=1)
        mix = jnp.dot(both, wout_ref[...], preferred_element_type=F32)
        x1 = x_ref[rs, :] + g1_ref[...] * mix
        o_ref[rs, :] = x1
        ms = jnp.mean(x1 * x1, axis=-1, keepdims=True)
        h_ref[rs, :] = (x1 * lax.rsqrt(ms + EPS) * gain + sh_ref[...]).astype(BF16)


def _mix(ssm_sb, ret_out, x2d, mod, norm_w, w_glu_bf, b_glu, w_out_bf):
    tm = _MIX_TM
    per_b = SEQ // tm
    rows = pl.BlockSpec((tm, D_MODEL), lambda i: (i, 0))
    return pl.pallas_call(
        _mix_kernel,
        grid=(TOK // tm,),
        in_specs=[pl.BlockSpec((D_S5 // LANES, _MIX_NS, None, S5_SUB, LANES),
                               lambda i: (0, i % per_b, i // per_b, 0, 0)),
                  pl.BlockSpec((RET_HEADS, tm, RET_HEAD_DIM), lambda i: (0, i, 0)),
                  rows, _mod_spec(_G1, tm), _mod_spec(_SC2, tm), _mod_spec(_SH2, tm),
                  pl.BlockSpec((1, D_MODEL), lambda i: (0, 0)),
                  pl.BlockSpec((D_S5, D_S5), lambda i: (0, 0)),
                  pl.BlockSpec((1, D_S5), lambda i: (0, 0)),
                  pl.BlockSpec((D_RET + D_S5, D_MODEL), lambda i: (0, 0))],
        out_specs=[rows, rows],
        out_shape=[jax.ShapeDtypeStruct((TOK, D_MODEL), F32),
                   jax.ShapeDtypeStruct((TOK, D_MODEL), BF16)],
        compiler_params=pltpu.CompilerParams(
            dimension_semantics=("parallel",), vmem_limit_bytes=VMEM_LIMIT),
        name="mix",
    )(ssm_sb, ret_out, x2d, mod, mod, mod, norm_w, w_glu_bf, b_glu, w_out_bf)


_FFN_UP_TM = 2048
_FFN_UP_SUB = 1024
_FFN_UP_TF = 512
_FFN_DOWN_TM = 512
_FFN_DOWN_SUB = 256


def _ffn_up_kernel(h_ref, wg_ref, wu_ref, wd_ref, o_ref, wd_bf_ref, wg_scr, wu_scr):
    @pl.when(pl.program_id(1) == 0)
    def _():
        wg_scr[...] = wg_ref[...].astype(BF16)
        wu_scr[...] = wu_ref[...].astype(BF16)

    wd_bf_ref[...] = wd_ref[...].astype(BF16)
    for r in range(_FFN_UP_TM // _FFN_UP_SUB):
        rs = slice(r * _FFN_UP_SUB, (r + 1) * _FFN_UP_SUB)
        h = h_ref[rs, :]
        gate = jnp.dot(h, wg_scr[...], preferred_element_type=F32)
        up = jnp.dot(h, wu_scr[...], preferred_element_type=F32)
        o_ref[rs, :] = (_silu(gate) * up).astype(BF16)


def _ffn_up(h2, w_gate_up, w_down):
    tm, tf = _FFN_UP_TM, _FFN_UP_TF
    nf = D_FF // tf
    nm = TOK // tm
    slab = D_FF // (nf * nm)
    assert slab * nf * nm == D_FF and slab % 16 == 0
    return pl.pallas_call(
        _ffn_up_kernel,
        grid=(nf, nm),
        in_specs=[pl.BlockSpec((tm, D_MODEL), lambda f, i: (i, 0)),
                  pl.BlockSpec((D_MODEL, tf), lambda f, i: (0, f)),
                  pl.BlockSpec((D_MODEL, tf), lambda f, i: (0, nf + f)),
                  pl.BlockSpec((slab, D_MODEL), lambda f, i: (f * nm + i, 0))],
        out_specs=[pl.BlockSpec((tm, tf), lambda f, i: (i, f)),
                   pl.BlockSpec((slab, D_MODEL), lambda f, i: (f * nm + i, 0))],
        out_shape=[jax.ShapeDtypeStruct((TOK, D_FF), BF16),
                   jax.ShapeDtypeStruct((D_FF, D_MODEL), BF16)],
        scratch_shapes=[pltpu.VMEM((D_MODEL, tf), BF16),
                        pltpu.VMEM((D_MODEL, tf), BF16)],
        compiler_params=pltpu.CompilerParams(
            dimension_semantics=("arbitrary", "arbitrary"), vmem_limit_bytes=VMEM_LIMIT),
        name="ffn_up",
    )(h2, w_gate_up, w_gate_up, w_down)


def _ffn_down_kernel(a_ref, x_ref, g2_ref, fw_ref, wd_ref, o_ref):
    for r in range(_FFN_DOWN_TM // _FFN_DOWN_SUB):
        rs = slice(r * _FFN_DOWN_SUB, (r + 1) * _FFN_DOWN_SUB)
        down = jnp.dot(a_ref[rs, :], wd_ref[...], preferred_element_type=F32)
        x2 = x_ref[rs, :] + g2_ref[...] * down
        ms = jnp.mean(x2 * x2, axis=-1, keepdims=True)
        o_ref[rs, :] = x2 * lax.rsqrt(ms + EPS) * fw_ref[...]


def _ffn_down(act, x1, mod, final_w, w_down_bf):
    tm = _FFN_DOWN_TM
    rows = pl.BlockSpec((tm, D_MODEL), lambda i: (i, 0))
    return pl.pallas_call(
        _ffn_down_kernel,
        grid=(TOK // tm,),
        in_specs=[pl.BlockSpec((tm, D_FF), lambda i: (i, 0)),
                  rows,
                  _mod_spec(_G2, tm),
                  pl.BlockSpec((1, D_MODEL), lambda i: (0, 0)),
                  pl.BlockSpec((D_FF, D_MODEL), lambda i: (0, 0))],
        out_specs=rows,
        out_shape=jax.ShapeDtypeStruct((TOK, D_MODEL), F32),
        compiler_params=pltpu.CompilerParams(
            dimension_semantics=("parallel",), vmem_limit_bytes=VMEM_LIMIT),
        name="ffn_down",
    )(act, x1, mod, final_w, w_down_bf)


def kernel(x, c, w_ada, b_ada, norm1_w, w_in, ret_norm_w, s5_a_re, s5_a_im, s5_log_step,
           s5_b_re, s5_b_im, s5_c_re, s5_c_im, s5_d, w_glu, b_glu, w_out, norm2_w,
           w_gate_up, w_down, final_norm_w):
    x2d = x.reshape(TOK, D_MODEL)
    layer = 0
    mod, mt, wt, vt, a16r, a16i = _prep(
        c, w_ada[layer], b_ada[layer],
        s5_a_re[layer], s5_a_im[layer], s5_log_step[layer], s5_b_re[layer], s5_b_im[layer],
        s5_c_re[layer], s5_c_im[layer], s5_d[layer])

    qkvg, u_sb, w_out_bf, w_glu_bf = _inproj(x2d, mod, norm1_w[layer].reshape(1, D_MODEL),
                                             w_in[layer], w_out[layer], w_glu[layer])
    ret_out = _retention(qkvg, ret_norm_w[layer].reshape(1, D_RET))
    y_rows = _s5(u_sb.reshape(D_S5 // LANES, S5_COLS, S5_SUB, LANES), mt, wt, vt, a16r, a16i)
    ssm_sb = y_rows.reshape(D_S5 // LANES, S5_NSUB, BATCH, S5_SUB, LANES)

    x1, h2 = _mix(ssm_sb, ret_out, x2d, mod, norm2_w[layer].reshape(1, D_MODEL),
                  w_glu_bf, b_glu[layer].reshape(1, D_S5), w_out_bf)
    act, w_down_bf = _ffn_up(h2, w_gate_up[layer], w_down[layer])
    out = _ffn_down(act, x1, mod, final_norm_w.reshape(1, D_MODEL), w_down_bf)
    return out.reshape(BATCH, SEQ, D_MODEL)
```

```python
import math

import numpy as np
import jax
import jax.numpy as jnp
from jax import lax
from jax.experimental import pallas as pl
from jax.experimental.pallas import tpu as pltpu

D_MODEL = 2048
BATCH = 4
SEQ = 2048
TOK = BATCH * SEQ
D_RET = 1024
D_S5 = 1024
RET_HEADS = 4
RET_HEAD_DIM = 256
S5_GROUP = 16
S5_GROUPS = 64
S5_STATE = 64
S5_SUB = 16
S5_ROW = S5_SUB * S5_GROUP
S5_NSUB = SEQ // S5_SUB
S5_COLS = S5_NSUB * BATCH
D_FF = 5632
ROPE_BASE = 10000.0
EPS = 1e-6
LANES = 128

F32 = jnp.float32
BF16 = jnp.bfloat16
HI = lax.Precision.HIGHEST
VMEM_LIMIT = 58 * 1024 * 1024


def _silu(v):
    return v * jax.nn.sigmoid(v)


_SH1, _SC1, _G1, _SH2, _SC2, _G2 = range(6)


def _mod_spec(which, rows_per_step):
    per_b = SEQ // rows_per_step
    return pl.BlockSpec((None, 1, D_MODEL), lambda i: (i // per_b, 0, which))


_INPROJ_TM = 512
_INPROJ_SUB = 256
_INPROJ_NS = _INPROJ_SUB // S5_SUB


_INPROJ_WT = 512
_INPROJ_NW = (4 * D_RET + D_S5) // _INPROJ_WT


def _inproj_kernel(x_ref, sc_ref, sh_ref, nw_ref, w_ref, wout_ref, wglu_ref, lg_ref, gnw_ref,
                   ret_ref, u_ref, wout_bf_ref, wglu_bf_ref,
                   w_scr, h_scr, cos_scr, sin_scr, o_ref, st_scr, intra_scr, dec_scr):
    step = pl.program_id(0)
    C = _INPROJ_SUB
    scale = RET_HEAD_DIM ** -0.5

    @pl.when(step < _INPROJ_NW)
    def _():
        w_scr[step] = w_ref[...].astype(BF16)

        @pl.when(step < SEQ // _INPROJ_TM)
        def _():
            tm = _INPROJ_TM
            half = RET_HEAD_DIM // 2
            pos = (lax.broadcasted_iota(jnp.int32, (tm, half), 0) + step * tm).astype(F32)
            lane = lax.broadcasted_iota(jnp.int32, (tm, half), 1).astype(F32)
            ang = pos * jnp.exp(lane * (-math.log(ROPE_BASE) / half))
            cos_scr[step] = jnp.cos(ang)
            sin_scr[step] = jnp.sin(ang)

        @pl.when(step < RET_HEADS)
        def _():
            lg = lg_ref[step, 0:1, :]
            ii = lax.broadcasted_iota(jnp.int32, (C, C), 0)
            jj = lax.broadcasted_iota(jnp.int32, (C, C), 1)
            diff = (ii - jj).astype(F32)
            intra_scr[step] = jnp.where(
                diff >= 0.0, jnp.exp(lg * jnp.maximum(diff, 0.0)), 0.0) * scale
            row = lax.broadcasted_iota(jnp.int32, (C, LANES), 0).astype(F32)
            lgl = lg[:, 0:LANES]
            dec_scr[0, step] = jnp.exp(lgl * (C - 1.0 - row)) * scale
            dec_scr[1, step] = jnp.exp(lgl * (row + 1.0))

    @pl.when(step >= _INPROJ_NW)
    def _():
        _inproj_rows(step - _INPROJ_NW, x_ref, sc_ref, sh_ref, nw_ref, wout_ref, wglu_ref,
                     lg_ref, gnw_ref, ret_ref, u_ref, wout_bf_ref, wglu_bf_ref,
                     w_scr, h_scr, cos_scr, sin_scr, o_ref, st_scr, intra_scr, dec_scr)


def _inproj_rows(i, x_ref, sc_ref, sh_ref, nw_ref, wout_ref, wglu_ref, lg_ref, gnw_ref,
                 ret_ref, u_ref, wout_bf_ref, wglu_bf_ref,
                 w_scr, h_scr, cos_scr, sin_scr, o_ref, st_scr, intra_scr, dec_scr):
    wout_bf_ref[...] = wout_ref[...].astype(BF16)
    wglu_bf_ref[...] = wglu_ref[...].astype(BF16)
    tm = _INPROJ_TM
    half = RET_HEAD_DIM // 2
    per_b = SEQ // tm
    hpt = _INPROJ_WT // RET_HEAD_DIM
    slot = i % per_b
    C = _INPROJ_SUB
    dh = RET_HEAD_DIM

    @pl.when(slot == 0)
    def _():
        st_scr[...] = jnp.zeros_like(st_scr)

    gain = nw_ref[...] * (1.0 + sc_ref[...])
    shift = sh_ref[...]
    for r in range(tm // _INPROJ_SUB):
        rs = slice(r * _INPROJ_SUB, (r + 1) * _INPROJ_SUB)
        x = x_ref[rs, :]
        ms = jnp.mean(x * x, axis=-1, keepdims=True)
        h_scr[rs, :] = (x * lax.rsqrt(ms + EPS) * gain + shift).astype(BF16)
        cs = cos_scr[slot, rs, :]
        sn = sin_scr[slot, rs, :]
        for t in range(_INPROJ_NW):
            acc = jnp.dot(h_scr[rs, :], w_scr[t], preferred_element_type=F32)
            if t < 2 * RET_HEADS // hpt:
                for hh in range(hpt):
                    c0 = hh * RET_HEAD_DIM
                    x1 = acc[:, c0:c0 + half]
                    x2 = acc[:, c0 + half:c0 + 2 * half]
                    o_ref[t * hpt + hh, rs, 0:half] = (x1 * cs - x2 * sn).astype(BF16)
                    o_ref[t * hpt + hh, rs, half:2 * half] = (x2 * cs + x1 * sn).astype(BF16)
            elif t < 4 * RET_HEADS // hpt:
                for hh in range(hpt):
                    c0 = hh * RET_HEAD_DIM
                    o_ref[t * hpt + hh, rs, :] = acc[:, c0:c0 + RET_HEAD_DIM].astype(BF16)
            else:
                k0 = (t - 4 * RET_HEADS // hpt) * (_INPROJ_WT // LANES)
                for k in range(_INPROJ_WT // LANES):
                    u_ref[k0 + k, r * _INPROJ_NS:(r + 1) * _INPROJ_NS] = \
                        acc[:, k * LANES:(k + 1) * LANES].reshape(_INPROJ_NS, S5_SUB, LANES)

        for hh in range(RET_HEADS):
            q = o_ref[hh, rs, :]
            k = o_ref[RET_HEADS + hh, rs, :]
            v = o_ref[2 * RET_HEADS + hh, rs, :]
            g = o_ref[3 * RET_HEADS + hh, rs, :].astype(F32)
            kdec = jnp.concatenate([dec_scr[0, hh]] * (dh // LANES), axis=1)
            qdec = jnp.concatenate([dec_scr[1, hh]] * (dh // LANES), axis=1)
            st = st_scr[hh]
            s = lax.dot_general(q, k, (((1,), (1,)), ((), ())),
                                preferred_element_type=F32) * intra_scr[hh]
            y = jnp.dot(s.astype(BF16), v, preferred_element_type=F32)
            y = y + jnp.dot(q, st.astype(BF16), preferred_element_type=F32) * qdec
            kd = (k.astype(F32) * kdec).astype(BF16)
            kv = lax.dot_general(kd, v, (((0,), (0,)), ((), ())), preferred_element_type=F32)
            st_scr[hh] = st * jnp.exp(lg_ref[hh, 0:1, :] * float(C)) + kv
            mu = jnp.mean(y, axis=-1, keepdims=True)
            yc = y - mu
            var = jnp.mean(yc * yc, axis=-1, keepdims=True)
            yn = yc * lax.rsqrt(var + EPS) * gnw_ref[:, hh * dh:(hh + 1) * dh]
            ret_ref[hh, rs, :] = (_silu(g) * yn).astype(BF16)


def _inproj(x2d, mod, norm_w, w_in, w_out, w_glu, ret_norm_w):
    tm = _INPROJ_TM
    per_b = SEQ // tm
    half = RET_HEAD_DIM // 2
    dh = RET_HEAD_DIM
    nw = _INPROJ_NW
    steps = TOK // tm
    so = w_out.shape[0] // steps
    sg = w_glu.shape[0] // steps
    assert w_in.shape == (D_MODEL, 4 * D_RET + D_S5) and D_S5 == D_RET
    assert nw >= per_b and nw >= RET_HEADS
    assert so * steps == w_out.shape[0] and sg * steps == w_glu.shape[0] and sg % 16 == 0
    lg = np.log1p(-np.exp2(-5.0 - np.arange(RET_HEADS, dtype=np.float64)))
    lg_tab = jnp.asarray(np.broadcast_to(lg[:, None, None], (RET_HEADS, 8, dh)), F32)
    row = lambda s: jnp.maximum(s - nw, 0)
    slab_o = pl.BlockSpec((so, w_out.shape[1]), lambda s: (row(s), 0))
    slab_g = pl.BlockSpec((sg, w_glu.shape[1]), lambda s: (row(s), 0))
    mod_spec = lambda which: pl.BlockSpec((None, 1, D_MODEL),
                                          lambda s: (row(s) // per_b, 0, which))
    return pl.pallas_call(
        _inproj_kernel,
        grid=(nw + steps,),
        in_specs=[pl.BlockSpec((tm, D_MODEL), lambda s: (row(s), 0)),
                  mod_spec(_SC1), mod_spec(_SH1),
                  pl.BlockSpec((1, D_MODEL), lambda s: (0, 0)),
                  pl.BlockSpec((D_MODEL, _INPROJ_WT), lambda s: (0, jnp.minimum(s, nw - 1))),
                  slab_o, slab_g,
                  pl.BlockSpec((RET_HEADS, 8, dh), lambda s: (0, 0, 0)),
                  pl.BlockSpec((1, D_RET), lambda s: (0, 0))],
        out_specs=[pl.BlockSpec((RET_HEADS, tm, dh), lambda s: (0, row(s), 0)),
                   pl.BlockSpec((D_S5 // LANES, tm // S5_SUB, None, S5_SUB, LANES),
                                lambda s: (0, row(s) % per_b, row(s) // per_b, 0, 0)),
                   slab_o, slab_g],
        out_shape=[jax.ShapeDtypeStruct((RET_HEADS, TOK, dh), BF16),
                   jax.ShapeDtypeStruct((D_S5 // LANES, S5_NSUB, BATCH, S5_SUB, LANES), F32),
                   jax.ShapeDtypeStruct(w_out.shape, BF16),
                   jax.ShapeDtypeStruct(w_glu.shape, BF16)],
        scratch_shapes=[pltpu.VMEM((nw, D_MODEL, _INPROJ_WT), BF16),
                        pltpu.VMEM((tm, D_MODEL), BF16),
                        pltpu.VMEM((per_b, tm, half), F32),
                        pltpu.VMEM((per_b, tm, half), F32),
                        pltpu.VMEM((4 * RET_HEADS, tm, dh), BF16),
                        pltpu.VMEM((RET_HEADS, dh, dh), F32),
                        pltpu.VMEM((RET_HEADS, _INPROJ_SUB, _INPROJ_SUB), F32),
                        pltpu.VMEM((2, RET_HEADS, _INPROJ_SUB, LANES), F32)],
        compiler_params=pltpu.CompilerParams(
            dimension_semantics=("arbitrary",), vmem_limit_bytes=VMEM_LIMIT),
        name="inproj",
    )(x2d, mod, mod, norm_w, w_in, w_out, w_glu, lg_tab, ret_norm_w)


_RET_BLK = 256


def _ret_kernel(lg_ref, q_ref, k_ref, v_ref, g_ref, w_ref, o_ref,
                kv_scr, prev_scr, intra_scr, kdec_scr, qdec_scr):
    C = _RET_BLK
    dh = RET_HEAD_DIM
    nc = SEQ // C
    scale = dh ** -0.5
    lg = lg_ref[0:1, :]

    @pl.when(pl.program_id(1) == 0)
    def _():
        ii = lax.broadcasted_iota(jnp.int32, (C, C), 0)
        jj = lax.broadcasted_iota(jnp.int32, (C, C), 1)
        diff = (ii - jj).astype(F32)
        intra_scr[...] = jnp.where(diff >= 0.0, jnp.exp(lg * jnp.maximum(diff, 0.0)), 0.0) * scale
        row = lax.broadcasted_iota(jnp.int32, (C, dh), 0).astype(F32)
        kdec_scr[...] = jnp.exp(lg * (C - 1.0 - row)) * scale
        qdec_scr[...] = jnp.exp(lg * (row + 1.0))

    block_decay = jnp.exp(lg * float(C))
    gn_w = w_ref[...]

    for n in range(nc):
        rs = slice(n * C, (n + 1) * C)
        kd = (k_ref[rs, :].astype(F32) * kdec_scr[...]).astype(BF16)
        kv_scr[n] = lax.dot_general(kd, v_ref[rs, :], (((0,), (0,)), ((), ())),
                                    preferred_element_type=F32)

    band = 64
    for rb in range(dh // band):
        bs = slice(rb * band, (rb + 1) * band)
        st = jnp.zeros((band, dh), F32)
        for n in range(nc):
            prev_scr[n, bs, :] = st.astype(BF16)
            if n + 1 < nc:
                st = st * block_decay + kv_scr[n, bs, :]

    for n in range(nc):
        rs = slice(n * C, (n + 1) * C)
        q = q_ref[rs, :]
        s = lax.dot_general(q, k_ref[rs, :], (((1,), (1,)), ((), ())),
                            preferred_element_type=F32) * intra_scr[...]
        y = jnp.dot(s.astype(BF16), v_ref[rs, :], preferred_element_type=F32)
        y = y + jnp.dot(q, prev_scr[n], preferred_element_type=F32) * qdec_scr[...]
        mu = jnp.mean(y, axis=-1, keepdims=True)
        yc = y - mu
        var = jnp.mean(yc * yc, axis=-1, keepdims=True)
        yn = yc * lax.rsqrt(var + EPS) * gn_w
        g = g_ref[rs, :].astype(F32)
        o_ref[rs, :] = (_silu(g) * yn).astype(BF16)


def _retention(qkvg, ret_norm_w):
    dh = RET_HEAD_DIM
    nc = SEQ // _RET_BLK
    lg = np.log1p(-np.exp2(-5.0 - np.arange(RET_HEADS, dtype=np.float64)))
    lg_tab = jnp.asarray(np.broadcast_to(lg[:, None, None], (RET_HEADS, 8, dh)), F32)
    spec = lambda off: pl.BlockSpec((None, SEQ, dh), lambda h, b: (off + h, b, 0))
    return pl.pallas_call(
        _ret_kernel,
        grid=(RET_HEADS, BATCH),
        in_specs=[pl.BlockSpec((None, 8, dh), lambda h, b: (h, 0, 0)),
                  spec(0), spec(RET_HEADS), spec(2 * RET_HEADS), spec(3 * RET_HEADS),
                  pl.BlockSpec((1, dh), lambda h, b: (0, h))],
        out_specs=pl.BlockSpec((None, SEQ, dh), lambda h, b: (h, b, 0)),
        out_shape=jax.ShapeDtypeStruct((RET_HEADS, TOK, dh), BF16),
        scratch_shapes=[pltpu.VMEM((nc, dh, dh), F32),
                        pltpu.VMEM((nc, dh, dh), BF16),
                        pltpu.VMEM((_RET_BLK, _RET_BLK), F32),
                        pltpu.VMEM((_RET_BLK, dh), F32),
                        pltpu.VMEM((_RET_BLK, dh), F32)],
        compiler_params=pltpu.CompilerParams(
            dimension_semantics=("arbitrary", "arbitrary"), vmem_limit_bytes=VMEM_LIMIT),
        name="retention",
    )(lg_tab, qkvg, qkvg, qkvg, qkvg, ret_norm_w)


_S5GEN_GG = 4


def _prep_kernel(c_ref, wada_ref, bada_ref,
                 ls_ref, ar_ref, ai_ref, bt_ref, cr_ref, ci_ref, cc_ref, d_ref,
                 mod_ref, mt_ref, wt_ref, vt_ref, a16r_ref, a16i_ref):
    cond = _silu(c_ref[...])
    mod = jnp.dot(cond.astype(BF16), wada_ref[...].astype(BF16),
                  preferred_element_type=F32) + bada_ref[...]
    for b in range(BATCH):
        mod_ref[b] = mod[b:b + 1, :]
    N, P, R = S5_STATE, S5_GROUP, S5_ROW
    re_half = lax.broadcasted_iota(jnp.int32, (P, 2 * N), 1) < N
    sub_e = lax.broadcasted_iota(jnp.int32, (P, 2 * N), 0)
    lane_p = lax.broadcasted_iota(jnp.int32, (P, R), 1)
    sub_p = lax.broadcasted_iota(jnp.int32, (P, R), 0)
    lag0_diag = lane_p == (R - P) + sub_p

    def hdot(a, b):
        return jnp.dot(a, b, precision=HI, preferred_element_type=F32)

    def cmul(xr, xi, yr, yi):
        return xr * yr - xi * yi, xr * yi + xi * yr

    for k in range(_S5GEN_GG):
        dt = jnp.exp(ls_ref[k])
        ar = ar_ref[k]
        ai = ai_ref[k]
        mag = jnp.exp(ar * dt)
        lbr = mag * jnp.cos(ai * dt)
        lbi = mag * jnp.sin(ai * dt)
        den = ar * ar + ai * ai
        zr = ((lbr - 1.0) * ar + lbi * ai) / den
        zi = (lbi * ar - (lbr - 1.0) * ai) / den

        pw_r = jnp.ones((P, 2 * N), F32)
        pw_i = jnp.zeros((P, 2 * N), F32)
        sq_r, sq_i = lbr, lbi
        for bit in range(4):
            nr, ni = cmul(pw_r, pw_i, sq_r, sq_i)
            take = (lax.shift_right_logical(sub_e, bit) & 1) == 1
            pw_r = jnp.where(take, nr, pw_r)
            pw_i = jnp.where(take, ni, pw_i)
            if bit < 3:
                sq_r, sq_i = cmul(sq_r, sq_i, sq_r, sq_i)
        p1_r, p1_i = cmul(pw_r, pw_i, lbr, lbi)
        a16r_ref[k] = p1_r[S5_SUB - 1:S5_SUB, :]
        a16i_ref[k] = p1_i[S5_SUB - 1:S5_SUB, :]

        bt = bt_ref[k]
        x = zr * bt + jnp.where(re_half[0:1], -zi, zi) * pltpu.roll(bt, N, axis=1)
        xs = pltpu.roll(x, N, axis=1)
        pw_is = jnp.where(re_half, -pw_i, pw_i)
        w_rows = jnp.concatenate(
            [pw_r[e:e + 1] * x + pw_is[e:e + 1] * xs for e in range(S5_SUB - 1, -1, -1)], axis=0)
        w_t = w_rows.T
        w_r = w_t[0:N]
        w_i = w_t[N:2 * N]
        wt_ref[k] = w_t.astype(BF16)

        c_re = cr_ref[k]
        c_im = ci_ref[k]
        krev = hdot(c_re, w_r) - hdot(c_im, w_i)
        krev = krev + jnp.where(lag0_diag, d_ref[k], 0.0)
        for tp in range(S5_SUB):
            width = (tp + 1) * P
            piece = krev if width == R else jnp.where(
                lane_p < width, pltpu.roll(krev, width, axis=1), 0.0)
            mt_ref[k, tp * P:(tp + 1) * P, :] = piece.astype(BF16)

        cc = cc_ref[k]
        ccs = pltpu.roll(cc, N, axis=1)
        p1_rs = jnp.where(re_half, p1_r, -p1_r)
        for tp in range(S5_SUB):
            val = cc * p1_rs[tp:tp + 1] - ccs * p1_i[tp:tp + 1]
            vt_ref[k, tp * P:(tp + 1) * P, :] = val.astype(BF16)


def _prep(c, w_ada, b_ada, a_re, a_im, log_step, b_re, b_im, c_re, c_im, d_skip):
    G, N, P, R = S5_GROUPS, S5_STATE, S5_GROUP, S5_ROW
    gg = _S5GEN_GG
    steps = G // gg
    n_mod = w_ada.shape[1]
    tn = n_mod // steps
    assert tn * steps == n_mod and tn % LANES == 0
    dup = lambda a: jnp.concatenate([a, a], axis=-1).reshape(G, 1, 2 * N)
    bt = jnp.concatenate([jnp.swapaxes(b_re, 1, 2), jnp.swapaxes(b_im, 1, 2)], axis=-1)
    cc = jnp.concatenate([c_re, c_im], axis=-1)
    blk = lambda s: pl.BlockSpec((gg,) + s, lambda g: (g, 0, 0))
    return pl.pallas_call(
        _prep_kernel,
        grid=(steps,),
        in_specs=[pl.BlockSpec((BATCH, D_MODEL), lambda g: (0, 0)),
                  pl.BlockSpec((D_MODEL, tn), lambda g: (0, g)),
                  pl.BlockSpec((1, tn), lambda g: (0, g)),
                  blk((1, 1)), blk((1, 2 * N)), blk((1, 2 * N)), blk((P, 2 * N)),
                  blk((P, N)), blk((P, N)), blk((P, 2 * N)), blk((P, 1))],
        out_specs=[pl.BlockSpec((BATCH, 1, tn), lambda g: (0, 0, g)),
                   blk((R, R)), blk((2 * N, R)), blk((R, 2 * N)), blk((1, 2 * N)), blk((1, 2 * N))],
        out_shape=[jax.ShapeDtypeStruct((BATCH, 1, n_mod), F32),
                   jax.ShapeDtypeStruct((G, R, R), BF16),
                   jax.ShapeDtypeStruct((G, 2 * N, R), BF16),
                   jax.ShapeDtypeStruct((G, R, 2 * N), BF16)] +
                  [jax.ShapeDtypeStruct((G, 1, 2 * N), F32)] * 2,
        compiler_params=pltpu.CompilerParams(
            dimension_semantics=("parallel",), vmem_limit_bytes=VMEM_LIMIT,
            allow_input_fusion=[False, False, False, True, True, True, True, False, False, True,
                                True]),
        name="prep",
    )(c, w_ada, b_ada.reshape(1, n_mod),
      log_step.reshape(G, 1, 1), dup(a_re), dup(a_im), bt, c_re, c_im, cc,
      d_skip.reshape(G, P, 1))


_S5_GB = LANES // S5_GROUP
_S5_NP = _S5_GB // 2
_S5_NSTEP = S5_GROUPS // _S5_GB


def _s5_kernel(a16r_ref, a16i_ref, u_hbm, mt_ref, wt_ref, vt_ref, y_hbm,
               ubuf, ybuf, in_sem, out_sem, ut_scr, yt_scr, e_scr, p_scr):
    P = S5_GROUP
    N2 = 2 * S5_STATE
    cols = S5_COLS
    nblk = cols // LANES
    nt = cols // 8
    step = pl.program_id(0)
    nstep = pl.num_programs(0)
    slot = step % 2

    def in_copy(slab, sl, tau):
        return pltpu.make_async_copy(u_hbm.at[slab, :, tau, :], ubuf.at[sl, tau], in_sem.at[sl])

    def out_copy(slab, sl, tau):
        return pltpu.make_async_copy(ybuf.at[sl, tau], y_hbm.at[slab, :, tau, :], out_sem.at[sl])

    @pl.when(step == 0)
    def _():
        for tau in range(S5_SUB):
            in_copy(0, 0, tau).start()

    @pl.when(step + 1 < nstep)
    def _():
        for tau in range(S5_SUB):
            in_copy(step + 1, 1 - slot, tau).start()

    for tau in range(S5_SUB):
        in_copy(step, slot, tau).wait()

    def load_body(tau, carry):
        for c in range(nblk):
            xt = ubuf[slot, tau, c * LANES:(c + 1) * LANES, :].T
            r0 = pl.multiple_of(tau * P, P)
            for k in range(_S5_GB):
                ut_scr[k, pl.ds(r0, P), c * LANES:(c + 1) * LANES] = \
                    xt[k * P:(k + 1) * P, :].astype(BF16)
        return carry

    lax.fori_loop(0, S5_SUB, load_body, 0, unroll=4)

    NS = S5_STATE
    for pr in range(_S5_NP):
        e0 = jnp.dot(wt_ref[2 * pr], ut_scr[2 * pr], preferred_element_type=F32)
        e1 = jnp.dot(wt_ref[2 * pr + 1], ut_scr[2 * pr + 1], preferred_element_type=F32)
        et = jnp.concatenate([e0[0:NS], e1[0:NS], e0[NS:2 * NS], e1[NS:2 * NS]], axis=0)
        e_scr[pr] = et.T

    sub = lax.broadcasted_iota(jnp.int32, (8, N2), 0)
    lo = sub < BATCH
    first = lax.broadcasted_iota(jnp.int32, (1, N2), 1) < S5_STATE
    a_r, a_i = [], []
    for pr in range(_S5_NP):
        a_r.append(jnp.broadcast_to(
            jnp.where(first, a16r_ref[2 * pr], a16r_ref[2 * pr + 1]), (8, N2)))
        a_i.append(jnp.broadcast_to(
            jnp.where(first, a16i_ref[2 * pr], a16i_ref[2 * pr + 1]), (8, N2)))

    def scan_body(t, carry):
        r0 = pl.multiple_of(t * 8, 8)
        new = []
        for pr in range(_S5_NP):
            s_r, s_i = carry[2 * pr], carry[2 * pr + 1]
            e_r = e_scr[pr, pl.ds(r0, 8), 0:N2]
            e_i = e_scr[pr, pl.ds(r0, 8), N2:2 * N2]
            x_r = pltpu.roll(e_r, BATCH, axis=0)
            x_i = pltpu.roll(e_i, BATCH, axis=0)
            elo_r = jnp.where(lo, e_r, x_r)
            elo_i = jnp.where(lo, e_i, x_i)
            ehi_r = jnp.where(lo, x_r, e_r)
            ehi_i = jnp.where(lo, x_i, e_i)
            t_r = a_r[pr] * s_r - a_i[pr] * s_i + elo_r
            t_i = a_r[pr] * s_i + a_i[pr] * s_r + elo_i
            p_scr[pr, pl.ds(r0, 8), 0:N2] = jnp.where(lo, s_r, t_r)
            p_scr[pr, pl.ds(r0, 8), N2:2 * N2] = jnp.where(lo, s_i, t_i)
            new.append(a_r[pr] * t_r - a_i[pr] * t_i + ehi_r)
            new.append(a_r[pr] * t_i + a_i[pr] * t_r + ehi_i)
        return tuple(new)

    zero = jnp.zeros((8, N2), F32)
    lax.fori_loop(0, nt, scan_body, tuple(zero for _ in range(2 * _S5_NP)), unroll=2)

    for pr in range(_S5_NP):
        pt = p_scr[pr].astype(BF16).T
        for k in range(2):
            gi = 2 * pr + k
            pk = jnp.concatenate([pt[k * NS:(k + 1) * NS],
                                  pt[(2 + k) * NS:(3 + k) * NS]], axis=0)
            yt = (jnp.dot(mt_ref[gi], ut_scr[gi], preferred_element_type=F32) +
                  jnp.dot(vt_ref[gi], pk, preferred_element_type=F32))
            for tau in range(S5_SUB):
                yt_scr[tau, gi * P:(gi + 1) * P, :] = yt[tau * P:(tau + 1) * P, :]

    @pl.when(step >= 2)
    def _():
        for tau in range(S5_SUB):
            out_copy(step - 2, slot, tau).wait()

    def store_body(tau, carry):
        ybuf[slot, tau] = yt_scr[tau].T
        return carry

    lax.fori_loop(0, S5_SUB, store_body, 0, unroll=4)

    for tau in range(S5_SUB):
        out_copy(step, slot, tau).start()

    @pl.when(step == nstep - 1)
    def _():
        if _S5_NSTEP > 1:
            for tau in range(S5_SUB):
                out_copy(step - 1, 1 - slot, tau).wait()
        for tau in range(S5_SUB):
            out_copy(step, slot, tau).wait()


def _s5(u_rows, mt, wt, vt, a16r, a16i):
    G, N, R = S5_GROUPS, S5_STATE, S5_ROW
    gb = _S5_GB
    vec = pl.BlockSpec((gb, 1, 2 * N), lambda i: (i, 0, 0))
    mat = lambda r, c: pl.BlockSpec((gb, r, c), lambda i: (i, 0, 0))
    hbm = pl.BlockSpec(memory_space=pl.ANY)
    plane = pltpu.VMEM((2, S5_SUB, S5_COLS, LANES), F32)
    return pl.pallas_call(
        _s5_kernel,
        grid=(_S5_NSTEP,),
        in_specs=[vec, vec, hbm, mat(R, R), mat(2 * N, R), mat(R, 2 * N)],
        out_specs=hbm,
        out_shape=jax.ShapeDtypeStruct((D_S5 // LANES, S5_COLS, S5_SUB, LANES), F32),
        scratch_shapes=[plane, plane,
                        pltpu.SemaphoreType.DMA((2,)), pltpu.SemaphoreType.DMA((2,)),
                        pltpu.VMEM((gb, R, S5_COLS), BF16),
                        pltpu.VMEM((S5_SUB, LANES, S5_COLS), F32),
                        pltpu.VMEM((_S5_NP, S5_COLS, 4 * N), F32),
                        pltpu.VMEM((_S5_NP, S5_COLS, 4 * N), F32)],
        compiler_params=pltpu.CompilerParams(
            dimension_semantics=("arbitrary",), vmem_limit_bytes=VMEM_LIMIT),
        name="s5",
    )(a16r, a16i, u_rows, mt, wt, vt)


_MIX_TM = 512
_MIX_SUB = 256
_MIX_NS = _MIX_TM // S5_SUB


def _mix_kernel(ssm_ref, ret_ref, x_ref, g1_ref, sc_ref, sh_ref, nw_ref, wglu_ref, bglu_ref,
                wout_ref, o_ref, h_ref):
    gain = nw_ref[...] * (1.0 + sc_ref[...])
    nsub = _MIX_TM // _MIX_SUB
    ns = _MIX_SUB // S5_SUB
    gated = []
    for r in range(nsub):
        s = jnp.concatenate([ssm_ref[k, r * ns:(r + 1) * ns].reshape(_MIX_SUB, LANES)
                             for k in range(D_S5 // LANES)], axis=1)
        cdf = 0.5 * (1.0 + jnp.tanh(math.sqrt(2.0 / math.pi) * (s + 0.044715 * (s * s * s))))
        sg = s * cdf
        z = jnp.dot(sg.astype(BF16), wglu_ref[...], preferred_element_type=F32) + bglu_ref[...]
        gated.append((sg * jax.nn.sigmoid(z)).astype(BF16))
    for r in range(nsub):
        rs = slice(r * _MIX_SUB, (r + 1) * _MIX_SUB)
        both = jnp.concatenate([ret_ref[hh, rs, :] for hh in range(RET_HEADS)] + [gated[r]],
                               axis=1)
        mix = jnp.dot(both, wout_ref[...], preferred_element_type=F32)
        x1 = x_ref[rs, :] + g1_ref[...] * mix
        o_ref[rs, :] = x1
        ms = jnp.mean(x1 * x1, axis=-1, keepdims=True)
        h_ref[rs, :] = (x1 * lax.rsqrt(ms + EPS) * gain + sh_ref[...]).astype(BF16)


def _mix(ssm_sb, ret_out, x2d, mod, norm_w, w_glu_bf, b_glu, w_out_bf):
    tm = _MIX_TM
    per_b = SEQ // tm
    rows = pl.BlockSpec((tm, D_MODEL), lambda i: (i, 0))
    return pl.pallas_call(
        _mix_kernel,
        grid=(TOK // tm,),
        in_specs=[pl.BlockSpec((D_S5 // LANES, _MIX_NS, None, S5_SUB, LANES),
                               lambda i: (0, i % per_b, i // per_b, 0, 0)),
                  pl.BlockSpec((RET_HEADS, tm, RET_HEAD_DIM), lambda i: (0, i, 0)),
                  rows, _mod_spec(_G1, tm), _mod_spec(_SC2, tm), _mod_spec(_SH2, tm),
                  pl.BlockSpec((1, D_MODEL), lambda i: (0, 0)),
                  pl.BlockSpec((D_S5, D_S5), lambda i: (0, 0)),
                  pl.BlockSpec((1, D_S5), lambda i: (0, 0)),
                  pl.BlockSpec((D_RET + D_S5, D_MODEL), lambda i: (0, 0))],
        out_specs=[rows, rows],
        out_shape=[jax.ShapeDtypeStruct((TOK, D_MODEL), F32),
                   jax.ShapeDtypeStruct((TOK, D_MODEL), BF16)],
        compiler_params=pltpu.CompilerParams(
            dimension_semantics=("parallel",), vmem_limit_bytes=VMEM_LIMIT),
        name="mix",
    )(ssm_sb, ret_out, x2d, mod, mod, mod, norm_w, w_glu_bf, b_glu, w_out_bf)


_FFN_UP_TM = 2048
_FFN_UP_SUB = 1024
_FFN_UP_TF = 512
_FFN_DOWN_TM = 512
_FFN_DOWN_SUB = 256


def _ffn_up_kernel(h_ref, wg_ref, wu_ref, wd_ref, o_ref, wd_bf_ref, wg_scr, wu_scr):
    @pl.when(pl.program_id(1) == 0)
    def _():
        wg_scr[...] = wg_ref[...].astype(BF16)
        wu_scr[...] = wu_ref[...].astype(BF16)

    wd_bf_ref[...] = wd_ref[...].astype(BF16)
    for r in range(_FFN_UP_TM // _FFN_UP_SUB):
        rs = slice(r * _FFN_UP_SUB, (r + 1) * _FFN_UP_SUB)
        h = h_ref[rs, :]
        gate = jnp.dot(h, wg_scr[...], preferred_element_type=F32)
        up = jnp.dot(h, wu_scr[...], preferred_element_type=F32)
        o_ref[rs, :] = (_silu(gate) * up).astype(BF16)


def _ffn_up(h2, w_gate_up, w_down):
    tm, tf = _FFN_UP_TM, _FFN_UP_TF
    nf = D_FF // tf
    nm = TOK // tm
    slab = D_FF // (nf * nm)
    assert slab * nf * nm == D_FF and slab % 16 == 0
    return pl.pallas_call(
        _ffn_up_kernel,
        grid=(nf, nm),
        in_specs=[pl.BlockSpec((tm, D_MODEL), lambda f, i: (i, 0)),
                  pl.BlockSpec((D_MODEL, tf), lambda f, i: (0, f)),
                  pl.BlockSpec((D_MODEL, tf), lambda f, i: (0, nf + f)),
                  pl.BlockSpec((slab, D_MODEL), lambda f, i: (f * nm + i, 0))],
        out_specs=[pl.BlockSpec((tm, tf), lambda f, i: (i, f)),
                   pl.BlockSpec((slab, D_MODEL), lambda f, i: (f * nm + i, 0))],
        out_shape=[jax.ShapeDtypeStruct((TOK, D_FF), BF16),
                   jax.ShapeDtypeStruct((D_FF, D_MODEL), BF16)],
        scratch_shapes=[pltpu.VMEM((D_MODEL, tf), BF16),
                        pltpu.VMEM((D_MODEL, tf), BF16)],
        compiler_params=pltpu.CompilerParams(
            dimension_semantics=("arbitrary", "arbitrary"), vmem_limit_bytes=VMEM_LIMIT),
        name="ffn_up",
    )(h2, w_gate_up, w_gate_up, w_down)


def _ffn_down_kernel(a_ref, x_ref, g2_ref, fw_ref, wd_ref, o_ref):
    for r in range(_FFN_DOWN_TM // _FFN_DOWN_SUB):
        rs = slice(r * _FFN_DOWN_SUB, (r + 1) * _FFN_DOWN_SUB)
        down = jnp.dot(a_ref[rs, :], wd_ref[...], preferred_element_type=F32)
        x2 = x_ref[rs, :] + g2_ref[...] * down
        ms = jnp.mean(x2 * x2, axis=-1, keepdims=True)
        o_ref[rs, :] = x2 * lax.rsqrt(ms + EPS) * fw_ref[...]


def _ffn_down(act, x1, mod, final_w, w_down_bf):
    tm = _FFN_DOWN_TM
    rows = pl.BlockSpec((tm, D_MODEL), lambda i: (i, 0))
    return pl.pallas_call(
        _ffn_down_kernel,
        grid=(TOK // tm,),
        in_specs=[pl.BlockSpec((tm, D_FF), lambda i: (i, 0)),
                  rows,
                  _mod_spec(_G2, tm),
                  pl.BlockSpec((1, D_MODEL), lambda i: (0, 0)),
                  pl.BlockSpec((D_FF, D_MODEL), lambda i: (0, 0))],
        out_specs=rows,
        out_shape=jax.ShapeDtypeStruct((TOK, D_MODEL), F32),
        compiler_params=pltpu.CompilerParams(
            dimension_semantics=("parallel",), vmem_limit_bytes=VMEM_LIMIT),
        name="ffn_down",
    )(act, x1, mod, final_w, w_down_bf)


def kernel(x, c, w_ada, b_ada, norm1_w, w_in, ret_norm_w, s5_a_re, s5_a_im, s5_log_step,
           s5_b_re, s5_b_im, s5_c_re, s5_c_im, s5_d, w_glu, b_glu, w_out, norm2_w,
           w_gate_up, w_down, final_norm_w):
    x2d = x.reshape(TOK, D_MODEL)
    layer = 0
    mod, mt, wt, vt, a16r, a16i = _prep(
        c, w_ada[layer], b_ada[layer],
        s5_a_re[layer], s5_a_im[layer], s5_log_step[layer], s5_b_re[layer], s5_b_im[layer],
        s5_c_re[layer], s5_c_im[layer], s5_d[layer])

    ret_out, u_sb, w_out_bf, w_glu_bf = _inproj(x2d, mod, norm1_w[layer].reshape(1, D_MODEL),
                                                w_in[layer], w_out[layer], w_glu[layer],
                                                ret_norm_w[layer].reshape(1, D_RET))
    y_rows = _s5(u_sb.reshape(D_S5 // LANES, S5_COLS, S5_SUB, LANES), mt, wt, vt, a16r, a16i)
    ssm_sb = y_rows.reshape(D_S5 // LANES, S5_NSUB, BATCH, S5_SUB, LANES)

    x1, h2 = _mix(ssm_sb, ret_out, x2d, mod, norm2_w[layer].reshape(1, D_MODEL),
                  w_glu_bf, b_glu[layer].reshape(1, D_S5), w_out_bf)
    act, w_down_bf = _ffn_up(h2, w_gate_up[layer], w_down[layer])
    out = _ffn_down(act, x1, mod, final_norm_w.reshape(1, D_MODEL), w_down_bf)
    return out.reshape(BATCH, SEQ, D_MODEL)
```

```python
import math

import numpy as np
import jax
import jax.numpy as jnp
from jax import lax
from jax.experimental import pallas as pl
from jax.experimental.pallas import tpu as pltpu

D_MODEL = 2048
BATCH = 4
SEQ = 2048
TOK = BATCH * SEQ
D_RET = 1024
D_S5 = 1024
RET_HEADS = 4
RET_HEAD_DIM = 256
S5_GROUP = 16
S5_GROUPS = 64
S5_STATE = 64
S5_SUB = 16
S5_ROW = S5_SUB * S5_GROUP
S5_NSUB = SEQ // S5_SUB
S5_COLS = S5_NSUB * BATCH
D_FF = 5632
ROPE_BASE = 10000.0
EPS = 1e-6
LANES = 128

F32 = jnp.float32
BF16 = jnp.bfloat16
HI = lax.Precision.HIGHEST
VMEM_LIMIT = 58 * 1024 * 1024


def _silu(v):
    return v * jax.nn.sigmoid(v)


_SH1, _SC1, _G1, _SH2, _SC2, _G2 = range(6)


def _mod_spec(which, rows_per_step):
    per_b = SEQ // rows_per_step
    return pl.BlockSpec((None, 1, D_MODEL), lambda i: (i // per_b, 0, which))


_INPROJ_TM = 512
_INPROJ_SUB = 256
_INPROJ_NS = _INPROJ_SUB // S5_SUB


_INPROJ_WT = 512
_INPROJ_NW = (4 * D_RET + D_S5) // _INPROJ_WT


def _inproj_kernel(x_ref, sc_ref, sh_ref, nw_ref, w_ref, wout_ref, wglu_ref, lg_ref, gnw_ref,
                   ret_ref, u_ref, wout_bf_ref, wglu_bf_ref,
                   w_scr, h_scr, cos_scr, sin_scr, o_ref, st_scr, intra_scr, dec_scr):
    step = pl.program_id(0)
    C = _INPROJ_SUB
    scale = RET_HEAD_DIM ** -0.5

    @pl.when(step < _INPROJ_NW)
    def _():
        w_scr[step] = w_ref[...].astype(BF16)

        @pl.when(step < SEQ // _INPROJ_TM)
        def _():
            tm = _INPROJ_TM
            half = RET_HEAD_DIM // 2
            pos = (lax.broadcasted_iota(jnp.int32, (tm, half), 0) + step * tm).astype(F32)
            lane = lax.broadcasted_iota(jnp.int32, (tm, half), 1).astype(F32)
            ang = pos * jnp.exp(lane * (-math.log(ROPE_BASE) / half))
            cos_scr[step] = jnp.cos(ang)
            sin_scr[step] = jnp.sin(ang)

        @pl.when(step < RET_HEADS)
        def _():
            lg = lg_ref[step, 0:1, :]
            ii = lax.broadcasted_iota(jnp.int32, (C, C), 0)
            jj = lax.broadcasted_iota(jnp.int32, (C, C), 1)
            diff = (ii - jj).astype(F32)
            intra_scr[step] = jnp.where(
                diff >= 0.0, jnp.exp(lg * jnp.maximum(diff, 0.0)), 0.0) * scale
            row = lax.broadcasted_iota(jnp.int32, (C, LANES), 0).astype(F32)
            lgl = lg[:, 0:LANES]
            dec_scr[0, step] = jnp.exp(lgl * (C - 1.0 - row)) * scale
            dec_scr[1, step] = jnp.exp(lgl * (row + 1.0))

    @pl.when(step >= _INPROJ_NW)
    def _():
        _inproj_rows(step - _INPROJ_NW, x_ref, sc_ref, sh_ref, nw_ref, wout_ref, wglu_ref,
                     lg_ref, gnw_ref, ret_ref, u_ref, wout_bf_ref, wglu_bf_ref,
                     w_scr, h_scr, cos_scr, sin_scr, o_ref, st_scr, intra_scr, dec_scr)


def _inproj_rows(i, x_ref, sc_ref, sh_ref, nw_ref, wout_ref, wglu_ref, lg_ref, gnw_ref,
                 ret_ref, u_ref, wout_bf_ref, wglu_bf_ref,
                 w_scr, h_scr, cos_scr, sin_scr, o_ref, st_scr, intra_scr, dec_scr):
    wout_bf_ref[...] = wout_ref[...].astype(BF16)
    wglu_bf_ref[...] = wglu_ref[...].astype(BF16)
    tm = _INPROJ_TM
    half = RET_HEAD_DIM // 2
    per_b = SEQ // tm
    hpt = _INPROJ_WT // RET_HEAD_DIM
    slot = i % per_b
    C = _INPROJ_SUB
    dh = RET_HEAD_DIM

    @pl.when(slot == 0)
    def _():
        st_scr[...] = jnp.zeros_like(st_scr)

    gain = nw_ref[...] * (1.0 + sc_ref[...])
    shift = sh_ref[...]
    for r in range(tm // _INPROJ_SUB):
        rs = slice(r * _INPROJ_SUB, (r + 1) * _INPROJ_SUB)
        x = x_ref[rs, :]
        ms = jnp.mean(x * x, axis=-1, keepdims=True)
        h_scr[rs, :] = (x * lax.rsqrt(ms + EPS) * gain + shift).astype(BF16)
        cs = cos_scr[slot, rs, :]
        sn = sin_scr[slot, rs, :]
        for t in range(_INPROJ_NW):
            acc = jnp.dot(h_scr[rs, :], w_scr[t], preferred_element_type=F32)
            if t < 2 * RET_HEADS // hpt:
                for hh in range(hpt):
                    c0 = hh * RET_HEAD_DIM
                    x1 = acc[:, c0:c0 + half]
                    x2 = acc[:, c0 + half:c0 + 2 * half]
                    o_ref[t * hpt + hh, rs, 0:half] = (x1 * cs - x2 * sn).astype(BF16)
                    o_ref[t * hpt + hh, rs, half:2 * half] = (x2 * cs + x1 * sn).astype(BF16)
            elif t < 4 * RET_HEADS // hpt:
                for hh in range(hpt):
                    c0 = hh * RET_HEAD_DIM
                    o_ref[t * hpt + hh, rs, :] = acc[:, c0:c0 + RET_HEAD_DIM].astype(BF16)
            else:
                k0 = (t - 4 * RET_HEADS // hpt) * (_INPROJ_WT // LANES)
                for k in range(_INPROJ_WT // LANES):
                    u_ref[k0 + k, r * _INPROJ_NS:(r + 1) * _INPROJ_NS] = \
                        acc[:, k * LANES:(k + 1) * LANES].reshape(_INPROJ_NS, S5_SUB, LANES)

        for hh in range(RET_HEADS):
            q = o_ref[hh, rs, :]
            k = o_ref[RET_HEADS + hh, rs, :]
            v = o_ref[2 * RET_HEADS + hh, rs, :]
            g = o_ref[3 * RET_HEADS + hh, rs, :].astype(F32)
            kdec = jnp.concatenate([dec_scr[0, hh]] * (dh // LANES), axis=1)
            qdec = jnp.concatenate([dec_scr[1, hh]] * (dh // LANES), axis=1)
            st = st_scr[hh]
            s = lax.dot_general(q, k, (((1,), (1,)), ((), ())),
                                preferred_element_type=F32) * intra_scr[hh]
            y = jnp.dot(s.astype(BF16), v, preferred_element_type=F32)
            y = y + jnp.dot(q, st.astype(BF16), preferred_element_type=F32) * qdec
            kd = (k.astype(F32) * kdec).astype(BF16)
            kv = lax.dot_general(kd, v, (((0,), (0,)), ((), ())), preferred_element_type=F32)
            st_scr[hh] = st * jnp.exp(lg_ref[hh, 0:1, :] * float(C)) + kv
            mu = jnp.mean(y, axis=-1, keepdims=True)
            yc = y - mu
            var = jnp.mean(yc * yc, axis=-1, keepdims=True)
            yn = yc * lax.rsqrt(var + EPS) * gnw_ref[:, hh * dh:(hh + 1) * dh]
            ret_ref[hh, rs, :] = (_silu(g) * yn).astype(BF16)


def _inproj(x2d, mod, norm_w, w_in, w_out, w_glu, ret_norm_w):
    tm = _INPROJ_TM
    per_b = SEQ // tm
    half = RET_HEAD_DIM // 2
    dh = RET_HEAD_DIM
    nw = _INPROJ_NW
    steps = TOK // tm
    so = w_out.shape[0] // steps
    sg = w_glu.shape[0] // steps
    assert w_in.shape == (D_MODEL, 4 * D_RET + D_S5) and D_S5 == D_RET
    assert nw >= per_b and nw >= RET_HEADS
    assert so * steps == w_out.shape[0] and sg * steps == w_glu.shape[0] and sg % 16 == 0
    lg = np.log1p(-np.exp2(-5.0 - np.arange(RET_HEADS, dtype=np.float64)))
    lg_tab = jnp.asarray(np.broadcast_to(lg[:, None, None], (RET_HEADS, 8, dh)), F32)
    row = lambda s: jnp.maximum(s - nw, 0)
    slab_o = pl.BlockSpec((so, w_out.shape[1]), lambda s: (row(s), 0))
    slab_g = pl.BlockSpec((sg, w_glu.shape[1]), lambda s: (row(s), 0))
    mod_spec = lambda which: pl.BlockSpec((None, 1, D_MODEL),
                                          lambda s: (row(s) // per_b, 0, which))
    return pl.pallas_call(
        _inproj_kernel,
        grid=(nw + steps,),
        in_specs=[pl.BlockSpec((tm, D_MODEL), lambda s: (row(s), 0)),
                  mod_spec(_SC1), mod_spec(_SH1),
                  pl.BlockSpec((1, D_MODEL), lambda s: (0, 0)),
                  pl.BlockSpec((D_MODEL, _INPROJ_WT), lambda s: (0, jnp.minimum(s, nw - 1))),
                  slab_o, slab_g,
                  pl.BlockSpec((RET_HEADS, 8, dh), lambda s: (0, 0, 0)),
                  pl.BlockSpec((1, D_RET), lambda s: (0, 0))],
        out_specs=[pl.BlockSpec((RET_HEADS, tm, dh), lambda s: (0, row(s), 0)),
                   pl.BlockSpec((D_S5 // LANES, tm // S5_SUB, None, S5_SUB, LANES),
                                lambda s: (0, row(s) % per_b, row(s) // per_b, 0, 0)),
                   slab_o, slab_g],
        out_shape=[jax.ShapeDtypeStruct((RET_HEADS, TOK, dh), BF16),
                   jax.ShapeDtypeStruct((D_S5 // LANES, S5_NSUB, BATCH, S5_SUB, LANES), F32),
                   jax.ShapeDtypeStruct(w_out.shape, BF16),
                   jax.ShapeDtypeStruct(w_glu.shape, BF16)],
        scratch_shapes=[pltpu.VMEM((nw, D_MODEL, _INPROJ_WT), BF16),
                        pltpu.VMEM((tm, D_MODEL), BF16),
                        pltpu.VMEM((per_b, tm, half), F32),
                        pltpu.VMEM((per_b, tm, half), F32),
                        pltpu.VMEM((4 * RET_HEADS, tm, dh), BF16),
                        pltpu.VMEM((RET_HEADS, dh, dh), F32),
                        pltpu.VMEM((RET_HEADS, _INPROJ_SUB, _INPROJ_SUB), F32),
                        pltpu.VMEM((2, RET_HEADS, _INPROJ_SUB, LANES), F32)],
        compiler_params=pltpu.CompilerParams(
            dimension_semantics=("arbitrary",), vmem_limit_bytes=VMEM_LIMIT),
        name="inproj",
    )(x2d, mod, mod, norm_w, w_in, w_out, w_glu, lg_tab, ret_norm_w)


_S5GEN_GG = 4


def _prep_kernel(c_ref, wada_ref, bada_ref,
                 ls_ref, ar_ref, ai_ref, bt_ref, cr_ref, ci_ref, cc_ref, d_ref,
                 mod_ref, mt_ref, wt_ref, vt_ref, a16r_ref, a16i_ref):
    cond = _silu(c_ref[...])
    mod = jnp.dot(cond.astype(BF16), wada_ref[...].astype(BF16),
                  preferred_element_type=F32) + bada_ref[...]
    for b in range(BATCH):
        mod_ref[b] = mod[b:b + 1, :]
    N, P, R = S5_STATE, S5_GROUP, S5_ROW
    re_half = lax.broadcasted_iota(jnp.int32, (P, 2 * N), 1) < N
    sub_e = lax.broadcasted_iota(jnp.int32, (P, 2 * N), 0)
    lane_p = lax.broadcasted_iota(jnp.int32, (P, R), 1)
    sub_p = lax.broadcasted_iota(jnp.int32, (P, R), 0)
    lag0_diag = lane_p == (R - P) + sub_p

    def hdot(a, b):
        return jnp.dot(a, b, precision=HI, preferred_element_type=F32)

    def cmul(xr, xi, yr, yi):
        return xr * yr - xi * yi, xr * yi + xi * yr

    for k in range(_S5GEN_GG):
        dt = jnp.exp(ls_ref[k])
        ar = ar_ref[k]
        ai = ai_ref[k]
        mag = jnp.exp(ar * dt)
        lbr = mag * jnp.cos(ai * dt)
        lbi = mag * jnp.sin(ai * dt)
        den = ar * ar + ai * ai
        zr = ((lbr - 1.0) * ar + lbi * ai) / den
        zi = (lbi * ar - (lbr - 1.0) * ai) / den

        pw_r = jnp.ones((P, 2 * N), F32)
        pw_i = jnp.zeros((P, 2 * N), F32)
        sq_r, sq_i = lbr, lbi
        for bit in range(4):
            nr, ni = cmul(pw_r, pw_i, sq_r, sq_i)
            take = (lax.shift_right_logical(sub_e, bit) & 1) == 1
            pw_r = jnp.where(take, nr, pw_r)
            pw_i = jnp.where(take, ni, pw_i)
            if bit < 3:
                sq_r, sq_i = cmul(sq_r, sq_i, sq_r, sq_i)
        p1_r, p1_i = cmul(pw_r, pw_i, lbr, lbi)
        a16r_ref[k] = p1_r[S5_SUB - 1:S5_SUB, :]
        a16i_ref[k] = p1_i[S5_SUB - 1:S5_SUB, :]

        bt = bt_ref[k]
        x = zr * bt + jnp.where(re_half[0:1], -zi, zi) * pltpu.roll(bt, N, axis=1)
        xs = pltpu.roll(x, N, axis=1)
        pw_is = jnp.where(re_half, -pw_i, pw_i)
        w_rows = jnp.concatenate(
            [pw_r[e:e + 1] * x + pw_is[e:e + 1] * xs for e in range(S5_SUB - 1, -1, -1)], axis=0)
        w_t = w_rows.T
        w_r = w_t[0:N]
        w_i = w_t[N:2 * N]
        wt_ref[k] = w_t.astype(BF16)

        c_re = cr_ref[k]
        c_im = ci_ref[k]
        krev = hdot(c_re, w_r) - hdot(c_im, w_i)
        krev = krev + jnp.where(lag0_diag, d_ref[k], 0.0)
        for tp in range(S5_SUB):
            width = (tp + 1) * P
            piece = krev if width == R else jnp.where(
                lane_p < width, pltpu.roll(krev, width, axis=1), 0.0)
            mt_ref[k, tp * P:(tp + 1) * P, :] = piece.astype(BF16)

        cc = cc_ref[k]
        ccs = pltpu.roll(cc, N, axis=1)
        p1_rs = jnp.where(re_half, p1_r, -p1_r)
        for tp in range(S5_SUB):
            val = cc * p1_rs[tp:tp + 1] - ccs * p1_i[tp:tp + 1]
            vt_ref[k, tp * P:(tp + 1) * P, :] = val.astype(BF16)


def _prep(c, w_ada, b_ada, a_re, a_im, log_step, b_re, b_im, c_re, c_im, d_skip):
    G, N, P, R = S5_GROUPS, S5_STATE, S5_GROUP, S5_ROW
    gg = _S5GEN_GG
    steps = G // gg
    n_mod = w_ada.shape[1]
    tn = n_mod // steps
    assert tn * steps == n_mod and tn % LANES == 0
    dup = lambda a: jnp.concatenate([a, a], axis=-1).reshape(G, 1, 2 * N)
    bt = jnp.concatenate([jnp.swapaxes(b_re, 1, 2), jnp.swapaxes(b_im, 1, 2)], axis=-1)
    cc = jnp.concatenate([c_re, c_im], axis=-1)
    blk = lambda s: pl.BlockSpec((gg,) + s, lambda g: (g, 0, 0))
    return pl.pallas_call(
        _prep_kernel,
        grid=(steps,),
        in_specs=[pl.BlockSpec((BATCH, D_MODEL), lambda g: (0, 0)),
                  pl.BlockSpec((D_MODEL, tn), lambda g: (0, g)),
                  pl.BlockSpec((1, tn), lambda g: (0, g)),
                  blk((1, 1)), blk((1, 2 * N)), blk((1, 2 * N)), blk((P, 2 * N)),
                  blk((P, N)), blk((P, N)), blk((P, 2 * N)), blk((P, 1))],
        out_specs=[pl.BlockSpec((BATCH, 1, tn), lambda g: (0, 0, g)),
                   blk((R, R)), blk((2 * N, R)), blk((R, 2 * N)), blk((1, 2 * N)), blk((1, 2 * N))],
        out_shape=[jax.ShapeDtypeStruct((BATCH, 1, n_mod), F32),
                   jax.ShapeDtypeStruct((G, R, R), BF16),
                   jax.ShapeDtypeStruct((G, 2 * N, R), BF16),
                   jax.ShapeDtypeStruct((G, R, 2 * N), BF16)] +
                  [jax.ShapeDtypeStruct((G, 1, 2 * N), F32)] * 2,
        compiler_params=pltpu.CompilerParams(
            dimension_semantics=("parallel",), vmem_limit_bytes=VMEM_LIMIT,
            allow_input_fusion=[False, False, False, True, True, True, True, False, False, True,
                                True]),
        name="prep",
    )(c, w_ada, b_ada.reshape(1, n_mod),
      log_step.reshape(G, 1, 1), dup(a_re), dup(a_im), bt, c_re, c_im, cc,
      d_skip.reshape(G, P, 1))


_S5_GB = LANES // S5_GROUP
_S5_NP = _S5_GB // 2
_S5_NSTEP = S5_GROUPS // _S5_GB


def _s5_kernel(a16r_ref, a16i_ref, u_hbm, mt_ref, wt_ref, vt_ref, y_hbm,
               ubuf, ybuf, in_sem, out_sem, ut_scr, yt_scr, e_scr, p_scr):
    P = S5_GROUP
    N2 = 2 * S5_STATE
    cols = S5_COLS
    nblk = cols // LANES
    nt = cols // 8
    step = pl.program_id(0)
    nstep = pl.num_programs(0)
    slot = step % 2

    def in_copy(slab, sl, tau):
        return pltpu.make_async_copy(u_hbm.at[slab, :, tau, :], ubuf.at[sl, tau], in_sem.at[sl])

    def out_copy(slab, sl, tau):
        return pltpu.make_async_copy(ybuf.at[sl, tau], y_hbm.at[slab, :, tau, :], out_sem.at[sl])

    @pl.when(step == 0)
    def _():
        for tau in range(S5_SUB):
            in_copy(0, 0, tau).start()

    @pl.when(step + 1 < nstep)
    def _():
        for tau in range(S5_SUB):
            in_copy(step + 1, 1 - slot, tau).start()

    for tau in range(S5_SUB):
        in_copy(step, slot, tau).wait()

    def load_body(tau, carry):
        for c in range(nblk):
            xt = ubuf[slot, tau, c * LANES:(c + 1) * LANES, :].T
            r0 = pl.multiple_of(tau * P, P)
            for k in range(_S5_GB):
                ut_scr[k, pl.ds(r0, P), c * LANES:(c + 1) * LANES] = \
                    xt[k * P:(k + 1) * P, :].astype(BF16)
        return carry

    lax.fori_loop(0, S5_SUB, load_body, 0, unroll=4)

    NS = S5_STATE
    for pr in range(_S5_NP):
        e0 = jnp.dot(wt_ref[2 * pr], ut_scr[2 * pr], preferred_element_type=F32)
        e1 = jnp.dot(wt_ref[2 * pr + 1], ut_scr[2 * pr + 1], preferred_element_type=F32)
        et = jnp.concatenate([e0[0:NS], e1[0:NS], e0[NS:2 * NS], e1[NS:2 * NS]], axis=0)
        e_scr[pr] = et.T

    sub = lax.broadcasted_iota(jnp.int32, (8, N2), 0)
    lo = sub < BATCH
    first = lax.broadcasted_iota(jnp.int32, (1, N2), 1) < S5_STATE
    a_r, a_i = [], []
    for pr in range(_S5_NP):
        a_r.append(jnp.broadcast_to(
            jnp.where(first, a16r_ref[2 * pr], a16r_ref[2 * pr + 1]), (8, N2)))
        a_i.append(jnp.broadcast_to(
            jnp.where(first, a16i_ref[2 * pr], a16i_ref[2 * pr + 1]), (8, N2)))

    def scan_body(t, carry):
        r0 = pl.multiple_of(t * 8, 8)
        new = []
        for pr in range(_S5_NP):
            s_r, s_i = carry[2 * pr], carry[2 * pr + 1]
            e_r = e_scr[pr, pl.ds(r0, 8), 0:N2]
            e_i = e_scr[pr, pl.ds(r0, 8), N2:2 * N2]
            x_r = pltpu.roll(e_r, BATCH, axis=0)
            x_i = pltpu.roll(e_i, BATCH, axis=0)
            elo_r = jnp.where(lo, e_r, x_r)
            elo_i = jnp.where(lo, e_i, x_i)
            ehi_r = jnp.where(lo, x_r, e_r)
            ehi_i = jnp.where(lo, x_i, e_i)
            t_r = a_r[pr] * s_r - a_i[pr] * s_i + elo_r
            t_i = a_r[pr] * s_i + a_i[pr] * s_r + elo_i
            p_scr[pr, pl.ds(r0, 8), 0:N2] = jnp.where(lo, s_r, t_r)
            p_scr[pr, pl.ds(r0, 8), N2:2 * N2] = jnp.where(lo, s_i, t_i)
            new.append(a_r[pr] * t_r - a_i[pr] * t_i + ehi_r)
            new.append(a_r[pr] * t_i + a_i[pr] * t_r + ehi_i)
        return tuple(new)

    zero = jnp.zeros((8, N2), F32)
    lax.fori_loop(0, nt, scan_body, tuple(zero for _ in range(2 * _S5_NP)), unroll=2)

    for pr in range(_S5_NP):
        pt = p_scr[pr].astype(BF16).T
        for k in range(2):
            gi = 2 * pr + k
            pk = jnp.concatenate([pt[k * NS:(k + 1) * NS],
                                  pt[(2 + k) * NS:(3 + k) * NS]], axis=0)
            yt = (jnp.dot(mt_ref[gi], ut_scr[gi], preferred_element_type=F32) +
                  jnp.dot(vt_ref[gi], pk, preferred_element_type=F32))
            for tau in range(S5_SUB):
                yt_scr[tau, gi * P:(gi + 1) * P, :] = yt[tau * P:(tau + 1) * P, :]

    @pl.when(step >= 2)
    def _():
        for tau in range(S5_SUB):
            out_copy(step - 2, slot, tau).wait()

    def store_body(tau, carry):
        ybuf[slot, tau] = yt_scr[tau].T
        return carry

    lax.fori_loop(0, S5_SUB, store_body, 0, unroll=4)

    for tau in range(S5_SUB):
        out_copy(step, slot, tau).start()

    @pl.when(step == nstep - 1)
    def _():
        if _S5_NSTEP > 1:
            for tau in range(S5_SUB):
                out_copy(step - 1, 1 - slot, tau).wait()
        for tau in range(S5_SUB):
            out_copy(step, slot, tau).wait()


def _s5(u_rows, mt, wt, vt, a16r, a16i):
    G, N, R = S5_GROUPS, S5_STATE, S5_ROW
    gb = _S5_GB
    vec = pl.BlockSpec((gb, 1, 2 * N), lambda i: (i, 0, 0))
    mat = lambda r, c: pl.BlockSpec((gb, r, c), lambda i: (i, 0, 0))
    hbm = pl.BlockSpec(memory_space=pl.ANY)
    plane = pltpu.VMEM((2, S5_SUB, S5_COLS, LANES), F32)
    return pl.pallas_call(
        _s5_kernel,
        grid=(_S5_NSTEP,),
        in_specs=[vec, vec, hbm, mat(R, R), mat(2 * N, R), mat(R, 2 * N)],
        out_specs=hbm,
        out_shape=jax.ShapeDtypeStruct((D_S5 // LANES, S5_COLS, S5_SUB, LANES), F32),
        scratch_shapes=[plane, plane,
                        pltpu.SemaphoreType.DMA((2,)), pltpu.SemaphoreType.DMA((2,)),
                        pltpu.VMEM((gb, R, S5_COLS), BF16),
                        pltpu.VMEM((S5_SUB, LANES, S5_COLS), F32),
                        pltpu.VMEM((_S5_NP, S5_COLS, 4 * N), F32),
                        pltpu.VMEM((_S5_NP, S5_COLS, 4 * N), F32)],
        compiler_params=pltpu.CompilerParams(
            dimension_semantics=("arbitrary",), vmem_limit_bytes=VMEM_LIMIT),
        name="s5",
    )(a16r, a16i, u_rows, mt, wt, vt)


_MIX_TM = 512
_MIX_SUB = 256
_MIX_NS = _MIX_TM // S5_SUB


def _mix_kernel(ssm_ref, ret_ref, x_ref, g1_ref, sc_ref, sh_ref, nw_ref, wglu_ref, bglu_ref,
                wout_ref, o_ref, h_ref):
    gain = nw_ref[...] * (1.0 + sc_ref[...])
    nsub = _MIX_TM // _MIX_SUB
    ns = _MIX_SUB // S5_SUB
    gated = []
    for r in range(nsub):
        s = jnp.concatenate([ssm_ref[k, r * ns:(r + 1) * ns].reshape(_MIX_SUB, LANES)
                             for k in range(D_S5 // LANES)], axis=1)
        cdf = 0.5 * (1.0 + jnp.tanh(math.sqrt(2.0 / math.pi) * (s + 0.044715 * (s * s * s))))
        sg = s * cdf
        z = jnp.dot(sg.astype(BF16), wglu_ref[...], preferred_element_type=F32) + bglu_ref[...]
        gated.append((sg * jax.nn.sigmoid(z)).astype(BF16))
    for r in range(nsub):
        rs = slice(r * _MIX_SUB, (r + 1) * _MIX_SUB)
        both = jnp.concatenate([ret_ref[hh, rs, :] for hh in range(RET_HEADS)] + [gated[r]],
                               axis=1)
        mix = jnp.dot(both, wout_ref[...], preferred_element_type=F32)
        x1 = x_ref[rs, :] + g1_ref[...] * mix
        o_ref[rs, :] = x1
        ms = jnp.mean(x1 * x1, axis=-1, keepdims=True)
        h_ref[rs, :] = (x1 * lax.rsqrt(ms + EPS) * gain + sh_ref[...]).astype(BF16)


def _mix(ssm_sb, ret_out, x2d, mod, norm_w, w_glu_bf, b_glu, w_out_bf):
    tm = _MIX_TM
    per_b = SEQ // tm
    rows = pl.BlockSpec((tm, D_MODEL), lambda i: (i, 0))
    return pl.pallas_call(
        _mix_kernel,
        grid=(TOK // tm,),
        in_specs=[pl.BlockSpec((D_S5 // LANES, _MIX_NS, None, S5_SUB, LANES),
                               lambda i: (0, i % per_b, i // per_b, 0, 0)),
                  pl.BlockSpec((RET_HEADS, tm, RET_HEAD_DIM), lambda i: (0, i, 0)),
                  rows, _mod_spec(_G1, tm), _mod_spec(_SC2, tm), _mod_spec(_SH2, tm),
                  pl.BlockSpec((1, D_MODEL), lambda i: (0, 0)),
                  pl.BlockSpec((D_S5, D_S5), lambda i: (0, 0)),
                  pl.BlockSpec((1, D_S5), lambda i: (0, 0)),
                  pl.BlockSpec((D_RET + D_S5, D_MODEL), lambda i: (0, 0))],
        out_specs=[rows, rows],
        out_shape=[jax.ShapeDtypeStruct((TOK, D_MODEL), F32),
                   jax.ShapeDtypeStruct((TOK, D_MODEL), BF16)],
        compiler_params=pltpu.CompilerParams(
            dimension_semantics=("parallel",), vmem_limit_bytes=VMEM_LIMIT),
        name="mix",
    )(ssm_sb, ret_out, x2d, mod, mod, mod, norm_w, w_glu_bf, b_glu, w_out_bf)


_FFN_UP_TM = 2048
_FFN_UP_SUB = 1024
_FFN_UP_TF = 512
_FFN_DOWN_TM = 512
_FFN_DOWN_SUB = 256


def _ffn_up_kernel(h_ref, wg_ref, wu_ref, wd_ref, o_ref, wd_bf_ref, wg_scr, wu_scr):
    @pl.when(pl.program_id(1) == 0)
    def _():
        wg_scr[...] = wg_ref[...].astype(BF16)
        wu_scr[...] = wu_ref[...].astype(BF16)

    wd_bf_ref[...] = wd_ref[...].astype(BF16)
    for r in range(_FFN_UP_TM // _FFN_UP_SUB):
        rs = slice(r * _FFN_UP_SUB, (r + 1) * _FFN_UP_SUB)
        h = h_ref[rs, :]
        gate = jnp.dot(h, wg_scr[...], preferred_element_type=F32)
        up = jnp.dot(h, wu_scr[...], preferred_element_type=F32)
        o_ref[rs, :] = (_silu(gate) * up).astype(BF16)


def _ffn_up(h2, w_gate_up, w_down):
    tm, tf = _FFN_UP_TM, _FFN_UP_TF
    nf = D_FF // tf
    nm = TOK // tm
    slab = D_FF // (nf * nm)
    assert slab * nf * nm == D_FF and slab % 16 == 0
    return pl.pallas_call(
        _ffn_up_kernel,
        grid=(nf, nm),
        in_specs=[pl.BlockSpec((tm, D_MODEL), lambda f, i: (i, 0)),
                  pl.BlockSpec((D_MODEL, tf), lambda f, i: (0, f)),
                  pl.BlockSpec((D_MODEL, tf), lambda f, i: (0, nf + f)),
                  pl.BlockSpec((slab, D_MODEL), lambda f, i: (f * nm + i, 0))],
        out_specs=[pl.BlockSpec((tm, tf), lambda f, i: (i, f)),
                   pl.BlockSpec((slab, D_MODEL), lambda f, i: (f * nm + i, 0))],
        out_shape=[jax.ShapeDtypeStruct((TOK, D_FF), BF16),
                   jax.ShapeDtypeStruct((D_FF, D_MODEL), BF16)],
        scratch_shapes=[pltpu.VMEM((D_MODEL, tf), BF16),
                        pltpu.VMEM((D_MODEL, tf), BF16)],
        compiler_params=pltpu.CompilerParams(
            dimension_semantics=("arbitrary", "arbitrary"), vmem_limit_bytes=VMEM_LIMIT),
        name="ffn_up",
    )(h2, w_gate_up, w_gate_up, w_down)


def _ffn_down_kernel(a_ref, x_ref, g2_ref, fw_ref, wd_ref, o_ref):
    for r in range(_FFN_DOWN_TM // _FFN_DOWN_SUB):
        rs = slice(r * _FFN_DOWN_SUB, (r + 1) * _FFN_DOWN_SUB)
        down = jnp.dot(a_ref[rs, :], wd_ref[...], preferred_element_type=F32)
        x2 = x_ref[rs, :] + g2_ref[...] * down
        ms = jnp.mean(x2 * x2, axis=-1, keepdims=True)
        o_ref[rs, :] = x2 * lax.rsqrt(ms + EPS) * fw_ref[...]


def _ffn_down(act, x1, mod, final_w, w_down_bf):
    tm = _FFN_DOWN_TM
    rows = pl.BlockSpec((tm, D_MODEL), lambda i: (i, 0))
    return pl.pallas_call(
        _ffn_down_kernel,
        grid=(TOK // tm,),
        in_specs=[pl.BlockSpec((tm, D_FF), lambda i: (i, 0)),
                  rows,
                  _mod_spec(_G2, tm),
                  pl.BlockSpec((1, D_MODEL), lambda i: (0, 0)),
                  pl.BlockSpec((D_FF, D_MODEL), lambda i: (0, 0))],
        out_specs=rows,
        out_shape=jax.ShapeDtypeStruct((TOK, D_MODEL), F32),
        compiler_params=pltpu.CompilerParams(
            dimension_semantics=("parallel",), vmem_limit_bytes=VMEM_LIMIT),
        name="ffn_down",
    )(act, x1, mod, final_w, w_down_bf)


def kernel(x, c, w_ada, b_ada, norm1_w, w_in, ret_norm_w, s5_a_re, s5_a_im, s5_log_step,
           s5_b_re, s5_b_im, s5_c_re, s5_c_im, s5_d, w_glu, b_glu, w_out, norm2_w,
           w_gate_up, w_down, final_norm_w):
    x2d = x.reshape(TOK, D_MODEL)
    layer = 0
    mod, mt, wt, vt, a16r, a16i = _prep(
        c, w_ada[layer], b_ada[layer],
        s5_a_re[layer], s5_a_im[layer], s5_log_step[layer], s5_b_re[layer], s5_b_im[layer],
        s5_c_re[layer], s5_c_im[layer], s5_d[layer])

    ret_out, u_sb, w_out_bf, w_glu_bf = _inproj(x2d, mod, norm1_w[layer].reshape(1, D_MODEL),
                                                w_in[layer], w_out[layer], w_glu[layer],
                                                ret_norm_w[layer].reshape(1, D_RET))
    y_rows = _s5(u_sb.reshape(D_S5 // LANES, S5_COLS, S5_SUB, LANES), mt, wt, vt, a16r, a16i)
    ssm_sb = y_rows.reshape(D_S5 // LANES, S5_NSUB, BATCH, S5_SUB, LANES)

    x1, h2 = _mix(ssm_sb, ret_out, x2d, mod, norm2_w[layer].reshape(1, D_MODEL),
                  w_glu_bf, b_glu[layer].reshape(1, D_S5), w_out_bf)
    act, w_down_bf = _ffn_up(h2, w_gate_up[layer], w_down[layer])
    out = _ffn_down(act, x1, mod, final_norm_w.reshape(1, D_MODEL), w_down_bf)
    return out.reshape(BATCH, SEQ, D_MODEL)
```

```python
import math

import numpy as np
import jax
import jax.numpy as jnp
from jax import lax
from jax.experimental import pallas as pl
from jax.experimental.pallas import tpu as pltpu

D_MODEL = 2048
BATCH = 4
SEQ = 2048
TOK = BATCH * SEQ
D_RET = 1024
D_S5 = 1024
RET_HEADS = 4
RET_HEAD_DIM = 256
S5_GROUP = 16
S5_GROUPS = 64
S5_STATE = 64
S5_SUB = 16
S5_ROW = S5_SUB * S5_GROUP
S5_NSUB = SEQ // S5_SUB
S5_COLS = S5_NSUB * BATCH
D_FF = 5632
ROPE_BASE = 10000.0
EPS = 1e-6
LANES = 128

F32 = jnp.float32
BF16 = jnp.bfloat16
HI = lax.Precision.HIGHEST
VMEM_LIMIT = 58 * 1024 * 1024


def _silu(v):
    return v * jax.nn.sigmoid(v)


_SH1, _SC1, _G1, _SH2, _SC2, _G2 = range(6)


def _mod_spec(which, rows_per_step):
    per_b = SEQ // rows_per_step
    return pl.BlockSpec((None, 1, D_MODEL), lambda i: (i // per_b, 0, which))


_INPROJ_TM = 512
_INPROJ_SUB = 256
_INPROJ_NS = _INPROJ_SUB // S5_SUB


_INPROJ_WT = 512
_INPROJ_NW = (4 * D_RET + D_S5) // _INPROJ_WT


def _inproj_kernel(x_ref, sc_ref, sh_ref, nw_ref, w_ref, wout_ref, wglu_ref, lg_ref, gnw_ref,
                   ret_ref, u_ref, wout_bf_ref, wglu_bf_ref,
                   w_scr, h_scr, cos_scr, sin_scr, o_ref, st_scr, intra_scr, dec_scr):
    step = pl.program_id(0)
    C = _INPROJ_SUB
    scale = RET_HEAD_DIM ** -0.5

    @pl.when(step < _INPROJ_NW)
    def _():
        w_scr[step] = w_ref[...].astype(BF16)

        @pl.when(step < SEQ // _INPROJ_TM)
        def _():
            tm = _INPROJ_TM
            half = RET_HEAD_DIM // 2
            pos = (lax.broadcasted_iota(jnp.int32, (tm, half), 0) + step * tm).astype(F32)
            lane = lax.broadcasted_iota(jnp.int32, (tm, half), 1).astype(F32)
            ang = pos * jnp.exp(lane * (-math.log(ROPE_BASE) / half))
            cos_scr[step] = jnp.cos(ang)
            sin_scr[step] = jnp.sin(ang)

        @pl.when(step < RET_HEADS)
        def _():
            lg = lg_ref[step, 0:1, :]
            ii = lax.broadcasted_iota(jnp.int32, (C, C), 0)
            jj = lax.broadcasted_iota(jnp.int32, (C, C), 1)
            diff = (ii - jj).astype(F32)
            intra_scr[step] = jnp.where(
                diff >= 0.0, jnp.exp(lg * jnp.maximum(diff, 0.0)), 0.0) * scale
            row = lax.broadcasted_iota(jnp.int32, (C, LANES), 0).astype(F32)
            lgl = lg[:, 0:LANES]
            dec_scr[0, step] = jnp.exp(lgl * (C - 1.0 - row)) * scale
            dec_scr[1, step] = jnp.exp(lgl * (row + 1.0))

    @pl.when(step >= _INPROJ_NW)
    def _():
        _inproj_rows(step - _INPROJ_NW, x_ref, sc_ref, sh_ref, nw_ref, wout_ref, wglu_ref,
                     lg_ref, gnw_ref, ret_ref, u_ref, wout_bf_ref, wglu_bf_ref,
                     w_scr, h_scr, cos_scr, sin_scr, o_ref, st_scr, intra_scr, dec_scr)


def _inproj_rows(i, x_ref, sc_ref, sh_ref, nw_ref, wout_ref, wglu_ref, lg_ref, gnw_ref,
                 ret_ref, u_ref, wout_bf_ref, wglu_bf_ref,
                 w_scr, h_scr, cos_scr, sin_scr, o_ref, st_scr, intra_scr, dec_scr):
    wout_bf_ref[...] = wout_ref[...].astype(BF16)
    wglu_bf_ref[...] = wglu_ref[...].astype(BF16)
    tm = _INPROJ_TM
    half = RET_HEAD_DIM // 2
    per_b = SEQ // tm
    hpt = _INPROJ_WT // RET_HEAD_DIM
    slot = i % per_b
    C = _INPROJ_SUB
    dh = RET_HEAD_DIM

    @pl.when(slot == 0)
    def _():
        st_scr[...] = jnp.zeros_like(st_scr)

    gain = nw_ref[...] * (1.0 + sc_ref[...])
    shift = sh_ref[...]
    for r in range(tm // _INPROJ_SUB):
        rs = slice(r * _INPROJ_SUB, (r + 1) * _INPROJ_SUB)
        x = x_ref[rs, :]
        ms = jnp.mean(x * x, axis=-1, keepdims=True)
        h_scr[rs, :] = (x * lax.rsqrt(ms + EPS) * gain + shift).astype(BF16)
        cs = cos_scr[slot, rs, :]
        sn = sin_scr[slot, rs, :]
        def project(t):
            acc = jnp.dot(h_scr[rs, :], w_scr[t], preferred_element_type=F32)
            if t < 2 * RET_HEADS // hpt:
                for hh in range(hpt):
                    c0 = hh * RET_HEAD_DIM
                    x1 = acc[:, c0:c0 + half]
                    x2 = acc[:, c0 + half:c0 + 2 * half]
                    o_ref[t * hpt + hh, rs, 0:half] = (x1 * cs - x2 * sn).astype(BF16)
                    o_ref[t * hpt + hh, rs, half:2 * half] = (x2 * cs + x1 * sn).astype(BF16)
            elif t < 4 * RET_HEADS // hpt:
                for hh in range(hpt):
                    c0 = hh * RET_HEAD_DIM
                    o_ref[t * hpt + hh, rs, :] = acc[:, c0:c0 + RET_HEAD_DIM].astype(BF16)
            else:
                k0 = (t - 4 * RET_HEADS // hpt) * (_INPROJ_WT // LANES)
                for k in range(_INPROJ_WT // LANES):
                    u_ref[k0 + k, r * _INPROJ_NS:(r + 1) * _INPROJ_NS] = \
                        acc[:, k * LANES:(k + 1) * LANES].reshape(_INPROJ_NS, S5_SUB, LANES)

        n_qkvg = 4 * RET_HEADS // hpt
        for t in range(n_qkvg):
            project(t)

        for hh in range(RET_HEADS):
            q = o_ref[hh, rs, :]
            k = o_ref[RET_HEADS + hh, rs, :]
            v = o_ref[2 * RET_HEADS + hh, rs, :]
            g = o_ref[3 * RET_HEADS + hh, rs, :].astype(F32)
            kdec = jnp.concatenate([dec_scr[0, hh]] * (dh // LANES), axis=1)
            qdec = jnp.concatenate([dec_scr[1, hh]] * (dh // LANES), axis=1)
            st = st_scr[hh]
            s = lax.dot_general(q, k, (((1,), (1,)), ((), ())),
                                preferred_element_type=F32) * intra_scr[hh]
            y = jnp.dot(s.astype(BF16), v, preferred_element_type=F32)
            y = y + jnp.dot(q, st.astype(BF16), preferred_element_type=F32) * qdec
            kd = (k.astype(F32) * kdec).astype(BF16)
            kv = lax.dot_general(kd, v, (((0,), (0,)), ((), ())), preferred_element_type=F32)
            st_scr[hh] = st * jnp.exp(lg_ref[hh, 0:1, :] * float(C)) + kv
            mu = jnp.mean(y, axis=-1, keepdims=True)
            yc = y - mu
            var = jnp.mean(yc * yc, axis=-1, keepdims=True)
            yn = yc * lax.rsqrt(var + EPS) * gnw_ref[:, hh * dh:(hh + 1) * dh]
            ret_ref[hh, rs, :] = (_silu(g) * yn).astype(BF16)

        for t in range(n_qkvg, _INPROJ_NW):
            project(t)


def _inproj(x2d, mod, norm_w, w_in, w_out, w_glu, ret_norm_w):
    tm = _INPROJ_TM
    per_b = SEQ // tm
    half = RET_HEAD_DIM // 2
    dh = RET_HEAD_DIM
    nw = _INPROJ_NW
    steps = TOK // tm
    so = w_out.shape[0] // steps
    sg = w_glu.shape[0] // steps
    assert w_in.shape == (D_MODEL, 4 * D_RET + D_S5) and D_S5 == D_RET
    assert nw >= per_b and nw >= RET_HEADS
    assert so * steps == w_out.shape[0] and sg * steps == w_glu.shape[0] and sg % 16 == 0
    lg = np.log1p(-np.exp2(-5.0 - np.arange(RET_HEADS, dtype=np.float64)))
    lg_tab = jnp.asarray(np.broadcast_to(lg[:, None, None], (RET_HEADS, 8, dh)), F32)
    row = lambda s: jnp.maximum(s - nw, 0)
    slab_o = pl.BlockSpec((so, w_out.shape[1]), lambda s: (row(s), 0))
    slab_g = pl.BlockSpec((sg, w_glu.shape[1]), lambda s: (row(s), 0))
    mod_spec = lambda which: pl.BlockSpec((None, 1, D_MODEL),
                                          lambda s: (row(s) // per_b, 0, which))
    return pl.pallas_call(
        _inproj_kernel,
        grid=(nw + steps,),
        in_specs=[pl.BlockSpec((tm, D_MODEL), lambda s: (row(s), 0)),
                  mod_spec(_SC1), mod_spec(_SH1),
                  pl.BlockSpec((1, D_MODEL), lambda s: (0, 0)),
                  pl.BlockSpec((D_MODEL, _INPROJ_WT), lambda s: (0, jnp.minimum(s, nw - 1))),
                  slab_o, slab_g,
                  pl.BlockSpec((RET_HEADS, 8, dh), lambda s: (0, 0, 0)),
                  pl.BlockSpec((1, D_RET), lambda s: (0, 0))],
        out_specs=[pl.BlockSpec((RET_HEADS, tm, dh), lambda s: (0, row(s), 0)),
                   pl.BlockSpec((D_S5 // LANES, tm // S5_SUB, None, S5_SUB, LANES),
                                lambda s: (0, row(s) % per_b, row(s) // per_b, 0, 0)),
                   slab_o, slab_g],
        out_shape=[jax.ShapeDtypeStruct((RET_HEADS, TOK, dh), BF16),
                   jax.ShapeDtypeStruct((D_S5 // LANES, S5_NSUB, BATCH, S5_SUB, LANES), F32),
                   jax.ShapeDtypeStruct(w_out.shape, BF16),
                   jax.ShapeDtypeStruct(w_glu.shape, BF16)],
        scratch_shapes=[pltpu.VMEM((nw, D_MODEL, _INPROJ_WT), BF16),
                        pltpu.VMEM((tm, D_MODEL), BF16),
                        pltpu.VMEM((per_b, tm, half), F32),
                        pltpu.VMEM((per_b, tm, half), F32),
                        pltpu.VMEM((4 * RET_HEADS, tm, dh), BF16),
                        pltpu.VMEM((RET_HEADS, dh, dh), F32),
                        pltpu.VMEM((RET_HEADS, _INPROJ_SUB, _INPROJ_SUB), F32),
                        pltpu.VMEM((2, RET_HEADS, _INPROJ_SUB, LANES), F32)],
        compiler_params=pltpu.CompilerParams(
            dimension_semantics=("arbitrary",), vmem_limit_bytes=VMEM_LIMIT),
        name="inproj",
    )(x2d, mod, mod, norm_w, w_in, w_out, w_glu, lg_tab, ret_norm_w)


_S5GEN_GG = 4


def _prep_kernel(c_ref, wada_ref, bada_ref,
                 ls_ref, ar_ref, ai_ref, bt_ref, cr_ref, ci_ref, cc_ref, d_ref,
                 mod_ref, mt_ref, wt_ref, vt_ref, a16r_ref, a16i_ref):
    cond = _silu(c_ref[...])
    mod = jnp.dot(cond.astype(BF16), wada_ref[...].astype(BF16),
                  preferred_element_type=F32) + bada_ref[...]
    for b in range(BATCH):
        mod_ref[b] = mod[b:b + 1, :]
    N, P, R = S5_STATE, S5_GROUP, S5_ROW
    re_half = lax.broadcasted_iota(jnp.int32, (P, 2 * N), 1) < N
    sub_e = lax.broadcasted_iota(jnp.int32, (P, 2 * N), 0)
    lane_p = lax.broadcasted_iota(jnp.int32, (P, R), 1)
    sub_p = lax.broadcasted_iota(jnp.int32, (P, R), 0)
    lag0_diag = lane_p == (R - P) + sub_p

    def hdot(a, b):
        return jnp.dot(a, b, precision=HI, preferred_element_type=F32)

    def cmul(xr, xi, yr, yi):
        return xr * yr - xi * yi, xr * yi + xi * yr

    for k in range(_S5GEN_GG):
        dt = jnp.exp(ls_ref[k])
        ar = ar_ref[k]
        ai = ai_ref[k]
        mag = jnp.exp(ar * dt)
        lbr = mag * jnp.cos(ai * dt)
        lbi = mag * jnp.sin(ai * dt)
        den = ar * ar + ai * ai
        zr = ((lbr - 1.0) * ar + lbi * ai) / den
        zi = (lbi * ar - (lbr - 1.0) * ai) / den

        pw_r = jnp.ones((P, 2 * N), F32)
        pw_i = jnp.zeros((P, 2 * N), F32)
        sq_r, sq_i = lbr, lbi
        for bit in range(4):
            nr, ni = cmul(pw_r, pw_i, sq_r, sq_i)
            take = (lax.shift_right_logical(sub_e, bit) & 1) == 1
            pw_r = jnp.where(take, nr, pw_r)
            pw_i = jnp.where(take, ni, pw_i)
            if bit < 3:
                sq_r, sq_i = cmul(sq_r, sq_i, sq_r, sq_i)
        p1_r, p1_i = cmul(pw_r, pw_i, lbr, lbi)
        a16r_ref[k] = p1_r[S5_SUB - 1:S5_SUB, :]
        a16i_ref[k] = p1_i[S5_SUB - 1:S5_SUB, :]

        bt = bt_ref[k]
        x = zr * bt + jnp.where(re_half[0:1], -zi, zi) * pltpu.roll(bt, N, axis=1)
        xs = pltpu.roll(x, N, axis=1)
        pw_is = jnp.where(re_half, -pw_i, pw_i)
        w_rows = jnp.concatenate(
            [pw_r[e:e + 1] * x + pw_is[e:e + 1] * xs for e in range(S5_SUB - 1, -1, -1)], axis=0)
        w_t = w_rows.T
        w_r = w_t[0:N]
        w_i = w_t[N:2 * N]
        wt_ref[k] = w_t.astype(BF16)

        c_re = cr_ref[k]
        c_im = ci_ref[k]
        krev = hdot(c_re, w_r) - hdot(c_im, w_i)
        krev = krev + jnp.where(lag0_diag, d_ref[k], 0.0)
        for tp in range(S5_SUB):
            width = (tp + 1) * P
            piece = krev if width == R else jnp.where(
                lane_p < width, pltpu.roll(krev, width, axis=1), 0.0)
            mt_ref[k, tp * P:(tp + 1) * P, :] = piece.astype(BF16)

        cc = cc_ref[k]
        ccs = pltpu.roll(cc, N, axis=1)
        p1_rs = jnp.where(re_half, p1_r, -p1_r)
        for tp in range(S5_SUB):
            val = cc * p1_rs[tp:tp + 1] - ccs * p1_i[tp:tp + 1]
            vt_ref[k, tp * P:(tp + 1) * P, :] = val.astype(BF16)


def _prep(c, w_ada, b_ada, a_re, a_im, log_step, b_re, b_im, c_re, c_im, d_skip):
    G, N, P, R = S5_GROUPS, S5_STATE, S5_GROUP, S5_ROW
    gg = _S5GEN_GG
    steps = G // gg
    n_mod = w_ada.shape[1]
    tn = n_mod // steps
    assert tn * steps == n_mod and tn % LANES == 0
    dup = lambda a: jnp.concatenate([a, a], axis=-1).reshape(G, 1, 2 * N)
    bt = jnp.concatenate([jnp.swapaxes(b_re, 1, 2), jnp.swapaxes(b_im, 1, 2)], axis=-1)
    cc = jnp.concatenate([c_re, c_im], axis=-1)
    blk = lambda s: pl.BlockSpec((gg,) + s, lambda g: (g, 0, 0))
    return pl.pallas_call(
        _prep_kernel,
        grid=(steps,),
        in_specs=[pl.BlockSpec((BATCH, D_MODEL), lambda g: (0, 0)),
                  pl.BlockSpec((D_MODEL, tn), lambda g: (0, g)),
                  pl.BlockSpec((1, tn), lambda g: (0, g)),
                  blk((1, 1)), blk((1, 2 * N)), blk((1, 2 * N)), blk((P, 2 * N)),
                  blk((P, N)), blk((P, N)), blk((P, 2 * N)), blk((P, 1))],
        out_specs=[pl.BlockSpec((BATCH, 1, tn), lambda g: (0, 0, g)),
                   blk((R, R)), blk((2 * N, R)), blk((R, 2 * N)), blk((1, 2 * N)), blk((1, 2 * N))],
        out_shape=[jax.ShapeDtypeStruct((BATCH, 1, n_mod), F32),
                   jax.ShapeDtypeStruct((G, R, R), BF16),
                   jax.ShapeDtypeStruct((G, 2 * N, R), BF16),
                   jax.ShapeDtypeStruct((G, R, 2 * N), BF16)] +
                  [jax.ShapeDtypeStruct((G, 1, 2 * N), F32)] * 2,
        compiler_params=pltpu.CompilerParams(
            dimension_semantics=("parallel",), vmem_limit_bytes=VMEM_LIMIT,
            allow_input_fusion=[False, False, False, True, True, True, True, False, False, True,
                                True]),
        name="prep",
    )(c, w_ada, b_ada.reshape(1, n_mod),
      log_step.reshape(G, 1, 1), dup(a_re), dup(a_im), bt, c_re, c_im, cc,
      d_skip.reshape(G, P, 1))


_S5_GB = LANES // S5_GROUP
_S5_NP = _S5_GB // 2
_S5_NSTEP = S5_GROUPS // _S5_GB


def _s5_kernel(a16r_ref, a16i_ref, u_hbm, mt_ref, wt_ref, vt_ref, y_hbm,
               ubuf, ybuf, in_sem, out_sem, ut_scr, yt_scr, e_scr, p_scr):
    P = S5_GROUP
    N2 = 2 * S5_STATE
    cols = S5_COLS
    nblk = cols // LANES
    nt = cols // 8
    step = pl.program_id(0)
    nstep = pl.num_programs(0)
    slot = step % 2

    def in_copy(slab, sl, tau):
        return pltpu.make_async_copy(u_hbm.at[slab, :, tau, :], ubuf.at[sl, tau], in_sem.at[sl])

    def out_copy(slab, sl, tau):
        return pltpu.make_async_copy(ybuf.at[sl, tau], y_hbm.at[slab, :, tau, :], out_sem.at[sl])

    @pl.when(step == 0)
    def _():
        for tau in range(S5_SUB):
            in_copy(0, 0, tau).start()

    @pl.when(step + 1 < nstep)
    def _():
        for tau in range(S5_SUB):
            in_copy(step + 1, 1 - slot, tau).start()

    for tau in range(S5_SUB):
        in_copy(step, slot, tau).wait()

    def load_body(tau, carry):
        for c in range(nblk):
            xt = ubuf[slot, tau, c * LANES:(c + 1) * LANES, :].T
            r0 = pl.multiple_of(tau * P, P)
            for k in range(_S5_GB):
                ut_scr[k, pl.ds(r0, P), c * LANES:(c + 1) * LANES] = \
                    xt[k * P:(k + 1) * P, :].astype(BF16)
        return carry

    lax.fori_loop(0, S5_SUB, load_body, 0, unroll=4)

    NS = S5_STATE
    for pr in range(_S5_NP):
        e0 = jnp.dot(wt_ref[2 * pr], ut_scr[2 * pr], preferred_element_type=F32)
        e1 = jnp.dot(wt_ref[2 * pr + 1], ut_scr[2 * pr + 1], preferred_element_type=F32)
        et = jnp.concatenate([e0[0:NS], e1[0:NS], e0[NS:2 * NS], e1[NS:2 * NS]], axis=0)
        e_scr[pr] = et.T

    sub = lax.broadcasted_iota(jnp.int32, (8, N2), 0)
    lo = sub < BATCH
    first = lax.broadcasted_iota(jnp.int32, (1, N2), 1) < S5_STATE
    a_r, a_i = [], []
    for pr in range(_S5_NP):
        a_r.append(jnp.broadcast_to(
            jnp.where(first, a16r_ref[2 * pr], a16r_ref[2 * pr + 1]), (8, N2)))
        a_i.append(jnp.broadcast_to(
            jnp.where(first, a16i_ref[2 * pr], a16i_ref[2 * pr + 1]), (8, N2)))

    def scan_body(t, carry):
        r0 = pl.multiple_of(t * 8, 8)
        new = []
        for pr in range(_S5_NP):
            s_r, s_i = carry[2 * pr], carry[2 * pr + 1]
            e_r = e_scr[pr, pl.ds(r0, 8), 0:N2]
            e_i = e_scr[pr, pl.ds(r0, 8), N2:2 * N2]
            x_r = pltpu.roll(e_r, BATCH, axis=0)
            x_i = pltpu.roll(e_i, BATCH, axis=0)
            elo_r = jnp.where(lo, e_r, x_r)
            elo_i = jnp.where(lo, e_i, x_i)
            ehi_r = jnp.where(lo, x_r, e_r)
            ehi_i = jnp.where(lo, x_i, e_i)
            t_r = a_r[pr] * s_r - a_i[pr] * s_i + elo_r
            t_i = a_r[pr] * s_i + a_i[pr] * s_r + elo_i
            p_scr[pr, pl.ds(r0, 8), 0:N2] = jnp.where(lo, s_r, t_r)
            p_scr[pr, pl.ds(r0, 8), N2:2 * N2] = jnp.where(lo, s_i, t_i)
            new.append(a_r[pr] * t_r - a_i[pr] * t_i + ehi_r)
            new.append(a_r[pr] * t_i + a_i[pr] * t_r + ehi_i)
        return tuple(new)

    zero = jnp.zeros((8, N2), F32)
    lax.fori_loop(0, nt, scan_body, tuple(zero for _ in range(2 * _S5_NP)), unroll=2)

    for pr in range(_S5_NP):
        pt = p_scr[pr].astype(BF16).T
        for k in range(2):
            gi = 2 * pr + k
            pk = jnp.concatenate([pt[k * NS:(k + 1) * NS],
                                  pt[(2 + k) * NS:(3 + k) * NS]], axis=0)
            yt = (jnp.dot(mt_ref[gi], ut_scr[gi], preferred_element_type=F32) +
                  jnp.dot(vt_ref[gi], pk, preferred_element_type=F32))
            for tau in range(S5_SUB):
                yt_scr[tau, gi * P:(gi + 1) * P, :] = yt[tau * P:(tau + 1) * P, :]

    @pl.when(step >= 2)
    def _():
        for tau in range(S5_SUB):
            out_copy(step - 2, slot, tau).wait()

    def store_body(tau, carry):
        ybuf[slot, tau] = yt_scr[tau].T
        return carry

    lax.fori_loop(0, S5_SUB, store_body, 0, unroll=4)

    for tau in range(S5_SUB):
        out_copy(step, slot, tau).start()

    @pl.when(step == nstep - 1)
    def _():
        if _S5_NSTEP > 1:
            for tau in range(S5_SUB):
                out_copy(step - 1, 1 - slot, tau).wait()
        for tau in range(S5_SUB):
            out_copy(step, slot, tau).wait()


def _s5(u_rows, mt, wt, vt, a16r, a16i):
    G, N, R = S5_GROUPS, S5_STATE, S5_ROW
    gb = _S5_GB
    vec = pl.BlockSpec((gb, 1, 2 * N), lambda i: (i, 0, 0))
    mat = lambda r, c: pl.BlockSpec((gb, r, c), lambda i: (i, 0, 0))
    hbm = pl.BlockSpec(memory_space=pl.ANY)
    plane = pltpu.VMEM((2, S5_SUB, S5_COLS, LANES), F32)
    return pl.pallas_call(
        _s5_kernel,
        grid=(_S5_NSTEP,),
        in_specs=[vec, vec, hbm, mat(R, R), mat(2 * N, R), mat(R, 2 * N)],
        out_specs=hbm,
        out_shape=jax.ShapeDtypeStruct((D_S5 // LANES, S5_COLS, S5_SUB, LANES), F32),
        scratch_shapes=[plane, plane,
                        pltpu.SemaphoreType.DMA((2,)), pltpu.SemaphoreType.DMA((2,)),
                        pltpu.VMEM((gb, R, S5_COLS), BF16),
                        pltpu.VMEM((S5_SUB, LANES, S5_COLS), F32),
                        pltpu.VMEM((_S5_NP, S5_COLS, 4 * N), F32),
                        pltpu.VMEM((_S5_NP, S5_COLS, 4 * N), F32)],
        compiler_params=pltpu.CompilerParams(
            dimension_semantics=("arbitrary",), vmem_limit_bytes=VMEM_LIMIT),
        name="s5",
    )(a16r, a16i, u_rows, mt, wt, vt)


_MIX_TM = 512
_MIX_SUB = 256
_MIX_NS = _MIX_TM // S5_SUB


def _mix_kernel(ssm_ref, ret_ref, x_ref, g1_ref, sc_ref, sh_ref, nw_ref, wglu_ref, bglu_ref,
                wout_ref, o_ref, h_ref):
    gain = nw_ref[...] * (1.0 + sc_ref[...])
    nsub = _MIX_TM // _MIX_SUB
    ns = _MIX_SUB // S5_SUB
    gated = []
    for r in range(nsub):
        s = jnp.concatenate([ssm_ref[k, r * ns:(r + 1) * ns].reshape(_MIX_SUB, LANES)
                             for k in range(D_S5 // LANES)], axis=1)
        cdf = 0.5 * (1.0 + jnp.tanh(math.sqrt(2.0 / math.pi) * (s + 0.044715 * (s * s * s))))
        sg = s * cdf
        z = jnp.dot(sg.astype(BF16), wglu_ref[...], preferred_element_type=F32) + bglu_ref[...]
        gated.append((sg * jax.nn.sigmoid(z)).astype(BF16))
    for r in range(nsub):
        rs = slice(r * _MIX_SUB, (r + 1) * _MIX_SUB)
        both = jnp.concatenate([ret_ref[hh, rs, :] for hh in range(RET_HEADS)] + [gated[r]],
                               axis=1)
        mix = jnp.dot(both, wout_ref[...], preferred_element_type=F32)
        x1 = x_ref[rs, :] + g1_ref[...] * mix
        o_ref[rs, :] = x1
        ms = jnp.mean(x1 * x1, axis=-1, keepdims=True)
        h_ref[rs, :] = (x1 * lax.rsqrt(ms + EPS) * gain + sh_ref[...]).astype(BF16)


def _mix(ssm_sb, ret_out, x2d, mod, norm_w, w_glu_bf, b_glu, w_out_bf):
    tm = _MIX_TM
    per_b = SEQ // tm
    rows = pl.BlockSpec((tm, D_MODEL), lambda i: (i, 0))
    return pl.pallas_call(
        _mix_kernel,
        grid=(TOK // tm,),
        in_specs=[pl.BlockSpec((D_S5 // LANES, _MIX_NS, None, S5_SUB, LANES),
                               lambda i: (0, i % per_b, i // per_b, 0, 0)),
                  pl.BlockSpec((RET_HEADS, tm, RET_HEAD_DIM), lambda i: (0, i, 0)),
                  rows, _mod_spec(_G1, tm), _mod_spec(_SC2, tm), _mod_spec(_SH2, tm),
                  pl.BlockSpec((1, D_MODEL), lambda i: (0, 0)),
                  pl.BlockSpec((D_S5, D_S5), lambda i: (0, 0)),
                  pl.BlockSpec((1, D_S5), lambda i: (0, 0)),
                  pl.BlockSpec((D_RET + D_S5, D_MODEL), lambda i: (0, 0))],
        out_specs=[rows, rows],
        out_shape=[jax.ShapeDtypeStruct((TOK, D_MODEL), F32),
                   jax.ShapeDtypeStruct((TOK, D_MODEL), BF16)],
        compiler_params=pltpu.CompilerParams(
            dimension_semantics=("parallel",), vmem_limit_bytes=VMEM_LIMIT),
        name="mix",
    )(ssm_sb, ret_out, x2d, mod, mod, mod, norm_w, w_glu_bf, b_glu, w_out_bf)


_FFN_UP_TM = 2048
_FFN_UP_SUB = 1024
_FFN_UP_TF = 512
_FFN_DOWN_TM = 512
_FFN_DOWN_SUB = 256


def _ffn_up_kernel(h_ref, wg_ref, wu_ref, wd_ref, o_ref, wd_bf_ref, wg_scr, wu_scr):
    @pl.when(pl.program_id(1) == 0)
    def _():
        wg_scr[...] = wg_ref[...].astype(BF16)
        wu_scr[...] = wu_ref[...].astype(BF16)

    wd_bf_ref[...] = wd_ref[...].astype(BF16)
    for r in range(_FFN_UP_TM // _FFN_UP_SUB):
        rs = slice(r * _FFN_UP_SUB, (r + 1) * _FFN_UP_SUB)
        h = h_ref[rs, :]
        gate = jnp.dot(h, wg_scr[...], preferred_element_type=F32)
        up = jnp.dot(h, wu_scr[...], preferred_element_type=F32)
        o_ref[rs, :] = (_silu(gate) * up).astype(BF16)


def _ffn_up(h2, w_gate_up, w_down):
    tm, tf = _FFN_UP_TM, _FFN_UP_TF
    nf = D_FF // tf
    nm = TOK // tm
    slab = D_FF // (nf * nm)
    assert slab * nf * nm == D_FF and slab % 16 == 0
    return pl.pallas_call(
        _ffn_up_kernel,
        grid=(nf, nm),
        in_specs=[pl.BlockSpec((tm, D_MODEL), lambda f, i: (i, 0)),
                  pl.BlockSpec((D_MODEL, tf), lambda f, i: (0, f)),
                  pl.BlockSpec((D_MODEL, tf), lambda f, i: (0, nf + f)),
                  pl.BlockSpec((slab, D_MODEL), lambda f, i: (f * nm + i, 0))],
        out_specs=[pl.BlockSpec((tm, tf), lambda f, i: (i, f)),
                   pl.BlockSpec((slab, D_MODEL), lambda f, i: (f * nm + i, 0))],
        out_shape=[jax.ShapeDtypeStruct((TOK, D_FF), BF16),
                   jax.ShapeDtypeStruct((D_FF, D_MODEL), BF16)],
        scratch_shapes=[pltpu.VMEM((D_MODEL, tf), BF16),
                        pltpu.VMEM((D_MODEL, tf), BF16)],
        compiler_params=pltpu.CompilerParams(
            dimension_semantics=("arbitrary", "arbitrary"), vmem_limit_bytes=VMEM_LIMIT),
        name="ffn_up",
    )(h2, w_gate_up, w_gate_up, w_down)


def _ffn_down_kernel(a_ref, x_ref, g2_ref, fw_ref, wd_ref, o_ref):
    for r in range(_FFN_DOWN_TM // _FFN_DOWN_SUB):
        rs = slice(r * _FFN_DOWN_SUB, (r + 1) * _FFN_DOWN_SUB)
        down = jnp.dot(a_ref[rs, :], wd_ref[...], preferred_element_type=F32)
        x2 = x_ref[rs, :] + g2_ref[...] * down
        ms = jnp.mean(x2 * x2, axis=-1, keepdims=True)
        o_ref[rs, :] = x2 * lax.rsqrt(ms + EPS) * fw_ref[...]


def _ffn_down(act, x1, mod, final_w, w_down_bf):
    tm = _FFN_DOWN_TM
    rows = pl.BlockSpec((tm, D_MODEL), lambda i: (i, 0))
    return pl.pallas_call(
        _ffn_down_kernel,
        grid=(TOK // tm,),
        in_specs=[pl.BlockSpec((tm, D_FF), lambda i: (i, 0)),
                  rows,
                  _mod_spec(_G2, tm),
                  pl.BlockSpec((1, D_MODEL), lambda i: (0, 0)),
                  pl.BlockSpec((D_FF, D_MODEL), lambda i: (0, 0))],
        out_specs=rows,
        out_shape=jax.ShapeDtypeStruct((TOK, D_MODEL), F32),
        compiler_params=pltpu.CompilerParams(
            dimension_semantics=("parallel",), vmem_limit_bytes=VMEM_LIMIT),
        name="ffn_down",
    )(act, x1, mod, final_w, w_down_bf)


def kernel(x, c, w_ada, b_ada, norm1_w, w_in, ret_norm_w, s5_a_re, s5_a_im, s5_log_step,
           s5_b_re, s5_b_im, s5_c_re, s5_c_im, s5_d, w_glu, b_glu, w_out, norm2_w,
           w_gate_up, w_down, final_norm_w):
    x2d = x.reshape(TOK, D_MODEL)
    layer = 0
    mod, mt, wt, vt, a16r, a16i = _prep(
        c, w_ada[layer], b_ada[layer],
        s5_a_re[layer], s5_a_im[layer], s5_log_step[layer], s5_b_re[layer], s5_b_im[layer],
        s5_c_re[layer], s5_c_im[layer], s5_d[layer])

    ret_out, u_sb, w_out_bf, w_glu_bf = _inproj(x2d, mod, norm1_w[layer].reshape(1, D_MODEL),
                                                w_in[layer], w_out[layer], w_glu[layer],
                                                ret_norm_w[layer].reshape(1, D_RET))
    y_rows = _s5(u_sb.reshape(D_S5 // LANES, S5_COLS, S5_SUB, LANES), mt, wt, vt, a16r, a16i)
    ssm_sb = y_rows.reshape(D_S5 // LANES, S5_NSUB, BATCH, S5_SUB, LANES)

    x1, h2 = _mix(ssm_sb, ret_out, x2d, mod, norm2_w[layer].reshape(1, D_MODEL),
                  w_glu_bf, b_glu[layer].reshape(1, D_S5), w_out_bf)
    act, w_down_bf = _ffn_up(h2, w_gate_up[layer], w_down[layer])
    out = _ffn_down(act, x1, mod, final_norm_w.reshape(1, D_MODEL), w_down_bf)
    return out.reshape(BATCH, SEQ, D_MODEL)
```

```python
import math

import numpy as np
import jax
import jax.numpy as jnp
from jax import lax
from jax.experimental import pallas as pl
from jax.experimental.pallas import tpu as pltpu

D_MODEL = 2048
BATCH = 4
SEQ = 2048
TOK = BATCH * SEQ
D_RET = 1024
D_S5 = 1024
RET_HEADS = 4
RET_HEAD_DIM = 256
S5_GROUP = 16
S5_GROUPS = 64
S5_STATE = 64
S5_SUB = 16
S5_ROW = S5_SUB * S5_GROUP
S5_NSUB = SEQ // S5_SUB
S5_COLS = S5_NSUB * BATCH
D_FF = 5632
ROPE_BASE = 10000.0
EPS = 1e-6
LANES = 128

F32 = jnp.float32
BF16 = jnp.bfloat16
HI = lax.Precision.HIGHEST
VMEM_LIMIT = 58 * 1024 * 1024


def _silu(v):
    return v * jax.nn.sigmoid(v)


_SH1, _SC1, _G1, _SH2, _SC2, _G2 = range(6)


def _mod_spec(which, rows_per_step):
    per_b = SEQ // rows_per_step
    return pl.BlockSpec((None, 1, D_MODEL), lambda i: (i // per_b, 0, which))


_INPROJ_TM = 512
_INPROJ_SUB = 256
_INPROJ_NS = _INPROJ_SUB // S5_SUB


_INPROJ_WT = 512
_INPROJ_NW = (4 * D_RET + D_S5) // _INPROJ_WT


def _inproj_kernel(x_ref, sc_ref, sh_ref, nw_ref, w_ref, wout_ref, wglu_ref, lg_ref, gnw_ref,
                   ret_ref, u_ref, wout_bf_ref, wglu_bf_ref,
                   w_scr, h_scr, cos_scr, sin_scr, o_ref, st_scr, intra_scr, dec_scr):
    step = pl.program_id(0)
    C = _INPROJ_SUB
    scale = RET_HEAD_DIM ** -0.5

    @pl.when(step < _INPROJ_NW)
    def _():
        w_scr[step] = w_ref[...].astype(BF16)

        @pl.when(step < SEQ // _INPROJ_TM)
        def _():
            tm = _INPROJ_TM
            half = RET_HEAD_DIM // 2
            pos = (lax.broadcasted_iota(jnp.int32, (tm, half), 0) + step * tm).astype(F32)
            lane = lax.broadcasted_iota(jnp.int32, (tm, half), 1).astype(F32)
            ang = pos * jnp.exp(lane * (-math.log(ROPE_BASE) / half))
            cos_scr[step] = jnp.cos(ang)
            sin_scr[step] = jnp.sin(ang)

        @pl.when(step < RET_HEADS)
        def _():
            lg = lg_ref[step, 0:1, :]
            ii = lax.broadcasted_iota(jnp.int32, (C, C), 0)
            jj = lax.broadcasted_iota(jnp.int32, (C, C), 1)
            diff = (ii - jj).astype(F32)
            intra_scr[step] = jnp.where(
                diff >= 0.0, jnp.exp(lg * jnp.maximum(diff, 0.0)), 0.0) * scale
            row = lax.broadcasted_iota(jnp.int32, (C, LANES), 0).astype(F32)
            lgl = lg[:, 0:LANES]
            dec_scr[0, step] = jnp.exp(lgl * (C - 1.0 - row)) * scale
            dec_scr[1, step] = jnp.exp(lgl * (row + 1.0))

    @pl.when(step >= _INPROJ_NW)
    def _():
        _inproj_rows(step - _INPROJ_NW, x_ref, sc_ref, sh_ref, nw_ref, wout_ref, wglu_ref,
                     lg_ref, gnw_ref, ret_ref, u_ref, wout_bf_ref, wglu_bf_ref,
                     w_scr, h_scr, cos_scr, sin_scr, o_ref, st_scr, intra_scr, dec_scr)


def _inproj_rows(i, x_ref, sc_ref, sh_ref, nw_ref, wout_ref, wglu_ref, lg_ref, gnw_ref,
                 ret_ref, u_ref, wout_bf_ref, wglu_bf_ref,
                 w_scr, h_scr, cos_scr, sin_scr, o_ref, st_scr, intra_scr, dec_scr):
    wout_bf_ref[...] = wout_ref[...].astype(BF16)
    wglu_bf_ref[...] = wglu_ref[...].astype(BF16)
    tm = _INPROJ_TM
    half = RET_HEAD_DIM // 2
    per_b = SEQ // tm
    hpt = _INPROJ_WT // RET_HEAD_DIM
    slot = i % per_b
    C = _INPROJ_SUB
    dh = RET_HEAD_DIM

    @pl.when(slot == 0)
    def _():
        st_scr[...] = jnp.zeros_like(st_scr)

    gain = nw_ref[...] * (1.0 + sc_ref[...])
    shift = sh_ref[...]
    for r in range(tm // _INPROJ_SUB):
        rs = slice(r * _INPROJ_SUB, (r + 1) * _INPROJ_SUB)
        x = x_ref[rs, :]
        ms = jnp.mean(x * x, axis=-1, keepdims=True)
        h_scr[rs, :] = (x * lax.rsqrt(ms + EPS) * gain + shift).astype(BF16)
        cs = cos_scr[slot, rs, :]
        sn = sin_scr[slot, rs, :]
        def project(t):
            acc = jnp.dot(h_scr[rs, :], w_scr[t], preferred_element_type=F32)
            if t < 2 * RET_HEADS // hpt:
                for hh in range(hpt):
                    c0 = hh * RET_HEAD_DIM
                    x1 = acc[:, c0:c0 + half]
                    x2 = acc[:, c0 + half:c0 + 2 * half]
                    o_ref[t * hpt + hh, rs, 0:half] = (x1 * cs - x2 * sn).astype(BF16)
                    o_ref[t * hpt + hh, rs, half:2 * half] = (x2 * cs + x1 * sn).astype(BF16)
            elif t < 4 * RET_HEADS // hpt:
                for hh in range(hpt):
                    c0 = hh * RET_HEAD_DIM
                    o_ref[t * hpt + hh, rs, :] = acc[:, c0:c0 + RET_HEAD_DIM].astype(BF16)
            else:
                k0 = (t - 4 * RET_HEADS // hpt) * (_INPROJ_WT // LANES)
                for k in range(_INPROJ_WT // LANES):
                    u_ref[k0 + k, r * _INPROJ_NS:(r + 1) * _INPROJ_NS] = \
                        acc[:, k * LANES:(k + 1) * LANES].reshape(_INPROJ_NS, S5_SUB, LANES)

        n_qkvg = 4 * RET_HEADS // hpt
        for t in range(n_qkvg):
            project(t)

        for hh in range(RET_HEADS):
            q = o_ref[hh, rs, :]
            k = o_ref[RET_HEADS + hh, rs, :]
            v = o_ref[2 * RET_HEADS + hh, rs, :]
            g = o_ref[3 * RET_HEADS + hh, rs, :].astype(F32)
            kdec = jnp.concatenate([dec_scr[0, hh]] * (dh // LANES), axis=1)
            qdec = jnp.concatenate([dec_scr[1, hh]] * (dh // LANES), axis=1)
            st = st_scr[hh]
            s = lax.dot_general(q, k, (((1,), (1,)), ((), ())),
                                preferred_element_type=F32) * intra_scr[hh]
            y = jnp.dot(s.astype(BF16), v, preferred_element_type=F32)
            y = y + jnp.dot(q, st.astype(BF16), preferred_element_type=F32) * qdec
            kd = (k.astype(F32) * kdec).astype(BF16)
            kv = lax.dot_general(kd, v, (((0,), (0,)), ((), ())), preferred_element_type=F32)
            st_scr[hh] = st * jnp.exp(lg_ref[hh, 0:1, :] * float(C)) + kv
            mu = jnp.mean(y, axis=-1, keepdims=True)
            yc = y - mu
            var = jnp.mean(yc * yc, axis=-1, keepdims=True)
            yn = yc * lax.rsqrt(var + EPS) * gnw_ref[:, hh * dh:(hh + 1) * dh]
            ret_ref[hh, rs, :] = (_silu(g) * yn).astype(BF16)

        for t in range(n_qkvg, _INPROJ_NW):
            project(t)


def _inproj(x2d, mod, norm_w, w_in, w_out, w_glu, ret_norm_w):
    tm = _INPROJ_TM
    per_b = SEQ // tm
    half = RET_HEAD_DIM // 2
    dh = RET_HEAD_DIM
    nw = _INPROJ_NW
    steps = TOK // tm
    so = w_out.shape[0] // steps
    sg = w_glu.shape[0] // steps
    assert w_in.shape == (D_MODEL, 4 * D_RET + D_S5) and D_S5 == D_RET
    assert nw >= per_b and nw >= RET_HEADS
    assert so * steps == w_out.shape[0] and sg * steps == w_glu.shape[0] and sg % 16 == 0
    lg = np.log1p(-np.exp2(-5.0 - np.arange(RET_HEADS, dtype=np.float64)))
    lg_tab = jnp.asarray(np.broadcast_to(lg[:, None, None], (RET_HEADS, 8, dh)), F32)
    row = lambda s: jnp.maximum(s - nw, 0)
    slab_o = pl.BlockSpec((so, w_out.shape[1]), lambda s: (row(s), 0))
    slab_g = pl.BlockSpec((sg, w_glu.shape[1]), lambda s: (row(s), 0))
    mod_spec = lambda which: pl.BlockSpec((None, 1, D_MODEL),
                                          lambda s: (row(s) // per_b, 0, which))
    return pl.pallas_call(
        _inproj_kernel,
        grid=(nw + steps,),
        in_specs=[pl.BlockSpec((tm, D_MODEL), lambda s: (row(s), 0)),
                  mod_spec(_SC1), mod_spec(_SH1),
                  pl.BlockSpec((1, D_MODEL), lambda s: (0, 0)),
                  pl.BlockSpec((D_MODEL, _INPROJ_WT), lambda s: (0, jnp.minimum(s, nw - 1))),
                  slab_o, slab_g,
                  pl.BlockSpec((RET_HEADS, 8, dh), lambda s: (0, 0, 0)),
                  pl.BlockSpec((1, D_RET), lambda s: (0, 0))],
        out_specs=[pl.BlockSpec((RET_HEADS, tm, dh), lambda s: (0, row(s), 0)),
                   pl.BlockSpec((D_S5 // LANES, tm // S5_SUB, None, S5_SUB, LANES),
                                lambda s: (0, row(s) % per_b, row(s) // per_b, 0, 0)),
                   slab_o, slab_g],
        out_shape=[jax.ShapeDtypeStruct((RET_HEADS, TOK, dh), BF16),
                   jax.ShapeDtypeStruct((D_S5 // LANES, S5_NSUB, BATCH, S5_SUB, LANES), F32),
                   jax.ShapeDtypeStruct(w_out.shape, BF16),
                   jax.ShapeDtypeStruct(w_glu.shape, BF16)],
        scratch_shapes=[pltpu.VMEM((nw, D_MODEL, _INPROJ_WT), BF16),
                        pltpu.VMEM((tm, D_MODEL), BF16),
                        pltpu.VMEM((per_b, tm, half), F32),
                        pltpu.VMEM((per_b, tm, half), F32),
                        pltpu.VMEM((4 * RET_HEADS, tm, dh), BF16),
                        pltpu.VMEM((RET_HEADS, dh, dh), F32),
                        pltpu.VMEM((RET_HEADS, _INPROJ_SUB, _INPROJ_SUB), F32),
                        pltpu.VMEM((2, RET_HEADS, _INPROJ_SUB, LANES), F32)],
        compiler_params=pltpu.CompilerParams(
            dimension_semantics=("arbitrary",), vmem_limit_bytes=VMEM_LIMIT),
        name="inproj",
    )(x2d, mod, mod, norm_w, w_in, w_out, w_glu, lg_tab, ret_norm_w)


_S5GEN_GG = 4


_PREP_NBUF = 3


def _prep_kernel(c_ref, wada_hbm, bada_ref,
                 ls_ref, ar_ref, ai_ref, bt_ref, cr_ref, ci_ref, cc_ref, d_ref,
                 mod_ref, mt_ref, wt_ref, vt_ref, a16r_ref, a16i_ref, wbuf, wsem):
    g = pl.program_id(0)
    ng = pl.num_programs(0)
    tn = wbuf.shape[2]
    slot = g % _PREP_NBUF

    def w_copy(tile, sl):
        return pltpu.make_async_copy(
            wada_hbm.at[:, pl.ds(pl.multiple_of(tile * tn, LANES), tn)], wbuf.at[sl], wsem.at[sl])

    @pl.when(g == 0)
    def _():
        for t in range(_PREP_NBUF):
            w_copy(t, t).start()

    w_copy(g, slot).wait()
    cond = _silu(c_ref[...])
    mod = jnp.dot(cond.astype(BF16), wbuf[slot].astype(BF16),
                  preferred_element_type=F32) + bada_ref[...]

    @pl.when(g + _PREP_NBUF < ng)
    def _():
        w_copy(g + _PREP_NBUF, slot).start()

    for b in range(BATCH):
        mod_ref[b] = mod[b:b + 1, :]
    N, P, R = S5_STATE, S5_GROUP, S5_ROW
    re_half = lax.broadcasted_iota(jnp.int32, (P, 2 * N), 1) < N
    sub_e = lax.broadcasted_iota(jnp.int32, (P, 2 * N), 0)
    lane_p = lax.broadcasted_iota(jnp.int32, (P, R), 1)
    sub_p = lax.broadcasted_iota(jnp.int32, (P, R), 0)
    lag0_diag = lane_p == (R - P) + sub_p

    def hdot(a, b):
        return jnp.dot(a, b, precision=HI, preferred_element_type=F32)

    def cmul(xr, xi, yr, yi):
        return xr * yr - xi * yi, xr * yi + xi * yr

    for k in range(_S5GEN_GG):
        dt = jnp.exp(ls_ref[k])
        ar = ar_ref[k]
        ai = ai_ref[k]
        mag = jnp.exp(ar * dt)
        lbr = mag * jnp.cos(ai * dt)
        lbi = mag * jnp.sin(ai * dt)
        den = ar * ar + ai * ai
        zr = ((lbr - 1.0) * ar + lbi * ai) / den
        zi = (lbi * ar - (lbr - 1.0) * ai) / den

        pw_r = jnp.ones((P, 2 * N), F32)
        pw_i = jnp.zeros((P, 2 * N), F32)
        sq_r, sq_i = lbr, lbi
        for bit in range(4):
            nr, ni = cmul(pw_r, pw_i, sq_r, sq_i)
            take = (lax.shift_right_logical(sub_e, bit) & 1) == 1
            pw_r = jnp.where(take, nr, pw_r)
            pw_i = jnp.where(take, ni, pw_i)
            if bit < 3:
                sq_r, sq_i = cmul(sq_r, sq_i, sq_r, sq_i)
        p1_r, p1_i = cmul(pw_r, pw_i, lbr, lbi)
        a16r_ref[k] = p1_r[S5_SUB - 1:S5_SUB, :]
        a16i_ref[k] = p1_i[S5_SUB - 1:S5_SUB, :]

        bt = bt_ref[k]
        x = zr * bt + jnp.where(re_half[0:1], -zi, zi) * pltpu.roll(bt, N, axis=1)
        xs = pltpu.roll(x, N, axis=1)
        pw_is = jnp.where(re_half, -pw_i, pw_i)
        w_rows = jnp.concatenate(
            [pw_r[e:e + 1] * x + pw_is[e:e + 1] * xs for e in range(S5_SUB - 1, -1, -1)], axis=0)
        w_t = w_rows.T
        w_r = w_t[0:N]
        w_i = w_t[N:2 * N]
        wt_ref[k] = w_t.astype(BF16)

        c_re = cr_ref[k]
        c_im = ci_ref[k]
        krev = hdot(c_re, w_r) - hdot(c_im, w_i)
        krev = krev + jnp.where(lag0_diag, d_ref[k], 0.0)
        for tp in range(S5_SUB):
            width = (tp + 1) * P
            piece = krev if width == R else jnp.where(
                lane_p < width, pltpu.roll(krev, width, axis=1), 0.0)
            mt_ref[k, tp * P:(tp + 1) * P, :] = piece.astype(BF16)

        cc = cc_ref[k]
        ccs = pltpu.roll(cc, N, axis=1)
        p1_rs = jnp.where(re_half, p1_r, -p1_r)
        for tp in range(S5_SUB):
            val = cc * p1_rs[tp:tp + 1] - ccs * p1_i[tp:tp + 1]
            vt_ref[k, tp * P:(tp + 1) * P, :] = val.astype(BF16)


def _prep(c, w_ada, b_ada, a_re, a_im, log_step, b_re, b_im, c_re, c_im, d_skip):
    G, N, P, R = S5_GROUPS, S5_STATE, S5_GROUP, S5_ROW
    gg = _S5GEN_GG
    steps = G // gg
    n_mod = w_ada.shape[1]
    tn = n_mod // steps
    assert tn * steps == n_mod and tn % LANES == 0
    dup = lambda a: jnp.concatenate([a, a], axis=-1).reshape(G, 1, 2 * N)
    bt = jnp.concatenate([jnp.swapaxes(b_re, 1, 2), jnp.swapaxes(b_im, 1, 2)], axis=-1)
    cc = jnp.concatenate([c_re, c_im], axis=-1)
    blk = lambda s: pl.BlockSpec((gg,) + s, lambda g: (g, 0, 0))
    return pl.pallas_call(
        _prep_kernel,
        grid=(steps,),
        in_specs=[pl.BlockSpec((BATCH, D_MODEL), lambda g: (0, 0)),
                  pl.BlockSpec(memory_space=pl.ANY),
                  pl.BlockSpec((1, tn), lambda g: (0, g)),
                  blk((1, 1)), blk((1, 2 * N)), blk((1, 2 * N)), blk((P, 2 * N)),
                  blk((P, N)), blk((P, N)), blk((P, 2 * N)), blk((P, 1))],
        out_specs=[pl.BlockSpec((BATCH, 1, tn), lambda g: (0, 0, g)),
                   blk((R, R)), blk((2 * N, R)), blk((R, 2 * N)), blk((1, 2 * N)), blk((1, 2 * N))],
        out_shape=[jax.ShapeDtypeStruct((BATCH, 1, n_mod), F32),
                   jax.ShapeDtypeStruct((G, R, R), BF16),
                   jax.ShapeDtypeStruct((G, 2 * N, R), BF16),
                   jax.ShapeDtypeStruct((G, R, 2 * N), BF16)] +
                  [jax.ShapeDtypeStruct((G, 1, 2 * N), F32)] * 2,
        scratch_shapes=[pltpu.VMEM((_PREP_NBUF, D_MODEL, tn), F32),
                        pltpu.SemaphoreType.DMA((_PREP_NBUF,))],
        compiler_params=pltpu.CompilerParams(
            dimension_semantics=("arbitrary",), vmem_limit_bytes=VMEM_LIMIT,
            allow_input_fusion=[False, False, False, True, True, True, True, False, False, True,
                                True]),
        name="prep",
    )(c, w_ada, b_ada.reshape(1, n_mod),
      log_step.reshape(G, 1, 1), dup(a_re), dup(a_im), bt, c_re, c_im, cc,
      d_skip.reshape(G, P, 1))


_S5_GB = LANES // S5_GROUP
_S5_NP = _S5_GB // 2
_S5_NSTEP = S5_GROUPS // _S5_GB


def _s5_kernel(a16r_ref, a16i_ref, u_hbm, mt_ref, wt_ref, vt_ref, y_hbm,
               ubuf, ybuf, in_sem, out_sem, ut_scr, yt_scr, e_scr, p_scr):
    P = S5_GROUP
    N2 = 2 * S5_STATE
    cols = S5_COLS
    nblk = cols // LANES
    nt = cols // 8
    step = pl.program_id(0)
    nstep = pl.num_programs(0)
    slot = step % 2

    def in_copy(slab, sl, tau):
        return pltpu.make_async_copy(u_hbm.at[slab, :, tau, :], ubuf.at[sl, tau], in_sem.at[sl])

    def out_copy(slab, sl, tau):
        return pltpu.make_async_copy(ybuf.at[sl, tau], y_hbm.at[slab, :, tau, :], out_sem.at[sl])

    @pl.when(step == 0)
    def _():
        for tau in range(S5_SUB):
            in_copy(0, 0, tau).start()

    @pl.when(step + 1 < nstep)
    def _():
        for tau in range(S5_SUB):
            in_copy(step + 1, 1 - slot, tau).start()

    for tau in range(S5_SUB):
        in_copy(step, slot, tau).wait()

    def load_body(tau, carry):
        for c in range(nblk):
            xt = ubuf[slot, tau, c * LANES:(c + 1) * LANES, :].T
            r0 = pl.multiple_of(tau * P, P)
            for k in range(_S5_GB):
                ut_scr[k, pl.ds(r0, P), c * LANES:(c + 1) * LANES] = \
                    xt[k * P:(k + 1) * P, :].astype(BF16)
        return carry

    lax.fori_loop(0, S5_SUB, load_body, 0, unroll=4)

    NS = S5_STATE
    for pr in range(_S5_NP):
        e0 = jnp.dot(wt_ref[2 * pr], ut_scr[2 * pr], preferred_element_type=F32)
        e1 = jnp.dot(wt_ref[2 * pr + 1], ut_scr[2 * pr + 1], preferred_element_type=F32)
        et = jnp.concatenate([e0[0:NS], e1[0:NS], e0[NS:2 * NS], e1[NS:2 * NS]], axis=0)
        e_scr[pr] = et.T

    sub = lax.broadcasted_iota(jnp.int32, (8, N2), 0)
    lo = sub < BATCH
    first = lax.broadcasted_iota(jnp.int32, (1, N2), 1) < S5_STATE
    a_r, a_i = [], []
    for pr in range(_S5_NP):
        a_r.append(jnp.broadcast_to(
            jnp.where(first, a16r_ref[2 * pr], a16r_ref[2 * pr + 1]), (8, N2)))
        a_i.append(jnp.broadcast_to(
            jnp.where(first, a16i_ref[2 * pr], a16i_ref[2 * pr + 1]), (8, N2)))

    def scan_body(t, carry):
        r0 = pl.multiple_of(t * 8, 8)
        new = []
        for pr in range(_S5_NP):
            s_r, s_i = carry[2 * pr], carry[2 * pr + 1]
            e_r = e_scr[pr, pl.ds(r0, 8), 0:N2]
            e_i = e_scr[pr, pl.ds(r0, 8), N2:2 * N2]
            x_r = pltpu.roll(e_r, BATCH, axis=0)
            x_i = pltpu.roll(e_i, BATCH, axis=0)
            elo_r = jnp.where(lo, e_r, x_r)
            elo_i = jnp.where(lo, e_i, x_i)
            ehi_r = jnp.where(lo, x_r, e_r)
            ehi_i = jnp.where(lo, x_i, e_i)
            t_r = a_r[pr] * s_r - a_i[pr] * s_i + elo_r
            t_i = a_r[pr] * s_i + a_i[pr] * s_r + elo_i
            p_scr[pr, pl.ds(r0, 8), 0:N2] = jnp.where(lo, s_r, t_r)
            p_scr[pr, pl.ds(r0, 8), N2:2 * N2] = jnp.where(lo, s_i, t_i)
            new.append(a_r[pr] * t_r - a_i[pr] * t_i + ehi_r)
            new.append(a_r[pr] * t_i + a_i[pr] * t_r + ehi_i)
        return tuple(new)

    zero = jnp.zeros((8, N2), F32)
    lax.fori_loop(0, nt, scan_body, tuple(zero for _ in range(2 * _S5_NP)), unroll=2)

    for pr in range(_S5_NP):
        pt = p_scr[pr].astype(BF16).T
        for k in range(2):
            gi = 2 * pr + k
            pk = jnp.concatenate([pt[k * NS:(k + 1) * NS],
                                  pt[(2 + k) * NS:(3 + k) * NS]], axis=0)
            yt = (jnp.dot(mt_ref[gi], ut_scr[gi], preferred_element_type=F32) +
                  jnp.dot(vt_ref[gi], pk, preferred_element_type=F32))
            for tau in range(S5_SUB):
                yt_scr[tau, gi * P:(gi + 1) * P, :] = yt[tau * P:(tau + 1) * P, :]

    @pl.when(step >= 2)
    def _():
        for tau in range(S5_SUB):
            out_copy(step - 2, slot, tau).wait()

    def store_body(tau, carry):
        ybuf[slot, tau] = yt_scr[tau].T
        return carry

    lax.fori_loop(0, S5_SUB, store_body, 0, unroll=4)

    for tau in range(S5_SUB):
        out_copy(step, slot, tau).start()

    @pl.when(step == nstep - 1)
    def _():
        if _S5_NSTEP > 1:
            for tau in range(S5_SUB):
                out_copy(step - 1, 1 - slot, tau).wait()
        for tau in range(S5_SUB):
            out_copy(step, slot, tau).wait()


def _s5(u_rows, mt, wt, vt, a16r, a16i):
    G, N, R = S5_GROUPS, S5_STATE, S5_ROW
    gb = _S5_GB
    vec = pl.BlockSpec((gb, 1, 2 * N), lambda i: (i, 0, 0))
    mat = lambda r, c: pl.BlockSpec((gb, r, c), lambda i: (i, 0, 0))
    hbm = pl.BlockSpec(memory_space=pl.ANY)
    plane = pltpu.VMEM((2, S5_SUB, S5_COLS, LANES), F32)
    return pl.pallas_call(
        _s5_kernel,
        grid=(_S5_NSTEP,),
        in_specs=[vec, vec, hbm, mat(R, R), mat(2 * N, R), mat(R, 2 * N)],
        out_specs=hbm,
        out_shape=jax.ShapeDtypeStruct((D_S5 // LANES, S5_COLS, S5_SUB, LANES), F32),
        scratch_shapes=[plane, plane,
                        pltpu.SemaphoreType.DMA((2,)), pltpu.SemaphoreType.DMA((2,)),
                        pltpu.VMEM((gb, R, S5_COLS), BF16),
                        pltpu.VMEM((S5_SUB, LANES, S5_COLS), F32),
                        pltpu.VMEM((_S5_NP, S5_COLS, 4 * N), F32),
                        pltpu.VMEM((_S5_NP, S5_COLS, 4 * N), F32)],
        compiler_params=pltpu.CompilerParams(
            dimension_semantics=("arbitrary",), vmem_limit_bytes=VMEM_LIMIT),
        name="s5",
    )(a16r, a16i, u_rows, mt, wt, vt)


_MIX_TM = 512
_MIX_SUB = 256
_MIX_NS = _MIX_TM // S5_SUB


def _mix_kernel(ssm_ref, ret_ref, x_ref, g1_ref, sc_ref, sh_ref, nw_ref, wglu_ref, bglu_ref,
                wout_ref, o_ref, h_ref):
    gain = nw_ref[...] * (1.0 + sc_ref[...])
    nsub = _MIX_TM // _MIX_SUB
    ns = _MIX_SUB // S5_SUB
    gated = []
    for r in range(nsub):
        s = jnp.concatenate([ssm_ref[k, r * ns:(r + 1) * ns].reshape(_MIX_SUB, LANES)
                             for k in range(D_S5 // LANES)], axis=1)
        cdf = 0.5 * (1.0 + jnp.tanh(math.sqrt(2.0 / math.pi) * (s + 0.044715 * (s * s * s))))
        sg = s * cdf
        z = jnp.dot(sg.astype(BF16), wglu_ref[...], preferred_element_type=F32) + bglu_ref[...]
        gated.append((sg * jax.nn.sigmoid(z)).astype(BF16))
    for r in range(nsub):
        rs = slice(r * _MIX_SUB, (r + 1) * _MIX_SUB)
        both = jnp.concatenate([ret_ref[hh, rs, :] for hh in range(RET_HEADS)] + [gated[r]],
                               axis=1)
        mix = jnp.dot(both, wout_ref[...], preferred_element_type=F32)
        x1 = x_ref[rs, :] + g1_ref[...] * mix
        o_ref[rs, :] = x1
        ms = jnp.mean(x1 * x1, axis=-1, keepdims=True)
        h_ref[rs, :] = (x1 * lax.rsqrt(ms + EPS) * gain + sh_ref[...]).astype(BF16)


def _mix(ssm_sb, ret_out, x2d, mod, norm_w, w_glu_bf, b_glu, w_out_bf):
    tm = _MIX_TM
    per_b = SEQ // tm
    rows = pl.BlockSpec((tm, D_MODEL), lambda i: (i, 0))
    return pl.pallas_call(
        _mix_kernel,
        grid=(TOK // tm,),
        in_specs=[pl.BlockSpec((D_S5 // LANES, _MIX_NS, None, S5_SUB, LANES),
                               lambda i: (0, i % per_b, i // per_b, 0, 0)),
                  pl.BlockSpec((RET_HEADS, tm, RET_HEAD_DIM), lambda i: (0, i, 0)),
                  rows, _mod_spec(_G1, tm), _mod_spec(_SC2, tm), _mod_spec(_SH2, tm),
                  pl.BlockSpec((1, D_MODEL), lambda i: (0, 0)),
                  pl.BlockSpec((D_S5, D_S5), lambda i: (0, 0)),
                  pl.BlockSpec((1, D_S5), lambda i: (0, 0)),
                  pl.BlockSpec((D_RET + D_S5, D_MODEL), lambda i: (0, 0))],
        out_specs=[rows, rows],
        out_shape=[jax.ShapeDtypeStruct((TOK, D_MODEL), F32),
                   jax.ShapeDtypeStruct((TOK, D_MODEL), BF16)],
        compiler_params=pltpu.CompilerParams(
            dimension_semantics=("parallel",), vmem_limit_bytes=VMEM_LIMIT),
        name="mix",
    )(ssm_sb, ret_out, x2d, mod, mod, mod, norm_w, w_glu_bf, b_glu, w_out_bf)


_FFN_UP_TM = 2048
_FFN_UP_SUB = 1024
_FFN_UP_TF = 512
_FFN_DOWN_TM = 512
_FFN_DOWN_SUB = 256


def _ffn_up_kernel(h_ref, wg_ref, wu_ref, wd_ref, o_ref, wd_bf_ref, wg_scr, wu_scr):
    @pl.when(pl.program_id(1) == 0)
    def _():
        wg_scr[...] = wg_ref[...].astype(BF16)
        wu_scr[...] = wu_ref[...].astype(BF16)

    wd_bf_ref[...] = wd_ref[...].astype(BF16)
    for r in range(_FFN_UP_TM // _FFN_UP_SUB):
        rs = slice(r * _FFN_UP_SUB, (r + 1) * _FFN_UP_SUB)
        h = h_ref[rs, :]
        gate = jnp.dot(h, wg_scr[...], preferred_element_type=F32)
        up = jnp.dot(h, wu_scr[...], preferred_element_type=F32)
        o_ref[rs, :] = (_silu(gate) * up).astype(BF16)


def _ffn_up(h2, w_gate_up, w_down):
    tm, tf = _FFN_UP_TM, _FFN_UP_TF
    nf = D_FF // tf
    nm = TOK // tm
    slab = D_FF // (nf * nm)
    assert slab * nf * nm == D_FF and slab % 16 == 0
    return pl.pallas_call(
        _ffn_up_kernel,
        grid=(nf, nm),
        in_specs=[pl.BlockSpec((tm, D_MODEL), lambda f, i: (i, 0)),
                  pl.BlockSpec((D_MODEL, tf), lambda f, i: (0, f)),
                  pl.BlockSpec((D_MODEL, tf), lambda f, i: (0, nf + f)),
                  pl.BlockSpec((slab, D_MODEL), lambda f, i: (f * nm + i, 0))],
        out_specs=[pl.BlockSpec((tm, tf), lambda f, i: (i, f)),
                   pl.BlockSpec((slab, D_MODEL), lambda f, i: (f * nm + i, 0))],
        out_shape=[jax.ShapeDtypeStruct((TOK, D_FF), BF16),
                   jax.ShapeDtypeStruct((D_FF, D_MODEL), BF16)],
        scratch_shapes=[pltpu.VMEM((D_MODEL, tf), BF16),
                        pltpu.VMEM((D_MODEL, tf), BF16)],
        compiler_params=pltpu.CompilerParams(
            dimension_semantics=("arbitrary", "arbitrary"), vmem_limit_bytes=VMEM_LIMIT),
        name="ffn_up",
    )(h2, w_gate_up, w_gate_up, w_down)


def _ffn_down_kernel(a_ref, x_ref, g2_ref, fw_ref, wd_ref, o_ref):
    for r in range(_FFN_DOWN_TM // _FFN_DOWN_SUB):
        rs = slice(r * _FFN_DOWN_SUB, (r + 1) * _FFN_DOWN_SUB)
        down = jnp.dot(a_ref[rs, :], wd_ref[...], preferred_element_type=F32)
        x2 = x_ref[rs, :] + g2_ref[...] * down
        ms = jnp.mean(x2 * x2, axis=-1, keepdims=True)
        o_ref[rs, :] = x2 * lax.rsqrt(ms + EPS) * fw_ref[...]


def _ffn_down(act, x1, mod, final_w, w_down_bf):
    tm = _FFN_DOWN_TM
    rows = pl.BlockSpec((tm, D_MODEL), lambda i: (i, 0))
    return pl.pallas_call(
        _ffn_down_kernel,
        grid=(TOK // tm,),
        in_specs=[pl.BlockSpec((tm, D_FF), lambda i: (i, 0)),
                  rows,
                  _mod_spec(_G2, tm),
                  pl.BlockSpec((1, D_MODEL), lambda i: (0, 0)),
                  pl.BlockSpec((D_FF, D_MODEL), lambda i: (0, 0))],
        out_specs=rows,
        out_shape=jax.ShapeDtypeStruct((TOK, D_MODEL), F32),
        compiler_params=pltpu.CompilerParams(
            dimension_semantics=("parallel",), vmem_limit_bytes=VMEM_LIMIT),
        name="ffn_down",
    )(act, x1, mod, final_w, w_down_bf)


def kernel(x, c, w_ada, b_ada, norm1_w, w_in, ret_norm_w, s5_a_re, s5_a_im, s5_log_step,
           s5_b_re, s5_b_im, s5_c_re, s5_c_im, s5_d, w_glu, b_glu, w_out, norm2_w,
           w_gate_up, w_down, final_norm_w):
    x2d = x.reshape(TOK, D_MODEL)
    layer = 0
    mod, mt, wt, vt, a16r, a16i = _prep(
        c, w_ada[layer], b_ada[layer],
        s5_a_re[layer], s5_a_im[layer], s5_log_step[layer], s5_b_re[layer], s5_b_im[layer],
        s5_c_re[layer], s5_c_im[layer], s5_d[layer])

    ret_out, u_sb, w_out_bf, w_glu_bf = _inproj(x2d, mod, norm1_w[layer].reshape(1, D_MODEL),
                                                w_in[layer], w_out[layer], w_glu[layer],
                                                ret_norm_w[layer].reshape(1, D_RET))
    y_rows = _s5(u_sb.reshape(D_S5 // LANES, S5_COLS, S5_SUB, LANES), mt, wt, vt, a16r, a16i)
    ssm_sb = y_rows.reshape(D_S5 // LANES, S5_NSUB, BATCH, S5_SUB, LANES)

    x1, h2 = _mix(ssm_sb, ret_out, x2d, mod, norm2_w[layer].reshape(1, D_MODEL),
                  w_glu_bf, b_glu[layer].reshape(1, D_S5), w_out_bf)
    act, w_down_bf = _ffn_up(h2, w_gate_up[layer], w_down[layer])
    out = _ffn_down(act, x1, mod, final_norm_w.reshape(1, D_MODEL), w_down_bf)
    return out.reshape(BATCH, SEQ, D_MODEL)
```
